```python
import math
import jax
import jax.numpy as jnp
from jax import lax
import numpy as np

D_MODEL = 2048
BATCH = 8
SEQ = 2048
DEPTH = 1
DEC_BATCH = 32
DEC_SEQ = 32
PAST_LEN = 2048

CHUNK = 64
NORM_EPS = 1e-6
H_A = 16
HD_A = 64
D_A = H_A * HD_A
W_LORA = 64
A_LORA = 64
G_LORA = 160
LNX_EPS = 64e-5
A_COLS = 3 * D_A + W_LORA + A_LORA + G_LORA
H_B = 8
N_KV_B = 2
G_B = H_B // N_KV_B
HD_B = 128
D_B = H_B * HD_B
H_I = 16
D_I = 64
TOPK_MAX = 256
Q_BLOCK = 128
REL_BUCKETS = 32
REL_MAX_DIST = 128
N_MEM = 256
H_M = 4
HD_M = 256
D_M = H_M * HD_M
N_BRANCH = 3
IN_SIZES = (A_COLS, D_B, N_KV_B * HD_B, N_KV_B * HD_B, H_I * D_I, D_I, H_I, D_M, N_BRANCH * D_MODEL)
D_IN = sum(IN_SIZES)
N_GROUPS = 4
EXPERTS_PER_GROUP = 4
N_EXPERTS = N_GROUPS * EXPERTS_PER_GROUP
TOP_K_INNER = 2
D_EXPERT = 512

kernel_name = 'hybrid_rwkv7_dsa_memxattn_hmoe_stream_step'


def _split(z, sizes):
    return jnp.split(z, [int(i) for i in np.cumsum(sizes)[:-1]], axis=-1)


def _rmsnorm(x, g):
    xf = x.astype(jnp.float32)
    y = xf * lax.rsqrt(jnp.mean(xf * xf, axis=-1, keepdims=True) + NORM_EPS)
    return y.astype(x.dtype) * g


def _t5_bucket(rel):
    half = REL_BUCKETS // 2
    exact = half // 2
    side = jnp.where(rel > 0, half, 0)
    n = jnp.abs(rel)
    nf = jnp.maximum(n, 1).astype(jnp.float32)
    large = exact + (jnp.log(nf / exact) / math.log(REL_MAX_DIST / exact) * (half - exact)).astype(jnp.int32)
    large = jnp.minimum(large, half - 1)
    return side + jnp.where(n < exact, n, large)


def _wkv_step(S, inp):
    r, w, k, v, a, b = inp
    sa = jnp.einsum('bhvk,bhk->bhv', S, a)
    S = S * w[:, :, None, :] + sa[..., None] * b[:, :, None, :] + v[..., None] * k[:, :, None, :]
    return S, jnp.einsum('bhvk,bhk->bhv', S, r)


def _rwkv7_time_mix(z, shift0, wkv0, lw):
    B, T, _ = z.shape
    prev = jnp.concatenate([shift0, z[:, :-1]], axis=1)
    zm = z + (prev - z) * lw['rwkv_mu']
    r, k, v, wl, al, gl = _split(zm, (D_A, D_A, D_A, W_LORA, A_LORA, G_LORA))
    w = -jax.nn.softplus(-(lw['rwkv_w0'] + jnp.tanh(wl) @ lw['rwkv_w_dec'])) - 0.5
    decay = jnp.exp(-jnp.exp(w.astype(jnp.float32)))
    a = jax.nn.sigmoid(lw['rwkv_a0'] + al @ lw['rwkv_w_a'])
    g = jax.nn.sigmoid(gl) @ lw['rwkv_w_g']
    heads = lambda t: t.astype(jnp.float32).reshape(B, T, H_A, HD_A)
    kk = heads(k * lw['rwkv_k_k'])
    kk = kk * lax.rsqrt(jnp.maximum(jnp.sum(kk * kk, axis=-1, keepdims=True), 1e-24))
    k = k * (1 + (a - 1) * lw['rwkv_k_a'])
    rh, kh, vh, ah, wh = heads(r), heads(k), heads(v), heads(a), heads(decay)
    seq = tuple(jnp.moveaxis(t, 1, 0) for t in (rh, wh, kh, vh, -kk, kk * ah))
    s_final, y = lax.scan(_wkv_step, wkv0.astype(jnp.float32), seq)
    y = jnp.moveaxis(y, 0, 1)
    mean = jnp.mean(y, axis=-1, keepdims=True)
    var = jnp.mean(jnp.square(y - mean), axis=-1, keepdims=True)
    y = ((y - mean) * lax.rsqrt(var + LNX_EPS)).reshape(B, T, D_A) * lw['rwkv_lnx_w'] + lw['rwkv_lnx_b']
    bonus = jnp.sum(rh * kh * lw['rwkv_r_k'], axis=-1, keepdims=True) * vh
    y = (y + bonus.reshape(B, T, D_A)) * g
    return y.astype(z.dtype), z[:, -1:], s_final.astype(z.dtype)


def _dsa_attention(q, iq, iw, k_all, v_all, ik_all, rel_bias, q_start):
    B, T = q.shape[:2]
    L = k_all.shape[1]
    topk = min(TOPK_MAX, L // 4)
    qb = Q_BLOCK if T % Q_BLOCK == 0 else T
    nb = T // qb
    qpos = q_start + jnp.arange(T, dtype=jnp.int32)
    kchunk = jnp.arange(L, dtype=jnp.int32) // CHUNK

    def blocks(t):
        return jnp.moveaxis(t.reshape((B, nb, qb) + t.shape[2:]), 1, 0)

    def one_block(args):
        qs, iqs, iws, ps = args
        qchunk = ps // CHUNK
        visible = kchunk[None, :] <= qchunk[:, None]
        dots = jnp.einsum('bthd,bsd->bths', iqs, ik_all)
        score = jnp.einsum('bths,bth->bts', jax.nn.relu(dots), iws).astype(jnp.float32)
        score = jnp.where(visible[None], score, -jnp.inf)
        _, idx = lax.top_k(score, topk)
        sel_ok = (idx // CHUNK) <= qchunk[None, :, None]
        ks = jax.vmap(lambda kb, ib: kb[ib])(k_all, idx)
        vs = jax.vmap(lambda vb, ib: vb[ib])(v_all, idx)
        qg = qs.reshape(B, qb, N_KV_B, G_B, HD_B)
        logits = jnp.einsum('btngd,btsnd->btngs', qg, ks).astype(jnp.float32) * (HD_B ** -0.5)
        bias = rel_bias[_t5_bucket(idx - ps[None, :, None])].astype(jnp.float32)
        bias = jnp.moveaxis(bias.reshape(B, qb, topk, N_KV_B, G_B), 2, -1)
        logits = jnp.where(sel_ok[:, :, None, None, :], logits + bias, -jnp.inf)
        probs = jax.nn.softmax(logits, axis=-1).astype(vs.dtype)
        out = jnp.einsum('btngs,btsnd->btngd', probs, vs)
        return out.reshape(B, qb, D_B)

    out = lax.map(one_block, (blocks(q), blocks(iq), blocks(iw), qpos.reshape(nb, qb)))
    return jnp.moveaxis(out, 0, 1).reshape(B, T, D_B)


def _memory_attention(q, mem_k, mem_v):
    B, T = q.shape[:2]
    logits = jnp.einsum('bthd,bmhd->bhtm', q, mem_k).astype(jnp.float32) * (HD_M ** -0.5)
    probs = jax.nn.softmax(logits, axis=-1).astype(mem_v.dtype)
    return jnp.einsum('bhtm,bmhd->bthd', probs, mem_v).reshape(B, T, D_M)


def _hier_moe(u, lw):
    B, T, D = u.shape
    x = u.reshape(B * T, D)
    g_logits = (x @ lw['w_grp'] + lw['b_grp']).astype(jnp.float32)
    g_idx = jnp.argmax(g_logits, axis=-1)
    g_w = jnp.take_along_axis(jax.nn.softmax(g_logits, axis=-1), g_idx[:, None], axis=-1)
    e_logits = (x @ lw['w_rt'] + lw['b_rt']).astype(jnp.float32).reshape(-1, N_GROUPS, EXPERTS_PER_GROUP)
    e_in = jnp.take_along_axis(e_logits, g_idx[:, None, None], axis=1)[:, 0]
    top_v, top_i = lax.top_k(e_in, TOP_K_INNER)
    weights = g_w * jax.nn.softmax(top_v, axis=-1)
    eid = g_idx[:, None] * EXPERTS_PER_GROUP + top_i
    gate = jnp.einsum('nke,nk->ne', jax.nn.one_hot(eid, N_EXPERTS, dtype=jnp.float32), weights)
    h = jax.nn.silu(jnp.einsum('nd,edf->nef', x, lw['w1'])) * jnp.einsum('nd,edf->nef', x, lw['w3'])
    h = h * gate.astype(h.dtype)[..., None]
    return jnp.einsum('nef,efd->nd', h, lw['w2']).reshape(B, T, D)


def _layer(x, mem_k, mem_v, shift0, wkv0, past_k, past_v, past_ik, rel_bias, lw):
    B, T, _ = x.shape
    u = _rmsnorm(x, lw['g_attn'])
    zA, zq, zk, zv, ziq, zik, ziw, zmq, zg = _split(u @ lw['w_in'], IN_SIZES)
    o_a, new_shift, new_wkv = _rwkv7_time_mix(zA, shift0, wkv0, lw)
    k_new = zk.reshape(B, T, N_KV_B, HD_B)
    v_new = zv.reshape(B, T, N_KV_B, HD_B)
    o_b = _dsa_attention(zq.reshape(B, T, H_B, HD_B), ziq.reshape(B, T, H_I, D_I), ziw * ((H_I * D_I) ** -0.5),
                         jnp.concatenate([past_k, k_new], axis=1), jnp.concatenate([past_v, v_new], axis=1),
                         jnp.concatenate([past_ik, zik], axis=1), rel_bias, past_k.shape[1])
    o_m = _memory_attention(zmq.reshape(B, T, H_M, HD_M), mem_k, mem_v)
    gates = jax.nn.sigmoid(zg).reshape(B, T, N_BRANCH, D_MODEL)
    mixed = (gates[:, :, 0] * (o_a @ lw['p_a']) + gates[:, :, 1] * (o_b @ lw['p_b'])
             + gates[:, :, 2] * (o_m @ lw['p_m']))
    h = x + mixed @ lw['w_o']
    h = h + _hier_moe(_rmsnorm(h, lw['g_ffn']), lw)
    return h, new_shift, new_wkv, k_new, v_new, zik


def setup_inputs(seed: int = 0) -> dict:
    key = jax.random.key(seed)
    keys = iter(jax.random.split(key, 48))
    f32 = jnp.float32

    def nrm(shape, scale=1.0):
        return jax.random.normal(next(keys), shape, f32) * scale

    return {
        'x_prompt': nrm((BATCH, SEQ, D_MODEL)),
        'x_sample': nrm((DEC_BATCH, DEC_SEQ, D_MODEL)),
        'cache_dsa_k': nrm((DEPTH, DEC_BATCH, PAST_LEN, N_KV_B, HD_B)),
        'cache_dsa_v': nrm((DEPTH, DEC_BATCH, PAST_LEN, N_KV_B, HD_B)),
        'cache_idx_k': nrm((DEPTH, DEC_BATCH, PAST_LEN, D_I)),
        'state_rwkv_shift': nrm((DEPTH, DEC_BATCH, 1, A_COLS)),
        'state_rwkv_wkv': nrm((DEPTH, DEC_BATCH, H_A, HD_A, HD_A)),
        'cache_mem_k': nrm((DEPTH, DEC_BATCH, N_MEM, H_M, HD_M)),
        'cache_mem_v': nrm((DEPTH, DEC_BATCH, N_MEM, H_M, HD_M)),
        'mem_prompt': nrm((BATCH, N_MEM, D_MODEL)),
        'rel_bias': nrm((REL_BUCKETS, H_B), 0.5),
        'g_attn': 1.0 + nrm((DEPTH, D_MODEL), 0.02),
        'w_in': nrm((DEPTH, D_MODEL, D_IN), D_MODEL ** -0.5),
        'rwkv_mu': jax.random.uniform(next(keys), (DEPTH, A_COLS), f32),
        'rwkv_w0': -1.0 + nrm((DEPTH, D_A), 0.5),
        'rwkv_w_dec': nrm((DEPTH, W_LORA, D_A), 0.5 * W_LORA ** -0.5),
        'rwkv_a0': nrm((DEPTH, D_A), 0.1),
        'rwkv_w_a': nrm((DEPTH, A_LORA, D_A), A_LORA ** -0.5),
        'rwkv_w_g': nrm((DEPTH, G_LORA, D_A), G_LORA ** -0.5),
        'rwkv_k_k': 0.85 + nrm((DEPTH, D_A), 0.05),
        'rwkv_k_a': 1.0 + nrm((DEPTH, D_A), 0.05),
        'rwkv_r_k': nrm((DEPTH, H_A, HD_A), 0.1),
        'rwkv_lnx_w': 1.0 + nrm((DEPTH, D_A), 0.02),
        'rwkv_lnx_b': nrm((DEPTH, D_A), 0.01),
        'g_mem': 1.0 + nrm((DEPTH, D_MODEL), 0.02),
        'w_mem_kv': nrm((DEPTH, D_MODEL, 2 * D_M), D_MODEL ** -0.5),
        'p_a': nrm((DEPTH, D_A, D_MODEL), D_A ** -0.5),
        'p_b': nrm((DEPTH, D_B, D_MODEL), D_B ** -0.5),
        'p_m': nrm((DEPTH, D_M, D_MODEL), D_M ** -0.5),
        'w_o': nrm((DEPTH, D_MODEL, D_MODEL), D_MODEL ** -0.5),
        'g_ffn': 1.0 + nrm((DEPTH, D_MODEL), 0.02),
        'w_grp': nrm((DEPTH, D_MODEL, N_GROUPS), D_MODEL ** -0.5),
        'b_grp': nrm((DEPTH, N_GROUPS), 0.01),
        'w_rt': nrm((DEPTH, D_MODEL, N_EXPERTS), D_MODEL ** -0.5),
        'b_rt': nrm((DEPTH, N_EXPERTS), 0.01),
        'w1': nrm((DEPTH, N_EXPERTS, D_MODEL, D_EXPERT), D_MODEL ** -0.5),
        'w3': nrm((DEPTH, N_EXPERTS, D_MODEL, D_EXPERT), D_MODEL ** -0.5),
        'w2': nrm((DEPTH, N_EXPERTS, D_EXPERT, D_MODEL), D_EXPERT ** -0.5),
        'g_final': 1.0 + nrm((D_MODEL,), 0.02),
    }


def reference(x_prompt, x_sample, cache_dsa_k, cache_dsa_v, cache_idx_k, state_rwkv_shift, state_rwkv_wkv,
              cache_mem_k, cache_mem_v, mem_prompt, rel_bias, g_attn, w_in, rwkv_mu, rwkv_w0, rwkv_w_dec,
              rwkv_a0, rwkv_w_a, rwkv_w_g, rwkv_k_k, rwkv_k_a, rwkv_r_k, rwkv_lnx_w, rwkv_lnx_b, g_mem,
              w_mem_kv, p_a, p_b, p_m, w_o, g_ffn, w_grp, b_grp, w_rt, b_rt, w1, w3, w2, g_final):
    B = x_prompt.shape[0]
    dt = x_prompt.dtype
    h_p, h_s = x_prompt, x_sample
    st_p = [[] for _ in range(7)]
    st_s = [[] for _ in range(5)]
    for l in range(DEPTH):
        lw = dict(g_attn=g_attn[l], w_in=w_in[l], rwkv_mu=rwkv_mu[l], rwkv_w0=rwkv_w0[l],
                  rwkv_w_dec=rwkv_w_dec[l], rwkv_a0=rwkv_a0[l], rwkv_w_a=rwkv_w_a[l], rwkv_w_g=rwkv_w_g[l],
                  rwkv_k_k=rwkv_k_k[l], rwkv_k_a=rwkv_k_a[l], rwkv_r_k=rwkv_r_k[l], rwkv_lnx_w=rwkv_lnx_w[l],
                  rwkv_lnx_b=rwkv_lnx_b[l], p_a=p_a[l], p_b=p_b[l], p_m=p_m[l], w_o=w_o[l], g_ffn=g_ffn[l],
                  w_grp=w_grp[l], b_grp=b_grp[l], w_rt=w_rt[l], b_rt=b_rt[l], w1=w1[l], w3=w3[l], w2=w2[l])
        mk, mv = _split(_rmsnorm(mem_prompt, g_mem[l]) @ w_mem_kv[l], (D_M, D_M))
        mk = mk.reshape(B, N_MEM, H_M, HD_M)
        mv = mv.reshape(B, N_MEM, H_M, HD_M)
        h_p, sh_p, wkv_p, k_p, v_p, ik_p = _layer(
            h_p, mk, mv, jnp.zeros((B, 1, A_COLS), dt), jnp.zeros((B, H_A, HD_A, HD_A), dt),
            jnp.zeros((B, 0, N_KV_B, HD_B), dt), jnp.zeros((B, 0, N_KV_B, HD_B), dt),
            jnp.zeros((B, 0, D_I), dt), rel_bias, lw)
        for lst, arr in zip(st_p, (k_p, v_p, ik_p, sh_p, wkv_p, mk, mv)):
            lst.append(arr)
        h_s, sh_s, wkv_s, k_s, v_s, ik_s = _layer(
            h_s, cache_mem_k[l], cache_mem_v[l], state_rwkv_shift[l], state_rwkv_wkv[l],
            cache_dsa_k[l], cache_dsa_v[l], cache_idx_k[l], rel_bias, lw)
        for lst, arr in zip(st_s, (k_s, v_s, ik_s, sh_s, wkv_s)):
            lst.append(arr)
    y_prompt = _rmsnorm(h_p, g_final)
    y_sample = _rmsnorm(h_s, g_final)
    new_dsa_k_p, new_dsa_v_p, new_idx_k_p, new_shift_p, new_wkv_p, new_mem_k_p, new_mem_v_p = (
        jnp.stack(s) for s in st_p)
    new_dsa_k_s, new_dsa_v_s, new_idx_k_s, new_shift_s, new_wkv_s = (jnp.stack(s) for s in st_s)
    return (y_prompt, y_sample, new_dsa_k_p, new_dsa_v_p, new_idx_k_p, new_shift_p, new_wkv_p,
            new_mem_k_p, new_mem_v_p, new_dsa_k_s, new_dsa_v_s, new_idx_k_s, new_shift_s, new_wkv_s)
```

```python
import functools
import math

import jax
import jax.numpy as jnp
from jax import lax
from jax.experimental import pallas as pl
from jax.experimental.pallas import tpu as pltpu

F32 = jnp.float32
BF16 = jnp.bfloat16

LANES = 128
D_MODEL = 2048
CHUNK = 64
NORM_EPS = 1e-6
H_A, HD_A = 16, 64
D_A = H_A * HD_A
W_LORA, A_LORA, G_LORA = 64, 64, 160
LNX_EPS = 64e-5
A_COLS = 3 * D_A + W_LORA + A_LORA + G_LORA
H_B, N_KV_B, HD_B = 8, 2, 128
G_B = H_B // N_KV_B
D_B = H_B * HD_B
D_KV = N_KV_B * HD_B
H_I, D_I = 16, 64
TOPK_MAX = 256
Q_BLOCK = 128
REL_BUCKETS = 32
REL_MAX_DIST = 128
N_MEM, H_M, HD_M = 256, 4, 256
D_M = H_M * HD_M
N_BRANCH = 3
IN_SIZES = (A_COLS, D_B, D_KV, D_KV, H_I * D_I, D_I, H_I, D_M, N_BRANCH * D_MODEL)
N_GROUPS, EXPERTS_PER_GROUP = 4, 4
N_EXPERTS = N_GROUPS * EXPERTS_PER_GROUP
TOP_K_INNER = 2
D_EXPERT = 512

WL_PAD, AL_PAD, GL_PAD = 128, 128, 256
A_PAD = 3 * D_A + WL_PAD + AL_PAD + GL_PAD
KVI_PAD = 2 * D_KV + LANES
INT_MIN = -(2 ** 31)
VMEM_LIMIT = 48 * 1024 * 1024


def _params(*sem):
    return pltpu.CompilerParams(dimension_semantics=sem, vmem_limit_bytes=VMEM_LIMIT)


def _norm_kernel(x_ref, g_ref, o_ref):
    x = x_ref[...]
    y = x * lax.rsqrt(jnp.mean(x * x, axis=-1, keepdims=True) + NORM_EPS)
    o_ref[...] = (y * g_ref[...]).astype(o_ref.dtype)


def _rmsnorm_bf16(x, g, tm):
    n, d = x.shape
    return pl.pallas_call(
        _norm_kernel,
        grid=(n // tm,),
        in_specs=[pl.BlockSpec((tm, d), lambda i: (i, 0)), pl.BlockSpec((1, d), lambda i: (0, 0))],
        out_specs=pl.BlockSpec((tm, d), lambda i: (i, 0)),
        out_shape=jax.ShapeDtypeStruct((n, d), BF16),
        compiler_params=_params("parallel"),
        name="rmsnorm_bf16",
    )(x, g.reshape(1, d))


def _mm_kernel(a_ref, w_ref, o_ref):
    o_ref[...] = jnp.dot(a_ref[...], w_ref[...], preferred_element_type=F32).astype(o_ref.dtype)


def _matmul(a, w, out_dtype, tm, tn, name):
    n, k = a.shape
    m = w.shape[1]
    return pl.pallas_call(
        _mm_kernel,
        grid=(n // tm, m // tn),
        in_specs=[pl.BlockSpec((tm, k), lambda i, j: (i, 0)), pl.BlockSpec((k, tn), lambda i, j: (0, j))],
        out_specs=pl.BlockSpec((tm, tn), lambda i, j: (i, j)),
        out_shape=jax.ShapeDtypeStruct((n, m), out_dtype),
        compiler_params=_params("parallel", "parallel"),
        name=name,
    )(a, w)


def _softplus(x):
    return jnp.maximum(x, 0.0) + jnp.log1p(jnp.exp(-jnp.abs(x)))


def _sigmoid(x):
    return 1.0 / (1.0 + jnp.exp(-x))


def _rwkv_prep_kernel(z_ref, sh_ref, mu_ref, w0_ref, wdec_ref, a0_ref, wa_ref, wg_ref, kk_ref, ka_ref,
                      r_out, w_out, k_out, v_out, kk_out, a_out, g_out, carry_ref):
    tb = z_ref.shape[0]

    @pl.when(pl.program_id(1) == 0)
    def _():
        carry_ref[...] = sh_ref[...]

    z = z_ref[...]
    row = lax.broadcasted_iota(jnp.int32, (tb, 1), 0)
    prev = jnp.where(row == 0, carry_ref[...], pltpu.roll(z, 1, axis=0))
    carry_ref[...] = z[tb - 1:tb, :]
    zm = z + (prev - z) * mu_ref[...]
    r = zm[:, 0:D_A]
    k = zm[:, D_A:2 * D_A]
    v = zm[:, 2 * D_A:3 * D_A]
    o = 3 * D_A
    wl = zm[:, o:o + WL_PAD]
    al = zm[:, o + WL_PAD:o + WL_PAD + AL_PAD]
    gl = zm[:, o + WL_PAD + AL_PAD:]
    lw = jnp.dot(jnp.tanh(wl).astype(BF16), wdec_ref[...], preferred_element_type=F32)
    wv = -_softplus(-(w0_ref[...] + lw)) - 0.5
    a = _sigmoid(a0_ref[...] + jnp.dot(al.astype(BF16), wa_ref[...], preferred_element_type=F32))
    r_out[...] = r
    w_out[...] = jnp.exp(-jnp.exp(wv))
    k_out[...] = k * (1.0 + (a - 1.0) * ka_ref[...])
    v_out[...] = v
    kk_out[...] = k * kk_ref[...]
    a_out[...] = a
    g_out[...] = jnp.dot(_sigmoid(gl).astype(BF16), wg_ref[...], preferred_element_type=F32)


def _rwkv_prep(z, shift0, wts, tb):
    nseq, t, _ = z.shape
    row = lambda c: pl.BlockSpec((1, c), lambda s, i: (0, 0))
    full = lambda a, b: pl.BlockSpec((a, b), lambda s, i: (0, 0))
    out_spec = pl.BlockSpec((None, tb, D_A), lambda s, i: (s, i, 0))
    out_sds = jax.ShapeDtypeStruct((nseq, t, D_A), F32)
    return pl.pallas_call(
        _rwkv_prep_kernel,
        grid=(nseq, t // tb),
        in_specs=[pl.BlockSpec((None, tb, A_PAD), lambda s, i: (s, i, 0)),
                  pl.BlockSpec((None, 1, A_PAD), lambda s, i: (s, 0, 0)),
                  row(A_PAD), row(D_A), full(WL_PAD, D_A), row(D_A), full(AL_PAD, D_A), full(GL_PAD, D_A),
                  row(D_A), row(D_A)],
        out_specs=[out_spec] * 7,
        out_shape=[out_sds] * 7,
        scratch_shapes=[pltpu.VMEM((1, A_PAD), F32)],
        compiler_params=_params("arbitrary", "arbitrary"),
        name="rwkv_prep",
    )(z, shift0, wts["mu"], wts["w0"], wts["w_dec"], wts["a0"], wts["w_a"], wts["w_g"], wts["k_k"], wts["k_a"])


def _wkv_kernel(r_ref, w_ref, k_ref, v_ref, kk_ref, a_ref, s0_ref, lw_ref, lb_ref, rk_ref,
                y_ref, st_ref, s_ref, ab_ref):
    tt = r_ref.shape[0]

    @pl.when(pl.program_id(1) == 0)
    def _():
        s_ref[...] = s0_ref[...]

    def step(t, carry):
        kk = kk_ref[t]
        ss = jnp.sum(kk * kk, axis=0, keepdims=True)
        kkn = kk * lax.rsqrt(jnp.maximum(ss, 1e-24))
        ab_ref[0] = -kkn
        ab_ref[1] = kkn * a_ref[t]
        vv = v_ref[t]
        sa = jnp.zeros((HD_A, LANES), F32)
        for k in range(HD_A):
            sa = sa + s_ref[k] * ab_ref[0, k:k + 1, :]
        y = jnp.zeros((HD_A, LANES), F32)
        for k in range(HD_A):
            s_new = (s_ref[k] * w_ref[t, k:k + 1, :] + sa * ab_ref[1, k:k + 1, :]
                     + vv * k_ref[t, k:k + 1, :])
            s_ref[k] = s_new
            y = y + s_new * r_ref[t, k:k + 1, :]
        mean = jnp.mean(y, axis=0, keepdims=True)
        d = y - mean
        var = jnp.mean(d * d, axis=0, keepdims=True)
        yn = d * lax.rsqrt(var + LNX_EPS) * lw_ref[...] + lb_ref[...]
        bonus = jnp.sum(r_ref[t] * k_ref[t] * rk_ref[...], axis=0, keepdims=True) * vv
        y_ref[t] = yn + bonus
        return carry

    lax.fori_loop(0, tt, step, 0)

    @pl.when(pl.program_id(1) == pl.num_programs(1) - 1)
    def _():
        st_ref[...] = s_ref[...]


def _wkv(r, w, k, v, kk, a, s0, lw, lb, rk, tt):
    t, _, p = r.shape
    seq = pl.BlockSpec((tt, HD_A, LANES), lambda g, i: (i, 0, g))
    vec = pl.BlockSpec((HD_A, LANES), lambda g, i: (0, g))
    st = pl.BlockSpec((HD_A, HD_A, LANES), lambda g, i: (0, 0, g))
    return pl.pallas_call(
        _wkv_kernel,
        grid=(p // LANES, t // tt),
        in_specs=[seq] * 6 + [st, vec, vec, vec],
        out_specs=[seq, st],
        out_shape=[jax.ShapeDtypeStruct((t, HD_A, p), F32), jax.ShapeDtypeStruct((HD_A, HD_A, p), F32)],
        scratch_shapes=[pltpu.VMEM((HD_A, HD_A, LANES), F32), pltpu.VMEM((2, HD_A, LANES), F32)],
        compiler_params=_params("arbitrary", "arbitrary"),
        name="wkv_recurrence",
    )(r, w, k, v, kk, a, s0, lw, lb, rk)


def _to_time_major(x):
    nseq, t, _ = x.shape
    return x.reshape(nseq, t, H_A, HD_A).transpose(1, 3, 0, 2).reshape(t, HD_A, nseq * H_A)


def _from_time_major(x, nseq):
    t = x.shape[0]
    return x.reshape(t, HD_A, nseq, H_A).transpose(2, 0, 3, 1).reshape(nseq, t, D_A)


def _lane_tile(vec, nseq):
    return jnp.tile(vec.reshape(H_A, HD_A).T, (1, nseq))


def _dsa_kernel(q_ref, iq_ref, iw_ref, k_ref, v_ref, ika_ref, ikb_ref, d0_ref, d1_ref, cf_ref, o_ref,
                score_ref, madd_ref, logit_ref, iwb_ref, key_ref, *, q_start, l_valid, topk):
    tq = q_ref.shape[0]
    lp = k_ref.shape[0]
    nch = lp // LANES
    q0 = q_start + pl.program_id(1) * tq
    nt = (((1,), (1,)), ((), ()))

    iw = iw_ref[...] * ((H_I * D_I) ** -0.5)
    for h in range(H_I):
        iwb_ref[h] = jnp.broadcast_to(iw[:, h:h + 1], (tq, LANES))

    def score_chunk(c, carry):
        off = pl.multiple_of(c * LANES, LANES)
        ka = ika_ref[pl.ds(off, LANES), :]
        kb = ikb_ref[pl.ds(off, LANES), :]
        acc = jnp.zeros((tq, LANES), F32)
        for hp in range(H_I // 2):
            lhs = iq_ref[:, hp * LANES:(hp + 1) * LANES]
            da = lax.dot_general(lhs, ka, nt, preferred_element_type=F32)
            db = lax.dot_general(lhs, kb, nt, preferred_element_type=F32)
            acc = acc + jnp.maximum(da, 0.0) * iwb_ref[2 * hp] + jnp.maximum(db, 0.0) * iwb_ref[2 * hp + 1]
        score_ref[:, pl.ds(off, LANES)] = acc
        return carry

    lax.fori_loop(0, nch, score_chunk, 0)

    shift = int(math.log2(CHUNK))
    qchunk = lax.shift_right_arithmetic(q0 + lax.broadcasted_iota(jnp.int32, (tq, 1), 0), shift)

    def visible(lo, width):
        kpos = lo + lax.broadcasted_iota(jnp.int32, (1, width), 1)
        return (lax.shift_right_arithmetic(kpos, shift) <= qchunk) & (kpos < l_valid)

    bits = pltpu.bitcast(score_ref[...] + 0.0, jnp.int32)
    key = bits ^ (lax.shift_right_arithmetic(bits, 31) & jnp.int32(0x7FFFFFFF))
    key_ref[...] = jnp.where(visible(0, lp), key, jnp.int32(INT_MIN))
    kf = jnp.float32(topk)

    def search(i, thr):
        cand = thr ^ lax.shift_left(jnp.int32(1), 31 - i)
        cnt = jnp.sum(jnp.where(key_ref[...] >= cand, 1.0, 0.0), axis=1, keepdims=True)
        return jnp.where(cnt >= kf, cand, thr)

    thr = lax.fori_loop(0, 32, search, jnp.full((tq, 1), INT_MIN, jnp.int32))

    need = kf - jnp.sum(jnp.where(key_ref[...] > thr, 1.0, 0.0), axis=1, keepdims=True)
    tri = jnp.where(lax.broadcasted_iota(jnp.int32, (LANES, LANES), 0)
                    <= lax.broadcasted_iota(jnp.int32, (LANES, LANES), 1), 1.0, 0.0).astype(BF16)
    run = jnp.zeros((tq, 1), F32)
    for c in range(nch):
        sl = slice(c * LANES, (c + 1) * LANES)
        keyc = key_ref[:, sl]
        eqf = jnp.where(keyc == thr, 1.0, 0.0)
        within = jnp.dot(eqf.astype(BF16), tri, preferred_element_type=F32)
        take = jnp.where(keyc > thr, 1.0, jnp.where(run + within <= need, eqf, 0.0))
        sel = jnp.where(visible(c * LANES, LANES), take, 0.0)
        madd_ref[:, sl] = jnp.where(sel > 0.5, 0.0, -jnp.inf)
        run = run + within[:, LANES - 1:LANES]

    dc = q0 // LANES
    for h in range(H_B):
        kv = h // G_B
        s = lax.dot_general(q_ref[:, h * HD_B:(h + 1) * HD_B], k_ref[:, kv * HD_B:(kv + 1) * HD_B], nt,
                            preferred_element_type=F32)
        logit_ref[...] = s * (HD_B ** -0.5) + cf_ref[h:h + 1, 0:1] + madd_ref[...]
        off0 = pl.multiple_of(dc * LANES, LANES)
        logit_ref[:, pl.ds(off0, LANES)] += d0_ref[h]

        @pl.when(dc > 0)
        def _():
            off1 = pl.multiple_of((dc - 1) * LANES, LANES)
            logit_ref[:, pl.ds(off1, LANES)] += d1_ref[h]

        lg = logit_ref[...]
        m = jnp.max(lg, axis=1, keepdims=True)
        p = jnp.exp(lg - m)
        den = jnp.sum(p, axis=1, keepdims=True)
        o = jnp.dot(p.astype(BF16), v_ref[:, kv * HD_B:(kv + 1) * HD_B], preferred_element_type=F32)
        o_ref[:, h * HD_B:(h + 1) * HD_B] = (o / den).astype(o_ref.dtype)


def _t5_bucket(rel):
    half = REL_BUCKETS // 2
    exact = half // 2
    side = jnp.where(rel > 0, half, 0)
    n = jnp.abs(rel)
    nf = jnp.maximum(n, 1).astype(F32)
    large = exact + (jnp.log(nf / exact) / math.log(REL_MAX_DIST / exact) * (half - exact)).astype(jnp.int32)
    large = jnp.minimum(large, half - 1)
    return side + jnp.where(n < exact, n, large)


def _dsa(q, iq, iw, k_all, v_all, ik_all, rel_bias, tq, q_start, l_valid):
    b, t, _ = q.shape
    lp = k_all.shape[1]
    topk = min(TOPK_MAX, l_valid // 4)
    assert q_start % LANES == 0 and (tq == LANES or t == tq) and lp % LANES == 0
    zeros = jnp.zeros_like(ik_all)
    ika = jnp.concatenate([ik_all, zeros], axis=-1)
    ikb = jnp.concatenate([zeros, ik_all], axis=-1)
    ji = jnp.arange(LANES, dtype=jnp.int32)[None, :] - jnp.arange(tq, dtype=jnp.int32)[:, None]
    far = rel_bias[_t5_bucket(jnp.int32(-2 * LANES))]
    d0 = jnp.moveaxis(rel_bias[_t5_bucket(ji)], -1, 0) - far[:, None, None]
    d1 = jnp.moveaxis(rel_bias[_t5_bucket(ji - LANES)], -1, 0) - far[:, None, None]
    cf = jnp.broadcast_to(far[:, None], (H_B, LANES))
    kern = functools.partial(_dsa_kernel, q_start=q_start, l_valid=l_valid, topk=topk)
    qspec = lambda w: pl.BlockSpec((None, tq, w), lambda bi, i: (bi, i, 0))
    kspec = lambda w: pl.BlockSpec((None, lp, w), lambda bi, i: (bi, 0, 0))
    cspec = pl.BlockSpec((H_B, tq, LANES), lambda bi, i: (0, 0, 0))
    return pl.pallas_call(
        kern,
        grid=(b, t // tq),
        in_specs=[qspec(D_B), qspec(H_I * D_I), qspec(H_I), kspec(D_KV), kspec(D_KV), kspec(LANES), kspec(LANES),
                  cspec, cspec, pl.BlockSpec((H_B, LANES), lambda bi, i: (0, 0))],
        out_specs=qspec(D_B),
        out_shape=jax.ShapeDtypeStruct((b, t, D_B), BF16),
        scratch_shapes=[pltpu.VMEM((tq, lp), F32), pltpu.VMEM((tq, lp), F32), pltpu.VMEM((tq, lp), F32),
                        pltpu.VMEM((H_I, tq, LANES), F32), pltpu.VMEM((tq, lp), jnp.int32)],
        compiler_params=_params("parallel", "arbitrary"),
        name="dsa_attention",
    )(q, iq, iw, k_all, v_all, ika, ikb, d0, d1, cf)


def _mem_kernel(q_ref, k_ref, v_ref, o_ref):
    nt = (((1,), (1,)), ((), ()))
    for h in range(H_M):
        sl = slice(h * HD_M, (h + 1) * HD_M)
        s = lax.dot_general(q_ref[:, sl], k_ref[:, sl], nt, preferred_element_type=F32) * (HD_M ** -0.5)
        m = jnp.max(s, axis=1, keepdims=True)
        p = jnp.exp(s - m)
        den = jnp.sum(p, axis=1, keepdims=True)
        o = jnp.dot(p.astype(BF16), v_ref[:, sl], preferred_element_type=F32)
        o_ref[:, sl] = (o / den).astype(o_ref.dtype)


def _mem_attention(q, mk, mv, tq):
    b, t, _ = q.shape
    return pl.pallas_call(
        _mem_kernel,
        grid=(b, t // tq),
        in_specs=[pl.BlockSpec((None, tq, D_M), lambda bi, i: (bi, i, 0)),
                  pl.BlockSpec((None, N_MEM, D_M), lambda bi, i: (bi, 0, 0)),
                  pl.BlockSpec((None, N_MEM, D_M), lambda bi, i: (bi, 0, 0))],
        out_specs=pl.BlockSpec((None, tq, D_M), lambda bi, i: (bi, i, 0)),
        out_shape=jax.ShapeDtypeStruct((b, t, D_M), BF16),
        compiler_params=_params("parallel", "parallel"),
        name="memory_attention",
    )(q, mk, mv)


def _mix_kernel(ya_ref, g_ref, ob_ref, om_ref, ga_ref, gb_ref, gm_ref, pa_ref, pb_ref, pm_ref, o_ref):
    oa = (ya_ref[...] * g_ref[...]).astype(BF16)
    acc = _sigmoid(ga_ref[...]) * jnp.dot(oa, pa_ref[...], preferred_element_type=F32)
    acc = acc + _sigmoid(gb_ref[...]) * jnp.dot(ob_ref[...], pb_ref[...], preferred_element_type=F32)
    acc = acc + _sigmoid(gm_ref[...]) * jnp.dot(om_ref[...], pm_ref[...], preferred_element_type=F32)
    o_ref[...] = acc.astype(o_ref.dtype)


def _mix(ya, g, ob, om, zg, pa, pb, pm, tm, tn):
    n = ya.shape[0]
    nj = D_MODEL // tn
    act = lambda w: pl.BlockSpec((tm, w), lambda j, i: (i, 0))
    gate = lambda br: pl.BlockSpec((tm, tn), lambda j, i, br=br: (i, br * nj + j))
    wt = pl.BlockSpec((D_A, tn), lambda j, i: (0, j))
    return pl.pallas_call(
        _mix_kernel,
        grid=(nj, n // tm),
        in_specs=[act(D_A), act(D_A), act(D_B), act(D_M), gate(0), gate(1), gate(2), wt, wt, wt],
        out_specs=pl.BlockSpec((tm, tn), lambda j, i: (i, j)),
        out_shape=jax.ShapeDtypeStruct((n, D_MODEL), BF16),
        compiler_params=_params("parallel", "parallel"),
        name="branch_mix",
    )(ya, g, ob, om, zg, zg, zg, pa, pb, pm)


def _wo_kernel(m_ref, w_ref, x_ref, g_ref, h_ref, u_ref):
    h = x_ref[...] + jnp.dot(m_ref[...], w_ref[...], preferred_element_type=F32)
    h_ref[...] = h
    y = h * lax.rsqrt(jnp.mean(h * h, axis=-1, keepdims=True) + NORM_EPS)
    u_ref[...] = (y * g_ref[...]).astype(u_ref.dtype)


def _wo_residual_norm(mixed, w_o, x, g, tm):
    n = x.shape[0]
    tile = pl.BlockSpec((tm, D_MODEL), lambda i: (i, 0))
    return pl.pallas_call(
        _wo_kernel,
        grid=(n // tm,),
        in_specs=[tile, pl.BlockSpec((D_MODEL, D_MODEL), lambda i: (0, 0)), tile,
                  pl.BlockSpec((1, D_MODEL), lambda i: (0, 0))],
        out_specs=[tile, tile],
        out_shape=[jax.ShapeDtypeStruct((n, D_MODEL), F32), jax.ShapeDtypeStruct((n, D_MODEL), BF16)],
        compiler_params=_params("parallel"),
        name="wo_residual_norm",
    )(mixed, w_o, x, g.reshape(1, D_MODEL))


def _moe_kernel(u_ref, gate_ref, w1_ref, w3_ref, w2_ref, h_ref, g_ref, y_ref, acc_ref):
    e = pl.program_id(1)

    @pl.when(e == 0)
    def _():
        acc_ref[...] = h_ref[...]

    u = u_ref[...]
    a = jnp.dot(u, w1_ref[...], preferred_element_type=F32)
    b = jnp.dot(u, w3_ref[...], preferred_element_type=F32)
    lane = lax.broadcasted_iota(jnp.int32, gate_ref.shape, 1)
    ge = jnp.sum(jnp.where(lane == e, gate_ref[...], 0.0), axis=1, keepdims=True)
    hid = (a * _sigmoid(a)) * b * ge
    acc_ref[...] += jnp.dot(hid.astype(BF16), w2_ref[...], preferred_element_type=F32)

    @pl.when(e == pl.num_programs(1) - 1)
    def _():
        h = acc_ref[...]
        y = h * lax.rsqrt(jnp.mean(h * h, axis=-1, keepdims=True) + NORM_EPS)
        y_ref[...] = y * g_ref[...]


def _moe_final(u, gate, w1, w3, w2, h, g_final, tm):
    n = u.shape[0]
    tile = pl.BlockSpec((tm, D_MODEL), lambda i, e: (i, 0))
    return pl.pallas_call(
        _moe_kernel,
        grid=(n // tm, N_EXPERTS),
        in_specs=[tile, pl.BlockSpec((tm, N_EXPERTS), lambda i, e: (i, 0)),
                  pl.BlockSpec((None, D_MODEL, D_EXPERT), lambda i, e: (e, 0, 0)),
                  pl.BlockSpec((None, D_MODEL, D_EXPERT), lambda i, e: (e, 0, 0)),
                  pl.BlockSpec((None, D_EXPERT, D_MODEL), lambda i, e: (e, 0, 0)),
                  tile, pl.BlockSpec((1, D_MODEL), lambda i, e: (0, 0))],
        out_specs=tile,
        out_shape=jax.ShapeDtypeStruct((n, D_MODEL), F32),
        scratch_shapes=[pltpu.VMEM((tm, D_MODEL), F32)],
        compiler_params=_params("parallel", "arbitrary"),
        name="moe_final_norm",
    )(u, gate, w1, w3, w2, h, g_final.reshape(1, D_MODEL))


def _route(logits, b_grp, b_rt):
    g_logits = logits[:, :N_GROUPS] + b_grp
    e_logits = (logits[:, N_GROUPS:N_GROUPS + N_EXPERTS] + b_rt).reshape(-1, N_GROUPS, EXPERTS_PER_GROUP)
    g_idx = jnp.argmax(g_logits, axis=-1)
    g_w = jnp.take_along_axis(jax.nn.softmax(g_logits, axis=-1), g_idx[:, None], axis=-1)
    e_in = jnp.take_along_axis(e_logits, g_idx[:, None, None], axis=1)[:, 0]
    top_v, top_i = lax.top_k(e_in, TOP_K_INNER)
    weights = g_w * jax.nn.softmax(top_v, axis=-1)
    eid = g_idx[:, None] * EXPERTS_PER_GROUP + top_i
    return jnp.einsum('nke,nk->ne', jax.nn.one_hot(eid, N_EXPERTS, dtype=F32), weights)


def _pad_cols(w, width):
    return jnp.pad(w, ((0, 0), (0, width - w.shape[1])))


def _pad_rows(w, height):
    return jnp.pad(w, ((0, height - w.shape[0]), (0, 0)))


def _pad_lora(x):
    o = 3 * D_A
    pad = lambda t, w: jnp.pad(t, [(0, 0)] * (t.ndim - 1) + [(0, w - t.shape[-1])])
    return jnp.concatenate([x[..., :o], pad(x[..., o:o + W_LORA], WL_PAD),
                            pad(x[..., o + W_LORA:o + W_LORA + A_LORA], AL_PAD),
                            pad(x[..., o + W_LORA + A_LORA:], GL_PAD)], axis=-1)


def _unpad_lora(x):
    o = 3 * D_A
    return jnp.concatenate([x[..., :o], x[..., o:o + W_LORA], x[..., o + WL_PAD:o + WL_PAD + A_LORA],
                            x[..., o + WL_PAD + AL_PAD:o + WL_PAD + AL_PAD + G_LORA]], axis=-1)


def kernel(x_prompt, x_sample, cache_dsa_k, cache_dsa_v, cache_idx_k, state_rwkv_shift, state_rwkv_wkv, cache_mem_k, cache_mem_v, mem_prompt, rel_bias, g_attn, w_in, rwkv_mu, rwkv_w0, rwkv_w_dec, rwkv_a0, rwkv_w_a, rwkv_w_g, rwkv_k_k, rwkv_k_a, rwkv_r_k, rwkv_lnx_w, rwkv_lnx_b, g_mem, w_mem_kv, p_a, p_b, p_m, w_o, g_ffn, w_grp, b_grp, w_rt, b_rt, w1, w3, w2, g_final):
    assert w_in.shape[0] == 1, "single layer"
    bp, tp, _ = x_prompt.shape
    bs, ts, _ = x_sample.shape
    past = cache_dsa_k.shape[2]
    n_p, n_s = bp * tp, bs * ts
    n = n_p + n_s
    tm = 1024

    offs = [0]
    for s in IN_SIZES:
        offs.append(offs[-1] + s)
    seg = lambda i: w_in[0][:, offs[i]:offs[i + 1]]
    w_a_cols = _pad_lora(seg(0)).astype(BF16)
    w_qim = jnp.concatenate([seg(1), seg(4), seg(7)], axis=1).astype(BF16)
    w_kvi = _pad_cols(jnp.concatenate([seg(2), seg(3), seg(5), seg(6)], axis=1), KVI_PAD).astype(BF16)
    w_gate = seg(8).astype(BF16)
    rw = dict(mu=_pad_lora(rwkv_mu[0])[None], w0=rwkv_w0, a0=rwkv_a0, k_k=rwkv_k_k, k_a=rwkv_k_a,
              w_dec=_pad_rows(rwkv_w_dec[0], WL_PAD).astype(BF16), w_a=_pad_rows(rwkv_w_a[0], AL_PAD).astype(BF16),
              w_g=_pad_rows(rwkv_w_g[0], GL_PAD).astype(BF16))

    x = jnp.concatenate([x_prompt.reshape(n_p, D_MODEL), x_sample.reshape(n_s, D_MODEL)], axis=0)
    u = _rmsnorm_bf16(x, g_attn[0], tm)
    z_a = _matmul(u, w_a_cols, F32, tm, A_PAD // 4, "proj_rwkv")
    z_qim = _matmul(u, w_qim, BF16, tm, 1024, "proj_queries")
    z_kvi = _matmul(u, w_kvi, F32, tm, KVI_PAD, "proj_kv")
    z_g = _matmul(u, w_gate, F32, tm, 1024, "proj_gates")

    um = _rmsnorm_bf16(mem_prompt.reshape(bp * N_MEM, D_MODEL), g_mem[0], 1024)
    mkv = _matmul(um, w_mem_kv[0].astype(BF16), F32, 1024, 1024, "proj_mem_kv")
    mk_p = mkv[:, :D_M].reshape(bp, N_MEM, D_M)
    mv_p = mkv[:, D_M:].reshape(bp, N_MEM, D_M)

    lw_vec, lb_vec, rk_vec = rwkv_lnx_w[0], rwkv_lnx_b[0], rwkv_r_k[0].reshape(D_A)

    def rwkv_group(z, shift0, wkv0, tb, tt):
        nseq = z.shape[0]
        r, w, k, v, kk, a, g = _rwkv_prep(z, shift0, rw, tb)
        tmj = [_to_time_major(t) for t in (r, w, k, v, kk, a)]
        s0 = wkv0.transpose(3, 2, 0, 1).reshape(HD_A, HD_A, nseq * H_A)
        y, s_fin = _wkv(*tmj, s0, _lane_tile(lw_vec, nseq), _lane_tile(lb_vec, nseq), _lane_tile(rk_vec, nseq), tt)
        new_wkv = s_fin.reshape(HD_A, HD_A, nseq, H_A).transpose(2, 3, 1, 0)
        return _from_time_major(y, nseq), g, new_wkv

    za_p = z_a[:n_p].reshape(bp, tp, A_PAD)
    za_s = z_a[n_p:].reshape(bs, ts, A_PAD)
    ya_p, g_p, wkv_p = rwkv_group(za_p, jnp.zeros((bp, 1, A_PAD), F32), jnp.zeros((bp, H_A, HD_A, HD_A), F32),
                                  256, 32)
    ya_s, g_s, wkv_s = rwkv_group(za_s, _pad_lora(state_rwkv_shift[0]), state_rwkv_wkv[0], ts, ts)
    ya = jnp.concatenate([ya_p.reshape(n_p, D_A), ya_s.reshape(n_s, D_A)], axis=0)
    g_rw = jnp.concatenate([g_p.reshape(n_p, D_A), g_s.reshape(n_s, D_A)], axis=0)
    shift_p = _unpad_lora(za_p[:, tp - 1:tp, :])
    shift_s = _unpad_lora(za_s[:, ts - 1:ts, :])

    k_new, v_new = z_kvi[:, :D_KV], z_kvi[:, D_KV:2 * D_KV]
    ik_new = z_kvi[:, 2 * D_KV:2 * D_KV + D_I]
    iw = z_kvi[:, 2 * D_KV + D_I:2 * D_KV + D_I + H_I]
    q_all, iq_all, mq_all = z_qim[:, :D_B], z_qim[:, D_B:D_B + H_I * D_I], z_qim[:, D_B + H_I * D_I:]
    grp = lambda t, sl, b, tlen: t[sl].reshape(b, tlen, t.shape[-1])
    sp, ss = slice(0, n_p), slice(n_p, n)
    ob_p = _dsa(grp(q_all, sp, bp, tp), grp(iq_all, sp, bp, tp), grp(iw, sp, bp, tp),
                grp(k_new, sp, bp, tp).astype(BF16), grp(v_new, sp, bp, tp).astype(BF16),
                grp(ik_new, sp, bp, tp).astype(BF16), rel_bias, Q_BLOCK if tp % Q_BLOCK == 0 else tp, 0, tp)
    l_s = past + ts
    lp_s = -(-l_s // LANES) * LANES
    cat = lambda cache, new: jnp.pad(jnp.concatenate([cache.astype(BF16), new.astype(BF16)], axis=1),
                                     ((0, 0), (0, lp_s - l_s), (0, 0)))
    ob_s = _dsa(grp(q_all, ss, bs, ts), grp(iq_all, ss, bs, ts), grp(iw, ss, bs, ts),
                cat(cache_dsa_k[0].reshape(bs, past, D_KV), grp(k_new, ss, bs, ts)),
                cat(cache_dsa_v[0].reshape(bs, past, D_KV), grp(v_new, ss, bs, ts)),
                cat(cache_idx_k[0], grp(ik_new, ss, bs, ts)), rel_bias,
                Q_BLOCK if ts % Q_BLOCK == 0 else ts, past, l_s)
    ob = jnp.concatenate([ob_p.reshape(n_p, D_B), ob_s.reshape(n_s, D_B)], axis=0)

    om_p = _mem_attention(grp(mq_all, sp, bp, tp), mk_p.astype(BF16), mv_p.astype(BF16), 256)
    om_s = _mem_attention(grp(mq_all, ss, bs, ts), cache_mem_k[0].reshape(bs, N_MEM, D_M).astype(BF16),
                          cache_mem_v[0].reshape(bs, N_MEM, D_M).astype(BF16), ts)
    om = jnp.concatenate([om_p.reshape(n_p, D_M), om_s.reshape(n_s, D_M)], axis=0)

    mixed = _mix(ya, g_rw, ob, om, z_g, p_a[0].astype(BF16), p_b[0].astype(BF16), p_m[0].astype(BF16), 512, 1024)
    h, u2 = _wo_residual_norm(mixed, w_o[0].astype(BF16), x, g_ffn[0], 512)

    w_route = _pad_cols(jnp.concatenate([w_grp[0], w_rt[0]], axis=1), LANES).astype(BF16)
    logits = _matmul(u2, w_route, F32, tm, LANES, "moe_router")
    gate = _route(logits, b_grp[0], b_rt[0])
    y = _moe_final(u2, gate, w1[0].astype(BF16), w3[0].astype(BF16), w2[0].astype(BF16), h, g_final, 512)

    y_prompt = y[:n_p].reshape(bp, tp, D_MODEL)
    y_sample = y[n_p:].reshape(bs, ts, D_MODEL)
    st = lambda t, b, tlen, shape: t.reshape((1, b, tlen) + shape)
    return (y_prompt, y_sample,
            st(k_new[sp], bp, tp, (N_KV_B, HD_B)), st(v_new[sp], bp, tp, (N_KV_B, HD_B)), st(ik_new[sp], bp, tp, (D_I,)),
            shift_p[None], wkv_p[None],
            mk_p.reshape(1, bp, N_MEM, H_M, HD_M), mv_p.reshape(1, bp, N_MEM, H_M, HD_M),
            st(k_new[ss], bs, ts, (N_KV_B, HD_B)), st(v_new[ss], bs, ts, (N_KV_B, HD_B)), st(ik_new[ss], bs, ts, (D_I,)),
            shift_s[None], wkv_s[None])
```

```python
import functools
import math

import jax
import jax.numpy as jnp
from jax import lax
from jax.experimental import pallas as pl
from jax.experimental.pallas import tpu as pltpu

F32 = jnp.float32
BF16 = jnp.bfloat16

LANES = 128
D_MODEL = 2048
CHUNK = 64
NORM_EPS = 1e-6
H_A, HD_A = 16, 64
D_A = H_A * HD_A
W_LORA, A_LORA, G_LORA = 64, 64, 160
LNX_EPS = 64e-5
A_COLS = 3 * D_A + W_LORA + A_LORA + G_LORA
H_B, N_KV_B, HD_B = 8, 2, 128
G_B = H_B // N_KV_B
D_B = H_B * HD_B
D_KV = N_KV_B * HD_B
H_I, D_I = 16, 64
TOPK_MAX = 256
Q_BLOCK = 128
REL_BUCKETS = 32
REL_MAX_DIST = 128
N_MEM, H_M, HD_M = 256, 4, 256
D_M = H_M * HD_M
N_BRANCH = 3
IN_SIZES = (A_COLS, D_B, D_KV, D_KV, H_I * D_I, D_I, H_I, D_M, N_BRANCH * D_MODEL)
N_GROUPS, EXPERTS_PER_GROUP = 4, 4
N_EXPERTS = N_GROUPS * EXPERTS_PER_GROUP
TOP_K_INNER = 2
D_EXPERT = 512

WL_PAD, AL_PAD, GL_PAD = 128, 128, 256
A_PAD = 3 * D_A + WL_PAD + AL_PAD + GL_PAD
KVI_PAD = 2 * D_KV + LANES
KEY_TILE = 2 * LANES
MOE_TM = 512
INT_MIN = -(2 ** 31)
VMEM_LIMIT = 48 * 1024 * 1024


def _params(*sem):
    return pltpu.CompilerParams(dimension_semantics=sem, vmem_limit_bytes=VMEM_LIMIT)


def _two_part_specs(tm, width, na):
    return (pl.BlockSpec((tm, width), lambda i: (jnp.minimum(i, na - 1), 0)),
            pl.BlockSpec((tm, width), lambda i: (jnp.maximum(i - na, 0), 0)))


def _pick(i, na, a_ref, b_ref):
    return jnp.where(i < na, a_ref[...], b_ref[...])


def _norm2_kernel(xa_ref, xb_ref, g_ref, o_ref, *, na):
    x = _pick(pl.program_id(0), na, xa_ref, xb_ref)
    y = x * lax.rsqrt(jnp.mean(x * x, axis=-1, keepdims=True) + NORM_EPS)
    o_ref[...] = (y * g_ref[...]).astype(o_ref.dtype)


def _rmsnorm_bf16(xa, xb, g, tm):
    d = xa.shape[1]
    na, nb = xa.shape[0] // tm, xb.shape[0] // tm
    sa, sb = _two_part_specs(tm, d, na)
    return pl.pallas_call(
        functools.partial(_norm2_kernel, na=na),
        grid=(na + nb,),
        in_specs=[sa, sb, pl.BlockSpec((1, d), lambda i: (0, 0))],
        out_specs=pl.BlockSpec((tm, d), lambda i: (i, 0)),
        out_shape=jax.ShapeDtypeStruct(((na + nb) * tm, d), BF16),
        compiler_params=_params("parallel"),
        name="rmsnorm_bf16",
    )(xa, xb, g.reshape(1, d))


def _mm_kernel(a_ref, w_ref, o_ref):
    o_ref[...] = jnp.dot(a_ref[...], w_ref[...], preferred_element_type=F32).astype(o_ref.dtype)


def _matmul(a, w, out_dtype, tm, tn, name):
    n, k = a.shape
    m = w.shape[1]
    return pl.pallas_call(
        _mm_kernel,
        grid=(n // tm, m // tn),
        in_specs=[pl.BlockSpec((tm, k), lambda i, j: (i, 0)), pl.BlockSpec((k, tn), lambda i, j: (0, j))],
        out_specs=pl.BlockSpec((tm, tn), lambda i, j: (i, j)),
        out_shape=jax.ShapeDtypeStruct((n, m), out_dtype),
        compiler_params=_params("parallel", "parallel"),
        name=name,
    )(a, w)


def _softplus(x):
    return jnp.maximum(x, 0.0) + jnp.log1p(jnp.exp(-jnp.abs(x)))


def _sigmoid(x):
    return 1.0 / (1.0 + jnp.exp(-x))


def _rwkv_prep_kernel(z_ref, sh_ref, mu_ref, w0_ref, wdec_ref, a0_ref, wa_ref, wg_ref, kk_ref, ka_ref,
                      r_out, w_out, k_out, v_out, kk_out, a_out, g_out, carry_ref):
    tb = z_ref.shape[0]

    @pl.when(pl.program_id(1) == 0)
    def _():
        carry_ref[...] = sh_ref[...]

    z = z_ref[...]
    row = lax.broadcasted_iota(jnp.int32, (tb, 1), 0)
    prev = jnp.where(row == 0, carry_ref[...], pltpu.roll(z, 1, axis=0))
    carry_ref[...] = z[tb - 1:tb, :]
    zm = z + (prev - z) * mu_ref[...]
    r = zm[:, 0:D_A]
    k = zm[:, D_A:2 * D_A]
    v = zm[:, 2 * D_A:3 * D_A]
    o = 3 * D_A
    wl = zm[:, o:o + WL_PAD]
    al = zm[:, o + WL_PAD:o + WL_PAD + AL_PAD]
    gl = zm[:, o + WL_PAD + AL_PAD:]
    lw = jnp.dot(jnp.tanh(wl).astype(BF16), wdec_ref[...], preferred_element_type=F32)
    wv = -_softplus(-(w0_ref[...] + lw)) - 0.5
    a = _sigmoid(a0_ref[...] + jnp.dot(al.astype(BF16), wa_ref[...], preferred_element_type=F32))
    r_out[...] = r
    w_out[...] = jnp.exp(-jnp.exp(wv))
    k_out[...] = k * (1.0 + (a - 1.0) * ka_ref[...])
    v_out[...] = v
    kk_out[...] = k * kk_ref[...]
    a_out[...] = a
    g_out[...] = jnp.dot(_sigmoid(gl).astype(BF16), wg_ref[...], preferred_element_type=F32)


def _rwkv_prep(z, row0, nseq, t, shift0, wts, tb):
    nt = t // tb
    blk0 = row0 // tb
    row = lambda c: pl.BlockSpec((1, c), lambda s, i: (0, 0))
    full = lambda a, b: pl.BlockSpec((a, b), lambda s, i: (0, 0))
    out_spec = pl.BlockSpec((None, tb, D_A), lambda s, i: (s, i, 0))
    out_sds = jax.ShapeDtypeStruct((nseq, t, D_A), F32)
    g_spec = pl.BlockSpec((tb, D_A), lambda s, i: (s * nt + i, 0))
    g_sds = jax.ShapeDtypeStruct((nseq * t, D_A), F32)
    return pl.pallas_call(
        _rwkv_prep_kernel,
        grid=(nseq, nt),
        in_specs=[pl.BlockSpec((tb, A_PAD), lambda s, i: (blk0 + s * nt + i, 0)),
                  pl.BlockSpec((None, 1, A_PAD), lambda s, i: (s, 0, 0)),
                  row(A_PAD), row(D_A), full(WL_PAD, D_A), row(D_A), full(AL_PAD, D_A), full(GL_PAD, D_A),
                  row(D_A), row(D_A)],
        out_specs=[out_spec] * 6 + [g_spec],
        out_shape=[out_sds] * 6 + [g_sds],
        scratch_shapes=[pltpu.VMEM((1, A_PAD), F32)],
        compiler_params=_params("arbitrary", "arbitrary"),
        name="rwkv_prep",
    )(z, shift0, wts["mu"], wts["w0"], wts["w_dec"], wts["a0"], wts["w_a"], wts["w_g"], wts["k_k"], wts["k_a"])


def _wkv_kernel(r_ref, w_ref, k_ref, v_ref, kk_ref, a_ref, s0_ref, lw_ref, lb_ref, rk_ref,
                y_ref, st_ref, s_ref, ab_ref):
    tt = r_ref.shape[0]

    @pl.when(pl.program_id(1) == 0)
    def _():
        s_ref[...] = s0_ref[...]

    def step(t, carry):
        kk = kk_ref[t]
        ss = jnp.sum(kk * kk, axis=0, keepdims=True)
        kkn = kk * lax.rsqrt(jnp.maximum(ss, 1e-24))
        ab_ref[0] = -kkn
        ab_ref[1] = kkn * a_ref[t]
        vv = v_ref[t]
        sa = jnp.zeros((HD_A, LANES), F32)
        for k in range(HD_A):
            sa = sa + s_ref[k] * ab_ref[0, k:k + 1, :]
        y = jnp.zeros((HD_A, LANES), F32)
        for k in range(HD_A):
            s_new = (s_ref[k] * w_ref[t, k:k + 1, :] + sa * ab_ref[1, k:k + 1, :]
                     + vv * k_ref[t, k:k + 1, :])
            s_ref[k] = s_new
            y = y + s_new * r_ref[t, k:k + 1, :]
        mean = jnp.mean(y, axis=0, keepdims=True)
        d = y - mean
        var = jnp.mean(d * d, axis=0, keepdims=True)
        yn = d * lax.rsqrt(var + LNX_EPS) * lw_ref[...] + lb_ref[...]
        bonus = jnp.sum(r_ref[t] * k_ref[t] * rk_ref[...], axis=0, keepdims=True) * vv
        y_ref[t] = yn + bonus
        return carry

    lax.fori_loop(0, tt, step, 0)

    @pl.when(pl.program_id(1) == pl.num_programs(1) - 1)
    def _():
        st_ref[...] = s_ref[...]


def _wkv(r, w, k, v, kk, a, s0, lw, lb, rk, tt):
    t, _, p = r.shape
    seq = pl.BlockSpec((tt, HD_A, LANES), lambda g, i: (i, 0, g))
    vec = pl.BlockSpec((HD_A, LANES), lambda g, i: (0, g))
    st = pl.BlockSpec((HD_A, HD_A, LANES), lambda g, i: (0, 0, g))
    return pl.pallas_call(
        _wkv_kernel,
        grid=(p // LANES, t // tt),
        in_specs=[seq] * 6 + [st, vec, vec, vec],
        out_specs=[seq, st],
        out_shape=[jax.ShapeDtypeStruct((t, HD_A, p), F32), jax.ShapeDtypeStruct((HD_A, HD_A, p), F32)],
        scratch_shapes=[pltpu.VMEM((HD_A, HD_A, LANES), F32), pltpu.VMEM((2, HD_A, LANES), F32)],
        compiler_params=_params("arbitrary", "arbitrary"),
        name="wkv_recurrence",
    )(r, w, k, v, kk, a, s0, lw, lb, rk)


def _to_time_major(x):
    nseq, t, _ = x.shape
    return x.reshape(nseq, t, H_A, HD_A).transpose(1, 3, 0, 2).reshape(t, HD_A, nseq * H_A)


def _from_time_major(x, nseq):
    t = x.shape[0]
    return x.reshape(t, HD_A, nseq, H_A).transpose(2, 0, 3, 1).reshape(nseq * t, D_A)


def _lane_tile(vec, nseq):
    return jnp.tile(vec.reshape(H_A, HD_A).T, (1, nseq))


def _dsa_kernel(q_ref, iq_ref, iw_ref, k_ref, v_ref, ik2_ref, d0_ref, d1_ref, o_ref,
                score_ref, madd_ref, logit_ref, iwb_ref, key_ref, mx_ref, ls_ref, acc_ref,
                *, q_start, l_valid, topk):
    tq = q_ref.shape[0]
    lp = k_ref.shape[0]
    q0 = q_start + pl.program_id(1) * tq
    nt = (((1,), (1,)), ((), ()))
    lane_shift = int(math.log2(LANES))
    chunk_shift = int(math.log2(CHUNK))
    dc = lax.shift_right_logical(q0, lane_shift)
    n_lane_tiles = jnp.minimum(lax.shift_right_logical(q0 + tq - 1, lane_shift) + 1, lp // LANES)
    n_tiles = lax.shift_right_logical(n_lane_tiles + 1, 1)

    def tile_off(c, width):
        return pl.multiple_of(c * width, width)

    iw = iw_ref[...] * ((H_I * D_I) ** -0.5)
    for h in range(H_I):
        iwb_ref[h] = jnp.broadcast_to(iw[:, h:h + 1], (tq, LANES))

    def score_tile(c, carry):
        kab = ik2_ref[c]
        acc_a = jnp.zeros((tq, LANES), F32)
        acc_b = jnp.zeros((tq, LANES), F32)
        for hp in range(H_I // 2):
            d = lax.dot_general(iq_ref[:, hp * LANES:(hp + 1) * LANES], kab, nt, preferred_element_type=F32)
            acc_a = acc_a + jnp.maximum(d[:, :LANES], 0.0) * iwb_ref[2 * hp]
            acc_b = acc_b + jnp.maximum(d[:, LANES:], 0.0) * iwb_ref[2 * hp + 1]
        score_ref[:, pl.ds(tile_off(c, LANES), LANES)] = acc_a + acc_b
        return carry

    lax.fori_loop(0, 2 * n_tiles, score_tile, 0)

    qchunk = lax.shift_right_arithmetic(q0 + lax.broadcasted_iota(jnp.int32, (tq, 1), 0), chunk_shift)

    def visible(lo, width):
        kpos = lo + lax.broadcasted_iota(jnp.int32, (1, width), 1)
        return (lax.shift_right_arithmetic(kpos, chunk_shift) <= qchunk) & (kpos < l_valid)

    def key_tile(c, carry):
        off = tile_off(c, KEY_TILE)
        bits = pltpu.bitcast(score_ref[:, pl.ds(off, KEY_TILE)] + 0.0, jnp.int32)
        key = bits ^ (lax.shift_right_arithmetic(bits, 31) & jnp.int32(0x7FFFFFFF))
        key_ref[:, pl.ds(off, KEY_TILE)] = jnp.where(visible(off, KEY_TILE), key, jnp.int32(INT_MIN))
        return carry

    lax.fori_loop(0, n_tiles, key_tile, 0)
    kf = jnp.float32(topk)

    def count(pred):
        def body(c, acc):
            m = jnp.where(pred(key_ref[:, pl.ds(tile_off(c, KEY_TILE), KEY_TILE)]), 1.0, 0.0)
            return acc + m[:, :LANES] + m[:, LANES:]
        acc = lax.fori_loop(0, n_tiles, body, jnp.zeros((tq, LANES), F32))
        return jnp.sum(acc, axis=1, keepdims=True)

    def search(i, thr):
        cand = thr ^ lax.shift_left(jnp.int32(1), 31 - i)
        return jnp.where(count(lambda kc: kc >= cand) >= kf, cand, thr)

    thr = lax.fori_loop(0, 32, search, jnp.full((tq, 1), INT_MIN, jnp.int32))

    surplus = (count(lambda kc: kc >= thr) > kf) & (thr != jnp.int32(INT_MIN))
    any_surplus = jnp.max(jnp.where(surplus, 1.0, 0.0))

    @pl.when(any_surplus == 0.0)
    def _():
        def select_tile(c, carry):
            off = tile_off(c, KEY_TILE)
            sel = visible(off, KEY_TILE) & (key_ref[:, pl.ds(off, KEY_TILE)] >= thr)
            madd_ref[:, pl.ds(off, KEY_TILE)] = jnp.where(sel, 0.0, -jnp.inf)
            return carry

        lax.fori_loop(0, n_tiles, select_tile, 0)

    @pl.when(any_surplus > 0.0)
    def _():
        need = kf - count(lambda kc: kc > thr)
        tri = jnp.where(lax.broadcasted_iota(jnp.int32, (LANES, LANES), 0)
                        <= lax.broadcasted_iota(jnp.int32, (LANES, LANES), 1), 1.0, 0.0).astype(BF16)

        def select_tile(c, run):
            off = tile_off(c, LANES)
            keyc = key_ref[:, pl.ds(off, LANES)]
            eqf = jnp.where(keyc == thr, 1.0, 0.0)
            within = jnp.dot(eqf.astype(BF16), tri, preferred_element_type=F32)
            take = jnp.where(keyc > thr, 1.0, jnp.where(run + within <= need, eqf, 0.0))
            sel = jnp.where(visible(off, LANES), take, 0.0)
            madd_ref[:, pl.ds(off, LANES)] = jnp.where(sel > 0.5, 0.0, -jnp.inf)
            return run + within[:, LANES - 1:LANES]

        lax.fori_loop(0, 2 * n_tiles, select_tile, jnp.zeros((tq, 1), F32))

    off0 = tile_off(dc, LANES)
    off1 = tile_off(jnp.maximum(dc - 1, 0), LANES)
    for g in range(N_KV_B):
        ksl = slice(g * HD_B, (g + 1) * HD_B)
        qs = jnp.concatenate([q_ref[:, (g * G_B + j) * HD_B:(g * G_B + j + 1) * HD_B] for j in range(G_B)], axis=0)

        def logits_tile(c, carry):
            off = tile_off(c, KEY_TILE)
            s = lax.dot_general(qs, k_ref[pl.ds(off, KEY_TILE), ksl], nt, preferred_element_type=F32)
            md = madd_ref[:, pl.ds(off, KEY_TILE)]
            for j in range(G_B):
                logit_ref[j * tq:(j + 1) * tq, pl.ds(off, KEY_TILE)] = s[j * tq:(j + 1) * tq] * (HD_B ** -0.5) + md
            return carry

        lax.fori_loop(0, n_tiles, logits_tile, 0)
        for j in range(G_B):
            logit_ref[j * tq:(j + 1) * tq, pl.ds(off0, LANES)] += d0_ref[g * G_B + j]

        @pl.when(dc > 0)
        def _():
            for j in range(G_B):
                logit_ref[j * tq:(j + 1) * tq, pl.ds(off1, LANES)] += d1_ref[g * G_B + j]

        mx_ref[...] = jnp.full(mx_ref.shape, -jnp.inf, F32)

        def max_tile(c, carry):
            lg = logit_ref[:, pl.ds(tile_off(c, KEY_TILE), KEY_TILE)]
            mx_ref[...] = jnp.maximum(mx_ref[...], jnp.maximum(lg[:, :LANES], lg[:, LANES:]))
            return carry

        lax.fori_loop(0, n_tiles, max_tile, 0)
        m = jnp.max(mx_ref[...], axis=1, keepdims=True)
        ls_ref[...] = jnp.zeros(ls_ref.shape, F32)
        acc_ref[...] = jnp.zeros(acc_ref.shape, F32)

        def pv_tile(c, carry):
            off = tile_off(c, KEY_TILE)
            p = jnp.exp(logit_ref[:, pl.ds(off, KEY_TILE)] - m)
            ls_ref[...] += p[:, :LANES] + p[:, LANES:]
            acc_ref[...] += jnp.dot(p.astype(BF16), v_ref[pl.ds(off, KEY_TILE), ksl], preferred_element_type=F32)
            return carry

        lax.fori_loop(0, n_tiles, pv_tile, 0)
        res = acc_ref[...] / jnp.sum(ls_ref[...], axis=1, keepdims=True)
        for j in range(G_B):
            h = g * G_B + j
            o_ref[:, h * HD_B:(h + 1) * HD_B] = res[j * tq:(j + 1) * tq].astype(o_ref.dtype)


def _t5_bucket(rel):
    half = REL_BUCKETS // 2
    exact = half // 2
    side = jnp.where(rel > 0, half, 0)
    n = jnp.abs(rel)
    nf = jnp.maximum(n, 1).astype(F32)
    large = exact + (jnp.log(nf / exact) / math.log(REL_MAX_DIST / exact) * (half - exact)).astype(jnp.int32)
    large = jnp.minimum(large, half - 1)
    return side + jnp.where(n < exact, n, large)


def _dsa(zq, row0, b, t, iw, k_all, v_all, ik_all, rel_bias, tq, q_start, l_valid):
    lp = k_all.shape[1]
    topk = min(TOPK_MAX, l_valid // 4)
    assert q_start % LANES == 0 and (tq == LANES or t == tq) and lp % KEY_TILE == 0 and row0 % tq == 0
    nq = t // tq
    blk0 = row0 // tq
    ikt = ik_all.reshape(b, lp // LANES, LANES, D_I)
    zeros = jnp.zeros_like(ikt)
    ik2 = jnp.concatenate([jnp.concatenate([ikt, zeros], axis=-1), jnp.concatenate([zeros, ikt], axis=-1)], axis=2)
    ji = jnp.arange(LANES, dtype=jnp.int32)[None, :] - jnp.arange(tq, dtype=jnp.int32)[:, None]
    far = rel_bias[_t5_bucket(jnp.int32(-2 * LANES))]
    d0 = jnp.moveaxis(rel_bias[_t5_bucket(ji)], -1, 0) - far[:, None, None]
    d1 = jnp.moveaxis(rel_bias[_t5_bucket(ji - LANES)], -1, 0) - far[:, None, None]
    kern = functools.partial(_dsa_kernel, q_start=q_start, l_valid=l_valid, topk=topk)
    qspec = lambda col: pl.BlockSpec((tq, D_B), lambda bi, i: (blk0 + bi * nq + i, col))
    kspec = lambda w: pl.BlockSpec((None, lp, w), lambda bi, i: (bi, 0, 0))
    cspec = pl.BlockSpec((H_B, tq, LANES), lambda bi, i: (0, 0, 0))
    rows = G_B * tq
    return pl.pallas_call(
        kern,
        grid=(b, nq),
        in_specs=[qspec(0), qspec(1), pl.BlockSpec((tq, H_I), lambda bi, i: (bi * nq + i, 0)),
                  kspec(D_KV), kspec(D_KV),
                  pl.BlockSpec((None, lp // LANES, 2 * LANES, LANES), lambda bi, i: (bi, 0, 0, 0)), cspec, cspec],
        out_specs=pl.BlockSpec((tq, D_B), lambda bi, i: (bi * nq + i, 0)),
        out_shape=jax.ShapeDtypeStruct((b * t, D_B), BF16),
        scratch_shapes=[pltpu.VMEM((tq, lp), F32), pltpu.VMEM((tq, lp), F32), pltpu.VMEM((rows, lp), F32),
                        pltpu.VMEM((H_I, tq, LANES), F32), pltpu.VMEM((tq, lp), jnp.int32),
                        pltpu.VMEM((rows, LANES), F32), pltpu.VMEM((rows, LANES), F32),
                        pltpu.VMEM((rows, HD_B), F32)],
        compiler_params=_params("parallel", "arbitrary"),
        name="dsa_attention",
    )(zq, zq, iw, k_all, v_all, ik2, d0, d1)


def _mem_kernel(q_ref, k_ref, v_ref, o_ref):
    nt = (((1,), (1,)), ((), ()))
    for h in range(H_M):
        sl = slice(h * HD_M, (h + 1) * HD_M)
        s = lax.dot_general(q_ref[:, sl], k_ref[:, sl], nt, preferred_element_type=F32) * (HD_M ** -0.5)
        m = jnp.max(s, axis=1, keepdims=True)
        p = jnp.exp(s - m)
        den = jnp.sum(p, axis=1, keepdims=True)
        o = jnp.dot(p.astype(BF16), v_ref[:, sl], preferred_element_type=F32)
        o_ref[:, sl] = (o / den).astype(o_ref.dtype)


def _mem_attention(zq, row0, b, t, mk, mv, tq):
    nq = t // tq
    blk0 = row0 // tq
    return pl.pallas_call(
        _mem_kernel,
        grid=(b, nq),
        in_specs=[pl.BlockSpec((tq, D_M), lambda bi, i: (blk0 + bi * nq + i, 2)),
                  pl.BlockSpec((None, N_MEM, D_M), lambda bi, i: (bi, 0, 0)),
                  pl.BlockSpec((None, N_MEM, D_M), lambda bi, i: (bi, 0, 0))],
        out_specs=pl.BlockSpec((tq, D_M), lambda bi, i: (bi * nq + i, 0)),
        out_shape=jax.ShapeDtypeStruct((b * t, D_M), BF16),
        compiler_params=_params("parallel", "parallel"),
        name="memory_attention",
    )(zq, mk, mv)


def _mix_kernel(ya_ref, gp_ref, gs_ref, obp_ref, obs_ref, omp_ref, oms_ref, ga_ref, gb_ref, gm_ref,
                pa_ref, pb_ref, pm_ref, o_ref, *, na):
    i = pl.program_id(1)
    oa = (ya_ref[...] * _pick(i, na, gp_ref, gs_ref)).astype(BF16)
    acc = _sigmoid(ga_ref[...]) * jnp.dot(oa, pa_ref[...], preferred_element_type=F32)
    acc = acc + _sigmoid(gb_ref[...]) * jnp.dot(_pick(i, na, obp_ref, obs_ref), pb_ref[...],
                                                preferred_element_type=F32)
    acc = acc + _sigmoid(gm_ref[...]) * jnp.dot(_pick(i, na, omp_ref, oms_ref), pm_ref[...],
                                                preferred_element_type=F32)
    o_ref[...] = acc.astype(o_ref.dtype)


def _mix(ya, g2, ob2, om2, zg, pa, pb, pm, tm, tn):
    n = ya.shape[0]
    nj = D_MODEL // tn
    na = g2[0].shape[0] // tm
    two = lambda w: (pl.BlockSpec((tm, w), lambda j, i: (jnp.minimum(i, na - 1), 0)),
                     pl.BlockSpec((tm, w), lambda j, i: (jnp.maximum(i - na, 0), 0)))
    gate = lambda br: pl.BlockSpec((tm, tn), lambda j, i, br=br: (i, br * nj + j))
    wt = pl.BlockSpec((D_A, tn), lambda j, i: (0, j))
    return pl.pallas_call(
        functools.partial(_mix_kernel, na=na),
        grid=(nj, n // tm),
        in_specs=[pl.BlockSpec((tm, D_A), lambda j, i: (i, 0)), *two(D_A), *two(D_B), *two(D_M),
                  gate(0), gate(1), gate(2), wt, wt, wt],
        out_specs=pl.BlockSpec((tm, tn), lambda j, i: (i, j)),
        out_shape=jax.ShapeDtypeStruct((n, D_MODEL), BF16),
        compiler_params=_params("parallel", "parallel"),
        name="branch_mix",
    )(ya, *g2, *ob2, *om2, zg, zg, zg, pa, pb, pm)


def _wo_kernel(m_ref, w_ref, xa_ref, xb_ref, g_ref, h_ref, u_ref, *, na):
    x = _pick(pl.program_id(0), na, xa_ref, xb_ref)
    h = x + jnp.dot(m_ref[...], w_ref[...], preferred_element_type=F32)
    h_ref[...] = h
    y = h * lax.rsqrt(jnp.mean(h * h, axis=-1, keepdims=True) + NORM_EPS)
    u_ref[...] = (y * g_ref[...]).astype(u_ref.dtype)


def _wo_residual_norm(mixed, w_o, xa, xb, g, tm):
    n = mixed.shape[0]
    na = xa.shape[0] // tm
    tile = pl.BlockSpec((tm, D_MODEL), lambda i: (i, 0))
    sa, sb = _two_part_specs(tm, D_MODEL, na)
    return pl.pallas_call(
        functools.partial(_wo_kernel, na=na),
        grid=(n // tm,),
        in_specs=[tile, pl.BlockSpec((D_MODEL, D_MODEL), lambda i: (0, 0)), sa, sb,
                  pl.BlockSpec((1, D_MODEL), lambda i: (0, 0))],
        out_specs=[tile, tile],
        out_shape=[jax.ShapeDtypeStruct((n, D_MODEL), F32), jax.ShapeDtypeStruct((n, D_MODEL), BF16)],
        compiler_params=_params("parallel"),
        name="wo_residual_norm",
    )(mixed, w_o, xa, xb, g.reshape(1, D_MODEL))


def _moe_kernel(tg_ref, nu_ref, x_ref, gate_ref, w1_ref, w3_ref, w2_ref, o_ref):
    i, j = pl.program_id(0), pl.program_id(1)

    @pl.when((i >= nu_ref[0]) & (j == 0))
    def _():
        o_ref[...] = jnp.zeros(o_ref.shape, F32)

    @pl.when(i < nu_ref[0])
    def _():
        x = x_ref[...]
        a = jnp.dot(x, w1_ref[...], preferred_element_type=F32)
        b = jnp.dot(x, w3_ref[...], preferred_element_type=F32)
        lane = lax.broadcasted_iota(jnp.int32, gate_ref.shape, 1)
        ge = jnp.sum(jnp.where(lane == j, gate_ref[...], 0.0), axis=1, keepdims=True)
        hid = (a * _sigmoid(a)) * b * ge
        out = jnp.dot(hid.astype(BF16), w2_ref[...], preferred_element_type=F32)

        @pl.when(j == 0)
        def _():
            o_ref[...] = out

        @pl.when(j > 0)
        def _():
            o_ref[...] += out


def _moe_grouped(x_sorted, gate_sorted, tile_group, n_used, w1, w3, w2):
    npad = x_sorted.shape[0]
    ew = lambda i, j, tg, nu: (tg[i] * EXPERTS_PER_GROUP + j, 0, 0)
    return pl.pallas_call(
        _moe_kernel,
        grid_spec=pltpu.PrefetchScalarGridSpec(
            num_scalar_prefetch=2,
            grid=(npad // MOE_TM, EXPERTS_PER_GROUP),
            in_specs=[pl.BlockSpec((MOE_TM, D_MODEL), lambda i, j, tg, nu: (i, 0)),
                      pl.BlockSpec((MOE_TM, EXPERTS_PER_GROUP), lambda i, j, tg, nu: (i, 0)),
                      pl.BlockSpec((None, D_MODEL, D_EXPERT), ew), pl.BlockSpec((None, D_MODEL, D_EXPERT), ew),
                      pl.BlockSpec((None, D_EXPERT, D_MODEL), ew)],
            out_specs=pl.BlockSpec((MOE_TM, D_MODEL), lambda i, j, tg, nu: (i, 0))),
        out_shape=jax.ShapeDtypeStruct((npad, D_MODEL), F32),
        compiler_params=_params("parallel", "arbitrary"),
        name="moe_grouped",
    )(tile_group, n_used, x_sorted, gate_sorted, w1, w3, w2)


def _route(logits, b_grp, b_rt):
    n = logits.shape[0]
    g_logits = logits[:, :N_GROUPS] + b_grp
    e_logits = (logits[:, N_GROUPS:N_GROUPS + N_EXPERTS] + b_rt).reshape(n, N_GROUPS, EXPERTS_PER_GROUP)
    g_idx = jnp.argmax(g_logits, axis=-1).astype(jnp.int32)
    g_w = jnp.max(jax.nn.softmax(g_logits, axis=-1), axis=-1, keepdims=True)
    onehot_g = g_idx[:, None] == jnp.arange(N_GROUPS, dtype=jnp.int32)[None, :]
    e_in = jnp.sum(jnp.where(onehot_g[:, :, None], e_logits, 0.0), axis=1)
    lane = jnp.arange(EXPERTS_PER_GROUP, dtype=jnp.int32)[None, :]
    i1 = jnp.argmax(e_in, axis=-1)[:, None]
    v1 = jnp.max(e_in, axis=-1, keepdims=True)
    rest = jnp.where(lane == i1, -jnp.inf, e_in)
    i2 = jnp.argmax(rest, axis=-1)[:, None]
    v2 = jnp.max(rest, axis=-1, keepdims=True)
    w12 = g_w * jax.nn.softmax(jnp.concatenate([v1, v2], axis=-1), axis=-1)
    gate4 = jnp.where(lane == i1, w12[:, 0:1], 0.0) + jnp.where(lane == i2, w12[:, 1:2], 0.0)

    counts = jnp.sum(onehot_g.astype(jnp.int32), axis=0)
    tiles = (counts + MOE_TM - 1) // MOE_TM
    tile_end = jnp.cumsum(tiles)
    slot0 = (tile_end - tiles) * MOE_TM
    start = jnp.cumsum(counts) - counts
    order = jnp.argsort(g_idx, stable=True).astype(jnp.int32)
    rank = jnp.sum(jnp.where(onehot_g, jnp.cumsum(onehot_g.astype(jnp.int32), axis=0) - 1, 0), axis=1)
    inv = jnp.sum(jnp.where(onehot_g, slot0[None, :], 0), axis=1) + rank
    npad = n + N_GROUPS * MOE_TM
    slots = jnp.arange(npad, dtype=jnp.int32)
    slot_group = jnp.minimum(jnp.sum(jnp.where(slots[:, None] >= tile_end[None, :] * MOE_TM, 1, 0), axis=1),
                             N_GROUPS - 1)
    in_group = slot_group[:, None] == jnp.arange(N_GROUPS, dtype=jnp.int32)[None, :]
    pos = slots - jnp.sum(jnp.where(in_group, slot0[None, :], 0), axis=1)
    used = pos < jnp.sum(jnp.where(in_group, counts[None, :], 0), axis=1)
    src = jnp.where(used, pos + jnp.sum(jnp.where(in_group, start[None, :], 0), axis=1), 0)
    perm = order[src]
    gate_sorted = jnp.where(used[:, None], gate4[perm], 0.0)
    tile_group = slot_group[::MOE_TM].astype(jnp.int32)
    return perm, inv, gate_sorted, tile_group, tile_end[-1:].astype(jnp.int32)


def _final_kernel(h_ref, m_ref, g_ref, yp_ref, ys_ref, *, na):
    h = h_ref[...] + m_ref[...]
    y = h * lax.rsqrt(jnp.mean(h * h, axis=-1, keepdims=True) + NORM_EPS) * g_ref[...]
    i = pl.program_id(0)

    @pl.when(i < na)
    def _():
        yp_ref[...] = y

    @pl.when(i >= na)
    def _():
        ys_ref[...] = y


def _final_norm(h, moe, g, n_p, tm):
    n = h.shape[0]
    na = n_p // tm
    tile = pl.BlockSpec((tm, D_MODEL), lambda i: (i, 0))
    sa, sb = _two_part_specs(tm, D_MODEL, na)
    return pl.pallas_call(
        functools.partial(_final_kernel, na=na),
        grid=(n // tm,),
        in_specs=[tile, tile, pl.BlockSpec((1, D_MODEL), lambda i: (0, 0))],
        out_specs=[sa, sb],
        out_shape=[jax.ShapeDtypeStruct((n_p, D_MODEL), F32), jax.ShapeDtypeStruct((n - n_p, D_MODEL), F32)],
        compiler_params=_params("arbitrary"),
        name="residual_final_norm",
    )(h, moe, g.reshape(1, D_MODEL))


def _pad_cols(w, width):
    return jnp.pad(w, ((0, 0), (0, width - w.shape[1])))


def _pad_rows(w, height):
    return jnp.pad(w, ((0, height - w.shape[0]), (0, 0)))


def _pad_lora(x):
    o = 3 * D_A
    pad = lambda t, w: jnp.pad(t, [(0, 0)] * (t.ndim - 1) + [(0, w - t.shape[-1])])
    return jnp.concatenate([x[..., :o], pad(x[..., o:o + W_LORA], WL_PAD),
                            pad(x[..., o + W_LORA:o + W_LORA + A_LORA], AL_PAD),
                            pad(x[..., o + W_LORA + A_LORA:], GL_PAD)], axis=-1)


def _unpad_lora(x):
    o = 3 * D_A
    return jnp.concatenate([x[..., :o], x[..., o:o + W_LORA], x[..., o + WL_PAD:o + WL_PAD + A_LORA],
                            x[..., o + WL_PAD + AL_PAD:o + WL_PAD + AL_PAD + G_LORA]], axis=-1)


def kernel(x_prompt, x_sample, cache_dsa_k, cache_dsa_v, cache_idx_k, state_rwkv_shift, state_rwkv_wkv, cache_mem_k, cache_mem_v, mem_prompt, rel_bias, g_attn, w_in, rwkv_mu, rwkv_w0, rwkv_w_dec, rwkv_a0, rwkv_w_a, rwkv_w_g, rwkv_k_k, rwkv_k_a, rwkv_r_k, rwkv_lnx_w, rwkv_lnx_b, g_mem, w_mem_kv, p_a, p_b, p_m, w_o, g_ffn, w_grp, b_grp, w_rt, b_rt, w1, w3, w2, g_final):
    assert w_in.shape[0] == 1, "single layer"
    bp, tp, _ = x_prompt.shape
    bs, ts, _ = x_sample.shape
    past = cache_dsa_k.shape[2]
    n_p, n_s = bp * tp, bs * ts
    n = n_p + n_s
    tm = 1024
    xp = x_prompt.reshape(n_p, D_MODEL)
    xs = x_sample.reshape(n_s, D_MODEL)

    offs = [0]
    for s in IN_SIZES:
        offs.append(offs[-1] + s)
    seg = lambda i: w_in[0][:, offs[i]:offs[i + 1]]
    w_a_cols = _pad_lora(seg(0)).astype(BF16)
    w_qim = jnp.concatenate([seg(1), seg(4), seg(7)], axis=1).astype(BF16)
    w_kvi = _pad_cols(jnp.concatenate([seg(2), seg(3), seg(5), seg(6)], axis=1), KVI_PAD).astype(BF16)
    w_gate = seg(8).astype(BF16)
    rw = dict(mu=_pad_lora(rwkv_mu[0])[None], w0=rwkv_w0, a0=rwkv_a0, k_k=rwkv_k_k, k_a=rwkv_k_a,
              w_dec=_pad_rows(rwkv_w_dec[0], WL_PAD).astype(BF16), w_a=_pad_rows(rwkv_w_a[0], AL_PAD).astype(BF16),
              w_g=_pad_rows(rwkv_w_g[0], GL_PAD).astype(BF16))

    u = _rmsnorm_bf16(xp, xs, g_attn[0], 512)
    z_a = _matmul(u, w_a_cols, F32, tm, A_PAD // 4, "proj_rwkv")
    z_qim = _matmul(u, w_qim, BF16, tm, 1024, "proj_queries")
    z_kvi = _matmul(u, w_kvi, F32, tm, KVI_PAD, "proj_kv")
    z_g = _matmul(u, w_gate, F32, tm, 1024, "proj_gates")

    mem = mem_prompt.reshape(bp * N_MEM, D_MODEL)
    um = _rmsnorm_bf16(mem[:bp * N_MEM // 2], mem[bp * N_MEM // 2:], g_mem[0], 512)
    mkv = _matmul(um, w_mem_kv[0].astype(BF16), F32, 1024, 1024, "proj_mem_kv")
    mk_p = mkv[:, :D_M].reshape(bp, N_MEM, D_M)
    mv_p = mkv[:, D_M:].reshape(bp, N_MEM, D_M)

    lw_vec, lb_vec, rk_vec = rwkv_lnx_w[0], rwkv_lnx_b[0], rwkv_r_k[0].reshape(D_A)

    def rwkv_group(row0, nseq, t, shift0, wkv0, tb, tt):
        r, w, k, v, kk, a, g = _rwkv_prep(z_a, row0, nseq, t, shift0, rw, tb)
        tmj = [_to_time_major(x) for x in (r, w, k, v, kk, a)]
        s0 = wkv0.transpose(3, 2, 0, 1).reshape(HD_A, HD_A, nseq * H_A)
        y, s_fin = _wkv(*tmj, s0, _lane_tile(lw_vec, nseq), _lane_tile(lb_vec, nseq), _lane_tile(rk_vec, nseq), tt)
        new_wkv = s_fin.reshape(HD_A, HD_A, nseq, H_A).transpose(2, 3, 1, 0)
        return _from_time_major(y, nseq), g, new_wkv

    ya_p, g_p, wkv_p = rwkv_group(0, bp, tp, jnp.zeros((bp, 1, A_PAD), F32), jnp.zeros((bp, H_A, HD_A, HD_A), F32),
                                  256, 32)
    ya_s, g_s, wkv_s = rwkv_group(n_p, bs, ts, _pad_lora(state_rwkv_shift[0]), state_rwkv_wkv[0], ts, ts)
    ya = jnp.concatenate([ya_p, ya_s], axis=0)
    last = lambda row0, b, t: _unpad_lora(z_a[row0:row0 + b * t].reshape(b, t, A_PAD)[:, t - 1:t, :])
    shift_p, shift_s = last(0, bp, tp), last(n_p, bs, ts)

    k_new, v_new = z_kvi[:, :D_KV], z_kvi[:, D_KV:2 * D_KV]
    ik_new = z_kvi[:, 2 * D_KV:2 * D_KV + D_I]
    iw = z_kvi[:, 2 * D_KV + D_I:2 * D_KV + D_I + H_I]
    grp = lambda t, sl, b, tlen: t[sl].reshape(b, tlen, t.shape[-1])
    sp, ss = slice(0, n_p), slice(n_p, n)
    ob_p = _dsa(z_qim, 0, bp, tp, iw[sp], grp(k_new, sp, bp, tp).astype(BF16), grp(v_new, sp, bp, tp).astype(BF16),
                grp(ik_new, sp, bp, tp).astype(BF16), rel_bias, Q_BLOCK if tp % Q_BLOCK == 0 else tp, 0, tp)
    l_s = past + ts
    lp_s = -(-l_s // KEY_TILE) * KEY_TILE
    cat = lambda cache, new: jnp.pad(jnp.concatenate([cache.astype(BF16), new.astype(BF16)], axis=1),
                                     ((0, 0), (0, lp_s - l_s), (0, 0)))
    ob_s = _dsa(z_qim, n_p, bs, ts, iw[ss],
                cat(cache_dsa_k[0].reshape(bs, past, D_KV), grp(k_new, ss, bs, ts)),
                cat(cache_dsa_v[0].reshape(bs, past, D_KV), grp(v_new, ss, bs, ts)),
                cat(cache_idx_k[0], grp(ik_new, ss, bs, ts)), rel_bias,
                Q_BLOCK if ts % Q_BLOCK == 0 else ts, past, l_s)

    om_p = _mem_attention(z_qim, 0, bp, tp, mk_p.astype(BF16), mv_p.astype(BF16), 256)
    om_s = _mem_attention(z_qim, n_p, bs, ts, cache_mem_k[0].reshape(bs, N_MEM, D_M).astype(BF16),
                          cache_mem_v[0].reshape(bs, N_MEM, D_M).astype(BF16), ts)

    mixed = _mix(ya, (g_p, g_s), (ob_p, ob_s), (om_p, om_s), z_g, p_a[0].astype(BF16), p_b[0].astype(BF16),
                 p_m[0].astype(BF16), 512, 1024)
    h, u2 = _wo_residual_norm(mixed, w_o[0].astype(BF16), xp, xs, g_ffn[0], 256)

    w_route = _pad_cols(jnp.concatenate([w_grp[0], w_rt[0]], axis=1), LANES).astype(BF16)
    logits = _matmul(u2, w_route, F32, tm, LANES, "moe_router")
    perm, inv, gate_sorted, tile_group, n_used = _route(logits, b_grp[0], b_rt[0])
    moe_sorted = _moe_grouped(u2[perm], gate_sorted, tile_group, n_used,
                              w1[0].astype(BF16), w3[0].astype(BF16), w2[0].astype(BF16))
    y_p, y_s = _final_norm(h, moe_sorted[inv], g_final, n_p, 512)

    st = lambda t, b, tlen, shape: t.reshape((1, b, tlen) + shape)
    return (y_p.reshape(bp, tp, D_MODEL), y_s.reshape(bs, ts, D_MODEL),
            st(k_new[sp], bp, tp, (N_KV_B, HD_B)), st(v_new[sp], bp, tp, (N_KV_B, HD_B)), st(ik_new[sp], bp, tp, (D_I,)),
            shift_p[None], wkv_p[None],
            mk_p.reshape(1, bp, N_MEM, H_M, HD_M), mv_p.reshape(1, bp, N_MEM, H_M, HD_M),
            st(k_new[ss], bs, ts, (N_KV_B, HD_B)), st(v_new[ss], bs, ts, (N_KV_B, HD_B)), st(ik_new[ss], bs, ts, (D_I,)),
            shift_s[None], wkv_s[None])
```

```python
import functools
import math

import jax
import jax.numpy as jnp
from jax import lax
from jax.experimental import pallas as pl
from jax.experimental.pallas import tpu as pltpu

F32 = jnp.float32
BF16 = jnp.bfloat16

LANES = 128
D_MODEL = 2048
CHUNK = 64
NORM_EPS = 1e-6
H_A, HD_A = 16, 64
D_A = H_A * HD_A
W_LORA, A_LORA, G_LORA = 64, 64, 160
LNX_EPS = 64e-5
A_COLS = 3 * D_A + W_LORA + A_LORA + G_LORA
H_B, N_KV_B, HD_B = 8, 2, 128
G_B = H_B // N_KV_B
D_B = H_B * HD_B
D_KV = N_KV_B * HD_B
H_I, D_I = 16, 64
TOPK_MAX = 256
Q_BLOCK = 128
REL_BUCKETS = 32
REL_MAX_DIST = 128
N_MEM, H_M, HD_M = 256, 4, 256
D_M = H_M * HD_M
N_BRANCH = 3
IN_SIZES = (A_COLS, D_B, D_KV, D_KV, H_I * D_I, D_I, H_I, D_M, N_BRANCH * D_MODEL)
N_GROUPS, EXPERTS_PER_GROUP = 4, 4
N_EXPERTS = N_GROUPS * EXPERTS_PER_GROUP
TOP_K_INNER = 2
D_EXPERT = 512

WL_PAD, AL_PAD, GL_PAD = 128, 128, 256
A_PAD = 3 * D_A + WL_PAD + AL_PAD + GL_PAD
KVI_PAD = 2 * D_KV + LANES
KEY_TILE = 2 * LANES
SEQ_PER_TILE = LANES // H_A
MOE_TM = 512
INT_MIN = -(2 ** 31)
VMEM_LIMIT = 48 * 1024 * 1024


def _params(*sem):
    return pltpu.CompilerParams(dimension_semantics=sem, vmem_limit_bytes=VMEM_LIMIT)


def _two_part_specs(tm, width, na):
    return (pl.BlockSpec((tm, width), lambda i: (jnp.minimum(i, na - 1), 0)),
            pl.BlockSpec((tm, width), lambda i: (jnp.maximum(i - na, 0), 0)))


def _pick(i, na, a_ref, b_ref):
    return jnp.where(i < na, a_ref[...], b_ref[...])


def _norm2_kernel(xa_ref, xb_ref, g_ref, o_ref, *, na):
    x = _pick(pl.program_id(0), na, xa_ref, xb_ref)
    y = x * lax.rsqrt(jnp.mean(x * x, axis=-1, keepdims=True) + NORM_EPS)
    o_ref[...] = (y * g_ref[...]).astype(o_ref.dtype)


def _rmsnorm_bf16(xa, xb, g, tm):
    d = xa.shape[1]
    na, nb = xa.shape[0] // tm, xb.shape[0] // tm
    sa, sb = _two_part_specs(tm, d, na)
    return pl.pallas_call(
        functools.partial(_norm2_kernel, na=na),
        grid=(na + nb,),
        in_specs=[sa, sb, pl.BlockSpec((1, d), lambda i: (0, 0))],
        out_specs=pl.BlockSpec((tm, d), lambda i: (i, 0)),
        out_shape=jax.ShapeDtypeStruct(((na + nb) * tm, d), BF16),
        compiler_params=_params("parallel"),
        name="rmsnorm_bf16",
    )(xa, xb, g.reshape(1, d))


def _mm_kernel(a_ref, w_ref, o_ref):
    o_ref[...] = jnp.dot(a_ref[...], w_ref[...], preferred_element_type=F32).astype(o_ref.dtype)


def _matmul(a, w, out_dtype, tm, tn, name):
    n, k = a.shape
    m = w.shape[1]
    return pl.pallas_call(
        _mm_kernel,
        grid=(n // tm, m // tn),
        in_specs=[pl.BlockSpec((tm, k), lambda i, j: (i, 0)), pl.BlockSpec((k, tn), lambda i, j: (0, j))],
        out_specs=pl.BlockSpec((tm, tn), lambda i, j: (i, j)),
        out_shape=jax.ShapeDtypeStruct((n, m), out_dtype),
        compiler_params=_params("parallel", "parallel"),
        name=name,
    )(a, w)


def _softplus(x):
    return jnp.maximum(x, 0.0) + jnp.log1p(jnp.exp(-jnp.abs(x)))


def _sigmoid(x):
    return 1.0 / (1.0 + jnp.exp(-x))


def _rwkv_prep_kernel(z_ref, sh_ref, mu_ref, w0_ref, wdec_ref, a0_ref, wa_ref, wg_ref, kk_ref, ka_ref,
                      r_out, w_out, k_out, v_out, kk_out, a_out, g_out, carry_ref):
    tb = z_ref.shape[0]

    @pl.when(pl.program_id(1) == 0)
    def _():
        carry_ref[...] = sh_ref[...]

    z = z_ref[...]
    row = lax.broadcasted_iota(jnp.int32, (tb, 1), 0)
    prev = jnp.where(row == 0, carry_ref[...], pltpu.roll(z, 1, axis=0))
    carry_ref[...] = z[tb - 1:tb, :]
    zm = z + (prev - z) * mu_ref[...]
    r = zm[:, 0:D_A]
    k = zm[:, D_A:2 * D_A]
    v = zm[:, 2 * D_A:3 * D_A]
    o = 3 * D_A
    wl = zm[:, o:o + WL_PAD]
    al = zm[:, o + WL_PAD:o + WL_PAD + AL_PAD]
    gl = zm[:, o + WL_PAD + AL_PAD:]
    lw = jnp.dot(jnp.tanh(wl).astype(BF16), wdec_ref[...], preferred_element_type=F32)
    wv = -_softplus(-(w0_ref[...] + lw)) - 0.5
    a = _sigmoid(a0_ref[...] + jnp.dot(al.astype(BF16), wa_ref[...], preferred_element_type=F32))
    r_out[...] = r
    w_out[...] = jnp.exp(-jnp.exp(wv))
    k_out[...] = k * (1.0 + (a - 1.0) * ka_ref[...])
    v_out[...] = v
    kk_out[...] = k * kk_ref[...]
    a_out[...] = a
    g_out[...] = jnp.dot(_sigmoid(gl).astype(BF16), wg_ref[...], preferred_element_type=F32)


def _rwkv_prep(z, row0, nseq, t, shift0, wts, tb):
    nt = t // tb
    blk0 = row0 // tb
    row = lambda c: pl.BlockSpec((1, c), lambda s, i: (0, 0))
    full = lambda a, b: pl.BlockSpec((a, b), lambda s, i: (0, 0))
    out_spec = pl.BlockSpec((None, tb, D_A), lambda s, i: (s, i, 0))
    out_sds = jax.ShapeDtypeStruct((nseq, t, D_A), F32)
    g_spec = pl.BlockSpec((tb, D_A), lambda s, i: (s * nt + i, 0))
    g_sds = jax.ShapeDtypeStruct((nseq * t, D_A), F32)
    return pl.pallas_call(
        _rwkv_prep_kernel,
        grid=(nseq, nt),
        in_specs=[pl.BlockSpec((tb, A_PAD), lambda s, i: (blk0 + s * nt + i, 0)),
                  pl.BlockSpec((None, 1, A_PAD), lambda s, i: (s, 0, 0)),
                  row(A_PAD), row(D_A), full(WL_PAD, D_A), row(D_A), full(AL_PAD, D_A), full(GL_PAD, D_A),
                  row(D_A), row(D_A)],
        out_specs=[out_spec] * 6 + [g_spec],
        out_shape=[out_sds] * 6 + [g_sds],
        scratch_shapes=[pltpu.VMEM((1, A_PAD), F32)],
        compiler_params=_params("arbitrary", "arbitrary"),
        name="rwkv_prep",
    )(z, shift0, wts["mu"], wts["w0"], wts["w_dec"], wts["a0"], wts["w_a"], wts["w_g"], wts["k_k"], wts["k_a"])


def _wkv_kernel(r_ref, w_ref, k_ref, v_ref, kk_ref, a_ref, s0_ref, lw_ref, lb_ref, rk_ref,
                y_ref, st_ref, s_ref, ab_ref, tm_ref, ytm_ref, *, n_tblocks):
    tt = r_ref.shape[1]
    low = lax.broadcasted_iota(jnp.int32, (SEQ_PER_TILE, LANES), 1) < HD_A

    @pl.when(pl.program_id(1) == 0)
    def _():
        s_ref[...] = s0_ref[...]

    def to_tiles(j, carry):
        for ai, ref in enumerate((r_ref, w_ref, k_ref, v_ref, kk_ref, a_ref)):
            x0 = ref[:, 2 * j, :]
            x1 = ref[:, 2 * j + 1, :]
            rows = []
            for h in range(H_A):
                sl = slice((h // 2) * LANES, (h // 2 + 1) * LANES)
                if h % 2 == 0:
                    rows.append(jnp.where(low, x0[:, sl], pltpu.roll(x1[:, sl], HD_A, axis=1)))
                else:
                    rows.append(jnp.where(low, pltpu.roll(x0[:, sl], HD_A, axis=1), x1[:, sl]))
            m = jnp.concatenate(rows, axis=0).T
            tm_ref[ai, 2 * j] = m[:HD_A]
            tm_ref[ai, 2 * j + 1] = m[HD_A:]
        return carry

    lax.fori_loop(0, tt // 2, to_tiles, 0)

    def step(t, carry):
        kk = tm_ref[4, t]
        ss = jnp.sum(kk * kk, axis=0, keepdims=True)
        kkn = kk * lax.rsqrt(jnp.maximum(ss, 1e-24))
        ab_ref[0] = -kkn
        ab_ref[1] = kkn * tm_ref[5, t]
        vv = tm_ref[3, t]
        sa = jnp.zeros((HD_A, LANES), F32)
        for k in range(HD_A):
            sa = sa + s_ref[k] * ab_ref[0, k:k + 1, :]
        y = jnp.zeros((HD_A, LANES), F32)
        for k in range(HD_A):
            s_new = (s_ref[k] * tm_ref[1, t, k:k + 1, :] + sa * ab_ref[1, k:k + 1, :]
                     + vv * tm_ref[2, t, k:k + 1, :])
            s_ref[k] = s_new
            y = y + s_new * tm_ref[0, t, k:k + 1, :]
        mean = jnp.mean(y, axis=0, keepdims=True)
        d = y - mean
        var = jnp.mean(d * d, axis=0, keepdims=True)
        yn = d * lax.rsqrt(var + LNX_EPS) * lw_ref[...] + lb_ref[...]
        bonus = jnp.sum(tm_ref[0, t] * tm_ref[2, t] * rk_ref[...], axis=0, keepdims=True) * vv
        ytm_ref[t] = yn + bonus
        return carry

    lax.fori_loop(0, tt, step, 0)

    for j in range(tt // 2):
        m = jnp.concatenate([ytm_ref[2 * j], ytm_ref[2 * j + 1]], axis=0).T
        for hp in range(H_A // 2):
            even = m[(2 * hp) * SEQ_PER_TILE:(2 * hp + 1) * SEQ_PER_TILE]
            odd = m[(2 * hp + 1) * SEQ_PER_TILE:(2 * hp + 2) * SEQ_PER_TILE]
            cols = slice(hp * LANES, (hp + 1) * LANES)
            y_ref[:, 2 * j, cols] = jnp.where(low, even, pltpu.roll(odd, HD_A, axis=1))
            y_ref[:, 2 * j + 1, cols] = jnp.where(low, pltpu.roll(even, HD_A, axis=1), odd)

    @pl.when(pl.program_id(1) == n_tblocks - 1)
    def _():
        st_ref[...] = s_ref[...]


def _wkv(r, w, k, v, kk, a, s0, lw, lb, rk, tt):
    nseq, t, _ = r.shape
    p = nseq * H_A
    seq = pl.BlockSpec((SEQ_PER_TILE, tt, D_A), lambda g, i: (g, i, 0))
    vec = pl.BlockSpec((HD_A, LANES), lambda g, i: (0, 0))
    st = pl.BlockSpec((HD_A, HD_A, LANES), lambda g, i: (0, 0, g))
    return pl.pallas_call(
        functools.partial(_wkv_kernel, n_tblocks=t // tt),
        grid=(p // LANES, t // tt),
        in_specs=[seq] * 6 + [st, vec, vec, vec],
        out_specs=[seq, st],
        out_shape=[jax.ShapeDtypeStruct((nseq, t, D_A), F32), jax.ShapeDtypeStruct((HD_A, HD_A, p), F32)],
        scratch_shapes=[pltpu.VMEM((HD_A, HD_A, LANES), F32), pltpu.VMEM((2, HD_A, LANES), F32),
                        pltpu.VMEM((6, tt, HD_A, LANES), F32), pltpu.VMEM((tt, HD_A, LANES), F32)],
        compiler_params=_params("arbitrary", "arbitrary"),
        name="wkv_recurrence",
    )(r, w, k, v, kk, a, s0, lw, lb, rk)


def _state_to_tiles(wkv):
    nseq = wkv.shape[0]
    x = wkv.reshape(nseq // SEQ_PER_TILE, SEQ_PER_TILE, H_A, HD_A, HD_A)
    return x.transpose(4, 3, 0, 2, 1).reshape(HD_A, HD_A, nseq * H_A)


def _state_from_tiles(s, nseq):
    x = s.reshape(HD_A, HD_A, nseq // SEQ_PER_TILE, H_A, SEQ_PER_TILE)
    return x.transpose(2, 4, 3, 1, 0).reshape(nseq, H_A, HD_A, HD_A)


def _lane_tile(vec):
    return jnp.repeat(vec.reshape(H_A, HD_A).T, SEQ_PER_TILE, axis=1)


def _dsa_kernel(q_ref, iq_ref, iw_ref, k_ref, v_ref, ik2_ref, d0_ref, d1_ref, o_ref,
                score_ref, madd_ref, logit_ref, iwb_ref, key_ref, mx_ref, ls_ref, acc_ref,
                *, q_start, l_valid, topk):
    tq = q_ref.shape[0]
    lp = k_ref.shape[0]
    q0 = q_start + pl.program_id(1) * tq
    nt = (((1,), (1,)), ((), ()))
    lane_shift = int(math.log2(LANES))
    chunk_shift = int(math.log2(CHUNK))
    dc = lax.shift_right_logical(q0, lane_shift)
    n_lane_tiles = jnp.minimum(lax.shift_right_logical(q0 + tq - 1, lane_shift) + 1, lp // LANES)
    n_tiles = lax.shift_right_logical(n_lane_tiles + 1, 1)

    def tile_off(c, width):
        return pl.multiple_of(c * width, width)

    iw = iw_ref[...] * ((H_I * D_I) ** -0.5)
    for h in range(H_I):
        iwb_ref[h] = jnp.broadcast_to(iw[:, h:h + 1], (tq, LANES))

    def score_tile(c, carry):
        kab = ik2_ref[c]
        acc_a = jnp.zeros((tq, LANES), F32)
        acc_b = jnp.zeros((tq, LANES), F32)
        for hp in range(H_I // 2):
            d = lax.dot_general(iq_ref[:, hp * LANES:(hp + 1) * LANES], kab, nt, preferred_element_type=F32)
            acc_a = acc_a + jnp.maximum(d[:, :LANES], 0.0) * iwb_ref[2 * hp]
            acc_b = acc_b + jnp.maximum(d[:, LANES:], 0.0) * iwb_ref[2 * hp + 1]
        score_ref[:, pl.ds(tile_off(c, LANES), LANES)] = acc_a + acc_b
        return carry

    lax.fori_loop(0, 2 * n_tiles, score_tile, 0)

    qchunk = lax.shift_right_arithmetic(q0 + lax.broadcasted_iota(jnp.int32, (tq, 1), 0), chunk_shift)

    def visible(lo, width):
        kpos = lo + lax.broadcasted_iota(jnp.int32, (1, width), 1)
        return (lax.shift_right_arithmetic(kpos, chunk_shift) <= qchunk) & (kpos < l_valid)

    def key_tile(c, carry):
        off = tile_off(c, KEY_TILE)
        bits = pltpu.bitcast(score_ref[:, pl.ds(off, KEY_TILE)] + 0.0, jnp.int32)
        key = bits ^ (lax.shift_right_arithmetic(bits, 31) & jnp.int32(0x7FFFFFFF))
        key_ref[:, pl.ds(off, KEY_TILE)] = jnp.where(visible(off, KEY_TILE), key, jnp.int32(INT_MIN))
        return carry

    lax.fori_loop(0, n_tiles, key_tile, 0)
    kf = jnp.float32(topk)

    def count(pred):
        def body(c, acc):
            m = jnp.where(pred(key_ref[:, pl.ds(tile_off(c, KEY_TILE), KEY_TILE)]), 1.0, 0.0)
            return acc + m[:, :LANES] + m[:, LANES:]
        acc = lax.fori_loop(0, n_tiles, body, jnp.zeros((tq, LANES), F32))
        return jnp.sum(acc, axis=1, keepdims=True)

    def search(i, thr):
        cand = thr ^ lax.shift_left(jnp.int32(1), 31 - i)
        return jnp.where(count(lambda kc: kc >= cand) >= kf, cand, thr)

    thr = lax.fori_loop(0, 32, search, jnp.full((tq, 1), INT_MIN, jnp.int32))

    surplus = (count(lambda kc: kc >= thr) > kf) & (thr != jnp.int32(INT_MIN))
    any_surplus = jnp.max(jnp.where(surplus, 1.0, 0.0))

    @pl.when(any_surplus == 0.0)
    def _():
        def select_tile(c, carry):
            off = tile_off(c, KEY_TILE)
            sel = visible(off, KEY_TILE) & (key_ref[:, pl.ds(off, KEY_TILE)] >= thr)
            madd_ref[:, pl.ds(off, KEY_TILE)] = jnp.where(sel, 0.0, -jnp.inf)
            return carry

        lax.fori_loop(0, n_tiles, select_tile, 0)

    @pl.when(any_surplus > 0.0)
    def _():
        need = kf - count(lambda kc: kc > thr)
        tri = jnp.where(lax.broadcasted_iota(jnp.int32, (LANES, LANES), 0)
                        <= lax.broadcasted_iota(jnp.int32, (LANES, LANES), 1), 1.0, 0.0).astype(BF16)

        def select_tile(c, run):
            off = tile_off(c, LANES)
            keyc = key_ref[:, pl.ds(off, LANES)]
            eqf = jnp.where(keyc == thr, 1.0, 0.0)
            within = jnp.dot(eqf.astype(BF16), tri, preferred_element_type=F32)
            take = jnp.where(keyc > thr, 1.0, jnp.where(run + within <= need, eqf, 0.0))
            sel = jnp.where(visible(off, LANES), take, 0.0)
            madd_ref[:, pl.ds(off, LANES)] = jnp.where(sel > 0.5, 0.0, -jnp.inf)
            return run + within[:, LANES - 1:LANES]

        lax.fori_loop(0, 2 * n_tiles, select_tile, jnp.zeros((tq, 1), F32))

    off0 = tile_off(dc, LANES)
    off1 = tile_off(jnp.maximum(dc - 1, 0), LANES)
    for g in range(N_KV_B):
        ksl = slice(g * HD_B, (g + 1) * HD_B)
        qs = jnp.concatenate([q_ref[:, (g * G_B + j) * HD_B:(g * G_B + j + 1) * HD_B] for j in range(G_B)], axis=0)

        def logits_tile(c, carry):
            off = tile_off(c, KEY_TILE)
            s = lax.dot_general(qs, k_ref[pl.ds(off, KEY_TILE), ksl], nt, preferred_element_type=F32)
            md = madd_ref[:, pl.ds(off, KEY_TILE)]
            for j in range(G_B):
                logit_ref[j * tq:(j + 1) * tq, pl.ds(off, KEY_TILE)] = s[j * tq:(j + 1) * tq] * (HD_B ** -0.5) + md
            return carry

        lax.fori_loop(0, n_tiles, logits_tile, 0)
        for j in range(G_B):
            logit_ref[j * tq:(j + 1) * tq, pl.ds(off0, LANES)] += d0_ref[g * G_B + j]

        @pl.when(dc > 0)
        def _():
            for j in range(G_B):
                logit_ref[j * tq:(j + 1) * tq, pl.ds(off1, LANES)] += d1_ref[g * G_B + j]

        mx_ref[...] = jnp.full(mx_ref.shape, -jnp.inf, F32)

        def max_tile(c, carry):
            lg = logit_ref[:, pl.ds(tile_off(c, KEY_TILE), KEY_TILE)]
            mx_ref[...] = jnp.maximum(mx_ref[...], jnp.maximum(lg[:, :LANES], lg[:, LANES:]))
            return carry

        lax.fori_loop(0, n_tiles, max_tile, 0)
        m = jnp.max(mx_ref[...], axis=1, keepdims=True)
        ls_ref[...] = jnp.zeros(ls_ref.shape, F32)
        acc_ref[...] = jnp.zeros(acc_ref.shape, F32)

        def pv_tile(c, carry):
            off = tile_off(c, KEY_TILE)
            p = jnp.exp(logit_ref[:, pl.ds(off, KEY_TILE)] - m)
            ls_ref[...] += p[:, :LANES] + p[:, LANES:]
            acc_ref[...] += jnp.dot(p.astype(BF16), v_ref[pl.ds(off, KEY_TILE), ksl], preferred_element_type=F32)
            return carry

        lax.fori_loop(0, n_tiles, pv_tile, 0)
        res = acc_ref[...] / jnp.sum(ls_ref[...], axis=1, keepdims=True)
        for j in range(G_B):
            h = g * G_B + j
            o_ref[:, h * HD_B:(h + 1) * HD_B] = res[j * tq:(j + 1) * tq].astype(o_ref.dtype)


def _t5_bucket(rel):
    half = REL_BUCKETS // 2
    exact = half // 2
    side = jnp.where(rel > 0, half, 0)
    n = jnp.abs(rel)
    nf = jnp.maximum(n, 1).astype(F32)
    large = exact + (jnp.log(nf / exact) / math.log(REL_MAX_DIST / exact) * (half - exact)).astype(jnp.int32)
    large = jnp.minimum(large, half - 1)
    return side + jnp.where(n < exact, n, large)


def _dsa(zq, row0, b, t, iw, k_all, v_all, ik_all, rel_bias, tq, q_start, l_valid):
    lp = k_all.shape[1]
    topk = min(TOPK_MAX, l_valid // 4)
    assert q_start % LANES == 0 and (tq == LANES or t == tq) and lp % KEY_TILE == 0 and row0 % tq == 0
    nq = t // tq
    blk0 = row0 // tq
    ikt = ik_all.reshape(b, lp // LANES, LANES, D_I)
    zeros = jnp.zeros_like(ikt)
    ik2 = jnp.concatenate([jnp.concatenate([ikt, zeros], axis=-1), jnp.concatenate([zeros, ikt], axis=-1)], axis=2)
    ji = jnp.arange(LANES, dtype=jnp.int32)[None, :] - jnp.arange(tq, dtype=jnp.int32)[:, None]
    def bias_of(rel):
        hit = _t5_bucket(rel)[None, None] == jnp.arange(REL_BUCKETS, dtype=jnp.int32).reshape((1, -1) + (1,) * rel.ndim)
        return jnp.sum(jnp.where(hit, rel_bias.T.reshape((H_B, REL_BUCKETS) + (1,) * rel.ndim), 0.0), axis=1)

    far = bias_of(jnp.full((1, 1), -2 * LANES, jnp.int32))
    d0 = bias_of(ji) - far
    d1 = bias_of(ji - LANES) - far
    kern = functools.partial(_dsa_kernel, q_start=q_start, l_valid=l_valid, topk=topk)
    qspec = lambda col: pl.BlockSpec((tq, D_B), lambda bi, i: (blk0 + bi * nq + i, col))
    kspec = lambda w: pl.BlockSpec((None, lp, w), lambda bi, i: (bi, 0, 0))
    cspec = pl.BlockSpec((H_B, tq, LANES), lambda bi, i: (0, 0, 0))
    rows = G_B * tq
    return pl.pallas_call(
        kern,
        grid=(b, nq),
        in_specs=[qspec(0), qspec(1), pl.BlockSpec((tq, H_I), lambda bi, i: (bi * nq + i, 0)),
                  kspec(D_KV), kspec(D_KV),
                  pl.BlockSpec((None, lp // LANES, 2 * LANES, LANES), lambda bi, i: (bi, 0, 0, 0)), cspec, cspec],
        out_specs=pl.BlockSpec((tq, D_B), lambda bi, i: (bi * nq + i, 0)),
        out_shape=jax.ShapeDtypeStruct((b * t, D_B), BF16),
        scratch_shapes=[pltpu.VMEM((tq, lp), F32), pltpu.VMEM((tq, lp), F32), pltpu.VMEM((rows, lp), F32),
                        pltpu.VMEM((H_I, tq, LANES), F32), pltpu.VMEM((tq, lp), jnp.int32),
                        pltpu.VMEM((rows, LANES), F32), pltpu.VMEM((rows, LANES), F32),
                        pltpu.VMEM((rows, HD_B), F32)],
        compiler_params=_params("parallel", "arbitrary"),
        name="dsa_attention",
    )(zq, zq, iw, k_all, v_all, ik2, d0, d1)


def _mem_kernel(q_ref, k_ref, v_ref, o_ref):
    nt = (((1,), (1,)), ((), ()))
    for h in range(H_M):
        sl = slice(h * HD_M, (h + 1) * HD_M)
        s = lax.dot_general(q_ref[:, sl], k_ref[:, sl], nt, preferred_element_type=F32) * (HD_M ** -0.5)
        m = jnp.max(s, axis=1, keepdims=True)
        p = jnp.exp(s - m)
        den = jnp.sum(p, axis=1, keepdims=True)
        o = jnp.dot(p.astype(BF16), v_ref[:, sl], preferred_element_type=F32)
        o_ref[:, sl] = (o / den).astype(o_ref.dtype)


def _mem_attention(zq, row0, b, t, mk, mv, tq):
    nq = t // tq
    blk0 = row0 // tq
    return pl.pallas_call(
        _mem_kernel,
        grid=(b, nq),
        in_specs=[pl.BlockSpec((tq, D_M), lambda bi, i: (blk0 + bi * nq + i, 2)),
                  pl.BlockSpec((None, N_MEM, D_M), lambda bi, i: (bi, 0, 0)),
                  pl.BlockSpec((None, N_MEM, D_M), lambda bi, i: (bi, 0, 0))],
        out_specs=pl.BlockSpec((tq, D_M), lambda bi, i: (bi * nq + i, 0)),
        out_shape=jax.ShapeDtypeStruct((b * t, D_M), BF16),
        compiler_params=_params("parallel", "parallel"),
        name="memory_attention",
    )(zq, mk, mv)


def _mix_kernel(yap_ref, yas_ref, gp_ref, gs_ref, obp_ref, obs_ref, omp_ref, oms_ref, ga_ref, gb_ref, gm_ref,
                pa_ref, pb_ref, pm_ref, o_ref, *, na):
    i = pl.program_id(1)
    oa = (_pick(i, na, yap_ref, yas_ref) * _pick(i, na, gp_ref, gs_ref)).astype(BF16)
    acc = _sigmoid(ga_ref[...]) * jnp.dot(oa, pa_ref[...], preferred_element_type=F32)
    acc = acc + _sigmoid(gb_ref[...]) * jnp.dot(_pick(i, na, obp_ref, obs_ref), pb_ref[...],
                                                preferred_element_type=F32)
    acc = acc + _sigmoid(gm_ref[...]) * jnp.dot(_pick(i, na, omp_ref, oms_ref), pm_ref[...],
                                                preferred_element_type=F32)
    o_ref[...] = acc.astype(o_ref.dtype)


def _mix(ya2, g2, ob2, om2, zg, pa, pb, pm, tm, tn):
    n = zg.shape[0]
    nj = D_MODEL // tn
    na = g2[0].shape[0] // tm
    two = lambda w: (pl.BlockSpec((tm, w), lambda j, i: (jnp.minimum(i, na - 1), 0)),
                     pl.BlockSpec((tm, w), lambda j, i: (jnp.maximum(i - na, 0), 0)))
    gate = lambda br: pl.BlockSpec((tm, tn), lambda j, i, br=br: (i, br * nj + j))
    wt = pl.BlockSpec((D_A, tn), lambda j, i: (0, j))
    return pl.pallas_call(
        functools.partial(_mix_kernel, na=na),
        grid=(nj, n // tm),
        in_specs=[*two(D_A), *two(D_A), *two(D_B), *two(D_M), gate(0), gate(1), gate(2), wt, wt, wt],
        out_specs=pl.BlockSpec((tm, tn), lambda j, i: (i, j)),
        out_shape=jax.ShapeDtypeStruct((n, D_MODEL), BF16),
        compiler_params=_params("parallel", "parallel"),
        name="branch_mix",
    )(*ya2, *g2, *ob2, *om2, zg, zg, zg, pa, pb, pm)


def _wo_kernel(m_ref, w_ref, xa_ref, xb_ref, g_ref, h_ref, u_ref, *, na):
    x = _pick(pl.program_id(0), na, xa_ref, xb_ref)
    h = x + jnp.dot(m_ref[...], w_ref[...], preferred_element_type=F32)
    h_ref[...] = h
    y = h * lax.rsqrt(jnp.mean(h * h, axis=-1, keepdims=True) + NORM_EPS)
    u_ref[...] = (y * g_ref[...]).astype(u_ref.dtype)


def _wo_residual_norm(mixed, w_o, xa, xb, g, tm):
    n = mixed.shape[0]
    na = xa.shape[0] // tm
    tile = pl.BlockSpec((tm, D_MODEL), lambda i: (i, 0))
    sa, sb = _two_part_specs(tm, D_MODEL, na)
    return pl.pallas_call(
        functools.partial(_wo_kernel, na=na),
        grid=(n // tm,),
        in_specs=[tile, pl.BlockSpec((D_MODEL, D_MODEL), lambda i: (0, 0)), sa, sb,
                  pl.BlockSpec((1, D_MODEL), lambda i: (0, 0))],
        out_specs=[tile, tile],
        out_shape=[jax.ShapeDtypeStruct((n, D_MODEL), F32), jax.ShapeDtypeStruct((n, D_MODEL), BF16)],
        compiler_params=_params("parallel"),
        name="wo_residual_norm",
    )(mixed, w_o, xa, xb, g.reshape(1, D_MODEL))


def _moe_kernel(tg_ref, nu_ref, x_ref, gate_ref, w1_ref, w3_ref, w2_ref, o_ref):
    i, j = pl.program_id(0), pl.program_id(1)

    @pl.when((i >= nu_ref[0]) & (j == 0))
    def _():
        o_ref[...] = jnp.zeros(o_ref.shape, F32)

    @pl.when(i < nu_ref[0])
    def _():
        x = x_ref[...]
        a = jnp.dot(x, w1_ref[...], preferred_element_type=F32)
        b = jnp.dot(x, w3_ref[...], preferred_element_type=F32)
        lane = lax.broadcasted_iota(jnp.int32, gate_ref.shape, 1)
        ge = jnp.sum(jnp.where(lane == j, gate_ref[...], 0.0), axis=1, keepdims=True)
        hid = (a * _sigmoid(a)) * b * ge
        out = jnp.dot(hid.astype(BF16), w2_ref[...], preferred_element_type=F32)

        @pl.when(j == 0)
        def _():
            o_ref[...] = out

        @pl.when(j > 0)
        def _():
            o_ref[...] += out


def _moe_grouped(x_sorted, gate_sorted, tile_group, n_used, w1, w3, w2):
    npad = x_sorted.shape[0]
    ew = lambda i, j, tg, nu: (tg[i] * EXPERTS_PER_GROUP + j, 0, 0)
    return pl.pallas_call(
        _moe_kernel,
        grid_spec=pltpu.PrefetchScalarGridSpec(
            num_scalar_prefetch=2,
            grid=(npad // MOE_TM, EXPERTS_PER_GROUP),
            in_specs=[pl.BlockSpec((MOE_TM, D_MODEL), lambda i, j, tg, nu: (i, 0)),
                      pl.BlockSpec((MOE_TM, EXPERTS_PER_GROUP), lambda i, j, tg, nu: (i, 0)),
                      pl.BlockSpec((None, D_MODEL, D_EXPERT), ew), pl.BlockSpec((None, D_MODEL, D_EXPERT), ew),
                      pl.BlockSpec((None, D_EXPERT, D_MODEL), ew)],
            out_specs=pl.BlockSpec((MOE_TM, D_MODEL), lambda i, j, tg, nu: (i, 0))),
        out_shape=jax.ShapeDtypeStruct((npad, D_MODEL), F32),
        compiler_params=_params("parallel", "arbitrary"),
        name="moe_grouped",
    )(tile_group, n_used, x_sorted, gate_sorted, w1, w3, w2)


def _route(logits, b_grp, b_rt):
    n = logits.shape[0]
    g_logits = logits[:, :N_GROUPS] + b_grp
    e_logits = (logits[:, N_GROUPS:N_GROUPS + N_EXPERTS] + b_rt).reshape(n, N_GROUPS, EXPERTS_PER_GROUP)
    g_idx = jnp.argmax(g_logits, axis=-1).astype(jnp.int32)
    g_w = jnp.max(jax.nn.softmax(g_logits, axis=-1), axis=-1, keepdims=True)
    onehot_g = g_idx[:, None] == jnp.arange(N_GROUPS, dtype=jnp.int32)[None, :]
    e_in = jnp.sum(jnp.where(onehot_g[:, :, None], e_logits, 0.0), axis=1)
    lane = jnp.arange(EXPERTS_PER_GROUP, dtype=jnp.int32)[None, :]
    i1 = jnp.argmax(e_in, axis=-1)[:, None]
    v1 = jnp.max(e_in, axis=-1, keepdims=True)
    rest = jnp.where(lane == i1, -jnp.inf, e_in)
    i2 = jnp.argmax(rest, axis=-1)[:, None]
    v2 = jnp.max(rest, axis=-1, keepdims=True)
    w12 = g_w * jax.nn.softmax(jnp.concatenate([v1, v2], axis=-1), axis=-1)
    gate4 = jnp.where(lane == i1, w12[:, 0:1], 0.0) + jnp.where(lane == i2, w12[:, 1:2], 0.0)

    counts = jnp.sum(onehot_g.astype(jnp.int32), axis=0)
    tiles = (counts + MOE_TM - 1) // MOE_TM
    tile_end = jnp.cumsum(tiles)
    slot0 = (tile_end - tiles) * MOE_TM
    start = jnp.cumsum(counts) - counts
    order = jnp.argsort(g_idx, stable=True).astype(jnp.int32)
    rank = jnp.sum(jnp.where(onehot_g, jnp.cumsum(onehot_g.astype(jnp.int32), axis=0) - 1, 0), axis=1)
    inv = jnp.sum(jnp.where(onehot_g, slot0[None, :], 0), axis=1) + rank
    npad = n + N_GROUPS * MOE_TM
    slots = jnp.arange(npad, dtype=jnp.int32)
    slot_group = jnp.minimum(jnp.sum(jnp.where(slots[:, None] >= tile_end[None, :] * MOE_TM, 1, 0), axis=1),
                             N_GROUPS - 1)
    in_group = slot_group[:, None] == jnp.arange(N_GROUPS, dtype=jnp.int32)[None, :]
    pos = slots - jnp.sum(jnp.where(in_group, slot0[None, :], 0), axis=1)
    used = pos < jnp.sum(jnp.where(in_group, counts[None, :], 0), axis=1)
    src = jnp.where(used, pos + jnp.sum(jnp.where(in_group, start[None, :], 0), axis=1), 0)
    perm = order[src]
    gate_sorted = jnp.where(used[:, None], gate4[perm], 0.0)
    tile_group = slot_group[::MOE_TM].astype(jnp.int32)
    return perm, inv, gate_sorted, tile_group, tile_end[-1:].astype(jnp.int32)


def _final_kernel(h_ref, m_ref, g_ref, yp_ref, ys_ref, *, na):
    h = h_ref[...] + m_ref[...]
    y = h * lax.rsqrt(jnp.mean(h * h, axis=-1, keepdims=True) + NORM_EPS) * g_ref[...]
    i = pl.program_id(0)

    @pl.when(i < na)
    def _():
        yp_ref[...] = y

    @pl.when(i >= na)
    def _():
        ys_ref[...] = y


def _final_norm(h, moe, g, n_p, tm):
    n = h.shape[0]
    na = n_p // tm
    tile = pl.BlockSpec((tm, D_MODEL), lambda i: (i, 0))
    sa, sb = _two_part_specs(tm, D_MODEL, na)
    return pl.pallas_call(
        functools.partial(_final_kernel, na=na),
        grid=(n // tm,),
        in_specs=[tile, tile, pl.BlockSpec((1, D_MODEL), lambda i: (0, 0))],
        out_specs=[sa, sb],
        out_shape=[jax.ShapeDtypeStruct((n_p, D_MODEL), F32), jax.ShapeDtypeStruct((n - n_p, D_MODEL), F32)],
        compiler_params=_params("arbitrary"),
        name="residual_final_norm",
    )(h, moe, g.reshape(1, D_MODEL))


def _pad_cols(w, width):
    return jnp.pad(w, ((0, 0), (0, width - w.shape[1])))


def _pad_rows(w, height):
    return jnp.pad(w, ((0, height - w.shape[0]), (0, 0)))


def _pad_lora(x):
    o = 3 * D_A
    pad = lambda t, w: jnp.pad(t, [(0, 0)] * (t.ndim - 1) + [(0, w - t.shape[-1])])
    return jnp.concatenate([x[..., :o], pad(x[..., o:o + W_LORA], WL_PAD),
                            pad(x[..., o + W_LORA:o + W_LORA + A_LORA], AL_PAD),
                            pad(x[..., o + W_LORA + A_LORA:], GL_PAD)], axis=-1)


def _unpad_lora(x):
    o = 3 * D_A
    return jnp.concatenate([x[..., :o], x[..., o:o + W_LORA], x[..., o + WL_PAD:o + WL_PAD + A_LORA],
                            x[..., o + WL_PAD + AL_PAD:o + WL_PAD + AL_PAD + G_LORA]], axis=-1)


def kernel(x_prompt, x_sample, cache_dsa_k, cache_dsa_v, cache_idx_k, state_rwkv_shift, state_rwkv_wkv, cache_mem_k, cache_mem_v, mem_prompt, rel_bias, g_attn, w_in, rwkv_mu, rwkv_w0, rwkv_w_dec, rwkv_a0, rwkv_w_a, rwkv_w_g, rwkv_k_k, rwkv_k_a, rwkv_r_k, rwkv_lnx_w, rwkv_lnx_b, g_mem, w_mem_kv, p_a, p_b, p_m, w_o, g_ffn, w_grp, b_grp, w_rt, b_rt, w1, w3, w2, g_final):
    assert w_in.shape[0] == 1, "single layer"
    bp, tp, _ = x_prompt.shape
    bs, ts, _ = x_sample.shape
    past = cache_dsa_k.shape[2]
    n_p, n_s = bp * tp, bs * ts
    n = n_p + n_s
    tm = 1024
    xp = x_prompt.reshape(n_p, D_MODEL)
    xs = x_sample.reshape(n_s, D_MODEL)

    offs = [0]
    for s in IN_SIZES:
        offs.append(offs[-1] + s)
    seg = lambda i: w_in[0][:, offs[i]:offs[i + 1]]
    w_a_cols = _pad_lora(seg(0)).astype(BF16)
    w_qim = jnp.concatenate([seg(1), seg(4), seg(7)], axis=1).astype(BF16)
    w_kvi = _pad_cols(jnp.concatenate([seg(2), seg(3), seg(5), seg(6)], axis=1), KVI_PAD).astype(BF16)
    w_gate = seg(8).astype(BF16)
    rw = dict(mu=_pad_lora(rwkv_mu[0])[None], w0=rwkv_w0, a0=rwkv_a0, k_k=rwkv_k_k, k_a=rwkv_k_a,
              w_dec=_pad_rows(rwkv_w_dec[0], WL_PAD).astype(BF16), w_a=_pad_rows(rwkv_w_a[0], AL_PAD).astype(BF16),
              w_g=_pad_rows(rwkv_w_g[0], GL_PAD).astype(BF16))

    u = _rmsnorm_bf16(xp, xs, g_attn[0], 512)
    z_a = _matmul(u, w_a_cols, F32, tm, A_PAD // 4, "proj_rwkv")
    z_qim = _matmul(u, w_qim, BF16, tm, 1024, "proj_queries")
    z_kvi = _matmul(u, w_kvi, F32, tm, KVI_PAD, "proj_kv")
    z_g = _matmul(u, w_gate, F32, tm, 1024, "proj_gates")

    mem = mem_prompt.reshape(bp * N_MEM, D_MODEL)
    um = _rmsnorm_bf16(mem[:bp * N_MEM // 2], mem[bp * N_MEM // 2:], g_mem[0], 512)
    mkv = _matmul(um, w_mem_kv[0].astype(BF16), F32, 1024, 1024, "proj_mem_kv")
    mk_p = mkv[:, :D_M].reshape(bp, N_MEM, D_M)
    mv_p = mkv[:, D_M:].reshape(bp, N_MEM, D_M)

    lw_vec, lb_vec, rk_vec = rwkv_lnx_w[0], rwkv_lnx_b[0], rwkv_r_k[0].reshape(D_A)

    def rwkv_group(row0, nseq, t, shift0, wkv0, tb, tt):
        r, w, k, v, kk, a, g = _rwkv_prep(z_a, row0, nseq, t, shift0, rw, tb)
        y, s_fin = _wkv(r, w, k, v, kk, a, _state_to_tiles(wkv0), _lane_tile(lw_vec), _lane_tile(lb_vec),
                        _lane_tile(rk_vec), tt)
        return y.reshape(nseq * t, D_A), g, _state_from_tiles(s_fin, nseq)

    ya_p, g_p, wkv_p = rwkv_group(0, bp, tp, jnp.zeros((bp, 1, A_PAD), F32), jnp.zeros((bp, H_A, HD_A, HD_A), F32),
                                  256, 32)
    ya_s, g_s, wkv_s = rwkv_group(n_p, bs, ts, _pad_lora(state_rwkv_shift[0]), state_rwkv_wkv[0], ts, ts)
    last = lambda row0, b, t: _unpad_lora(z_a[row0 + t - 1:row0 + b * t:t])[:, None, :]
    shift_p, shift_s = last(0, bp, tp), last(n_p, bs, ts)

    k_new, v_new = z_kvi[:, :D_KV], z_kvi[:, D_KV:2 * D_KV]
    ik_new = z_kvi[:, 2 * D_KV:2 * D_KV + D_I]
    iw = z_kvi[:, 2 * D_KV + D_I:2 * D_KV + D_I + H_I]
    grp = lambda t, sl, b, tlen: t[sl].reshape(b, tlen, t.shape[-1])
    sp, ss = slice(0, n_p), slice(n_p, n)
    ob_p = _dsa(z_qim, 0, bp, tp, iw[sp], grp(k_new, sp, bp, tp).astype(BF16), grp(v_new, sp, bp, tp).astype(BF16),
                grp(ik_new, sp, bp, tp).astype(BF16), rel_bias, Q_BLOCK if tp % Q_BLOCK == 0 else tp, 0, tp)
    l_s = past + ts
    lp_s = -(-l_s // KEY_TILE) * KEY_TILE
    cat = lambda cache, new: jnp.pad(jnp.concatenate([cache.astype(BF16), new.astype(BF16)], axis=1),
                                     ((0, 0), (0, lp_s - l_s), (0, 0)))
    ob_s = _dsa(z_qim, n_p, bs, ts, iw[ss],
                cat(cache_dsa_k[0].reshape(bs, past, D_KV), grp(k_new, ss, bs, ts)),
                cat(cache_dsa_v[0].reshape(bs, past, D_KV), grp(v_new, ss, bs, ts)),
                cat(cache_idx_k[0], grp(ik_new, ss, bs, ts)), rel_bias,
                Q_BLOCK if ts % Q_BLOCK == 0 else ts, past, l_s)

    om_p = _mem_attention(z_qim, 0, bp, tp, mk_p.astype(BF16), mv_p.astype(BF16), 256)
    om_s = _mem_attention(z_qim, n_p, bs, ts, cache_mem_k[0].reshape(bs, N_MEM, D_M).astype(BF16),
                          cache_mem_v[0].reshape(bs, N_MEM, D_M).astype(BF16), ts)

    mixed = _mix((ya_p, ya_s), (g_p, g_s), (ob_p, ob_s), (om_p, om_s), z_g, p_a[0].astype(BF16), p_b[0].astype(BF16),
                 p_m[0].astype(BF16), 256, 1024)
    h, u2 = _wo_residual_norm(mixed, w_o[0].astype(BF16), xp, xs, g_ffn[0], 256)

    w_route = _pad_cols(jnp.concatenate([w_grp[0], w_rt[0]], axis=1), LANES).astype(BF16)
    logits = _matmul(u2, w_route, F32, tm, LANES, "moe_router")
    perm, inv, gate_sorted, tile_group, n_used = _route(logits, b_grp[0], b_rt[0])
    moe_sorted = _moe_grouped(u2[perm], gate_sorted, tile_group, n_used,
                              w1[0].astype(BF16), w3[0].astype(BF16), w2[0].astype(BF16))
    y_p, y_s = _final_norm(h, moe_sorted[inv], g_final, n_p, 512)

    st = lambda t, b, tlen, shape: t.reshape((1, b, tlen) + shape)
    return (y_p.reshape(bp, tp, D_MODEL), y_s.reshape(bs, ts, D_MODEL),
            st(k_new[sp], bp, tp, (N_KV_B, HD_B)), st(v_new[sp], bp, tp, (N_KV_B, HD_B)), st(ik_new[sp], bp, tp, (D_I,)),
            shift_p[None], wkv_p[None],
            mk_p.reshape(1, bp, N_MEM, H_M, HD_M), mv_p.reshape(1, bp, N_MEM, H_M, HD_M),
            st(k_new[ss], bs, ts, (N_KV_B, HD_B)), st(v_new[ss], bs, ts, (N_KV_B, HD_B)), st(ik_new[ss], bs, ts, (D_I,)),
            shift_s[None], wkv_s[None])
```

```python
import functools
import math

import jax
import jax.numpy as jnp
from jax import lax
from jax.experimental import pallas as pl
from jax.experimental.pallas import tpu as pltpu

F32 = jnp.float32
BF16 = jnp.bfloat16

LANES = 128
D_MODEL = 2048
CHUNK = 64
NORM_EPS = 1e-6
H_A, HD_A = 16, 64
D_A = H_A * HD_A
W_LORA, A_LORA, G_LORA = 64, 64, 160
LNX_EPS = 64e-5
A_COLS = 3 * D_A + W_LORA + A_LORA + G_LORA
H_B, N_KV_B, HD_B = 8, 2, 128
G_B = H_B // N_KV_B
D_B = H_B * HD_B
D_KV = N_KV_B * HD_B
H_I, D_I = 16, 64
TOPK_MAX = 256
Q_BLOCK = 128
REL_BUCKETS = 32
REL_MAX_DIST = 128
N_MEM, H_M, HD_M = 256, 4, 256
D_M = H_M * HD_M
N_BRANCH = 3
IN_SIZES = (A_COLS, D_B, D_KV, D_KV, H_I * D_I, D_I, H_I, D_M, N_BRANCH * D_MODEL)
N_GROUPS, EXPERTS_PER_GROUP = 4, 4
N_EXPERTS = N_GROUPS * EXPERTS_PER_GROUP
TOP_K_INNER = 2
D_EXPERT = 512

WL_PAD, AL_PAD, GL_PAD = 128, 128, 256
A_PAD = 3 * D_A + WL_PAD + AL_PAD + GL_PAD
KVI_PAD = 2 * D_KV + LANES
KEY_TILE = 2 * LANES
SEQ_PER_TILE = LANES // H_A
MOE_TM = 512
INT_MIN = -(2 ** 31)
VMEM_LIMIT = 48 * 1024 * 1024


def _params(*sem):
    return pltpu.CompilerParams(dimension_semantics=sem, vmem_limit_bytes=VMEM_LIMIT)


def _two_part_specs(tm, width, na):
    return (pl.BlockSpec((tm, width), lambda i: (jnp.minimum(i, na - 1), 0)),
            pl.BlockSpec((tm, width), lambda i: (jnp.maximum(i - na, 0), 0)))


def _pick(i, na, a_ref, b_ref):
    return jnp.where(i < na, a_ref[...], b_ref[...])


def _norm2_kernel(xa_ref, xb_ref, g_ref, o_ref, *, na):
    x = _pick(pl.program_id(0), na, xa_ref, xb_ref)
    y = x * lax.rsqrt(jnp.mean(x * x, axis=-1, keepdims=True) + NORM_EPS)
    o_ref[...] = (y * g_ref[...]).astype(o_ref.dtype)


def _rmsnorm_bf16(xa, xb, g, tm):
    d = xa.shape[1]
    na, nb = xa.shape[0] // tm, xb.shape[0] // tm
    sa, sb = _two_part_specs(tm, d, na)
    return pl.pallas_call(
        functools.partial(_norm2_kernel, na=na),
        grid=(na + nb,),
        in_specs=[sa, sb, pl.BlockSpec((1, d), lambda i: (0, 0))],
        out_specs=pl.BlockSpec((tm, d), lambda i: (i, 0)),
        out_shape=jax.ShapeDtypeStruct(((na + nb) * tm, d), BF16),
        compiler_params=_params("parallel"),
        name="rmsnorm_bf16",
    )(xa, xb, g.reshape(1, d))


def _mm_kernel(a_ref, w_ref, o_ref):
    o_ref[...] = jnp.dot(a_ref[...], w_ref[...], preferred_element_type=F32).astype(o_ref.dtype)


def _matmul(a, w, out_dtype, tm, tn, name):
    n, k = a.shape
    m = w.shape[1]
    return pl.pallas_call(
        _mm_kernel,
        grid=(n // tm, m // tn),
        in_specs=[pl.BlockSpec((tm, k), lambda i, j: (i, 0)), pl.BlockSpec((k, tn), lambda i, j: (0, j))],
        out_specs=pl.BlockSpec((tm, tn), lambda i, j: (i, j)),
        out_shape=jax.ShapeDtypeStruct((n, m), out_dtype),
        compiler_params=_params("parallel", "parallel"),
        name=name,
    )(a, w)


def _softplus(x):
    return jnp.maximum(x, 0.0) + jnp.log1p(jnp.exp(-jnp.abs(x)))


def _sigmoid(x):
    return 1.0 / (1.0 + jnp.exp(-x))


def _rwkv_prep_kernel(z_ref, sh_ref, mu_ref, w0_ref, wdec_ref, a0_ref, wa_ref, wg_ref, kk_ref, ka_ref,
                      r_out, w_out, k_out, v_out, kk_out, a_out, g_out, last_out, carry_ref):
    tb = z_ref.shape[0]

    @pl.when(pl.program_id(1) == 0)
    def _():
        carry_ref[...] = sh_ref[...]

    z = z_ref[...]
    row = lax.broadcasted_iota(jnp.int32, (tb, 1), 0)
    prev = jnp.where(row == 0, carry_ref[...], pltpu.roll(z, 1, axis=0))
    carry_ref[...] = z[tb - 1:tb, :]
    last_out[...] = z[tb - 1:tb, :]
    zm = z + (prev - z) * mu_ref[...]
    r = zm[:, 0:D_A]
    k = zm[:, D_A:2 * D_A]
    v = zm[:, 2 * D_A:3 * D_A]
    o = 3 * D_A
    wl = zm[:, o:o + WL_PAD]
    al = zm[:, o + WL_PAD:o + WL_PAD + AL_PAD]
    gl = zm[:, o + WL_PAD + AL_PAD:]
    lw = jnp.dot(jnp.tanh(wl).astype(BF16), wdec_ref[...], preferred_element_type=F32)
    wv = -_softplus(-(w0_ref[...] + lw)) - 0.5
    a = _sigmoid(a0_ref[...] + jnp.dot(al.astype(BF16), wa_ref[...], preferred_element_type=F32))
    r_out[...] = r
    w_out[...] = jnp.exp(-jnp.exp(wv))
    k_out[...] = k * (1.0 + (a - 1.0) * ka_ref[...])
    v_out[...] = v
    kk_out[...] = k * kk_ref[...]
    a_out[...] = a
    g_out[...] = jnp.dot(_sigmoid(gl).astype(BF16), wg_ref[...], preferred_element_type=F32)


def _rwkv_prep(z, row0, nseq, t, shift0, wts, tb):
    nt = t // tb
    blk0 = row0 // tb
    row = lambda c: pl.BlockSpec((1, c), lambda s, i: (0, 0))
    full = lambda a, b: pl.BlockSpec((a, b), lambda s, i: (0, 0))
    out_spec = pl.BlockSpec((None, tb, D_A), lambda s, i: (s, i, 0))
    out_sds = jax.ShapeDtypeStruct((nseq, t, D_A), F32)
    g_spec = pl.BlockSpec((tb, D_A), lambda s, i: (s * nt + i, 0))
    g_sds = jax.ShapeDtypeStruct((nseq * t, D_A), F32)
    return pl.pallas_call(
        _rwkv_prep_kernel,
        grid=(nseq, nt),
        in_specs=[pl.BlockSpec((tb, A_PAD), lambda s, i: (blk0 + s * nt + i, 0)),
                  pl.BlockSpec((None, 1, A_PAD), lambda s, i: (s, 0, 0)),
                  row(A_PAD), row(D_A), full(WL_PAD, D_A), row(D_A), full(AL_PAD, D_A), full(GL_PAD, D_A),
                  row(D_A), row(D_A)],
        out_specs=[out_spec] * 6 + [g_spec, pl.BlockSpec((None, 1, A_PAD), lambda s, i: (s, 0, 0))],
        out_shape=[out_sds] * 6 + [g_sds, jax.ShapeDtypeStruct((nseq, 1, A_PAD), F32)],
        scratch_shapes=[pltpu.VMEM((1, A_PAD), F32)],
        compiler_params=_params("arbitrary", "arbitrary"),
        name="rwkv_prep",
    )(z, shift0, wts["mu"], wts["w0"], wts["w_dec"], wts["a0"], wts["w_a"], wts["w_g"], wts["k_k"], wts["k_a"])


def _wkv_kernel(r_ref, w_ref, k_ref, v_ref, kk_ref, a_ref, s0_ref, lw_ref, lb_ref, rk_ref,
                y_ref, st_ref, s_ref, ab_ref, tma_ref, tmb_ref, ytm_ref, *, n_tblocks):
    tt = r_ref.shape[1]
    low = lax.broadcasted_iota(jnp.int32, (SEQ_PER_TILE, LANES), 1) < HD_A

    @pl.when(pl.program_id(1) == 0)
    def _():
        s_ref[...] = s0_ref[...]

    n_pairs = tt // 2

    def to_tiles(buf_ref, j):
        j = jnp.minimum(j, n_pairs - 1)
        for ai, ref in enumerate((r_ref, w_ref, k_ref, v_ref, kk_ref, a_ref)):
            x0 = ref[:, 2 * j, :]
            x1 = ref[:, 2 * j + 1, :]
            rows = []
            for h in range(H_A):
                sl = slice((h // 2) * LANES, (h // 2 + 1) * LANES)
                if h % 2 == 0:
                    rows.append(jnp.where(low, x0[:, sl], pltpu.roll(x1[:, sl], HD_A, axis=1)))
                else:
                    rows.append(jnp.where(low, pltpu.roll(x0[:, sl], HD_A, axis=1), x1[:, sl]))
            m = jnp.concatenate(rows, axis=0).T
            buf_ref[ai, 0] = m[:HD_A]
            buf_ref[ai, 1] = m[HD_A:]

    def step(buf_ref, u, t):
        kk = buf_ref[4, u]
        ss = jnp.sum(kk * kk, axis=0, keepdims=True)
        kkn = kk * lax.rsqrt(jnp.maximum(ss, 1e-24))
        ab_ref[0] = -kkn
        ab_ref[1] = kkn * buf_ref[5, u]
        vv = buf_ref[3, u]
        sa = jnp.zeros((HD_A, LANES), F32)
        for k in range(HD_A):
            sa = sa + s_ref[k] * ab_ref[0, k:k + 1, :]
        y = jnp.zeros((HD_A, LANES), F32)
        for k in range(HD_A):
            s_new = (s_ref[k] * buf_ref[1, u, k:k + 1, :] + sa * ab_ref[1, k:k + 1, :]
                     + vv * buf_ref[2, u, k:k + 1, :])
            s_ref[k] = s_new
            y = y + s_new * buf_ref[0, u, k:k + 1, :]
        mean = jnp.mean(y, axis=0, keepdims=True)
        d = y - mean
        var = jnp.mean(d * d, axis=0, keepdims=True)
        yn = d * lax.rsqrt(var + LNX_EPS) * lw_ref[...] + lb_ref[...]
        bonus = jnp.sum(buf_ref[0, u] * buf_ref[2, u] * rk_ref[...], axis=0, keepdims=True) * vv
        ytm_ref[t] = yn + bonus

    to_tiles(tma_ref, 0)

    def two_pairs(m, carry):
        to_tiles(tmb_ref, 2 * m + 1)
        step(tma_ref, 0, 4 * m)
        step(tma_ref, 1, 4 * m + 1)
        to_tiles(tma_ref, 2 * m + 2)
        step(tmb_ref, 0, 4 * m + 2)
        step(tmb_ref, 1, 4 * m + 3)
        return carry

    lax.fori_loop(0, n_pairs // 2, two_pairs, 0)

    for j in range(tt // 2):
        m = jnp.concatenate([ytm_ref[2 * j], ytm_ref[2 * j + 1]], axis=0).T
        for hp in range(H_A // 2):
            even = m[(2 * hp) * SEQ_PER_TILE:(2 * hp + 1) * SEQ_PER_TILE]
            odd = m[(2 * hp + 1) * SEQ_PER_TILE:(2 * hp + 2) * SEQ_PER_TILE]
            cols = slice(hp * LANES, (hp + 1) * LANES)
            y_ref[:, 2 * j, cols] = jnp.where(low, even, pltpu.roll(odd, HD_A, axis=1))
            y_ref[:, 2 * j + 1, cols] = jnp.where(low, pltpu.roll(even, HD_A, axis=1), odd)

    @pl.when(pl.program_id(1) == n_tblocks - 1)
    def _():
        st_ref[...] = s_ref[...]


def _wkv(r, w, k, v, kk, a, s0, lw, lb, rk, tt):
    nseq, t, _ = r.shape
    p = nseq * H_A
    assert tt % 4 == 0 and t % tt == 0 and nseq % SEQ_PER_TILE == 0
    seq = pl.BlockSpec((SEQ_PER_TILE, tt, D_A), lambda g, i: (g, i, 0))
    vec = pl.BlockSpec((HD_A, LANES), lambda g, i: (0, 0))
    st = pl.BlockSpec((HD_A, HD_A, LANES), lambda g, i: (0, 0, g))
    return pl.pallas_call(
        functools.partial(_wkv_kernel, n_tblocks=t // tt),
        grid=(p // LANES, t // tt),
        in_specs=[seq] * 6 + [st, vec, vec, vec],
        out_specs=[seq, st],
        out_shape=[jax.ShapeDtypeStruct((nseq, t, D_A), F32), jax.ShapeDtypeStruct((HD_A, HD_A, p), F32)],
        scratch_shapes=[pltpu.VMEM((HD_A, HD_A, LANES), F32), pltpu.VMEM((2, HD_A, LANES), F32),
                        pltpu.VMEM((6, 2, HD_A, LANES), F32), pltpu.VMEM((6, 2, HD_A, LANES), F32),
                        pltpu.VMEM((tt, HD_A, LANES), F32)],
        compiler_params=_params("arbitrary", "arbitrary"),
        name="wkv_recurrence",
    )(r, w, k, v, kk, a, s0, lw, lb, rk)


def _state_to_tiles(wkv):
    nseq = wkv.shape[0]
    x = wkv.reshape(nseq // SEQ_PER_TILE, SEQ_PER_TILE, H_A, HD_A, HD_A)
    return x.transpose(4, 3, 0, 2, 1).reshape(HD_A, HD_A, nseq * H_A)


def _state_from_tiles(s, nseq):
    x = s.reshape(HD_A, HD_A, nseq // SEQ_PER_TILE, H_A, SEQ_PER_TILE)
    return x.transpose(2, 4, 3, 1, 0).reshape(nseq, H_A, HD_A, HD_A)


def _lane_tile(vec):
    return jnp.repeat(vec.reshape(H_A, HD_A).T, SEQ_PER_TILE, axis=1)


def _dsa_kernel(q_ref, iq_ref, iw_ref, k_ref, v_ref, ik2_ref, d0_ref, d1_ref, o_ref,
                madd_ref, logit_ref, mx_ref, ls_ref, acc_ref, score_ref, key_ref, *maybe_iwb_ref,
                q_start, l_valid, topk, keys_on_rows):
    tq = q_ref.shape[0]
    lp = k_ref.shape[0]
    q0 = q_start + pl.program_id(1) * tq
    nt = (((1,), (1,)), ((), ()))
    lane_shift = int(math.log2(LANES))
    chunk_shift = int(math.log2(CHUNK))
    kax = 0 if keys_on_rows else 1
    dc = lax.shift_right_logical(q0, lane_shift)
    n_lane_tiles = jnp.minimum(lax.shift_right_logical(q0 + tq - 1, lane_shift) + 1, lp // LANES)
    n_tiles = lax.shift_right_logical(n_lane_tiles + 1, 1)

    def tile_off(c, width):
        return pl.multiple_of(c * width, width)

    def keys_at(off, width):
        return (pl.ds(off, width), slice(None)) if keys_on_rows else (slice(None), pl.ds(off, width))

    def per_query(x):
        return jnp.sum(x, axis=kax, keepdims=True)

    iw = iw_ref[...] * ((H_I * D_I) ** -0.5)
    if not keys_on_rows:
        iwb_ref, = maybe_iwb_ref
        for h in range(H_I):
            iwb_ref[h] = jnp.broadcast_to(iw[:, h:h + 1], (tq, LANES))

    def score_tile(c, carry):
        kab = ik2_ref[c]
        acc = [jnp.zeros((LANES, tq) if keys_on_rows else (tq, LANES), F32) for _ in range(2)]
        for hp in range(H_I // 2):
            iq_pair = iq_ref[:, hp * LANES:(hp + 1) * LANES]
            if keys_on_rows:
                d = lax.dot_general(kab, iq_pair, nt, preferred_element_type=F32)
                halves = (d[:LANES], d[LANES:])
                wts = (iw[2 * hp:2 * hp + 1, :], iw[2 * hp + 1:2 * hp + 2, :])
            else:
                d = lax.dot_general(iq_pair, kab, nt, preferred_element_type=F32)
                halves = (d[:, :LANES], d[:, LANES:])
                wts = (iwb_ref[2 * hp], iwb_ref[2 * hp + 1])
            for u in range(2):
                acc[u] = acc[u] + jnp.maximum(halves[u], 0.0) * wts[u]
        score_ref[keys_at(tile_off(c, LANES), LANES)] = acc[0] + acc[1]
        return carry

    lax.fori_loop(0, 2 * n_tiles, score_tile, 0)

    qshape, kshape = ((1, tq), lambda w: (w, 1)) if keys_on_rows else ((tq, 1), lambda w: (1, w))
    qchunk = lax.shift_right_arithmetic(q0 + lax.broadcasted_iota(jnp.int32, qshape, 1 - kax), chunk_shift)

    def visible(lo, width):
        kpos = lo + lax.broadcasted_iota(jnp.int32, kshape(width), kax)
        return (lax.shift_right_arithmetic(kpos, chunk_shift) <= qchunk) & (kpos < l_valid)

    def key_tile(c, carry):
        off = tile_off(c, KEY_TILE)
        bits = pltpu.bitcast(score_ref[keys_at(off, KEY_TILE)] + 0.0, jnp.int32)
        key = bits ^ (lax.shift_right_arithmetic(bits, 31) & jnp.int32(0x7FFFFFFF))
        key_ref[keys_at(off, KEY_TILE)] = jnp.where(visible(off, KEY_TILE), key, jnp.int32(INT_MIN))
        return carry

    lax.fori_loop(0, n_tiles, key_tile, 0)
    kf = jnp.float32(topk)
    sub = 8

    def count(pred):
        def body(c, acc):
            m = jnp.where(pred(key_ref[keys_at(tile_off(c, KEY_TILE), KEY_TILE)]), 1.0, 0.0)
            if keys_on_rows:
                parts = [m[r:r + sub] for r in range(0, KEY_TILE, sub)]
            else:
                parts = [m[:, :LANES], m[:, LANES:]]
            while len(parts) > 1:
                parts = [parts[u] + parts[u + 1] for u in range(0, len(parts), 2)]
            return acc + parts[0]
        acc = lax.fori_loop(0, n_tiles, body, jnp.zeros((sub, tq) if keys_on_rows else (tq, LANES), F32))
        return per_query(acc)

    def search(i, thr):
        cand = thr ^ lax.shift_left(jnp.int32(1), 31 - i)
        return jnp.where(count(lambda kc: kc >= cand) >= kf, cand, thr)

    thr = lax.fori_loop(0, 32, search, jnp.full(qshape, INT_MIN, jnp.int32))

    def store_mask(off, width, sel):
        madd = jnp.where(sel, 0.0, -jnp.inf)
        if keys_on_rows:
            for u in range(width // LANES):
                madd_ref[:, pl.ds(pl.multiple_of(off + u * LANES, LANES), LANES)] = madd[u * LANES:(u + 1) * LANES].T
        else:
            madd_ref[:, pl.ds(off, width)] = madd

    surplus = (count(lambda kc: kc >= thr) > kf) & (thr != jnp.int32(INT_MIN))
    any_surplus = jnp.max(jnp.where(surplus, 1.0, 0.0))

    @pl.when(any_surplus == 0.0)
    def _():
        def select_tile(c, carry):
            off = tile_off(c, KEY_TILE)
            store_mask(off, KEY_TILE, visible(off, KEY_TILE) & (key_ref[keys_at(off, KEY_TILE)] >= thr))
            return carry

        lax.fori_loop(0, n_tiles, select_tile, 0)

    @pl.when(any_surplus > 0.0)
    def _():
        need = kf - count(lambda kc: kc > thr)
        ii = lax.broadcasted_iota(jnp.int32, (LANES, LANES), 0)
        jj = lax.broadcasted_iota(jnp.int32, (LANES, LANES), 1)
        tri = jnp.where((ii >= jj) if keys_on_rows else (ii <= jj), 1.0, 0.0).astype(BF16)

        def select_tile(c, run):
            off = tile_off(c, LANES)
            keyc = key_ref[keys_at(off, LANES)]
            eqf = jnp.where(keyc == thr, 1.0, 0.0)
            if keys_on_rows:
                within = jnp.dot(tri, eqf.astype(BF16), preferred_element_type=F32)
                total = within[LANES - 1:LANES, :]
            else:
                within = jnp.dot(eqf.astype(BF16), tri, preferred_element_type=F32)
                total = within[:, LANES - 1:LANES]
            take = jnp.where(keyc > thr, 1.0, jnp.where(run + within <= need, eqf, 0.0))
            store_mask(off, LANES, visible(off, LANES) & (take > 0.5))
            return run + total

        lax.fori_loop(0, 2 * n_tiles, select_tile, jnp.zeros(qshape, F32))

    off0 = tile_off(dc, LANES)
    off1 = tile_off(jnp.maximum(dc - 1, 0), LANES)
    for g in range(N_KV_B):
        ksl = slice(g * HD_B, (g + 1) * HD_B)
        qs = jnp.concatenate([q_ref[:, (g * G_B + j) * HD_B:(g * G_B + j + 1) * HD_B] for j in range(G_B)], axis=0)

        def logits_tile(c, carry):
            off = tile_off(c, KEY_TILE)
            s = lax.dot_general(qs, k_ref[pl.ds(off, KEY_TILE), ksl], nt, preferred_element_type=F32)
            md = madd_ref[:, pl.ds(off, KEY_TILE)]
            for j in range(G_B):
                logit_ref[j * tq:(j + 1) * tq, pl.ds(off, KEY_TILE)] = s[j * tq:(j + 1) * tq] * (HD_B ** -0.5) + md
            return carry

        lax.fori_loop(0, n_tiles, logits_tile, 0)
        for j in range(G_B):
            logit_ref[j * tq:(j + 1) * tq, pl.ds(off0, LANES)] += d0_ref[g * G_B + j]

        @pl.when(dc > 0)
        def _():
            for j in range(G_B):
                logit_ref[j * tq:(j + 1) * tq, pl.ds(off1, LANES)] += d1_ref[g * G_B + j]

        mx_ref[...] = jnp.full(mx_ref.shape, -jnp.inf, F32)

        def max_tile(c, carry):
            lg = logit_ref[:, pl.ds(tile_off(c, KEY_TILE), KEY_TILE)]
            mx_ref[...] = jnp.maximum(mx_ref[...], jnp.maximum(lg[:, :LANES], lg[:, LANES:]))
            return carry

        lax.fori_loop(0, n_tiles, max_tile, 0)
        m = jnp.max(mx_ref[...], axis=1, keepdims=True)
        ls_ref[...] = jnp.zeros(ls_ref.shape, F32)
        acc_ref[...] = jnp.zeros(acc_ref.shape, F32)

        def pv_tile(c, carry):
            off = tile_off(c, KEY_TILE)
            p = jnp.exp(logit_ref[:, pl.ds(off, KEY_TILE)] - m)
            ls_ref[...] += p[:, :LANES] + p[:, LANES:]
            acc_ref[...] += jnp.dot(p.astype(BF16), v_ref[pl.ds(off, KEY_TILE), ksl], preferred_element_type=F32)
            return carry

        lax.fori_loop(0, n_tiles, pv_tile, 0)
        res = acc_ref[...] / jnp.sum(ls_ref[...], axis=1, keepdims=True)
        for j in range(G_B):
            h = g * G_B + j
            o_ref[:, h * HD_B:(h + 1) * HD_B] = res[j * tq:(j + 1) * tq].astype(o_ref.dtype)


def _t5_bucket(rel):
    half = REL_BUCKETS // 2
    exact = half // 2
    side = jnp.where(rel > 0, half, 0)
    n = jnp.abs(rel)
    nf = jnp.maximum(n, 1).astype(F32)
    large = exact + (jnp.log(nf / exact) / math.log(REL_MAX_DIST / exact) * (half - exact)).astype(jnp.int32)
    large = jnp.minimum(large, half - 1)
    return side + jnp.where(n < exact, n, large)


def _dsa(zq, row0, b, t, iw, k_all, v_all, ik_all, rel_bias, tq, q_start, l_valid):
    lp = k_all.shape[1]
    topk = min(TOPK_MAX, l_valid // 4)
    assert q_start % LANES == 0 and (tq == LANES or t == tq) and lp % KEY_TILE == 0 and row0 % tq == 0
    nq = t // tq
    blk0 = row0 // tq
    ikt = ik_all.reshape(b, lp // LANES, LANES, D_I)
    zeros = jnp.zeros_like(ikt)
    ik2 = jnp.concatenate([jnp.concatenate([ikt, zeros], axis=-1), jnp.concatenate([zeros, ikt], axis=-1)], axis=2)
    ji = jnp.arange(LANES, dtype=jnp.int32)[None, :] - jnp.arange(tq, dtype=jnp.int32)[:, None]
    def bias_of(rel):
        hit = _t5_bucket(rel)[None, None] == jnp.arange(REL_BUCKETS, dtype=jnp.int32).reshape((1, -1) + (1,) * rel.ndim)
        return jnp.sum(jnp.where(hit, rel_bias.T.reshape((H_B, REL_BUCKETS) + (1,) * rel.ndim), 0.0), axis=1)

    far = bias_of(jnp.full((1, 1), -2 * LANES, jnp.int32))
    d0 = bias_of(ji) - far
    d1 = bias_of(ji - LANES) - far
    keys_on_rows = tq == LANES
    kern = functools.partial(_dsa_kernel, q_start=q_start, l_valid=l_valid, topk=topk, keys_on_rows=keys_on_rows)
    qspec = lambda col: pl.BlockSpec((tq, D_B), lambda bi, i: (blk0 + bi * nq + i, col))
    kspec = lambda w: pl.BlockSpec((None, lp, w), lambda bi, i: (bi, 0, 0))
    cspec = pl.BlockSpec((H_B, tq, LANES), lambda bi, i: (0, 0, 0))
    rows = G_B * tq
    if keys_on_rows:
        iw, iw_spec = iw.T, pl.BlockSpec((H_I, tq), lambda bi, i: (0, bi * nq + i))
        mask_scratch = [pltpu.VMEM((lp, tq), F32), pltpu.VMEM((lp, tq), jnp.int32)]
    else:
        iw_spec = pl.BlockSpec((tq, H_I), lambda bi, i: (bi * nq + i, 0))
        mask_scratch = [pltpu.VMEM((tq, lp), F32), pltpu.VMEM((tq, lp), jnp.int32), pltpu.VMEM((H_I, tq, LANES), F32)]
    return pl.pallas_call(
        kern,
        grid=(b, nq),
        in_specs=[qspec(0), qspec(1), iw_spec, kspec(D_KV), kspec(D_KV),
                  pl.BlockSpec((None, lp // LANES, 2 * LANES, LANES), lambda bi, i: (bi, 0, 0, 0)), cspec, cspec],
        out_specs=pl.BlockSpec((tq, D_B), lambda bi, i: (bi * nq + i, 0)),
        out_shape=jax.ShapeDtypeStruct((b * t, D_B), BF16),
        scratch_shapes=[pltpu.VMEM((tq, lp), F32), pltpu.VMEM((rows, lp), F32), pltpu.VMEM((rows, LANES), F32),
                        pltpu.VMEM((rows, LANES), F32), pltpu.VMEM((rows, HD_B), F32)] + mask_scratch,
        compiler_params=_params("parallel", "arbitrary"),
        name="dsa_attention",
    )(zq, zq, iw, k_all, v_all, ik2, d0, d1)


def _mem_kernel(q_ref, k_ref, v_ref, o_ref):
    nt = (((1,), (1,)), ((), ()))
    for h in range(H_M):
        sl = slice(h * HD_M, (h + 1) * HD_M)
        s = lax.dot_general(q_ref[:, sl], k_ref[:, sl], nt, preferred_element_type=F32) * (HD_M ** -0.5)
        m = jnp.max(s, axis=1, keepdims=True)
        p = jnp.exp(s - m)
        den = jnp.sum(p, axis=1, keepdims=True)
        o = jnp.dot(p.astype(BF16), v_ref[:, sl], preferred_element_type=F32)
        o_ref[:, sl] = (o / den).astype(o_ref.dtype)


def _mem_attention(zq, row0, b, t, mk, mv, tq):
    nq = t // tq
    blk0 = row0 // tq
    return pl.pallas_call(
        _mem_kernel,
        grid=(b, nq),
        in_specs=[pl.BlockSpec((tq, D_M), lambda bi, i: (blk0 + bi * nq + i, 2)),
                  pl.BlockSpec((None, N_MEM, D_M), lambda bi, i: (bi, 0, 0)),
                  pl.BlockSpec((None, N_MEM, D_M), lambda bi, i: (bi, 0, 0))],
        out_specs=pl.BlockSpec((tq, D_M), lambda bi, i: (bi * nq + i, 0)),
        out_shape=jax.ShapeDtypeStruct((b * t, D_M), BF16),
        compiler_params=_params("parallel", "parallel"),
        name="memory_attention",
    )(zq, mk, mv)


def _mix_kernel(yap_ref, yas_ref, gp_ref, gs_ref, obp_ref, obs_ref, omp_ref, oms_ref, ga_ref, gb_ref, gm_ref,
                pa_ref, pb_ref, pm_ref, o_ref, *, na):
    i = pl.program_id(1)
    oa = (_pick(i, na, yap_ref, yas_ref) * _pick(i, na, gp_ref, gs_ref)).astype(BF16)
    acc = _sigmoid(ga_ref[...]) * jnp.dot(oa, pa_ref[...], preferred_element_type=F32)
    acc = acc + _sigmoid(gb_ref[...]) * jnp.dot(_pick(i, na, obp_ref, obs_ref), pb_ref[...],
                                                preferred_element_type=F32)
    acc = acc + _sigmoid(gm_ref[...]) * jnp.dot(_pick(i, na, omp_ref, oms_ref), pm_ref[...],
                                                preferred_element_type=F32)
    o_ref[...] = acc.astype(o_ref.dtype)


def _mix(ya2, g2, ob2, om2, zg, pa, pb, pm, tm, tn):
    n = zg.shape[0]
    nj = D_MODEL // tn
    na = g2[0].shape[0] // tm
    two = lambda w: (pl.BlockSpec((tm, w), lambda j, i: (jnp.minimum(i, na - 1), 0)),
                     pl.BlockSpec((tm, w), lambda j, i: (jnp.maximum(i - na, 0), 0)))
    gate = lambda br: pl.BlockSpec((tm, tn), lambda j, i, br=br: (i, br * nj + j))
    wt = pl.BlockSpec((D_A, tn), lambda j, i: (0, j))
    return pl.pallas_call(
        functools.partial(_mix_kernel, na=na),
        grid=(nj, n // tm),
        in_specs=[*two(D_A), *two(D_A), *two(D_B), *two(D_M), gate(0), gate(1), gate(2), wt, wt, wt],
        out_specs=pl.BlockSpec((tm, tn), lambda j, i: (i, j)),
        out_shape=jax.ShapeDtypeStruct((n, D_MODEL), BF16),
        compiler_params=_params("parallel", "parallel"),
        name="branch_mix",
    )(*ya2, *g2, *ob2, *om2, zg, zg, zg, pa, pb, pm)


def _wo_kernel(m_ref, w_ref, xa_ref, xb_ref, g_ref, h_ref, u_ref, *, na):
    x = _pick(pl.program_id(0), na, xa_ref, xb_ref)
    h = x + jnp.dot(m_ref[...], w_ref[...], preferred_element_type=F32)
    h_ref[...] = h
    y = h * lax.rsqrt(jnp.mean(h * h, axis=-1, keepdims=True) + NORM_EPS)
    u_ref[...] = (y * g_ref[...]).astype(u_ref.dtype)


def _wo_residual_norm(mixed, w_o, xa, xb, g, tm):
    n = mixed.shape[0]
    na = xa.shape[0] // tm
    tile = pl.BlockSpec((tm, D_MODEL), lambda i: (i, 0))
    sa, sb = _two_part_specs(tm, D_MODEL, na)
    return pl.pallas_call(
        functools.partial(_wo_kernel, na=na),
        grid=(n // tm,),
        in_specs=[tile, pl.BlockSpec((D_MODEL, D_MODEL), lambda i: (0, 0)), sa, sb,
                  pl.BlockSpec((1, D_MODEL), lambda i: (0, 0))],
        out_specs=[tile, tile],
        out_shape=[jax.ShapeDtypeStruct((n, D_MODEL), F32), jax.ShapeDtypeStruct((n, D_MODEL), BF16)],
        compiler_params=_params("parallel"),
        name="wo_residual_norm",
    )(mixed, w_o, xa, xb, g.reshape(1, D_MODEL))


def _moe_kernel(tg_ref, nu_ref, x_ref, gate_ref, w1_ref, w3_ref, w2_ref, o_ref):
    i, j = pl.program_id(0), pl.program_id(1)

    @pl.when((i >= nu_ref[0]) & (j == 0))
    def _():
        o_ref[...] = jnp.zeros(o_ref.shape, F32)

    @pl.when(i < nu_ref[0])
    def _():
        x = x_ref[...]
        a = jnp.dot(x, w1_ref[...], preferred_element_type=F32)
        b = jnp.dot(x, w3_ref[...], preferred_element_type=F32)
        lane = lax.broadcasted_iota(jnp.int32, gate_ref.shape, 1)
        ge = jnp.sum(jnp.where(lane == j, gate_ref[...], 0.0), axis=1, keepdims=True)
        hid = (a * _sigmoid(a)) * b * ge
        out = jnp.dot(hid.astype(BF16), w2_ref[...], preferred_element_type=F32)

        @pl.when(j == 0)
        def _():
            o_ref[...] = out

        @pl.when(j > 0)
        def _():
            o_ref[...] += out


def _moe_grouped(x_sorted, gate_sorted, tile_group, n_used, w1, w3, w2):
    npad = x_sorted.shape[0]
    ew = lambda i, j, tg, nu: (tg[i] * EXPERTS_PER_GROUP + j, 0, 0)
    return pl.pallas_call(
        _moe_kernel,
        grid_spec=pltpu.PrefetchScalarGridSpec(
            num_scalar_prefetch=2,
            grid=(npad // MOE_TM, EXPERTS_PER_GROUP),
            in_specs=[pl.BlockSpec((MOE_TM, D_MODEL), lambda i, j, tg, nu: (i, 0)),
                      pl.BlockSpec((MOE_TM, EXPERTS_PER_GROUP), lambda i, j, tg, nu: (i, 0)),
                      pl.BlockSpec((None, D_MODEL, D_EXPERT), ew), pl.BlockSpec((None, D_MODEL, D_EXPERT), ew),
                      pl.BlockSpec((None, D_EXPERT, D_MODEL), ew)],
            out_specs=pl.BlockSpec((MOE_TM, D_MODEL), lambda i, j, tg, nu: (i, 0))),
        out_shape=jax.ShapeDtypeStruct((npad, D_MODEL), F32),
        compiler_params=_params("parallel", "arbitrary"),
        name="moe_grouped",
    )(tile_group, n_used, x_sorted, gate_sorted, w1, w3, w2)


def _route(logits, b_grp, b_rt):
    n = logits.shape[0]
    g_logits = logits[:, :N_GROUPS] + b_grp
    e_logits = (logits[:, N_GROUPS:N_GROUPS + N_EXPERTS] + b_rt).reshape(n, N_GROUPS, EXPERTS_PER_GROUP)
    g_idx = jnp.argmax(g_logits, axis=-1).astype(jnp.int32)
    g_w = jnp.max(jax.nn.softmax(g_logits, axis=-1), axis=-1, keepdims=True)
    onehot_g = g_idx[:, None] == jnp.arange(N_GROUPS, dtype=jnp.int32)[None, :]
    e_in = jnp.sum(jnp.where(onehot_g[:, :, None], e_logits, 0.0), axis=1)
    lane = jnp.arange(EXPERTS_PER_GROUP, dtype=jnp.int32)[None, :]
    i1 = jnp.argmax(e_in, axis=-1)[:, None]
    v1 = jnp.max(e_in, axis=-1, keepdims=True)
    rest = jnp.where(lane == i1, -jnp.inf, e_in)
    i2 = jnp.argmax(rest, axis=-1)[:, None]
    v2 = jnp.max(rest, axis=-1, keepdims=True)
    w12 = g_w * jax.nn.softmax(jnp.concatenate([v1, v2], axis=-1), axis=-1)
    gate4 = jnp.where(lane == i1, w12[:, 0:1], 0.0) + jnp.where(lane == i2, w12[:, 1:2], 0.0)

    counts = jnp.sum(onehot_g.astype(jnp.int32), axis=0)
    tiles = (counts + MOE_TM - 1) // MOE_TM
    tile_end = jnp.cumsum(tiles)
    slot0 = (tile_end - tiles) * MOE_TM
    start = jnp.cumsum(counts) - counts
    order = jnp.argsort(g_idx, stable=True).astype(jnp.int32)
    rank = jnp.sum(jnp.where(onehot_g, jnp.cumsum(onehot_g.astype(jnp.int32), axis=0) - 1, 0), axis=1)
    inv = jnp.sum(jnp.where(onehot_g, slot0[None, :], 0), axis=1) + rank
    npad = n + N_GROUPS * MOE_TM
    slots = jnp.arange(npad, dtype=jnp.int32)
    slot_group = jnp.minimum(jnp.sum(jnp.where(slots[:, None] >= tile_end[None, :] * MOE_TM, 1, 0), axis=1),
                             N_GROUPS - 1)
    in_group = slot_group[:, None] == jnp.arange(N_GROUPS, dtype=jnp.int32)[None, :]
    pos = slots - jnp.sum(jnp.where(in_group, slot0[None, :], 0), axis=1)
    used = pos < jnp.sum(jnp.where(in_group, counts[None, :], 0), axis=1)
    src = jnp.where(used, pos + jnp.sum(jnp.where(in_group, start[None, :], 0), axis=1), 0)
    perm = order[src]
    gate_sorted = jnp.where(used[:, None], gate4[perm], 0.0)
    tile_group = slot_group[::MOE_TM].astype(jnp.int32)
    return perm, inv, gate_sorted, tile_group, tile_end[-1:].astype(jnp.int32)


def _final_kernel(h_ref, m_ref, g_ref, yp_ref, ys_ref, *, na):
    h = h_ref[...] + m_ref[...]
    y = h * lax.rsqrt(jnp.mean(h * h, axis=-1, keepdims=True) + NORM_EPS) * g_ref[...]
    i = pl.program_id(0)

    @pl.when(i < na)
    def _():
        yp_ref[...] = y

    @pl.when(i >= na)
    def _():
        ys_ref[...] = y


def _final_norm(h, moe, g, n_p, tm):
    n = h.shape[0]
    na = n_p // tm
    tile = pl.BlockSpec((tm, D_MODEL), lambda i: (i, 0))
    sa, sb = _two_part_specs(tm, D_MODEL, na)
    return pl.pallas_call(
        functools.partial(_final_kernel, na=na),
        grid=(n // tm,),
        in_specs=[tile, tile, pl.BlockSpec((1, D_MODEL), lambda i: (0, 0))],
        out_specs=[sa, sb],
        out_shape=[jax.ShapeDtypeStruct((n_p, D_MODEL), F32), jax.ShapeDtypeStruct((n - n_p, D_MODEL), F32)],
        compiler_params=_params("arbitrary"),
        name="residual_final_norm",
    )(h, moe, g.reshape(1, D_MODEL))


def _pad_cols(w, width):
    return jnp.pad(w, ((0, 0), (0, width - w.shape[1])))


def _pad_rows(w, height):
    return jnp.pad(w, ((0, height - w.shape[0]), (0, 0)))


def _pad_lora(x):
    o = 3 * D_A
    pad = lambda t, w: jnp.pad(t, [(0, 0)] * (t.ndim - 1) + [(0, w - t.shape[-1])])
    return jnp.concatenate([x[..., :o], pad(x[..., o:o + W_LORA], WL_PAD),
                            pad(x[..., o + W_LORA:o + W_LORA + A_LORA], AL_PAD),
                            pad(x[..., o + W_LORA + A_LORA:], GL_PAD)], axis=-1)


def _unpad_lora(x):
    o = 3 * D_A
    return jnp.concatenate([x[..., :o], x[..., o:o + W_LORA], x[..., o + WL_PAD:o + WL_PAD + A_LORA],
                            x[..., o + WL_PAD + AL_PAD:o + WL_PAD + AL_PAD + G_LORA]], axis=-1)


def kernel(x_prompt, x_sample, cache_dsa_k, cache_dsa_v, cache_idx_k, state_rwkv_shift, state_rwkv_wkv, cache_mem_k, cache_mem_v, mem_prompt, rel_bias, g_attn, w_in, rwkv_mu, rwkv_w0, rwkv_w_dec, rwkv_a0, rwkv_w_a, rwkv_w_g, rwkv_k_k, rwkv_k_a, rwkv_r_k, rwkv_lnx_w, rwkv_lnx_b, g_mem, w_mem_kv, p_a, p_b, p_m, w_o, g_ffn, w_grp, b_grp, w_rt, b_rt, w1, w3, w2, g_final):
    assert w_in.shape[0] == 1, "single layer"
    bp, tp, _ = x_prompt.shape
    bs, ts, _ = x_sample.shape
    past = cache_dsa_k.shape[2]
    n_p, n_s = bp * tp, bs * ts
    n = n_p + n_s
    tm = 1024
    xp = x_prompt.reshape(n_p, D_MODEL)
    xs = x_sample.reshape(n_s, D_MODEL)

    offs = [0]
    for s in IN_SIZES:
        offs.append(offs[-1] + s)
    seg = lambda i: w_in[0][:, offs[i]:offs[i + 1]]
    w_a_cols = _pad_lora(seg(0)).astype(BF16)
    w_qim = jnp.concatenate([seg(1), seg(4), seg(7)], axis=1).astype(BF16)
    w_kvi = _pad_cols(jnp.concatenate([seg(2), seg(3), seg(5), seg(6)], axis=1), KVI_PAD).astype(BF16)
    w_gate = seg(8).astype(BF16)
    rw = dict(mu=_pad_lora(rwkv_mu[0])[None], w0=rwkv_w0, a0=rwkv_a0, k_k=rwkv_k_k, k_a=rwkv_k_a,
              w_dec=_pad_rows(rwkv_w_dec[0], WL_PAD).astype(BF16), w_a=_pad_rows(rwkv_w_a[0], AL_PAD).astype(BF16),
              w_g=_pad_rows(rwkv_w_g[0], GL_PAD).astype(BF16))

    u = _rmsnorm_bf16(xp, xs, g_attn[0], 512)
    z_a = _matmul(u, w_a_cols, F32, tm, A_PAD // 4, "proj_rwkv")
    z_qim = _matmul(u, w_qim, BF16, tm, 1024, "proj_queries")
    z_kvi = _matmul(u, w_kvi, F32, tm, KVI_PAD, "proj_kv")
    z_g = _matmul(u, w_gate, F32, tm, 1024, "proj_gates")

    mem = mem_prompt.reshape(bp * N_MEM, D_MODEL)
    um = _rmsnorm_bf16(mem[:bp * N_MEM // 2], mem[bp * N_MEM // 2:], g_mem[0], 512)
    mkv = _matmul(um, w_mem_kv[0].astype(BF16), F32, 1024, 1024, "proj_mem_kv")
    mk_p = mkv[:, :D_M].reshape(bp, N_MEM, D_M)
    mv_p = mkv[:, D_M:].reshape(bp, N_MEM, D_M)

    lw_vec, lb_vec, rk_vec = rwkv_lnx_w[0], rwkv_lnx_b[0], rwkv_r_k[0].reshape(D_A)

    def rwkv_group(row0, nseq, t, shift0, wkv0, tb, tt):
        r, w, k, v, kk, a, g, last = _rwkv_prep(z_a, row0, nseq, t, shift0, rw, tb)
        y, s_fin = _wkv(r, w, k, v, kk, a, _state_to_tiles(wkv0), _lane_tile(lw_vec), _lane_tile(lb_vec),
                        _lane_tile(rk_vec), tt)
        return y.reshape(nseq * t, D_A), g, _state_from_tiles(s_fin, nseq), _unpad_lora(last)

    ya_p, g_p, wkv_p, shift_p = rwkv_group(0, bp, tp, jnp.zeros((bp, 1, A_PAD), F32),
                                           jnp.zeros((bp, H_A, HD_A, HD_A), F32), 256, 32)
    ya_s, g_s, wkv_s, shift_s = rwkv_group(n_p, bs, ts, _pad_lora(state_rwkv_shift[0]), state_rwkv_wkv[0], ts, ts)

    k_new, v_new = z_kvi[:, :D_KV], z_kvi[:, D_KV:2 * D_KV]
    ik_new = z_kvi[:, 2 * D_KV:2 * D_KV + D_I]
    iw = z_kvi[:, 2 * D_KV + D_I:2 * D_KV + D_I + H_I]
    grp = lambda t, sl, b, tlen: t[sl].reshape(b, tlen, t.shape[-1])
    sp, ss = slice(0, n_p), slice(n_p, n)
    ob_p = _dsa(z_qim, 0, bp, tp, iw[sp], grp(k_new, sp, bp, tp).astype(BF16), grp(v_new, sp, bp, tp).astype(BF16),
                grp(ik_new, sp, bp, tp).astype(BF16), rel_bias, Q_BLOCK if tp % Q_BLOCK == 0 else tp, 0, tp)
    l_s = past + ts
    lp_s = -(-l_s // KEY_TILE) * KEY_TILE
    cat = lambda cache, new: jnp.pad(jnp.concatenate([cache.astype(BF16), new.astype(BF16)], axis=1),
                                     ((0, 0), (0, lp_s - l_s), (0, 0)))
    ob_s = _dsa(z_qim, n_p, bs, ts, iw[ss],
                cat(cache_dsa_k[0].reshape(bs, past, D_KV), grp(k_new, ss, bs, ts)),
                cat(cache_dsa_v[0].reshape(bs, past, D_KV), grp(v_new, ss, bs, ts)),
                cat(cache_idx_k[0], grp(ik_new, ss, bs, ts)), rel_bias,
                Q_BLOCK if ts % Q_BLOCK == 0 else ts, past, l_s)

    om_p = _mem_attention(z_qim, 0, bp, tp, mk_p.astype(BF16), mv_p.astype(BF16), 256)
    om_s = _mem_attention(z_qim, n_p, bs, ts, cache_mem_k[0].reshape(bs, N_MEM, D_M).astype(BF16),
                          cache_mem_v[0].reshape(bs, N_MEM, D_M).astype(BF16), ts)

    mixed = _mix((ya_p, ya_s), (g_p, g_s), (ob_p, ob_s), (om_p, om_s), z_g, p_a[0].astype(BF16), p_b[0].astype(BF16),
                 p_m[0].astype(BF16), 256, 1024)
    h, u2 = _wo_residual_norm(mixed, w_o[0].astype(BF16), xp, xs, g_ffn[0], 256)

    w_route = _pad_cols(jnp.concatenate([w_grp[0], w_rt[0]], axis=1), LANES).astype(BF16)
    logits = _matmul(u2, w_route, F32, tm, LANES, "moe_router")
    perm, inv, gate_sorted, tile_group, n_used = _route(logits, b_grp[0], b_rt[0])
    moe_sorted = _moe_grouped(u2[perm], gate_sorted, tile_group, n_used,
                              w1[0].astype(BF16), w3[0].astype(BF16), w2[0].astype(BF16))
    y_p, y_s = _final_norm(h, moe_sorted[inv], g_final, n_p, 512)

    st = lambda t, b, tlen, shape: t.reshape((1, b, tlen) + shape)
    return (y_p.reshape(bp, tp, D_MODEL), y_s.reshape(bs, ts, D_MODEL),
            st(k_new[sp], bp, tp, (N_KV_B, HD_B)), st(v_new[sp], bp, tp, (N_KV_B, HD_B)), st(ik_new[sp], bp, tp, (D_I,)),
            shift_p[None], wkv_p[None],
            mk_p.reshape(1, bp, N_MEM, H_M, HD_M), mv_p.reshape(1, bp, N_MEM, H_M, HD_M),
            st(k_new[ss], bs, ts, (N_KV_B, HD_B)), st(v_new[ss], bs, ts, (N_KV_B, HD_B)), st(ik_new[ss], bs, ts, (D_I,)),
            shift_s[None], wkv_s[None])
```

```python
import functools
import math

import jax
import jax.numpy as jnp
from jax import lax
from jax.experimental import pallas as pl
from jax.experimental.pallas import tpu as pltpu

F32 = jnp.float32
BF16 = jnp.bfloat16

LANES = 128
D_MODEL = 2048
CHUNK = 64
NORM_EPS = 1e-6
H_A, HD_A = 16, 64
D_A = H_A * HD_A
W_LORA, A_LORA, G_LORA = 64, 64, 160
LNX_EPS = 64e-5
A_COLS = 3 * D_A + W_LORA + A_LORA + G_LORA
H_B, N_KV_B, HD_B = 8, 2, 128
G_B = H_B // N_KV_B
D_B = H_B * HD_B
D_KV = N_KV_B * HD_B
H_I, D_I = 16, 64
TOPK_MAX = 256
Q_BLOCK = 128
REL_BUCKETS = 32
REL_MAX_DIST = 128
N_MEM, H_M, HD_M = 256, 4, 256
D_M = H_M * HD_M
N_BRANCH = 3
IN_SIZES = (A_COLS, D_B, D_KV, D_KV, H_I * D_I, D_I, H_I, D_M, N_BRANCH * D_MODEL)
N_GROUPS, EXPERTS_PER_GROUP = 4, 4
N_EXPERTS = N_GROUPS * EXPERTS_PER_GROUP
TOP_K_INNER = 2
D_EXPERT = 512

WL_PAD, AL_PAD, GL_PAD = 128, 128, 256
A_PAD = 3 * D_A + WL_PAD + AL_PAD + GL_PAD
KVI_PAD = 2 * D_KV + LANES
KEY_TILE = 2 * LANES
SEQ_PER_TILE = LANES // H_A
MOE_TM = 512
INT_MIN = -(2 ** 31)
VMEM_LIMIT = 48 * 1024 * 1024


def _params(*sem):
    return pltpu.CompilerParams(dimension_semantics=sem, vmem_limit_bytes=VMEM_LIMIT)


def _two_part_specs(tm, width, na):
    return (pl.BlockSpec((tm, width), lambda i: (jnp.minimum(i, na - 1), 0)),
            pl.BlockSpec((tm, width), lambda i: (jnp.maximum(i - na, 0), 0)))


def _pick(i, na, a_ref, b_ref):
    return jnp.where(i < na, a_ref[...], b_ref[...])


def _norm2_kernel(xa_ref, xb_ref, g_ref, o_ref, *, na):
    x = _pick(pl.program_id(0), na, xa_ref, xb_ref)
    y = x * lax.rsqrt(jnp.mean(x * x, axis=-1, keepdims=True) + NORM_EPS)
    o_ref[...] = (y * g_ref[...]).astype(o_ref.dtype)


def _rmsnorm_bf16(xa, xb, g, tm):
    d = xa.shape[1]
    na, nb = xa.shape[0] // tm, xb.shape[0] // tm
    sa, sb = _two_part_specs(tm, d, na)
    return pl.pallas_call(
        functools.partial(_norm2_kernel, na=na),
        grid=(na + nb,),
        in_specs=[sa, sb, pl.BlockSpec((1, d), lambda i: (0, 0))],
        out_specs=pl.BlockSpec((tm, d), lambda i: (i, 0)),
        out_shape=jax.ShapeDtypeStruct(((na + nb) * tm, d), BF16),
        compiler_params=_params("parallel"),
        name="rmsnorm_bf16",
    )(xa, xb, g.reshape(1, d))


def _mm_kernel(a_ref, w_ref, o_ref):
    o_ref[...] = jnp.dot(a_ref[...], w_ref[...], preferred_element_type=F32).astype(o_ref.dtype)


def _matmul(a, w, out_dtype, tm, tn, name):
    n, k = a.shape
    m = w.shape[1]
    return pl.pallas_call(
        _mm_kernel,
        grid=(n // tm, m // tn),
        in_specs=[pl.BlockSpec((tm, k), lambda i, j: (i, 0)), pl.BlockSpec((k, tn), lambda i, j: (0, j))],
        out_specs=pl.BlockSpec((tm, tn), lambda i, j: (i, j)),
        out_shape=jax.ShapeDtypeStruct((n, m), out_dtype),
        compiler_params=_params("parallel", "parallel"),
        name=name,
    )(a, w)


def _mm_kv_kernel(a_ref, w_ref, o_ref, kv_ref):
    acc = jnp.dot(a_ref[...], w_ref[...], preferred_element_type=F32)
    o_ref[...] = acc
    kv_ref[...] = acc[:, :2 * D_KV].astype(kv_ref.dtype)


def _matmul_kv(a, w, tm):
    n, k = a.shape
    m = w.shape[1]
    return pl.pallas_call(
        _mm_kv_kernel,
        grid=(n // tm,),
        in_specs=[pl.BlockSpec((tm, k), lambda i: (i, 0)), pl.BlockSpec((k, m), lambda i: (0, 0))],
        out_specs=[pl.BlockSpec((tm, m), lambda i: (i, 0)), pl.BlockSpec((tm, 2 * D_KV), lambda i: (i, 0))],
        out_shape=[jax.ShapeDtypeStruct((n, m), F32), jax.ShapeDtypeStruct((n, 2 * D_KV), BF16)],
        compiler_params=_params("parallel"),
        name="proj_kv",
    )(a, w)


def _cast_kernel(x_ref, o_ref):
    o_ref[...] = x_ref[...].astype(o_ref.dtype)


def _cast_bf16(x):
    e, r, c = x.shape
    spec = pl.BlockSpec((None, r, c), lambda i: (i, 0, 0))
    return pl.pallas_call(
        _cast_kernel, grid=(e,), in_specs=[spec], out_specs=spec,
        out_shape=jax.ShapeDtypeStruct(x.shape, BF16), compiler_params=_params("parallel"), name="cast_bf16",
    )(x)


def _softplus(x):
    return jnp.maximum(x, 0.0) + jnp.log1p(jnp.exp(-jnp.abs(x)))


def _sigmoid(x):
    return 1.0 / (1.0 + jnp.exp(-x))


def _rwkv_prep_kernel(z_ref, sh_ref, mu_ref, w0_ref, wdec_ref, a0_ref, wa_ref, wg_ref, kk_ref, ka_ref,
                      r_out, w_out, k_out, v_out, kk_out, a_out, g_out, last_out, carry_ref):
    tb = z_ref.shape[0]

    @pl.when(pl.program_id(1) == 0)
    def _():
        carry_ref[...] = sh_ref[...]

    z = z_ref[...]
    row = lax.broadcasted_iota(jnp.int32, (tb, 1), 0)
    prev = jnp.where(row == 0, carry_ref[...], pltpu.roll(z, 1, axis=0))
    carry_ref[...] = z[tb - 1:tb, :]
    last_out[...] = z[tb - 1:tb, :]
    zm = z + (prev - z) * mu_ref[...]
    r = zm[:, 0:D_A]
    k = zm[:, D_A:2 * D_A]
    v = zm[:, 2 * D_A:3 * D_A]
    o = 3 * D_A
    wl = zm[:, o:o + WL_PAD]
    al = zm[:, o + WL_PAD:o + WL_PAD + AL_PAD]
    gl = zm[:, o + WL_PAD + AL_PAD:]
    lw = jnp.dot(jnp.tanh(wl).astype(BF16), wdec_ref[...], preferred_element_type=F32)
    wv = -_softplus(-(w0_ref[...] + lw)) - 0.5
    a = _sigmoid(a0_ref[...] + jnp.dot(al.astype(BF16), wa_ref[...], preferred_element_type=F32))
    r_out[...] = r
    w_out[...] = jnp.exp(-jnp.exp(wv))
    k_out[...] = k * (1.0 + (a - 1.0) * ka_ref[...])
    v_out[...] = v
    kk_out[...] = k * kk_ref[...]
    a_out[...] = a
    g_out[...] = jnp.dot(_sigmoid(gl).astype(BF16), wg_ref[...], preferred_element_type=F32)


def _rwkv_prep(z, row0, nseq, t, shift0, wts, tb):
    nt = t // tb
    blk0 = row0 // tb
    row = lambda c: pl.BlockSpec((1, c), lambda s, i: (0, 0))
    full = lambda a, b: pl.BlockSpec((a, b), lambda s, i: (0, 0))
    out_spec = pl.BlockSpec((None, tb, D_A), lambda s, i: (s, i, 0))
    out_sds = jax.ShapeDtypeStruct((nseq, t, D_A), F32)
    g_spec = pl.BlockSpec((tb, D_A), lambda s, i: (s * nt + i, 0))
    g_sds = jax.ShapeDtypeStruct((nseq * t, D_A), F32)
    return pl.pallas_call(
        _rwkv_prep_kernel,
        grid=(nseq, nt),
        in_specs=[pl.BlockSpec((tb, A_PAD), lambda s, i: (blk0 + s * nt + i, 0)),
                  pl.BlockSpec((None, 1, A_PAD), lambda s, i: (s, 0, 0)),
                  row(A_PAD), row(D_A), full(WL_PAD, D_A), row(D_A), full(AL_PAD, D_A), full(GL_PAD, D_A),
                  row(D_A), row(D_A)],
        out_specs=[out_spec] * 6 + [g_spec, pl.BlockSpec((None, 1, A_PAD), lambda s, i: (s, 0, 0))],
        out_shape=[out_sds] * 6 + [g_sds, jax.ShapeDtypeStruct((nseq, 1, A_PAD), F32)],
        scratch_shapes=[pltpu.VMEM((1, A_PAD), F32)],
        compiler_params=_params("arbitrary", "arbitrary"),
        name="rwkv_prep",
    )(z, shift0, wts["mu"], wts["w0"], wts["w_dec"], wts["a0"], wts["w_a"], wts["w_g"], wts["k_k"], wts["k_a"])


def _wkv_kernel(r_ref, w_ref, k_ref, v_ref, kk_ref, a_ref, s0_ref, lw_ref, lb_ref, rk_ref,
                y_ref, st_ref, s_ref, ab_ref, tma_ref, tmb_ref, ytm_ref, *, n_tblocks):
    tt = r_ref.shape[1]
    low = lax.broadcasted_iota(jnp.int32, (SEQ_PER_TILE, LANES), 1) < HD_A

    @pl.when(pl.program_id(1) == 0)
    def _():
        s_ref[...] = s0_ref[...]

    n_pairs = tt // 2

    def to_tiles(buf_ref, j):
        j = jnp.minimum(j, n_pairs - 1)
        for ai, ref in enumerate((r_ref, w_ref, k_ref, v_ref, kk_ref, a_ref)):
            x0 = ref[:, 2 * j, :]
            x1 = ref[:, 2 * j + 1, :]
            rows = []
            for h in range(H_A):
                sl = slice((h // 2) * LANES, (h // 2 + 1) * LANES)
                if h % 2 == 0:
                    rows.append(jnp.where(low, x0[:, sl], pltpu.roll(x1[:, sl], HD_A, axis=1)))
                else:
                    rows.append(jnp.where(low, pltpu.roll(x0[:, sl], HD_A, axis=1), x1[:, sl]))
            m = jnp.concatenate(rows, axis=0).T
            buf_ref[ai, 0] = m[:HD_A]
            buf_ref[ai, 1] = m[HD_A:]

    def step(buf_ref, u, t):
        kk = buf_ref[4, u]
        ss = jnp.sum(kk * kk, axis=0, keepdims=True)
        kkn = kk * lax.rsqrt(jnp.maximum(ss, 1e-24))
        ab_ref[0] = -kkn
        ab_ref[1] = kkn * buf_ref[5, u]
        vv = buf_ref[3, u]
        sa = jnp.zeros((HD_A, LANES), F32)
        for k in range(HD_A):
            sa = sa + s_ref[k] * ab_ref[0, k:k + 1, :]
        y = jnp.zeros((HD_A, LANES), F32)
        for k in range(HD_A):
            s_new = (s_ref[k] * buf_ref[1, u, k:k + 1, :] + sa * ab_ref[1, k:k + 1, :]
                     + vv * buf_ref[2, u, k:k + 1, :])
            s_ref[k] = s_new
            y = y + s_new * buf_ref[0, u, k:k + 1, :]
        mean = jnp.mean(y, axis=0, keepdims=True)
        d = y - mean
        var = jnp.mean(d * d, axis=0, keepdims=True)
        yn = d * lax.rsqrt(var + LNX_EPS) * lw_ref[...] + lb_ref[...]
        bonus = jnp.sum(buf_ref[0, u] * buf_ref[2, u] * rk_ref[...], axis=0, keepdims=True) * vv
        ytm_ref[t] = yn + bonus

    to_tiles(tma_ref, 0)

    def two_pairs(m, carry):
        to_tiles(tmb_ref, 2 * m + 1)
        step(tma_ref, 0, 4 * m)
        step(tma_ref, 1, 4 * m + 1)
        to_tiles(tma_ref, 2 * m + 2)
        step(tmb_ref, 0, 4 * m + 2)
        step(tmb_ref, 1, 4 * m + 3)
        return carry

    lax.fori_loop(0, n_pairs // 2, two_pairs, 0)

    for j in range(tt // 2):
        m = jnp.concatenate([ytm_ref[2 * j], ytm_ref[2 * j + 1]], axis=0).T
        for hp in range(H_A // 2):
            even = m[(2 * hp) * SEQ_PER_TILE:(2 * hp + 1) * SEQ_PER_TILE]
            odd = m[(2 * hp + 1) * SEQ_PER_TILE:(2 * hp + 2) * SEQ_PER_TILE]
            cols = slice(hp * LANES, (hp + 1) * LANES)
            y_ref[:, 2 * j, cols] = jnp.where(low, even, pltpu.roll(odd, HD_A, axis=1))
            y_ref[:, 2 * j + 1, cols] = jnp.where(low, pltpu.roll(even, HD_A, axis=1), odd)

    @pl.when(pl.program_id(1) == n_tblocks - 1)
    def _():
        st_ref[...] = s_ref[...]


def _wkv(r, w, k, v, kk, a, s0, lw, lb, rk, tt):
    nseq, t, _ = r.shape
    p = nseq * H_A
    assert tt % 4 == 0 and t % tt == 0 and nseq % SEQ_PER_TILE == 0
    seq = pl.BlockSpec((SEQ_PER_TILE, tt, D_A), lambda g, i: (g, i, 0))
    vec = pl.BlockSpec((HD_A, LANES), lambda g, i: (0, 0))
    st = pl.BlockSpec((HD_A, HD_A, LANES), lambda g, i: (0, 0, g))
    return pl.pallas_call(
        functools.partial(_wkv_kernel, n_tblocks=t // tt),
        grid=(p // LANES, t // tt),
        in_specs=[seq] * 6 + [st, vec, vec, vec],
        out_specs=[seq, st],
        out_shape=[jax.ShapeDtypeStruct((nseq, t, D_A), F32), jax.ShapeDtypeStruct((HD_A, HD_A, p), F32)],
        scratch_shapes=[pltpu.VMEM((HD_A, HD_A, LANES), F32), pltpu.VMEM((2, HD_A, LANES), F32),
                        pltpu.VMEM((6, 2, HD_A, LANES), F32), pltpu.VMEM((6, 2, HD_A, LANES), F32),
                        pltpu.VMEM((tt, HD_A, LANES), F32)],
        compiler_params=_params("arbitrary", "arbitrary"),
        name="wkv_recurrence",
    )(r, w, k, v, kk, a, s0, lw, lb, rk)


def _state_to_tiles(wkv):
    nseq = wkv.shape[0]
    x = wkv.reshape(nseq // SEQ_PER_TILE, SEQ_PER_TILE, H_A, HD_A, HD_A)
    return x.transpose(4, 3, 0, 2, 1).reshape(HD_A, HD_A, nseq * H_A)


def _state_from_tiles(s, nseq):
    x = s.reshape(HD_A, HD_A, nseq // SEQ_PER_TILE, H_A, SEQ_PER_TILE)
    return x.transpose(2, 4, 3, 1, 0).reshape(nseq, H_A, HD_A, HD_A)


def _lane_tile(vec):
    return jnp.repeat(vec.reshape(H_A, HD_A).T, SEQ_PER_TILE, axis=1)


def _dsa_kernel(*refs, q_start, l_valid, topk, keys_on_rows, n_past, lp):
    refs = list(refs)
    q_ref, iq_ref, iw_ref, k_ref, v_ref, ik2_ref, d0_ref, d1_ref = refs[:8]
    del refs[:8]
    kpast_ref = vpast_ref = ktail_ref = vtail_ref = None
    if n_past:
        kpast_ref, vpast_ref = refs[:2]
        del refs[:2]
    o_ref, madd_ref, logit_ref, mx_ref, ls_ref, acc_ref, score_ref, key_ref = refs[:8]
    del refs[:8]
    if not keys_on_rows:
        iwb_ref = refs.pop(0)
    if n_past:
        ktail_ref, vtail_ref = refs
        n_new = k_ref.shape[0]
        for tail_ref, new_ref in ((ktail_ref, k_ref), (vtail_ref, v_ref)):
            tail_ref[...] = jnp.zeros(tail_ref.shape, BF16)
            tail_ref[0:n_new, :] = new_ref[...]
    n_past_tiles = n_past // KEY_TILE

    def kv_tile(new_ref, past_ref, tail_ref, c, cols):
        if not n_past:
            return new_ref[pl.ds(pl.multiple_of(c * KEY_TILE, KEY_TILE), KEY_TILE), cols]
        pc = jnp.minimum(c, n_past_tiles - 1)
        past = past_ref[pl.ds(pl.multiple_of(pc * KEY_TILE, KEY_TILE), KEY_TILE), cols].astype(BF16)
        return jnp.where(c < n_past_tiles, past, tail_ref[:, cols])

    tq = q_ref.shape[0]
    q0 = q_start + pl.program_id(1) * tq
    nt = (((1,), (1,)), ((), ()))
    lane_shift = int(math.log2(LANES))
    chunk_shift = int(math.log2(CHUNK))
    kax = 0 if keys_on_rows else 1
    dc = lax.shift_right_logical(q0, lane_shift)
    n_lane_tiles = jnp.minimum(lax.shift_right_logical(q0 + tq - 1, lane_shift) + 1, lp // LANES)
    n_tiles = lax.shift_right_logical(n_lane_tiles + 1, 1)

    def tile_off(c, width):
        return pl.multiple_of(c * width, width)

    def keys_at(off, width):
        return (pl.ds(off, width), slice(None)) if keys_on_rows else (slice(None), pl.ds(off, width))

    def per_query(x):
        return jnp.sum(x, axis=kax, keepdims=True)

    iw = iw_ref[...] * ((H_I * D_I) ** -0.5)
    if not keys_on_rows:
        for h in range(H_I):
            iwb_ref[h] = jnp.broadcast_to(iw[:, h:h + 1], (tq, LANES))

    def score_lane_tile(c):
        kab = ik2_ref[c]
        acc = [jnp.zeros((LANES, tq) if keys_on_rows else (tq, LANES), F32) for _ in range(2)]
        for hp in range(H_I // 2):
            iq_pair = iq_ref[:, hp * LANES:(hp + 1) * LANES]
            if keys_on_rows:
                d = lax.dot_general(kab, iq_pair, nt, preferred_element_type=F32)
                halves = (d[:LANES], d[LANES:])
                wts = (iw[2 * hp:2 * hp + 1, :], iw[2 * hp + 1:2 * hp + 2, :])
            else:
                d = lax.dot_general(iq_pair, kab, nt, preferred_element_type=F32)
                halves = (d[:, :LANES], d[:, LANES:])
                wts = (iwb_ref[2 * hp], iwb_ref[2 * hp + 1])
            for u in range(2):
                acc[u] = acc[u] + jnp.maximum(halves[u], 0.0) * wts[u]
        score_ref[keys_at(tile_off(c, LANES), LANES)] = acc[0] + acc[1]

    def score_tile(c, carry):
        score_lane_tile(2 * c)
        score_lane_tile(2 * c + 1)
        return carry

    lax.fori_loop(0, n_tiles, score_tile, 0)

    qshape, kshape = ((1, tq), lambda w: (w, 1)) if keys_on_rows else ((tq, 1), lambda w: (1, w))
    qchunk = lax.shift_right_arithmetic(q0 + lax.broadcasted_iota(jnp.int32, qshape, 1 - kax), chunk_shift)

    def visible(lo, width):
        kpos = lo + lax.broadcasted_iota(jnp.int32, kshape(width), kax)
        return (lax.shift_right_arithmetic(kpos, chunk_shift) <= qchunk) & (kpos < l_valid)

    def key_tile(c, carry):
        off = tile_off(c, KEY_TILE)
        bits = pltpu.bitcast(score_ref[keys_at(off, KEY_TILE)] + 0.0, jnp.int32)
        key = bits ^ (lax.shift_right_arithmetic(bits, 31) & jnp.int32(0x7FFFFFFF))
        key_ref[keys_at(off, KEY_TILE)] = jnp.where(visible(off, KEY_TILE), key, jnp.int32(INT_MIN))
        return carry

    lax.fori_loop(0, n_tiles, key_tile, 0)
    kf = jnp.float32(topk)
    sub = 8

    def count(pred):
        def body(c, acc):
            m = jnp.where(pred(key_ref[keys_at(tile_off(c, KEY_TILE), KEY_TILE)]), 1.0, 0.0)
            if keys_on_rows:
                parts = [m[r:r + sub] for r in range(0, KEY_TILE, sub)]
            else:
                parts = [m[:, :LANES], m[:, LANES:]]
            while len(parts) > 1:
                parts = [parts[u] + parts[u + 1] for u in range(0, len(parts), 2)]
            return acc + parts[0]
        acc = lax.fori_loop(0, n_tiles, body, jnp.zeros((sub, tq) if keys_on_rows else (tq, LANES), F32))
        return per_query(acc)

    def search(i, thr):
        cand = thr ^ lax.shift_left(jnp.int32(1), 31 - i)
        return jnp.where(count(lambda kc: kc >= cand) >= kf, cand, thr)

    thr = lax.fori_loop(0, 32, search, jnp.full(qshape, INT_MIN, jnp.int32))

    def store_mask(off, width, sel):
        madd = jnp.where(sel, 0.0, -jnp.inf)
        if keys_on_rows:
            for u in range(width // LANES):
                madd_ref[:, pl.ds(pl.multiple_of(off + u * LANES, LANES), LANES)] = madd[u * LANES:(u + 1) * LANES].T
        else:
            madd_ref[:, pl.ds(off, width)] = madd

    surplus = (count(lambda kc: kc >= thr) > kf) & (thr != jnp.int32(INT_MIN))
    any_surplus = jnp.max(jnp.where(surplus, 1.0, 0.0))

    @pl.when(any_surplus == 0.0)
    def _():
        def select_tile(c, carry):
            off = tile_off(c, KEY_TILE)
            store_mask(off, KEY_TILE, visible(off, KEY_TILE) & (key_ref[keys_at(off, KEY_TILE)] >= thr))
            return carry

        lax.fori_loop(0, n_tiles, select_tile, 0)

    @pl.when(any_surplus > 0.0)
    def _():
        need = kf - count(lambda kc: kc > thr)
        ii = lax.broadcasted_iota(jnp.int32, (LANES, LANES), 0)
        jj = lax.broadcasted_iota(jnp.int32, (LANES, LANES), 1)
        tri = jnp.where((ii >= jj) if keys_on_rows else (ii <= jj), 1.0, 0.0).astype(BF16)

        def select_tile(c, run):
            off = tile_off(c, LANES)
            keyc = key_ref[keys_at(off, LANES)]
            eqf = jnp.where(keyc == thr, 1.0, 0.0)
            if keys_on_rows:
                within = jnp.dot(tri, eqf.astype(BF16), preferred_element_type=F32)
                total = within[LANES - 1:LANES, :]
            else:
                within = jnp.dot(eqf.astype(BF16), tri, preferred_element_type=F32)
                total = within[:, LANES - 1:LANES]
            take = jnp.where(keyc > thr, 1.0, jnp.where(run + within <= need, eqf, 0.0))
            store_mask(off, LANES, visible(off, LANES) & (take > 0.5))
            return run + total

        lax.fori_loop(0, 2 * n_tiles, select_tile, jnp.zeros(qshape, F32))

    off0 = tile_off(dc, LANES)
    off1 = tile_off(jnp.maximum(dc - 1, 0), LANES)
    for g in range(N_KV_B):
        ksl = slice(g * HD_B, (g + 1) * HD_B)
        qs = jnp.concatenate([q_ref[:, (g * G_B + j) * HD_B:(g * G_B + j + 1) * HD_B] for j in range(G_B)], axis=0)

        def logits_tile(c, carry):
            off = tile_off(c, KEY_TILE)
            s = lax.dot_general(qs, kv_tile(k_ref, kpast_ref, ktail_ref, c, ksl), nt, preferred_element_type=F32)
            md = madd_ref[:, pl.ds(off, KEY_TILE)]
            for j in range(G_B):
                logit_ref[j * tq:(j + 1) * tq, pl.ds(off, KEY_TILE)] = s[j * tq:(j + 1) * tq] * (HD_B ** -0.5) + md
            return carry

        lax.fori_loop(0, n_tiles, logits_tile, 0)
        for j in range(G_B):
            logit_ref[j * tq:(j + 1) * tq, pl.ds(off0, LANES)] += d0_ref[g * G_B + j]

        @pl.when(dc > 0)
        def _():
            for j in range(G_B):
                logit_ref[j * tq:(j + 1) * tq, pl.ds(off1, LANES)] += d1_ref[g * G_B + j]

        mx_ref[...] = jnp.full(mx_ref.shape, -jnp.inf, F32)

        def max_tile(c, carry):
            lg = logit_ref[:, pl.ds(tile_off(c, KEY_TILE), KEY_TILE)]
            mx_ref[...] = jnp.maximum(mx_ref[...], jnp.maximum(lg[:, :LANES], lg[:, LANES:]))
            return carry

        lax.fori_loop(0, n_tiles, max_tile, 0)
        m = jnp.max(mx_ref[...], axis=1, keepdims=True)
        ls_ref[...] = jnp.zeros(ls_ref.shape, F32)
        acc_ref[...] = jnp.zeros(acc_ref.shape, F32)

        def pv_tile(c, carry):
            off = tile_off(c, KEY_TILE)
            p = jnp.exp(logit_ref[:, pl.ds(off, KEY_TILE)] - m)
            ls_ref[...] += p[:, :LANES] + p[:, LANES:]
            acc_ref[...] += jnp.dot(p.astype(BF16), kv_tile(v_ref, vpast_ref, vtail_ref, c, ksl),
                                    preferred_element_type=F32)
            return carry

        lax.fori_loop(0, n_tiles, pv_tile, 0)
        res = acc_ref[...] / jnp.sum(ls_ref[...], axis=1, keepdims=True)
        for j in range(G_B):
            h = g * G_B + j
            o_ref[:, h * HD_B:(h + 1) * HD_B] = res[j * tq:(j + 1) * tq].astype(o_ref.dtype)


def _t5_bucket(rel):
    half = REL_BUCKETS // 2
    exact = half // 2
    side = jnp.where(rel > 0, half, 0)
    n = jnp.abs(rel)
    nf = jnp.maximum(n, 1).astype(F32)
    large = exact + (jnp.log(nf / exact) / math.log(REL_MAX_DIST / exact) * (half - exact)).astype(jnp.int32)
    large = jnp.minimum(large, half - 1)
    return side + jnp.where(n < exact, n, large)


def _dsa(zq, kv_new, row0, b, t, iw, ik_all, rel_bias, tq, q_start, l_valid, k_past=None, v_past=None):
    n_past = 0 if k_past is None else k_past.shape[1]
    lp = ik_all.shape[1]
    topk = min(TOPK_MAX, l_valid // 4)
    assert q_start % LANES == 0 and (tq == LANES or t == tq) and lp % KEY_TILE == 0 and row0 % tq == 0
    assert q_start == n_past and n_past % KEY_TILE == 0 and row0 % t == 0
    assert lp == (n_past + KEY_TILE if n_past else t) and l_valid == n_past + t and (not n_past or t <= KEY_TILE)
    nq = t // tq
    blk0 = row0 // tq
    ikt = ik_all.reshape(b, lp // LANES, LANES, D_I)
    zeros = jnp.zeros_like(ikt)
    ik2 = jnp.concatenate([jnp.concatenate([ikt, zeros], axis=-1), jnp.concatenate([zeros, ikt], axis=-1)], axis=2)
    ji = jnp.arange(LANES, dtype=jnp.int32)[None, :] - jnp.arange(tq, dtype=jnp.int32)[:, None]

    def bias_of(rel):
        hit = _t5_bucket(rel)[None, None] == jnp.arange(REL_BUCKETS, dtype=jnp.int32).reshape((1, -1) + (1,) * rel.ndim)
        return jnp.sum(jnp.where(hit, rel_bias.T.reshape((H_B, REL_BUCKETS) + (1,) * rel.ndim), 0.0), axis=1)

    far = bias_of(jnp.full((1, 1), -2 * LANES, jnp.int32))
    d0 = bias_of(ji) - far
    d1 = bias_of(ji - LANES) - far
    keys_on_rows = tq == LANES
    kern = functools.partial(_dsa_kernel, q_start=q_start, l_valid=l_valid, topk=topk, keys_on_rows=keys_on_rows,
                             n_past=n_past, lp=lp)
    qspec = lambda col: pl.BlockSpec((tq, D_B), lambda bi, i: (blk0 + bi * nq + i, col))
    new_spec = lambda col: pl.BlockSpec((t, D_KV), lambda bi, i: (row0 // t + bi, col))
    cspec = pl.BlockSpec((H_B, tq, LANES), lambda bi, i: (0, 0, 0))
    rows = G_B * tq
    if keys_on_rows:
        iw, iw_spec = iw.T, pl.BlockSpec((H_I, tq), lambda bi, i: (0, bi * nq + i))
        mask_scratch = [pltpu.VMEM((lp, tq), F32), pltpu.VMEM((lp, tq), jnp.int32)]
    else:
        iw_spec = pl.BlockSpec((tq, H_I), lambda bi, i: (bi * nq + i, 0))
        mask_scratch = [pltpu.VMEM((tq, lp), F32), pltpu.VMEM((tq, lp), jnp.int32), pltpu.VMEM((H_I, tq, LANES), F32)]
    past_specs, past_args, tail_scratch = [], [], []
    if n_past:
        past_specs = [pl.BlockSpec((None, n_past, D_KV), lambda bi, i: (bi, 0, 0))] * 2
        past_args = [k_past, v_past]
        tail_scratch = [pltpu.VMEM((KEY_TILE, D_KV), BF16)] * 2
    return pl.pallas_call(
        kern,
        grid=(b, nq),
        in_specs=[qspec(0), qspec(1), iw_spec, new_spec(0), new_spec(1),
                  pl.BlockSpec((None, lp // LANES, 2 * LANES, LANES), lambda bi, i: (bi, 0, 0, 0)), cspec, cspec]
        + past_specs,
        out_specs=pl.BlockSpec((tq, D_B), lambda bi, i: (bi * nq + i, 0)),
        out_shape=jax.ShapeDtypeStruct((b * t, D_B), BF16),
        scratch_shapes=[pltpu.VMEM((tq, lp), F32), pltpu.VMEM((rows, lp), F32), pltpu.VMEM((rows, LANES), F32),
                        pltpu.VMEM((rows, LANES), F32), pltpu.VMEM((rows, HD_B), F32)] + mask_scratch + tail_scratch,
        compiler_params=_params("parallel", "arbitrary"),
        name="dsa_attention",
    )(zq, zq, iw, kv_new, kv_new, ik2, d0, d1, *past_args)


def _mem_kernel(q_ref, k_ref, v_ref, o_ref):
    nt = (((1,), (1,)), ((), ()))
    for h in range(H_M):
        sl = slice(h * HD_M, (h + 1) * HD_M)
        s = lax.dot_general(q_ref[:, sl], k_ref[:, sl], nt, preferred_element_type=F32) * (HD_M ** -0.5)
        m = jnp.max(s, axis=1, keepdims=True)
        p = jnp.exp(s - m)
        den = jnp.sum(p, axis=1, keepdims=True)
        o = jnp.dot(p.astype(BF16), v_ref[:, sl], preferred_element_type=F32)
        o_ref[:, sl] = (o / den).astype(o_ref.dtype)


def _mem_attention(zq, row0, b, t, mk, mv, tq):
    nq = t // tq
    blk0 = row0 // tq
    return pl.pallas_call(
        _mem_kernel,
        grid=(b, nq),
        in_specs=[pl.BlockSpec((tq, D_M), lambda bi, i: (blk0 + bi * nq + i, 2)),
                  pl.BlockSpec((None, N_MEM, D_M), lambda bi, i: (bi, 0, 0)),
                  pl.BlockSpec((None, N_MEM, D_M), lambda bi, i: (bi, 0, 0))],
        out_specs=pl.BlockSpec((tq, D_M), lambda bi, i: (bi * nq + i, 0)),
        out_shape=jax.ShapeDtypeStruct((b * t, D_M), BF16),
        compiler_params=_params("parallel", "parallel"),
        name="memory_attention",
    )(zq, mk, mv)


def _mix_kernel(yap_ref, yas_ref, gp_ref, gs_ref, obp_ref, obs_ref, omp_ref, oms_ref, ga_ref, gb_ref, gm_ref,
                pa_ref, pb_ref, pm_ref, o_ref, *, na):
    i = pl.program_id(1)
    oa = (_pick(i, na, yap_ref, yas_ref) * _pick(i, na, gp_ref, gs_ref)).astype(BF16)
    acc = _sigmoid(ga_ref[...]) * jnp.dot(oa, pa_ref[...], preferred_element_type=F32)
    acc = acc + _sigmoid(gb_ref[...]) * jnp.dot(_pick(i, na, obp_ref, obs_ref), pb_ref[...],
                                                preferred_element_type=F32)
    acc = acc + _sigmoid(gm_ref[...]) * jnp.dot(_pick(i, na, omp_ref, oms_ref), pm_ref[...],
                                                preferred_element_type=F32)
    o_ref[...] = acc.astype(o_ref.dtype)


def _mix(ya2, g2, ob2, om2, zg, pa, pb, pm, tm, tn):
    n = zg.shape[0]
    nj = D_MODEL // tn
    na = g2[0].shape[0] // tm
    two = lambda w: (pl.BlockSpec((tm, w), lambda j, i: (jnp.minimum(i, na - 1), 0)),
                     pl.BlockSpec((tm, w), lambda j, i: (jnp.maximum(i - na, 0), 0)))
    gate = lambda br: pl.BlockSpec((tm, tn), lambda j, i, br=br: (i, br * nj + j))
    wt = pl.BlockSpec((D_A, tn), lambda j, i: (0, j))
    return pl.pallas_call(
        functools.partial(_mix_kernel, na=na),
        grid=(nj, n // tm),
        in_specs=[*two(D_A), *two(D_A), *two(D_B), *two(D_M), gate(0), gate(1), gate(2), wt, wt, wt],
        out_specs=pl.BlockSpec((tm, tn), lambda j, i: (i, j)),
        out_shape=jax.ShapeDtypeStruct((n, D_MODEL), BF16),
        compiler_params=_params("parallel", "parallel"),
        name="branch_mix",
    )(*ya2, *g2, *ob2, *om2, zg, zg, zg, pa, pb, pm)


def _wo_kernel(m_ref, w_ref, xa_ref, xb_ref, g_ref, h_ref, u_ref, *, na):
    x = _pick(pl.program_id(0), na, xa_ref, xb_ref)
    h = x + jnp.dot(m_ref[...], w_ref[...], preferred_element_type=F32)
    h_ref[...] = h
    y = h * lax.rsqrt(jnp.mean(h * h, axis=-1, keepdims=True) + NORM_EPS)
    u_ref[...] = (y * g_ref[...]).astype(u_ref.dtype)


def _wo_residual_norm(mixed, w_o, xa, xb, g, tm):
    n = mixed.shape[0]
    na = xa.shape[0] // tm
    tile = pl.BlockSpec((tm, D_MODEL), lambda i: (i, 0))
    sa, sb = _two_part_specs(tm, D_MODEL, na)
    return pl.pallas_call(
        functools.partial(_wo_kernel, na=na),
        grid=(n // tm,),
        in_specs=[tile, pl.BlockSpec((D_MODEL, D_MODEL), lambda i: (0, 0)), sa, sb,
                  pl.BlockSpec((1, D_MODEL), lambda i: (0, 0))],
        out_specs=[tile, tile],
        out_shape=[jax.ShapeDtypeStruct((n, D_MODEL), F32), jax.ShapeDtypeStruct((n, D_MODEL), BF16)],
        compiler_params=_params("parallel"),
        name="wo_residual_norm",
    )(mixed, w_o, xa, xb, g.reshape(1, D_MODEL))


def _moe_kernel(tg_ref, nu_ref, x_ref, gate_ref, w1_ref, w3_ref, w2_ref, o_ref):
    i, j = pl.program_id(0), pl.program_id(1)

    @pl.when((i >= nu_ref[0]) & (j == 0))
    def _():
        o_ref[...] = jnp.zeros(o_ref.shape, F32)

    @pl.when(i < nu_ref[0])
    def _():
        x = x_ref[...]
        a = jnp.dot(x, w1_ref[...], preferred_element_type=F32)
        b = jnp.dot(x, w3_ref[...], preferred_element_type=F32)
        lane = lax.broadcasted_iota(jnp.int32, gate_ref.shape, 1)
        ge = jnp.sum(jnp.where(lane == j, gate_ref[...], 0.0), axis=1, keepdims=True)
        hid = (a * _sigmoid(a)) * b * ge
        out = jnp.dot(hid.astype(BF16), w2_ref[...], preferred_element_type=F32)

        @pl.when(j == 0)
        def _():
            o_ref[...] = out

        @pl.when(j > 0)
        def _():
            o_ref[...] += out


def _moe_grouped(x_sorted, gate_sorted, tile_group, n_used, w1, w3, w2):
    npad = x_sorted.shape[0]
    ew = lambda i, j, tg, nu: (tg[i] * EXPERTS_PER_GROUP + j, 0, 0)
    return pl.pallas_call(
        _moe_kernel,
        grid_spec=pltpu.PrefetchScalarGridSpec(
            num_scalar_prefetch=2,
            grid=(npad // MOE_TM, EXPERTS_PER_GROUP),
            in_specs=[pl.BlockSpec((MOE_TM, D_MODEL), lambda i, j, tg, nu: (i, 0)),
                      pl.BlockSpec((MOE_TM, EXPERTS_PER_GROUP), lambda i, j, tg, nu: (i, 0)),
                      pl.BlockSpec((None, D_MODEL, D_EXPERT), ew), pl.BlockSpec((None, D_MODEL, D_EXPERT), ew),
                      pl.BlockSpec((None, D_EXPERT, D_MODEL), ew)],
            out_specs=pl.BlockSpec((MOE_TM, D_MODEL), lambda i, j, tg, nu: (i, 0))),
        out_shape=jax.ShapeDtypeStruct((npad, D_MODEL), F32),
        compiler_params=_params("parallel", "arbitrary"),
        name="moe_grouped",
    )(tile_group, n_used, x_sorted, gate_sorted, w1, w3, w2)


def _route(logits, b_grp, b_rt):
    n = logits.shape[0]
    g_logits = logits[:, :N_GROUPS] + b_grp
    e_logits = (logits[:, N_GROUPS:N_GROUPS + N_EXPERTS] + b_rt).reshape(n, N_GROUPS, EXPERTS_PER_GROUP)
    g_idx = jnp.argmax(g_logits, axis=-1).astype(jnp.int32)
    g_w = jnp.max(jax.nn.softmax(g_logits, axis=-1), axis=-1, keepdims=True)
    onehot_g = g_idx[:, None] == jnp.arange(N_GROUPS, dtype=jnp.int32)[None, :]
    e_in = jnp.sum(jnp.where(onehot_g[:, :, None], e_logits, 0.0), axis=1)
    lane = jnp.arange(EXPERTS_PER_GROUP, dtype=jnp.int32)[None, :]
    i1 = jnp.argmax(e_in, axis=-1)[:, None]
    v1 = jnp.max(e_in, axis=-1, keepdims=True)
    rest = jnp.where(lane == i1, -jnp.inf, e_in)
    i2 = jnp.argmax(rest, axis=-1)[:, None]
    v2 = jnp.max(rest, axis=-1, keepdims=True)
    w12 = g_w * jax.nn.softmax(jnp.concatenate([v1, v2], axis=-1), axis=-1)
    gate4 = jnp.where(lane == i1, w12[:, 0:1], 0.0) + jnp.where(lane == i2, w12[:, 1:2], 0.0)

    counts = jnp.sum(onehot_g.astype(jnp.int32), axis=0)
    tiles = (counts + MOE_TM - 1) // MOE_TM
    tile_end = jnp.cumsum(tiles)
    slot0 = (tile_end - tiles) * MOE_TM
    start = jnp.cumsum(counts) - counts
    order = jnp.argsort(g_idx, stable=True).astype(jnp.int32)
    rank = jnp.sum(jnp.where(onehot_g, jnp.cumsum(onehot_g.astype(jnp.int32), axis=0) - 1, 0), axis=1)
    inv = jnp.sum(jnp.where(onehot_g, slot0[None, :], 0), axis=1) + rank
    npad = n + N_GROUPS * MOE_TM
    slots = jnp.arange(npad, dtype=jnp.int32)
    slot_group = jnp.minimum(jnp.sum(jnp.where(slots[:, None] >= tile_end[None, :] * MOE_TM, 1, 0), axis=1),
                             N_GROUPS - 1)
    in_group = slot_group[:, None] == jnp.arange(N_GROUPS, dtype=jnp.int32)[None, :]
    pos = slots - jnp.sum(jnp.where(in_group, slot0[None, :], 0), axis=1)
    used = pos < jnp.sum(jnp.where(in_group, counts[None, :], 0), axis=1)
    src = jnp.where(used, pos + jnp.sum(jnp.where(in_group, start[None, :], 0), axis=1), 0)
    perm = order[src]
    gate_sorted = jnp.where(used[:, None], gate4[perm], 0.0)
    tile_group = slot_group[::MOE_TM].astype(jnp.int32)
    return perm, inv, gate_sorted, tile_group, tile_end[-1:].astype(jnp.int32)


def _final_kernel(h_ref, m_ref, g_ref, yp_ref, ys_ref, *, na):
    h = h_ref[...] + m_ref[...]
    y = h * lax.rsqrt(jnp.mean(h * h, axis=-1, keepdims=True) + NORM_EPS) * g_ref[...]
    i = pl.program_id(0)

    @pl.when(i < na)
    def _():
        yp_ref[...] = y

    @pl.when(i >= na)
    def _():
        ys_ref[...] = y


def _final_norm(h, moe, g, n_p, tm):
    n = h.shape[0]
    na = n_p // tm
    tile = pl.BlockSpec((tm, D_MODEL), lambda i: (i, 0))
    sa, sb = _two_part_specs(tm, D_MODEL, na)
    return pl.pallas_call(
        functools.partial(_final_kernel, na=na),
        grid=(n // tm,),
        in_specs=[tile, tile, pl.BlockSpec((1, D_MODEL), lambda i: (0, 0))],
        out_specs=[sa, sb],
        out_shape=[jax.ShapeDtypeStruct((n_p, D_MODEL), F32), jax.ShapeDtypeStruct((n - n_p, D_MODEL), F32)],
        compiler_params=_params("arbitrary"),
        name="residual_final_norm",
    )(h, moe, g.reshape(1, D_MODEL))


def _pad_cols(w, width):
    return jnp.pad(w, ((0, 0), (0, width - w.shape[1])))


def _pad_rows(w, height):
    return jnp.pad(w, ((0, height - w.shape[0]), (0, 0)))


def _pad_lora(x):
    o = 3 * D_A
    pad = lambda t, w: jnp.pad(t, [(0, 0)] * (t.ndim - 1) + [(0, w - t.shape[-1])])
    return jnp.concatenate([x[..., :o], pad(x[..., o:o + W_LORA], WL_PAD),
                            pad(x[..., o + W_LORA:o + W_LORA + A_LORA], AL_PAD),
                            pad(x[..., o + W_LORA + A_LORA:], GL_PAD)], axis=-1)


def _unpad_lora(x):
    o = 3 * D_A
    return jnp.concatenate([x[..., :o], x[..., o:o + W_LORA], x[..., o + WL_PAD:o + WL_PAD + A_LORA],
                            x[..., o + WL_PAD + AL_PAD:o + WL_PAD + AL_PAD + G_LORA]], axis=-1)


def kernel(x_prompt, x_sample, cache_dsa_k, cache_dsa_v, cache_idx_k, state_rwkv_shift, state_rwkv_wkv, cache_mem_k, cache_mem_v, mem_prompt, rel_bias, g_attn, w_in, rwkv_mu, rwkv_w0, rwkv_w_dec, rwkv_a0, rwkv_w_a, rwkv_w_g, rwkv_k_k, rwkv_k_a, rwkv_r_k, rwkv_lnx_w, rwkv_lnx_b, g_mem, w_mem_kv, p_a, p_b, p_m, w_o, g_ffn, w_grp, b_grp, w_rt, b_rt, w1, w3, w2, g_final):
    assert w_in.shape[0] == 1, "single layer"
    bp, tp, _ = x_prompt.shape
    bs, ts, _ = x_sample.shape
    past = cache_dsa_k.shape[2]
    n_p, n_s = bp * tp, bs * ts
    n = n_p + n_s
    tm = 1024
    xp = x_prompt.reshape(n_p, D_MODEL)
    xs = x_sample.reshape(n_s, D_MODEL)

    offs = [0]
    for s in IN_SIZES:
        offs.append(offs[-1] + s)
    seg = lambda i: w_in[0][:, offs[i]:offs[i + 1]]
    w_a_cols = _pad_lora(seg(0)).astype(BF16)
    w_qim = jnp.concatenate([seg(1), seg(4), seg(7)], axis=1).astype(BF16)
    w_kvi = _pad_cols(jnp.concatenate([seg(2), seg(3), seg(5), seg(6)], axis=1), KVI_PAD).astype(BF16)
    w_gate = seg(8).astype(BF16)
    rw = dict(mu=_pad_lora(rwkv_mu[0])[None], w0=rwkv_w0, a0=rwkv_a0, k_k=rwkv_k_k, k_a=rwkv_k_a,
              w_dec=_pad_rows(rwkv_w_dec[0], WL_PAD).astype(BF16), w_a=_pad_rows(rwkv_w_a[0], AL_PAD).astype(BF16),
              w_g=_pad_rows(rwkv_w_g[0], GL_PAD).astype(BF16))

    u = _rmsnorm_bf16(xp, xs, g_attn[0], 512)
    z_a = _matmul(u, w_a_cols, F32, tm, A_PAD // 4, "proj_rwkv")
    z_qim = _matmul(u, w_qim, BF16, tm, 1024, "proj_queries")
    z_kvi, kv_bf = _matmul_kv(u, w_kvi, tm)
    z_g = _matmul(u, w_gate, F32, tm, 1024, "proj_gates")

    mem = mem_prompt.reshape(bp * N_MEM, D_MODEL)
    um = _rmsnorm_bf16(mem[:bp * N_MEM // 2], mem[bp * N_MEM // 2:], g_mem[0], 512)
    mkv = _matmul(um, w_mem_kv[0].astype(BF16), F32, 1024, 1024, "proj_mem_kv")
    mk_p = mkv[:, :D_M].reshape(bp, N_MEM, D_M)
    mv_p = mkv[:, D_M:].reshape(bp, N_MEM, D_M)

    lw_vec, lb_vec, rk_vec = rwkv_lnx_w[0], rwkv_lnx_b[0], rwkv_r_k[0].reshape(D_A)

    def rwkv_group(row0, nseq, t, shift0, wkv0, tb, tt):
        r, w, k, v, kk, a, g, last = _rwkv_prep(z_a, row0, nseq, t, shift0, rw, tb)
        y, s_fin = _wkv(r, w, k, v, kk, a, _state_to_tiles(wkv0), _lane_tile(lw_vec), _lane_tile(lb_vec),
                        _lane_tile(rk_vec), tt)
        return y.reshape(nseq * t, D_A), g, _state_from_tiles(s_fin, nseq), _unpad_lora(last)

    ya_p, g_p, wkv_p, shift_p = rwkv_group(0, bp, tp, jnp.zeros((bp, 1, A_PAD), F32),
                                           jnp.zeros((bp, H_A, HD_A, HD_A), F32), 256, 32)
    ya_s, g_s, wkv_s, shift_s = rwkv_group(n_p, bs, ts, _pad_lora(state_rwkv_shift[0]), state_rwkv_wkv[0], ts, ts)

    k_new, v_new = z_kvi[:, :D_KV], z_kvi[:, D_KV:2 * D_KV]
    ik_new = z_kvi[:, 2 * D_KV:2 * D_KV + D_I]
    iw = z_kvi[:, 2 * D_KV + D_I:2 * D_KV + D_I + H_I]
    grp = lambda t, sl, b, tlen: t[sl].reshape(b, tlen, t.shape[-1])
    sp, ss = slice(0, n_p), slice(n_p, n)
    ob_p = _dsa(z_qim, kv_bf, 0, bp, tp, iw[sp], grp(ik_new, sp, bp, tp).astype(BF16), rel_bias,
                Q_BLOCK if tp % Q_BLOCK == 0 else tp, 0, tp)
    l_s = past + ts
    ik_s = jnp.pad(jnp.concatenate([cache_idx_k[0].astype(BF16), grp(ik_new, ss, bs, ts).astype(BF16)], axis=1),
                   ((0, 0), (0, past + KEY_TILE - l_s), (0, 0)))
    ob_s = _dsa(z_qim, kv_bf, n_p, bs, ts, iw[ss], ik_s, rel_bias, Q_BLOCK if ts % Q_BLOCK == 0 else ts, past, l_s,
                cache_dsa_k[0].reshape(bs, past, D_KV), cache_dsa_v[0].reshape(bs, past, D_KV))

    om_p = _mem_attention(z_qim, 0, bp, tp, mk_p.astype(BF16), mv_p.astype(BF16), 256)
    om_s = _mem_attention(z_qim, n_p, bs, ts, cache_mem_k[0].reshape(bs, N_MEM, D_M).astype(BF16),
                          cache_mem_v[0].reshape(bs, N_MEM, D_M).astype(BF16), ts)

    mixed = _mix((ya_p, ya_s), (g_p, g_s), (ob_p, ob_s), (om_p, om_s), z_g, p_a[0].astype(BF16), p_b[0].astype(BF16),
                 p_m[0].astype(BF16), 256, 1024)
    h, u2 = _wo_residual_norm(mixed, w_o[0].astype(BF16), xp, xs, g_ffn[0], 256)

    w_route = _pad_cols(jnp.concatenate([w_grp[0], w_rt[0]], axis=1), LANES).astype(BF16)
    logits = _matmul(u2, w_route, F32, tm, LANES, "moe_router")
    perm, inv, gate_sorted, tile_group, n_used = _route(logits, b_grp[0], b_rt[0])
    moe_sorted = _moe_grouped(u2[perm], gate_sorted, tile_group, n_used,
                              _cast_bf16(w1[0]), _cast_bf16(w3[0]), _cast_bf16(w2[0]))
    y_p, y_s = _final_norm(h, moe_sorted[inv], g_final, n_p, 512)

    st = lambda t, b, tlen, shape: t.reshape((1, b, tlen) + shape)
    return (y_p.reshape(bp, tp, D_MODEL), y_s.reshape(bs, ts, D_MODEL),
            st(k_new[sp], bp, tp, (N_KV_B, HD_B)), st(v_new[sp], bp, tp, (N_KV_B, HD_B)), st(ik_new[sp], bp, tp, (D_I,)),
            shift_p[None], wkv_p[None],
            mk_p.reshape(1, bp, N_MEM, H_M, HD_M), mv_p.reshape(1, bp, N_MEM, H_M, HD_M),
            st(k_new[ss], bs, ts, (N_KV_B, HD_B)), st(v_new[ss], bs, ts, (N_KV_B, HD_B)), st(ik_new[ss], bs, ts, (D_I,)),
            shift_s[None], wkv_s[None])
```

```python
import functools
import math

import jax
import jax.numpy as jnp
from jax import lax
from jax.experimental import pallas as pl
from jax.experimental.pallas import tpu as pltpu

F32 = jnp.float32
BF16 = jnp.bfloat16

LANES = 128
D_MODEL = 2048
CHUNK = 64
NORM_EPS = 1e-6
H_A, HD_A = 16, 64
D_A = H_A * HD_A
W_LORA, A_LORA, G_LORA = 64, 64, 160
LNX_EPS = 64e-5
A_COLS = 3 * D_A + W_LORA + A_LORA + G_LORA
H_B, N_KV_B, HD_B = 8, 2, 128
G_B = H_B // N_KV_B
D_B = H_B * HD_B
D_KV = N_KV_B * HD_B
H_I, D_I = 16, 64
TOPK_MAX = 256
Q_BLOCK = 128
REL_BUCKETS = 32
REL_MAX_DIST = 128
N_MEM, H_M, HD_M = 256, 4, 256
D_M = H_M * HD_M
N_BRANCH = 3
IN_SIZES = (A_COLS, D_B, D_KV, D_KV, H_I * D_I, D_I, H_I, D_M, N_BRANCH * D_MODEL)
N_GROUPS, EXPERTS_PER_GROUP = 4, 4
N_EXPERTS = N_GROUPS * EXPERTS_PER_GROUP
TOP_K_INNER = 2
D_EXPERT = 512

WL_PAD, AL_PAD, GL_PAD = 128, 128, 256
A_PAD = 3 * D_A + WL_PAD + AL_PAD + GL_PAD
KVI_PAD = 2 * D_KV + LANES
KEY_TILE = 2 * LANES
SEQ_PER_TILE = LANES // H_A
MOE_TM = 512
INT_MIN = -(2 ** 31)
VMEM_LIMIT = 48 * 1024 * 1024


def _params(*sem):
    return pltpu.CompilerParams(dimension_semantics=sem, vmem_limit_bytes=VMEM_LIMIT)


def _two_part_specs(tm, width, na):
    return (pl.BlockSpec((tm, width), lambda i: (jnp.minimum(i, na - 1), 0)),
            pl.BlockSpec((tm, width), lambda i: (jnp.maximum(i - na, 0), 0)))


def _pick(i, na, a_ref, b_ref):
    return jnp.where(i < na, a_ref[...], b_ref[...])


def _norm2_kernel(xa_ref, xb_ref, g_ref, o_ref, *, na):
    x = _pick(pl.program_id(0), na, xa_ref, xb_ref)
    y = x * lax.rsqrt(jnp.mean(x * x, axis=-1, keepdims=True) + NORM_EPS)
    o_ref[...] = (y * g_ref[...]).astype(o_ref.dtype)


def _rmsnorm_bf16(xa, xb, g, tm):
    d = xa.shape[1]
    na, nb = xa.shape[0] // tm, xb.shape[0] // tm
    sa, sb = _two_part_specs(tm, d, na)
    return pl.pallas_call(
        functools.partial(_norm2_kernel, na=na),
        grid=(na + nb,),
        in_specs=[sa, sb, pl.BlockSpec((1, d), lambda i: (0, 0))],
        out_specs=pl.BlockSpec((tm, d), lambda i: (i, 0)),
        out_shape=jax.ShapeDtypeStruct(((na + nb) * tm, d), BF16),
        compiler_params=_params("parallel"),
        name="rmsnorm_bf16",
    )(xa, xb, g.reshape(1, d))


def _mm_kernel(a_ref, w_ref, o_ref):
    o_ref[...] = jnp.dot(a_ref[...], w_ref[...], preferred_element_type=F32).astype(o_ref.dtype)


def _matmul(a, w, out_dtype, tm, tn, name):
    n, k = a.shape
    m = w.shape[1]
    return pl.pallas_call(
        _mm_kernel,
        grid=(n // tm, m // tn),
        in_specs=[pl.BlockSpec((tm, k), lambda i, j: (i, 0)), pl.BlockSpec((k, tn), lambda i, j: (0, j))],
        out_specs=pl.BlockSpec((tm, tn), lambda i, j: (i, j)),
        out_shape=jax.ShapeDtypeStruct((n, m), out_dtype),
        compiler_params=_params("parallel", "parallel"),
        name=name,
    )(a, w)


def _mm_kv_kernel(a_ref, w_ref, o_ref, kv_ref):
    acc = jnp.dot(a_ref[...], w_ref[...], preferred_element_type=F32)
    o_ref[...] = acc
    kv_ref[...] = acc[:, :2 * D_KV].astype(kv_ref.dtype)


def _matmul_kv(a, w, tm):
    n, k = a.shape
    m = w.shape[1]
    return pl.pallas_call(
        _mm_kv_kernel,
        grid=(n // tm,),
        in_specs=[pl.BlockSpec((tm, k), lambda i: (i, 0)), pl.BlockSpec((k, m), lambda i: (0, 0))],
        out_specs=[pl.BlockSpec((tm, m), lambda i: (i, 0)), pl.BlockSpec((tm, 2 * D_KV), lambda i: (i, 0))],
        out_shape=[jax.ShapeDtypeStruct((n, m), F32), jax.ShapeDtypeStruct((n, 2 * D_KV), BF16)],
        compiler_params=_params("parallel"),
        name="proj_kv",
    )(a, w)


def _cast_kernel(x_ref, o_ref):
    o_ref[...] = x_ref[...].astype(o_ref.dtype)


def _cast_bf16(x):
    e, r, c = x.shape
    spec = pl.BlockSpec((None, r, c), lambda i: (i, 0, 0))
    return pl.pallas_call(
        _cast_kernel, grid=(e,), in_specs=[spec], out_specs=spec,
        out_shape=jax.ShapeDtypeStruct(x.shape, BF16), compiler_params=_params("parallel"), name="cast_bf16",
    )(x)


def _softplus(x):
    return jnp.maximum(x, 0.0) + jnp.log1p(jnp.exp(-jnp.abs(x)))


def _sigmoid(x):
    return 0.5 * jnp.tanh(0.5 * x) + 0.5


def _rwkv_prep_kernel(z_ref, sh_ref, mu_ref, w0_ref, wdec_ref, a0_ref, wa_ref, wg_ref, kk_ref, ka_ref,
                      r_out, w_out, k_out, v_out, kk_out, a_out, g_out, last_out, carry_ref):
    tb = z_ref.shape[0]

    @pl.when(pl.program_id(1) == 0)
    def _():
        carry_ref[...] = sh_ref[...]

    z = z_ref[...]
    row = lax.broadcasted_iota(jnp.int32, (tb, 1), 0)
    prev = jnp.where(row == 0, carry_ref[...], pltpu.roll(z, 1, axis=0))
    carry_ref[...] = z[tb - 1:tb, :]
    last_out[...] = z[tb - 1:tb, :]
    zm = z + (prev - z) * mu_ref[...]
    r = zm[:, 0:D_A]
    k = zm[:, D_A:2 * D_A]
    v = zm[:, 2 * D_A:3 * D_A]
    o = 3 * D_A
    wl = zm[:, o:o + WL_PAD]
    al = zm[:, o + WL_PAD:o + WL_PAD + AL_PAD]
    gl = zm[:, o + WL_PAD + AL_PAD:]
    lw = jnp.dot(jnp.tanh(wl).astype(BF16), wdec_ref[...], preferred_element_type=F32)
    wv = -_softplus(-(w0_ref[...] + lw)) - 0.5
    a = _sigmoid(a0_ref[...] + jnp.dot(al.astype(BF16), wa_ref[...], preferred_element_type=F32))
    r_out[...] = r
    w_out[...] = jnp.exp(-jnp.exp(wv))
    k_out[...] = k * (1.0 + (a - 1.0) * ka_ref[...])
    v_out[...] = v
    kk_out[...] = k * kk_ref[...]
    a_out[...] = a
    g_out[...] = jnp.dot(_sigmoid(gl).astype(BF16), wg_ref[...], preferred_element_type=F32)


def _rwkv_prep(z, row0, nseq, t, shift0, wts, tb):
    nt = t // tb
    blk0 = row0 // tb
    row = lambda c: pl.BlockSpec((1, c), lambda s, i: (0, 0))
    full = lambda a, b: pl.BlockSpec((a, b), lambda s, i: (0, 0))
    out_spec = pl.BlockSpec((None, tb, D_A), lambda s, i: (s, i, 0))
    out_sds = jax.ShapeDtypeStruct((nseq, t, D_A), F32)
    g_spec = pl.BlockSpec((tb, D_A), lambda s, i: (s * nt + i, 0))
    g_sds = jax.ShapeDtypeStruct((nseq * t, D_A), F32)
    return pl.pallas_call(
        _rwkv_prep_kernel,
        grid=(nseq, nt),
        in_specs=[pl.BlockSpec((tb, A_PAD), lambda s, i: (blk0 + s * nt + i, 0)),
                  pl.BlockSpec((None, 1, A_PAD), lambda s, i: (s, 0, 0)),
                  row(A_PAD), row(D_A), full(WL_PAD, D_A), row(D_A), full(AL_PAD, D_A), full(GL_PAD, D_A),
                  row(D_A), row(D_A)],
        out_specs=[out_spec] * 6 + [g_spec, pl.BlockSpec((None, 1, A_PAD), lambda s, i: (s, 0, 0))],
        out_shape=[out_sds] * 6 + [g_sds, jax.ShapeDtypeStruct((nseq, 1, A_PAD), F32)],
        scratch_shapes=[pltpu.VMEM((1, A_PAD), F32)],
        compiler_params=_params("arbitrary", "arbitrary"),
        name="rwkv_prep",
    )(z, shift0, wts["mu"], wts["w0"], wts["w_dec"], wts["a0"], wts["w_a"], wts["w_g"], wts["k_k"], wts["k_a"])


def _wkv_kernel(r_ref, w_ref, k_ref, v_ref, kk_ref, a_ref, s0_ref, lw_ref, lb_ref, rk_ref,
                y_ref, st_ref, s_ref, ab_ref, tma_ref, tmb_ref, ytm_ref, *, n_tblocks):
    tt = r_ref.shape[1]
    low = lax.broadcasted_iota(jnp.int32, (SEQ_PER_TILE, LANES), 1) < HD_A

    @pl.when(pl.program_id(1) == 0)
    def _():
        s_ref[...] = s0_ref[...]

    n_pairs = tt // 2

    def to_tiles(buf_ref, j):
        j = jnp.minimum(j, n_pairs - 1)
        for ai, ref in enumerate((r_ref, w_ref, k_ref, v_ref, kk_ref, a_ref)):
            x0 = ref[:, 2 * j, :]
            x1 = ref[:, 2 * j + 1, :]
            rows = []
            for h in range(H_A):
                sl = slice((h // 2) * LANES, (h // 2 + 1) * LANES)
                if h % 2 == 0:
                    rows.append(jnp.where(low, x0[:, sl], pltpu.roll(x1[:, sl], HD_A, axis=1)))
                else:
                    rows.append(jnp.where(low, pltpu.roll(x0[:, sl], HD_A, axis=1), x1[:, sl]))
            m = jnp.concatenate(rows, axis=0).T
            buf_ref[ai, 0] = m[:HD_A]
            buf_ref[ai, 1] = m[HD_A:]

    def step(buf_ref, u, t):
        kk = buf_ref[4, u]
        ss = jnp.sum(kk * kk, axis=0, keepdims=True)
        kkn = kk * lax.rsqrt(jnp.maximum(ss, 1e-24))
        ab_ref[0] = -kkn
        ab_ref[1] = kkn * buf_ref[5, u]
        halves = []
        for rows in (slice(0, HD_A // 2), slice(HD_A // 2, HD_A)):
            vh = buf_ref[3, u, rows, :]
            sa = jnp.zeros((HD_A // 2, LANES), F32)
            for k in range(HD_A):
                sa = sa + s_ref[k, rows, :] * ab_ref[0, k:k + 1, :]
            yh = jnp.zeros((HD_A // 2, LANES), F32)
            for k in range(HD_A):
                s_new = (s_ref[k, rows, :] * buf_ref[1, u, k:k + 1, :] + sa * ab_ref[1, k:k + 1, :]
                         + vh * buf_ref[2, u, k:k + 1, :])
                s_ref[k, rows, :] = s_new
                yh = yh + s_new * buf_ref[0, u, k:k + 1, :]
            halves.append(yh)
        y = jnp.concatenate(halves, axis=0)
        vv = buf_ref[3, u]
        mean = jnp.mean(y, axis=0, keepdims=True)
        d = y - mean
        var = jnp.mean(d * d, axis=0, keepdims=True)
        yn = d * lax.rsqrt(var + LNX_EPS) * lw_ref[...] + lb_ref[...]
        bonus = jnp.sum(buf_ref[0, u] * buf_ref[2, u] * rk_ref[...], axis=0, keepdims=True) * vv
        ytm_ref[t] = yn + bonus

    to_tiles(tma_ref, 0)

    def two_pairs(m, carry):
        to_tiles(tmb_ref, 2 * m + 1)
        step(tma_ref, 0, 4 * m)
        step(tma_ref, 1, 4 * m + 1)
        to_tiles(tma_ref, 2 * m + 2)
        step(tmb_ref, 0, 4 * m + 2)
        step(tmb_ref, 1, 4 * m + 3)
        return carry

    lax.fori_loop(0, n_pairs // 2, two_pairs, 0)

    for j in range(tt // 2):
        m = jnp.concatenate([ytm_ref[2 * j], ytm_ref[2 * j + 1]], axis=0).T
        for hp in range(H_A // 2):
            even = m[(2 * hp) * SEQ_PER_TILE:(2 * hp + 1) * SEQ_PER_TILE]
            odd = m[(2 * hp + 1) * SEQ_PER_TILE:(2 * hp + 2) * SEQ_PER_TILE]
            cols = slice(hp * LANES, (hp + 1) * LANES)
            y_ref[:, 2 * j, cols] = jnp.where(low, even, pltpu.roll(odd, HD_A, axis=1))
            y_ref[:, 2 * j + 1, cols] = jnp.where(low, pltpu.roll(even, HD_A, axis=1), odd)

    @pl.when(pl.program_id(1) == n_tblocks - 1)
    def _():
        st_ref[...] = s_ref[...]


def _wkv(r, w, k, v, kk, a, s0, lw, lb, rk, tt):
    nseq, t, _ = r.shape
    p = nseq * H_A
    assert tt % 4 == 0 and t % tt == 0 and nseq % SEQ_PER_TILE == 0
    seq = pl.BlockSpec((SEQ_PER_TILE, tt, D_A), lambda g, i: (g, i, 0))
    vec = pl.BlockSpec((HD_A, LANES), lambda g, i: (0, 0))
    st = pl.BlockSpec((HD_A, HD_A, LANES), lambda g, i: (0, 0, g))
    return pl.pallas_call(
        functools.partial(_wkv_kernel, n_tblocks=t // tt),
        grid=(p // LANES, t // tt),
        in_specs=[seq] * 6 + [st, vec, vec, vec],
        out_specs=[seq, st],
        out_shape=[jax.ShapeDtypeStruct((nseq, t, D_A), F32), jax.ShapeDtypeStruct((HD_A, HD_A, p), F32)],
        scratch_shapes=[pltpu.VMEM((HD_A, HD_A, LANES), F32), pltpu.VMEM((2, HD_A, LANES), F32),
                        pltpu.VMEM((6, 2, HD_A, LANES), F32), pltpu.VMEM((6, 2, HD_A, LANES), F32),
                        pltpu.VMEM((tt, HD_A, LANES), F32)],
        compiler_params=_params("arbitrary", "arbitrary"),
        name="wkv_recurrence",
    )(r, w, k, v, kk, a, s0, lw, lb, rk)


def _state_to_tiles(wkv):
    nseq = wkv.shape[0]
    x = wkv.reshape(nseq // SEQ_PER_TILE, SEQ_PER_TILE, H_A, HD_A, HD_A)
    return x.transpose(4, 3, 0, 2, 1).reshape(HD_A, HD_A, nseq * H_A)


def _state_from_tiles(s, nseq):
    x = s.reshape(HD_A, HD_A, nseq // SEQ_PER_TILE, H_A, SEQ_PER_TILE)
    return x.transpose(2, 4, 3, 1, 0).reshape(nseq, H_A, HD_A, HD_A)


def _lane_tile(vec):
    return jnp.repeat(vec.reshape(H_A, HD_A).T, SEQ_PER_TILE, axis=1)


def _dsa_kernel(*refs, q_start, l_valid, topk, keys_on_rows, n_past, lp):
    refs = list(refs)
    q_ref, iq_ref, iw_ref, k_ref, v_ref, ik2_ref, d0_ref, d1_ref = refs[:8]
    del refs[:8]
    kpast_ref = vpast_ref = ktail_ref = vtail_ref = None
    if n_past:
        kpast_ref, vpast_ref = refs[:2]
        del refs[:2]
    o_ref, madd_ref, logit_ref, mx_ref, ls_ref, acc_ref, score_ref, key_ref = refs[:8]
    del refs[:8]
    if not keys_on_rows:
        iwb_ref = refs.pop(0)
    if n_past:
        ktail_ref, vtail_ref = refs
        n_new = k_ref.shape[0]
        for tail_ref, new_ref in ((ktail_ref, k_ref), (vtail_ref, v_ref)):
            tail_ref[...] = jnp.zeros(tail_ref.shape, BF16)
            tail_ref[0:n_new, :] = new_ref[...]
    n_past_tiles = n_past // KEY_TILE

    def kv_tile(new_ref, past_ref, tail_ref, c, cols):
        if not n_past:
            return new_ref[pl.ds(pl.multiple_of(c * KEY_TILE, KEY_TILE), KEY_TILE), cols]
        pc = jnp.minimum(c, n_past_tiles - 1)
        past = past_ref[pl.ds(pl.multiple_of(pc * KEY_TILE, KEY_TILE), KEY_TILE), cols].astype(BF16)
        return jnp.where(c < n_past_tiles, past, tail_ref[:, cols])

    tq = q_ref.shape[0]
    q0 = q_start + pl.program_id(1) * tq
    nt = (((1,), (1,)), ((), ()))
    lane_shift = int(math.log2(LANES))
    chunk_shift = int(math.log2(CHUNK))
    kax = 0 if keys_on_rows else 1
    dc = lax.shift_right_logical(q0, lane_shift)
    n_lane_tiles = jnp.minimum(lax.shift_right_logical(q0 + tq - 1, lane_shift) + 1, lp // LANES)
    n_tiles = lax.shift_right_logical(n_lane_tiles + 1, 1)

    def tile_off(c, width):
        return pl.multiple_of(c * width, width)

    def keys_at(off, width):
        return (pl.ds(off, width), slice(None)) if keys_on_rows else (slice(None), pl.ds(off, width))

    def per_query(x):
        return jnp.sum(x, axis=kax, keepdims=True)

    iw = iw_ref[...] * ((H_I * D_I) ** -0.5)
    if not keys_on_rows:
        for h in range(H_I):
            iwb_ref[h] = jnp.broadcast_to(iw[:, h:h + 1], (tq, LANES))

    def score_lane_tile(c):
        kab = ik2_ref[c]
        acc = [jnp.zeros((LANES, tq) if keys_on_rows else (tq, LANES), F32) for _ in range(2)]
        for hp in range(H_I // 2):
            iq_pair = iq_ref[:, hp * LANES:(hp + 1) * LANES]
            if keys_on_rows:
                d = lax.dot_general(kab, iq_pair, nt, preferred_element_type=F32)
                halves = (d[:LANES], d[LANES:])
                wts = (iw[2 * hp:2 * hp + 1, :], iw[2 * hp + 1:2 * hp + 2, :])
            else:
                d = lax.dot_general(iq_pair, kab, nt, preferred_element_type=F32)
                halves = (d[:, :LANES], d[:, LANES:])
                wts = (iwb_ref[2 * hp], iwb_ref[2 * hp + 1])
            for u in range(2):
                acc[u] = acc[u] + jnp.maximum(halves[u], 0.0) * wts[u]
        score_ref[keys_at(tile_off(c, LANES), LANES)] = acc[0] + acc[1]

    def score_tile(c, carry):
        score_lane_tile(2 * c)
        score_lane_tile(2 * c + 1)
        return carry

    lax.fori_loop(0, n_tiles, score_tile, 0)

    qshape, kshape = ((1, tq), lambda w: (w, 1)) if keys_on_rows else ((tq, 1), lambda w: (1, w))
    qchunk = lax.shift_right_arithmetic(q0 + lax.broadcasted_iota(jnp.int32, qshape, 1 - kax), chunk_shift)

    def visible(lo, width):
        kpos = lo + lax.broadcasted_iota(jnp.int32, kshape(width), kax)
        return (lax.shift_right_arithmetic(kpos, chunk_shift) <= qchunk) & (kpos < l_valid)

    def key_tile(c, carry):
        off = tile_off(c, KEY_TILE)
        bits = pltpu.bitcast(score_ref[keys_at(off, KEY_TILE)] + 0.0, jnp.int32)
        key = bits ^ (lax.shift_right_arithmetic(bits, 31) & jnp.int32(0x7FFFFFFF))
        key_ref[keys_at(off, KEY_TILE)] = jnp.where(visible(off, KEY_TILE), key, jnp.int32(INT_MIN))
        return carry

    lax.fori_loop(0, n_tiles, key_tile, 0)
    kf = jnp.float32(topk)
    sub = 8

    def count(pred):
        def body(c, acc):
            m = jnp.where(pred(key_ref[keys_at(tile_off(c, KEY_TILE), KEY_TILE)]), 1.0, 0.0)
            if keys_on_rows:
                parts = [m[r:r + sub] for r in range(0, KEY_TILE, sub)]
            else:
                parts = [m[:, :LANES], m[:, LANES:]]
            while len(parts) > 1:
                parts = [parts[u] + parts[u + 1] for u in range(0, len(parts), 2)]
            return acc + parts[0]
        acc = lax.fori_loop(0, n_tiles, body, jnp.zeros((sub, tq) if keys_on_rows else (tq, LANES), F32))
        return per_query(acc)

    def search(i, thr):
        cand = thr ^ lax.shift_left(jnp.int32(1), 31 - i)
        return jnp.where(count(lambda kc: kc >= cand) >= kf, cand, thr)

    thr = lax.fori_loop(0, 32, search, jnp.full(qshape, INT_MIN, jnp.int32))

    def store_mask(off, width, sel):
        madd = jnp.where(sel, 0.0, -jnp.inf)
        if keys_on_rows:
            for u in range(width // LANES):
                madd_ref[:, pl.ds(pl.multiple_of(off + u * LANES, LANES), LANES)] = madd[u * LANES:(u + 1) * LANES].T
        else:
            madd_ref[:, pl.ds(off, width)] = madd

    surplus = (count(lambda kc: kc >= thr) > kf) & (thr != jnp.int32(INT_MIN))
    any_surplus = jnp.max(jnp.where(surplus, 1.0, 0.0))

    @pl.when(any_surplus == 0.0)
    def _():
        def select_tile(c, carry):
            off = tile_off(c, KEY_TILE)
            store_mask(off, KEY_TILE, visible(off, KEY_TILE) & (key_ref[keys_at(off, KEY_TILE)] >= thr))
            return carry

        lax.fori_loop(0, n_tiles, select_tile, 0)

    @pl.when(any_surplus > 0.0)
    def _():
        need = kf - count(lambda kc: kc > thr)
        ii = lax.broadcasted_iota(jnp.int32, (LANES, LANES), 0)
        jj = lax.broadcasted_iota(jnp.int32, (LANES, LANES), 1)
        tri = jnp.where((ii >= jj) if keys_on_rows else (ii <= jj), 1.0, 0.0).astype(BF16)

        def select_tile(c, run):
            off = tile_off(c, LANES)
            keyc = key_ref[keys_at(off, LANES)]
            eqf = jnp.where(keyc == thr, 1.0, 0.0)
            if keys_on_rows:
                within = jnp.dot(tri, eqf.astype(BF16), preferred_element_type=F32)
                total = within[LANES - 1:LANES, :]
            else:
                within = jnp.dot(eqf.astype(BF16), tri, preferred_element_type=F32)
                total = within[:, LANES - 1:LANES]
            take = jnp.where(keyc > thr, 1.0, jnp.where(run + within <= need, eqf, 0.0))
            store_mask(off, LANES, visible(off, LANES) & (take > 0.5))
            return run + total

        lax.fori_loop(0, 2 * n_tiles, select_tile, jnp.zeros(qshape, F32))

    off0 = tile_off(dc, LANES)
    off1 = tile_off(jnp.maximum(dc - 1, 0), LANES)
    for g in range(N_KV_B):
        ksl = slice(g * HD_B, (g + 1) * HD_B)
        qs = jnp.concatenate([q_ref[:, (g * G_B + j) * HD_B:(g * G_B + j + 1) * HD_B] for j in range(G_B)], axis=0)

        def logits_tile(c, carry):
            off = tile_off(c, KEY_TILE)
            s = lax.dot_general(qs, kv_tile(k_ref, kpast_ref, ktail_ref, c, ksl), nt, preferred_element_type=F32)
            md = madd_ref[:, pl.ds(off, KEY_TILE)]
            for j in range(G_B):
                logit_ref[j * tq:(j + 1) * tq, pl.ds(off, KEY_TILE)] = s[j * tq:(j + 1) * tq] * (HD_B ** -0.5) + md
            return carry

        lax.fori_loop(0, n_tiles, logits_tile, 0)
        for j in range(G_B):
            logit_ref[j * tq:(j + 1) * tq, pl.ds(off0, LANES)] += d0_ref[g * G_B + j]

        @pl.when(dc > 0)
        def _():
            for j in range(G_B):
                logit_ref[j * tq:(j + 1) * tq, pl.ds(off1, LANES)] += d1_ref[g * G_B + j]

        mx_ref[...] = jnp.full(mx_ref.shape, -jnp.inf, F32)

        def max_tile(c, carry):
            lg = logit_ref[:, pl.ds(tile_off(c, KEY_TILE), KEY_TILE)]
            mx_ref[...] = jnp.maximum(mx_ref[...], jnp.maximum(lg[:, :LANES], lg[:, LANES:]))
            return carry

        lax.fori_loop(0, n_tiles, max_tile, 0)
        m = jnp.max(mx_ref[...], axis=1, keepdims=True)
        ls_ref[...] = jnp.zeros(ls_ref.shape, F32)
        acc_ref[...] = jnp.zeros(acc_ref.shape, F32)

        def pv_tile(c, carry):
            off = tile_off(c, KEY_TILE)
            p = jnp.exp(logit_ref[:, pl.ds(off, KEY_TILE)] - m)
            ls_ref[...] += p[:, :LANES] + p[:, LANES:]
            acc_ref[...] += jnp.dot(p.astype(BF16), kv_tile(v_ref, vpast_ref, vtail_ref, c, ksl),
                                    preferred_element_type=F32)
            return carry

        lax.fori_loop(0, n_tiles, pv_tile, 0)
        res = acc_ref[...] / jnp.sum(ls_ref[...], axis=1, keepdims=True)
        for j in range(G_B):
            h = g * G_B + j
            o_ref[:, h * HD_B:(h + 1) * HD_B] = res[j * tq:(j + 1) * tq].astype(o_ref.dtype)


def _t5_bucket(rel):
    half = REL_BUCKETS // 2
    exact = half // 2
    side = jnp.where(rel > 0, half, 0)
    n = jnp.abs(rel)
    nf = jnp.maximum(n, 1).astype(F32)
    large = exact + (jnp.log(nf / exact) / math.log(REL_MAX_DIST / exact) * (half - exact)).astype(jnp.int32)
    large = jnp.minimum(large, half - 1)
    return side + jnp.where(n < exact, n, large)


def _dsa(zq, kv_new, row0, b, t, iw, ik_all, rel_bias, tq, q_start, l_valid, k_past=None, v_past=None):
    n_past = 0 if k_past is None else k_past.shape[1]
    lp = ik_all.shape[1]
    topk = min(TOPK_MAX, l_valid // 4)
    assert q_start % LANES == 0 and (tq == LANES or t == tq) and lp % KEY_TILE == 0 and row0 % tq == 0
    assert q_start == n_past and n_past % KEY_TILE == 0 and row0 % t == 0
    assert lp == (n_past + KEY_TILE if n_past else t) and l_valid == n_past + t and (not n_past or t <= KEY_TILE)
    nq = t // tq
    blk0 = row0 // tq
    ikt = ik_all.reshape(b, lp // LANES, LANES, D_I)
    zeros = jnp.zeros_like(ikt)
    ik2 = jnp.concatenate([jnp.concatenate([ikt, zeros], axis=-1), jnp.concatenate([zeros, ikt], axis=-1)], axis=2)
    ji = jnp.arange(LANES, dtype=jnp.int32)[None, :] - jnp.arange(tq, dtype=jnp.int32)[:, None]

    def bias_of(rel):
        hit = _t5_bucket(rel)[None, None] == jnp.arange(REL_BUCKETS, dtype=jnp.int32).reshape((1, -1) + (1,) * rel.ndim)
        return jnp.sum(jnp.where(hit, rel_bias.T.reshape((H_B, REL_BUCKETS) + (1,) * rel.ndim), 0.0), axis=1)

    far = bias_of(jnp.full((1, 1), -2 * LANES, jnp.int32))
    d0 = bias_of(ji) - far
    d1 = bias_of(ji - LANES) - far
    keys_on_rows = tq == LANES
    kern = functools.partial(_dsa_kernel, q_start=q_start, l_valid=l_valid, topk=topk, keys_on_rows=keys_on_rows,
                             n_past=n_past, lp=lp)
    qspec = lambda col: pl.BlockSpec((tq, D_B), lambda bi, i: (blk0 + bi * nq + i, col))
    new_spec = lambda col: pl.BlockSpec((t, D_KV), lambda bi, i: (row0 // t + bi, col))
    cspec = pl.BlockSpec((H_B, tq, LANES), lambda bi, i: (0, 0, 0))
    rows = G_B * tq
    if keys_on_rows:
        iw, iw_spec = iw.T, pl.BlockSpec((H_I, tq), lambda bi, i: (0, bi * nq + i))
        mask_scratch = [pltpu.VMEM((lp, tq), F32), pltpu.VMEM((lp, tq), jnp.int32)]
    else:
        iw_spec = pl.BlockSpec((tq, H_I), lambda bi, i: (bi * nq + i, 0))
        mask_scratch = [pltpu.VMEM((tq, lp), F32), pltpu.VMEM((tq, lp), jnp.int32), pltpu.VMEM((H_I, tq, LANES), F32)]
    past_specs, past_args, tail_scratch = [], [], []
    if n_past:
        past_specs = [pl.BlockSpec((None, n_past, D_KV), lambda bi, i: (bi, 0, 0))] * 2
        past_args = [k_past, v_past]
        tail_scratch = [pltpu.VMEM((KEY_TILE, D_KV), BF16)] * 2
    return pl.pallas_call(
        kern,
        grid=(b, nq),
        in_specs=[qspec(0), qspec(1), iw_spec, new_spec(0), new_spec(1),
                  pl.BlockSpec((None, lp // LANES, 2 * LANES, LANES), lambda bi, i: (bi, 0, 0, 0)), cspec, cspec]
        + past_specs,
        out_specs=pl.BlockSpec((tq, D_B), lambda bi, i: (bi * nq + i, 0)),
        out_shape=jax.ShapeDtypeStruct((b * t, D_B), BF16),
        scratch_shapes=[pltpu.VMEM((tq, lp), F32), pltpu.VMEM((rows, lp), F32), pltpu.VMEM((rows, LANES), F32),
                        pltpu.VMEM((rows, LANES), F32), pltpu.VMEM((rows, HD_B), F32)] + mask_scratch + tail_scratch,
        compiler_params=_params("parallel", "arbitrary"),
        name="dsa_attention",
    )(zq, zq, iw, kv_new, kv_new, ik2, d0, d1, *past_args)


def _mem_kernel(q_ref, k_ref, v_ref, o_ref):
    nt = (((1,), (1,)), ((), ()))
    for h in range(H_M):
        sl = slice(h * HD_M, (h + 1) * HD_M)
        s = lax.dot_general(q_ref[:, sl], k_ref[:, sl], nt, preferred_element_type=F32) * (HD_M ** -0.5)
        m = jnp.max(s, axis=1, keepdims=True)
        p = jnp.exp(s - m)
        den = jnp.sum(p, axis=1, keepdims=True)
        o = jnp.dot(p.astype(BF16), v_ref[:, sl], preferred_element_type=F32)
        o_ref[:, sl] = (o / den).astype(o_ref.dtype)


def _mem_attention(zq, row0, b, t, mk, mv, tq):
    nq = t // tq
    blk0 = row0 // tq
    return pl.pallas_call(
        _mem_kernel,
        grid=(b, nq),
        in_specs=[pl.BlockSpec((tq, D_M), lambda bi, i: (blk0 + bi * nq + i, 2)),
                  pl.BlockSpec((None, N_MEM, D_M), lambda bi, i: (bi, 0, 0)),
                  pl.BlockSpec((None, N_MEM, D_M), lambda bi, i: (bi, 0, 0))],
        out_specs=pl.BlockSpec((tq, D_M), lambda bi, i: (bi * nq + i, 0)),
        out_shape=jax.ShapeDtypeStruct((b * t, D_M), BF16),
        compiler_params=_params("parallel", "parallel"),
        name="memory_attention",
    )(zq, mk, mv)


def _mix_kernel(yap_ref, yas_ref, gp_ref, gs_ref, obp_ref, obs_ref, omp_ref, oms_ref, ga_ref, gb_ref, gm_ref,
                pa_ref, pb_ref, pm_ref, o_ref, *, na):
    i = pl.program_id(1)
    oa = (_pick(i, na, yap_ref, yas_ref) * _pick(i, na, gp_ref, gs_ref)).astype(BF16)
    acc = _sigmoid(ga_ref[...]) * jnp.dot(oa, pa_ref[...], preferred_element_type=F32)
    acc = acc + _sigmoid(gb_ref[...]) * jnp.dot(_pick(i, na, obp_ref, obs_ref), pb_ref[...],
                                                preferred_element_type=F32)
    acc = acc + _sigmoid(gm_ref[...]) * jnp.dot(_pick(i, na, omp_ref, oms_ref), pm_ref[...],
                                                preferred_element_type=F32)
    o_ref[...] = acc.astype(o_ref.dtype)


def _mix(ya2, g2, ob2, om2, zg, pa, pb, pm, tm, tn):
    n = zg.shape[0]
    nj = D_MODEL // tn
    na = g2[0].shape[0] // tm
    two = lambda w: (pl.BlockSpec((tm, w), lambda j, i: (jnp.minimum(i, na - 1), 0)),
                     pl.BlockSpec((tm, w), lambda j, i: (jnp.maximum(i - na, 0), 0)))
    gate = lambda br: pl.BlockSpec((tm, tn), lambda j, i, br=br: (i, br * nj + j))
    wt = pl.BlockSpec((D_A, tn), lambda j, i: (0, j))
    return pl.pallas_call(
        functools.partial(_mix_kernel, na=na),
        grid=(nj, n // tm),
        in_specs=[*two(D_A), *two(D_A), *two(D_B), *two(D_M), gate(0), gate(1), gate(2), wt, wt, wt],
        out_specs=pl.BlockSpec((tm, tn), lambda j, i: (i, j)),
        out_shape=jax.ShapeDtypeStruct((n, D_MODEL), BF16),
        compiler_params=_params("parallel", "parallel"),
        name="branch_mix",
    )(*ya2, *g2, *ob2, *om2, zg, zg, zg, pa, pb, pm)


def _wo_kernel(m_ref, w_ref, xa_ref, xb_ref, g_ref, h_ref, u_ref, *, na):
    x = _pick(pl.program_id(0), na, xa_ref, xb_ref)
    h = x + jnp.dot(m_ref[...], w_ref[...], preferred_element_type=F32)
    h_ref[...] = h
    y = h * lax.rsqrt(jnp.mean(h * h, axis=-1, keepdims=True) + NORM_EPS)
    u_ref[...] = (y * g_ref[...]).astype(u_ref.dtype)


def _wo_residual_norm(mixed, w_o, xa, xb, g, tm):
    n = mixed.shape[0]
    na = xa.shape[0] // tm
    tile = pl.BlockSpec((tm, D_MODEL), lambda i: (i, 0))
    sa, sb = _two_part_specs(tm, D_MODEL, na)
    return pl.pallas_call(
        functools.partial(_wo_kernel, na=na),
        grid=(n // tm,),
        in_specs=[tile, pl.BlockSpec((D_MODEL, D_MODEL), lambda i: (0, 0)), sa, sb,
                  pl.BlockSpec((1, D_MODEL), lambda i: (0, 0))],
        out_specs=[tile, tile],
        out_shape=[jax.ShapeDtypeStruct((n, D_MODEL), F32), jax.ShapeDtypeStruct((n, D_MODEL), BF16)],
        compiler_params=_params("parallel"),
        name="wo_residual_norm",
    )(mixed, w_o, xa, xb, g.reshape(1, D_MODEL))


def _moe_kernel(tg_ref, nu_ref, x_ref, gate_ref, w1_ref, w3_ref, w2_ref, o_ref):
    i, j = pl.program_id(0), pl.program_id(1)

    @pl.when((i >= nu_ref[0]) & (j == 0))
    def _():
        o_ref[...] = jnp.zeros(o_ref.shape, F32)

    @pl.when(i < nu_ref[0])
    def _():
        x = x_ref[...]
        a = jnp.dot(x, w1_ref[...], preferred_element_type=F32)
        b = jnp.dot(x, w3_ref[...], preferred_element_type=F32)
        lane = lax.broadcasted_iota(jnp.int32, gate_ref.shape, 1)
        ge = jnp.sum(jnp.where(lane == j, gate_ref[...], 0.0), axis=1, keepdims=True)
        hid = (a * _sigmoid(a)) * b * ge
        out = jnp.dot(hid.astype(BF16), w2_ref[...], preferred_element_type=F32)

        @pl.when(j == 0)
        def _():
            o_ref[...] = out

        @pl.when(j > 0)
        def _():
            o_ref[...] += out


def _moe_grouped(x_sorted, gate_sorted, tile_group, n_used, w1, w3, w2):
    npad = x_sorted.shape[0]
    ew = lambda i, j, tg, nu: (tg[i] * EXPERTS_PER_GROUP + j, 0, 0)
    return pl.pallas_call(
        _moe_kernel,
        grid_spec=pltpu.PrefetchScalarGridSpec(
            num_scalar_prefetch=2,
            grid=(npad // MOE_TM, EXPERTS_PER_GROUP),
            in_specs=[pl.BlockSpec((MOE_TM, D_MODEL), lambda i, j, tg, nu: (i, 0)),
                      pl.BlockSpec((MOE_TM, EXPERTS_PER_GROUP), lambda i, j, tg, nu: (i, 0)),
                      pl.BlockSpec((None, D_MODEL, D_EXPERT), ew), pl.BlockSpec((None, D_MODEL, D_EXPERT), ew),
                      pl.BlockSpec((None, D_EXPERT, D_MODEL), ew)],
            out_specs=pl.BlockSpec((MOE_TM, D_MODEL), lambda i, j, tg, nu: (i, 0))),
        out_shape=jax.ShapeDtypeStruct((npad, D_MODEL), F32),
        compiler_params=_params("parallel", "arbitrary"),
        name="moe_grouped",
    )(tile_group, n_used, x_sorted, gate_sorted, w1, w3, w2)


def _route(logits, b_grp, b_rt):
    n = logits.shape[0]
    g_logits = logits[:, :N_GROUPS] + b_grp
    e_logits = (logits[:, N_GROUPS:N_GROUPS + N_EXPERTS] + b_rt).reshape(n, N_GROUPS, EXPERTS_PER_GROUP)
    g_idx = jnp.argmax(g_logits, axis=-1).astype(jnp.int32)
    g_w = jnp.max(jax.nn.softmax(g_logits, axis=-1), axis=-1, keepdims=True)
    onehot_g = g_idx[:, None] == jnp.arange(N_GROUPS, dtype=jnp.int32)[None, :]
    e_in = jnp.sum(jnp.where(onehot_g[:, :, None], e_logits, 0.0), axis=1)
    lane = jnp.arange(EXPERTS_PER_GROUP, dtype=jnp.int32)[None, :]
    i1 = jnp.argmax(e_in, axis=-1)[:, None]
    v1 = jnp.max(e_in, axis=-1, keepdims=True)
    rest = jnp.where(lane == i1, -jnp.inf, e_in)
    i2 = jnp.argmax(rest, axis=-1)[:, None]
    v2 = jnp.max(rest, axis=-1, keepdims=True)
    w12 = g_w * jax.nn.softmax(jnp.concatenate([v1, v2], axis=-1), axis=-1)
    gate4 = jnp.where(lane == i1, w12[:, 0:1], 0.0) + jnp.where(lane == i2, w12[:, 1:2], 0.0)

    counts = jnp.sum(onehot_g.astype(jnp.int32), axis=0)
    tiles = (counts + MOE_TM - 1) // MOE_TM
    tile_end = jnp.cumsum(tiles)
    slot0 = (tile_end - tiles) * MOE_TM
    start = jnp.cumsum(counts) - counts
    order = jnp.argsort(g_idx, stable=True).astype(jnp.int32)
    rank = jnp.sum(jnp.where(onehot_g, jnp.cumsum(onehot_g.astype(jnp.int32), axis=0) - 1, 0), axis=1)
    inv = jnp.sum(jnp.where(onehot_g, slot0[None, :], 0), axis=1) + rank
    npad = n + N_GROUPS * MOE_TM
    slots = jnp.arange(npad, dtype=jnp.int32)
    slot_group = jnp.minimum(jnp.sum(jnp.where(slots[:, None] >= tile_end[None, :] * MOE_TM, 1, 0), axis=1),
                             N_GROUPS - 1)
    in_group = slot_group[:, None] == jnp.arange(N_GROUPS, dtype=jnp.int32)[None, :]
    pos = slots - jnp.sum(jnp.where(in_group, slot0[None, :], 0), axis=1)
    used = pos < jnp.sum(jnp.where(in_group, counts[None, :], 0), axis=1)
    src = jnp.where(used, pos + jnp.sum(jnp.where(in_group, start[None, :], 0), axis=1), 0)
    perm = order[src]
    gate_sorted = jnp.where(used[:, None], gate4[perm], 0.0)
    tile_group = slot_group[::MOE_TM].astype(jnp.int32)
    return perm, inv, gate_sorted, tile_group, tile_end[-1:].astype(jnp.int32)


def _final_kernel(h_ref, m_ref, g_ref, yp_ref, ys_ref, *, na):
    h = h_ref[...] + m_ref[...]
    y = h * lax.rsqrt(jnp.mean(h * h, axis=-1, keepdims=True) + NORM_EPS) * g_ref[...]
    i = pl.program_id(0)

    @pl.when(i < na)
    def _():
        yp_ref[...] = y

    @pl.when(i >= na)
    def _():
        ys_ref[...] = y


def _final_norm(h, moe, g, n_p, tm):
    n = h.shape[0]
    na = n_p // tm
    tile = pl.BlockSpec((tm, D_MODEL), lambda i: (i, 0))
    sa, sb = _two_part_specs(tm, D_MODEL, na)
    return pl.pallas_call(
        functools.partial(_final_kernel, na=na),
        grid=(n // tm,),
        in_specs=[tile, tile, pl.BlockSpec((1, D_MODEL), lambda i: (0, 0))],
        out_specs=[sa, sb],
        out_shape=[jax.ShapeDtypeStruct((n_p, D_MODEL), F32), jax.ShapeDtypeStruct((n - n_p, D_MODEL), F32)],
        compiler_params=_params("arbitrary"),
        name="residual_final_norm",
    )(h, moe, g.reshape(1, D_MODEL))


def _pad_cols(w, width):
    return jnp.pad(w, ((0, 0), (0, width - w.shape[1])))


def _pad_rows(w, height):
    return jnp.pad(w, ((0, height - w.shape[0]), (0, 0)))


def _pad_lora(x):
    o = 3 * D_A
    pad = lambda t, w: jnp.pad(t, [(0, 0)] * (t.ndim - 1) + [(0, w - t.shape[-1])])
    return jnp.concatenate([x[..., :o], pad(x[..., o:o + W_LORA], WL_PAD),
                            pad(x[..., o + W_LORA:o + W_LORA + A_LORA], AL_PAD),
                            pad(x[..., o + W_LORA + A_LORA:], GL_PAD)], axis=-1)


def _unpad_lora(x):
    o = 3 * D_A
    return jnp.concatenate([x[..., :o], x[..., o:o + W_LORA], x[..., o + WL_PAD:o + WL_PAD + A_LORA],
                            x[..., o + WL_PAD + AL_PAD:o + WL_PAD + AL_PAD + G_LORA]], axis=-1)


def kernel(x_prompt, x_sample, cache_dsa_k, cache_dsa_v, cache_idx_k, state_rwkv_shift, state_rwkv_wkv, cache_mem_k, cache_mem_v, mem_prompt, rel_bias, g_attn, w_in, rwkv_mu, rwkv_w0, rwkv_w_dec, rwkv_a0, rwkv_w_a, rwkv_w_g, rwkv_k_k, rwkv_k_a, rwkv_r_k, rwkv_lnx_w, rwkv_lnx_b, g_mem, w_mem_kv, p_a, p_b, p_m, w_o, g_ffn, w_grp, b_grp, w_rt, b_rt, w1, w3, w2, g_final):
    assert w_in.shape[0] == 1, "single layer"
    bp, tp, _ = x_prompt.shape
    bs, ts, _ = x_sample.shape
    past = cache_dsa_k.shape[2]
    n_p, n_s = bp * tp, bs * ts
    n = n_p + n_s
    tm = 1024
    xp = x_prompt.reshape(n_p, D_MODEL)
    xs = x_sample.reshape(n_s, D_MODEL)

    offs = [0]
    for s in IN_SIZES:
        offs.append(offs[-1] + s)
    seg = lambda i: w_in[0][:, offs[i]:offs[i + 1]]
    w_a_cols = _pad_lora(seg(0)).astype(BF16)
    w_qim = jnp.concatenate([seg(1), seg(4), seg(7)], axis=1).astype(BF16)
    w_kvi = _pad_cols(jnp.concatenate([seg(2), seg(3), seg(5), seg(6)], axis=1), KVI_PAD).astype(BF16)
    w_gate = seg(8).astype(BF16)
    rw = dict(mu=_pad_lora(rwkv_mu[0])[None], w0=rwkv_w0, a0=rwkv_a0, k_k=rwkv_k_k, k_a=rwkv_k_a,
              w_dec=_pad_rows(rwkv_w_dec[0], WL_PAD).astype(BF16), w_a=_pad_rows(rwkv_w_a[0], AL_PAD).astype(BF16),
              w_g=_pad_rows(rwkv_w_g[0], GL_PAD).astype(BF16))

    u = _rmsnorm_bf16(xp, xs, g_attn[0], 512)
    z_a = _matmul(u, w_a_cols, F32, tm, A_PAD // 4, "proj_rwkv")
    z_qim = _matmul(u, w_qim, BF16, tm, 1024, "proj_queries")
    z_kvi, kv_bf = _matmul_kv(u, w_kvi, tm)
    z_g = _matmul(u, w_gate, F32, tm, 1024, "proj_gates")

    mem = mem_prompt.reshape(bp * N_MEM, D_MODEL)
    um = _rmsnorm_bf16(mem[:bp * N_MEM // 2], mem[bp * N_MEM // 2:], g_mem[0], 512)
    mkv = _matmul(um, w_mem_kv[0].astype(BF16), F32, 1024, 1024, "proj_mem_kv")
    mk_p = mkv[:, :D_M].reshape(bp, N_MEM, D_M)
    mv_p = mkv[:, D_M:].reshape(bp, N_MEM, D_M)

    lw_vec, lb_vec, rk_vec = rwkv_lnx_w[0], rwkv_lnx_b[0], rwkv_r_k[0].reshape(D_A)

    def rwkv_group(row0, nseq, t, shift0, wkv0, tb, tt):
        r, w, k, v, kk, a, g, last = _rwkv_prep(z_a, row0, nseq, t, shift0, rw, tb)
        y, s_fin = _wkv(r, w, k, v, kk, a, _state_to_tiles(wkv0), _lane_tile(lw_vec), _lane_tile(lb_vec),
                        _lane_tile(rk_vec), tt)
        return y.reshape(nseq * t, D_A), g, _state_from_tiles(s_fin, nseq), _unpad_lora(last)

    ya_p, g_p, wkv_p, shift_p = rwkv_group(0, bp, tp, jnp.zeros((bp, 1, A_PAD), F32),
                                           jnp.zeros((bp, H_A, HD_A, HD_A), F32), 256, 32)
    ya_s, g_s, wkv_s, shift_s = rwkv_group(n_p, bs, ts, _pad_lora(state_rwkv_shift[0]), state_rwkv_wkv[0], ts, ts)

    k_new, v_new = z_kvi[:, :D_KV], z_kvi[:, D_KV:2 * D_KV]
    ik_new = z_kvi[:, 2 * D_KV:2 * D_KV + D_I]
    iw = z_kvi[:, 2 * D_KV + D_I:2 * D_KV + D_I + H_I]
    grp = lambda t, sl, b, tlen: t[sl].reshape(b, tlen, t.shape[-1])
    sp, ss = slice(0, n_p), slice(n_p, n)
    ob_p = _dsa(z_qim, kv_bf, 0, bp, tp, iw[sp], grp(ik_new, sp, bp, tp).astype(BF16), rel_bias,
                Q_BLOCK if tp % Q_BLOCK == 0 else tp, 0, tp)
    l_s = past + ts
    ik_s = jnp.pad(jnp.concatenate([cache_idx_k[0].astype(BF16), grp(ik_new, ss, bs, ts).astype(BF16)], axis=1),
                   ((0, 0), (0, past + KEY_TILE - l_s), (0, 0)))
    ob_s = _dsa(z_qim, kv_bf, n_p, bs, ts, iw[ss], ik_s, rel_bias, Q_BLOCK if ts % Q_BLOCK == 0 else ts, past, l_s,
                cache_dsa_k[0].reshape(bs, past, D_KV), cache_dsa_v[0].reshape(bs, past, D_KV))

    om_p = _mem_attention(z_qim, 0, bp, tp, mk_p.astype(BF16), mv_p.astype(BF16), 256)
    om_s = _mem_attention(z_qim, n_p, bs, ts, cache_mem_k[0].reshape(bs, N_MEM, D_M).astype(BF16),
                          cache_mem_v[0].reshape(bs, N_MEM, D_M).astype(BF16), ts)

    mixed = _mix((ya_p, ya_s), (g_p, g_s), (ob_p, ob_s), (om_p, om_s), z_g, p_a[0].astype(BF16), p_b[0].astype(BF16),
                 p_m[0].astype(BF16), 256, 1024)
    h, u2 = _wo_residual_norm(mixed, w_o[0].astype(BF16), xp, xs, g_ffn[0], 256)

    w_route = _pad_cols(jnp.concatenate([w_grp[0], w_rt[0]], axis=1), LANES).astype(BF16)
    logits = _matmul(u2, w_route, F32, tm, LANES, "moe_router")
    perm, inv, gate_sorted, tile_group, n_used = _route(logits, b_grp[0], b_rt[0])
    moe_sorted = _moe_grouped(u2[perm], gate_sorted, tile_group, n_used,
                              _cast_bf16(w1[0]), _cast_bf16(w3[0]), _cast_bf16(w2[0]))
    y_p, y_s = _final_norm(h, moe_sorted[inv], g_final, n_p, 512)

    st = lambda t, b, tlen, shape: t.reshape((1, b, tlen) + shape)
    return (y_p.reshape(bp, tp, D_MODEL), y_s.reshape(bs, ts, D_MODEL),
            st(k_new[sp], bp, tp, (N_KV_B, HD_B)), st(v_new[sp], bp, tp, (N_KV_B, HD_B)), st(ik_new[sp], bp, tp, (D_I,)),
            shift_p[None], wkv_p[None],
            mk_p.reshape(1, bp, N_MEM, H_M, HD_M), mv_p.reshape(1, bp, N_MEM, H_M, HD_M),
            st(k_new[ss], bs, ts, (N_KV_B, HD_B)), st(v_new[ss], bs, ts, (N_KV_B, HD_B)), st(ik_new[ss], bs, ts, (D_I,)),
            shift_s[None], wkv_s[None])
```

```python
import functools
import math

import jax
import jax.numpy as jnp
from jax import lax
from jax.experimental import pallas as pl
from jax.experimental.pallas import tpu as pltpu

F32 = jnp.float32
BF16 = jnp.bfloat16

LANES = 128
D_MODEL = 2048
CHUNK = 64
NORM_EPS = 1e-6
H_A, HD_A = 16, 64
D_A = H_A * HD_A
W_LORA, A_LORA, G_LORA = 64, 64, 160
LNX_EPS = 64e-5
A_COLS = 3 * D_A + W_LORA + A_LORA + G_LORA
H_B, N_KV_B, HD_B = 8, 2, 128
G_B = H_B // N_KV_B
D_B = H_B * HD_B
D_KV = N_KV_B * HD_B
H_I, D_I = 16, 64
TOPK_MAX = 256
Q_BLOCK = 128
REL_BUCKETS = 32
REL_MAX_DIST = 128
N_MEM, H_M, HD_M = 256, 4, 256
D_M = H_M * HD_M
N_BRANCH = 3
IN_SIZES = (A_COLS, D_B, D_KV, D_KV, H_I * D_I, D_I, H_I, D_M, N_BRANCH * D_MODEL)
N_GROUPS, EXPERTS_PER_GROUP = 4, 4
N_EXPERTS = N_GROUPS * EXPERTS_PER_GROUP
TOP_K_INNER = 2
D_EXPERT = 512

WL_PAD, AL_PAD, GL_PAD = 128, 128, 256
A_PAD = 3 * D_A + WL_PAD + AL_PAD + GL_PAD
KVI_PAD = 2 * D_KV + LANES
KEY_TILE = 2 * LANES
SEQ_PER_TILE = LANES // H_A
MOE_TM = 512
INT_MIN = -(2 ** 31)
VMEM_LIMIT = 48 * 1024 * 1024


def _params(*sem):
    return pltpu.CompilerParams(dimension_semantics=sem, vmem_limit_bytes=VMEM_LIMIT)


def _two_part_specs(tm, width, na):
    return (pl.BlockSpec((tm, width), lambda i: (jnp.minimum(i, na - 1), 0)),
            pl.BlockSpec((tm, width), lambda i: (jnp.maximum(i - na, 0), 0)))


def _pick(i, na, a_ref, b_ref):
    return jnp.where(i < na, a_ref[...], b_ref[...])


def _norm2_kernel(xa_ref, xb_ref, g_ref, o_ref, *, na):
    x = _pick(pl.program_id(0), na, xa_ref, xb_ref)
    y = x * lax.rsqrt(jnp.mean(x * x, axis=-1, keepdims=True) + NORM_EPS)
    o_ref[...] = (y * g_ref[...]).astype(o_ref.dtype)


def _rmsnorm_bf16(xa, xb, g, tm):
    d = xa.shape[1]
    na, nb = xa.shape[0] // tm, xb.shape[0] // tm
    sa, sb = _two_part_specs(tm, d, na)
    return pl.pallas_call(
        functools.partial(_norm2_kernel, na=na),
        grid=(na + nb,),
        in_specs=[sa, sb, pl.BlockSpec((1, d), lambda i: (0, 0))],
        out_specs=pl.BlockSpec((tm, d), lambda i: (i, 0)),
        out_shape=jax.ShapeDtypeStruct(((na + nb) * tm, d), BF16),
        compiler_params=_params("parallel"),
        name="rmsnorm_bf16",
    )(xa, xb, g.reshape(1, d))


def _mm_kernel(a_ref, w_ref, o_ref):
    o_ref[...] = jnp.dot(a_ref[...], w_ref[...], preferred_element_type=F32).astype(o_ref.dtype)


def _matmul(a, w, out_dtype, tm, tn, name):
    n, k = a.shape
    m = w.shape[1]
    return pl.pallas_call(
        _mm_kernel,
        grid=(n // tm, m // tn),
        in_specs=[pl.BlockSpec((tm, k), lambda i, j: (i, 0)), pl.BlockSpec((k, tn), lambda i, j: (0, j))],
        out_specs=pl.BlockSpec((tm, tn), lambda i, j: (i, j)),
        out_shape=jax.ShapeDtypeStruct((n, m), out_dtype),
        compiler_params=_params("parallel", "parallel"),
        name=name,
    )(a, w)


def _mm_kv_kernel(a_ref, w_ref, o_ref, kv_ref):
    acc = jnp.dot(a_ref[...], w_ref[...], preferred_element_type=F32)
    o_ref[...] = acc
    kv_ref[...] = acc[:, :2 * D_KV].astype(kv_ref.dtype)


def _matmul_kv(a, w, tm):
    n, k = a.shape
    m = w.shape[1]
    return pl.pallas_call(
        _mm_kv_kernel,
        grid=(n // tm,),
        in_specs=[pl.BlockSpec((tm, k), lambda i: (i, 0)), pl.BlockSpec((k, m), lambda i: (0, 0))],
        out_specs=[pl.BlockSpec((tm, m), lambda i: (i, 0)), pl.BlockSpec((tm, 2 * D_KV), lambda i: (i, 0))],
        out_shape=[jax.ShapeDtypeStruct((n, m), F32), jax.ShapeDtypeStruct((n, 2 * D_KV), BF16)],
        compiler_params=_params("parallel"),
        name="proj_kv",
    )(a, w)


def _cast_kernel(x_ref, o_ref):
    o_ref[...] = x_ref[...].astype(o_ref.dtype)


def _cast_bf16(x):
    e, r, c = x.shape
    spec = pl.BlockSpec((None, r, c), lambda i: (i, 0, 0))
    return pl.pallas_call(
        _cast_kernel, grid=(e,), in_specs=[spec], out_specs=spec,
        out_shape=jax.ShapeDtypeStruct(x.shape, BF16), compiler_params=_params("parallel"), name="cast_bf16",
    )(x)


def _softplus(x):
    return jnp.maximum(x, 0.0) + jnp.log1p(jnp.exp(-jnp.abs(x)))


def _sigmoid(x):
    return 0.5 * jnp.tanh(0.5 * x) + 0.5


def _rwkv_prep_kernel(z_ref, sh_ref, mu_ref, w0_ref, wdec_ref, a0_ref, wa_ref, wg_ref, kk_ref, ka_ref,
                      r_out, w_out, k_out, v_out, kk_out, a_out, g_out, last_out, carry_ref):
    tb = z_ref.shape[0]

    @pl.when(pl.program_id(1) == 0)
    def _():
        carry_ref[...] = sh_ref[...]

    z = z_ref[...]
    row = lax.broadcasted_iota(jnp.int32, (tb, 1), 0)
    prev = jnp.where(row == 0, carry_ref[...], pltpu.roll(z, 1, axis=0))
    carry_ref[...] = z[tb - 1:tb, :]
    last_out[...] = z[tb - 1:tb, :]
    zm = z + (prev - z) * mu_ref[...]
    r = zm[:, 0:D_A]
    k = zm[:, D_A:2 * D_A]
    v = zm[:, 2 * D_A:3 * D_A]
    o = 3 * D_A
    wl = zm[:, o:o + WL_PAD]
    al = zm[:, o + WL_PAD:o + WL_PAD + AL_PAD]
    gl = zm[:, o + WL_PAD + AL_PAD:]
    lw = jnp.dot(jnp.tanh(wl).astype(BF16), wdec_ref[...], preferred_element_type=F32)
    wv = -_softplus(-(w0_ref[...] + lw)) - 0.5
    a = _sigmoid(a0_ref[...] + jnp.dot(al.astype(BF16), wa_ref[...], preferred_element_type=F32))
    r_out[...] = r
    w_out[...] = jnp.exp(-jnp.exp(wv))
    k_out[...] = k * (1.0 + (a - 1.0) * ka_ref[...])
    v_out[...] = v
    kk_out[...] = k * kk_ref[...]
    a_out[...] = a
    g_out[...] = jnp.dot(_sigmoid(gl).astype(BF16), wg_ref[...], preferred_element_type=F32)


def _rwkv_prep(z, row0, nseq, t, shift0, wts, tb):
    nt = t // tb
    blk0 = row0 // tb
    row = lambda c: pl.BlockSpec((1, c), lambda s, i: (0, 0))
    full = lambda a, b: pl.BlockSpec((a, b), lambda s, i: (0, 0))
    out_spec = pl.BlockSpec((None, tb, D_A), lambda s, i: (s, i, 0))
    out_sds = jax.ShapeDtypeStruct((nseq, t, D_A), F32)
    g_spec = pl.BlockSpec((tb, D_A), lambda s, i: (s * nt + i, 0))
    g_sds = jax.ShapeDtypeStruct((nseq * t, D_A), F32)
    return pl.pallas_call(
        _rwkv_prep_kernel,
        grid=(nseq, nt),
        in_specs=[pl.BlockSpec((tb, A_PAD), lambda s, i: (blk0 + s * nt + i, 0)),
                  pl.BlockSpec((None, 1, A_PAD), lambda s, i: (s, 0, 0)),
                  row(A_PAD), row(D_A), full(WL_PAD, D_A), row(D_A), full(AL_PAD, D_A), full(GL_PAD, D_A),
                  row(D_A), row(D_A)],
        out_specs=[out_spec] * 6 + [g_spec, pl.BlockSpec((None, 1, A_PAD), lambda s, i: (s, 0, 0))],
        out_shape=[out_sds] * 6 + [g_sds, jax.ShapeDtypeStruct((nseq, 1, A_PAD), F32)],
        scratch_shapes=[pltpu.VMEM((1, A_PAD), F32)],
        compiler_params=_params("arbitrary", "arbitrary"),
        name="rwkv_prep",
    )(z, shift0, wts["mu"], wts["w0"], wts["w_dec"], wts["a0"], wts["w_a"], wts["w_g"], wts["k_k"], wts["k_a"])


def _wkv_kernel(r_ref, w_ref, k_ref, v_ref, kk_ref, a_ref, s0_ref, lw_ref, lb_ref, rk_ref,
                y_ref, st_ref, s_ref, ab_ref, tma_ref, tmb_ref, ytm_ref, *, n_tblocks):
    tt = r_ref.shape[1]
    low = lax.broadcasted_iota(jnp.int32, (SEQ_PER_TILE, LANES), 1) < HD_A

    @pl.when(pl.program_id(1) == 0)
    def _():
        s_ref[...] = s0_ref[...]

    n_pairs = tt // 2

    def to_tiles(buf_ref, j):
        j = jnp.minimum(j, n_pairs - 1)
        for ai, ref in enumerate((r_ref, w_ref, k_ref, v_ref, kk_ref, a_ref)):
            x0 = ref[:, 2 * j, :]
            x1 = ref[:, 2 * j + 1, :]
            rows = []
            for h in range(H_A):
                sl = slice((h // 2) * LANES, (h // 2 + 1) * LANES)
                if h % 2 == 0:
                    rows.append(jnp.where(low, x0[:, sl], pltpu.roll(x1[:, sl], HD_A, axis=1)))
                else:
                    rows.append(jnp.where(low, pltpu.roll(x0[:, sl], HD_A, axis=1), x1[:, sl]))
            m = jnp.concatenate(rows, axis=0).T
            buf_ref[ai, 0] = m[:HD_A]
            buf_ref[ai, 1] = m[HD_A:]

    def step(buf_ref, u, t):
        kk = buf_ref[4, u]
        ss = jnp.sum(kk * kk, axis=0, keepdims=True)
        kkn = kk * lax.rsqrt(jnp.maximum(ss, 1e-24))
        ab_ref[0] = -kkn
        ab_ref[1] = kkn * buf_ref[5, u]
        halves = []
        for rows in (slice(0, HD_A // 2), slice(HD_A // 2, HD_A)):
            vh = buf_ref[3, u, rows, :]
            sa = jnp.zeros((HD_A // 2, LANES), F32)
            for k in range(HD_A):
                sa = sa + s_ref[k, rows, :] * ab_ref[0, k:k + 1, :]
            yh = jnp.zeros((HD_A // 2, LANES), F32)
            for k in range(HD_A):
                s_new = (s_ref[k, rows, :] * buf_ref[1, u, k:k + 1, :] + sa * ab_ref[1, k:k + 1, :]
                         + vh * buf_ref[2, u, k:k + 1, :])
                s_ref[k, rows, :] = s_new
                yh = yh + s_new * buf_ref[0, u, k:k + 1, :]
            halves.append(yh)
        y = jnp.concatenate(halves, axis=0)
        vv = buf_ref[3, u]
        mean = jnp.mean(y, axis=0, keepdims=True)
        d = y - mean
        var = jnp.mean(d * d, axis=0, keepdims=True)
        yn = d * lax.rsqrt(var + LNX_EPS) * lw_ref[...] + lb_ref[...]
        bonus = jnp.sum(buf_ref[0, u] * buf_ref[2, u] * rk_ref[...], axis=0, keepdims=True) * vv
        ytm_ref[t] = yn + bonus

    to_tiles(tma_ref, 0)

    def two_pairs(m, carry):
        to_tiles(tmb_ref, 2 * m + 1)
        step(tma_ref, 0, 4 * m)
        step(tma_ref, 1, 4 * m + 1)
        to_tiles(tma_ref, 2 * m + 2)
        step(tmb_ref, 0, 4 * m + 2)
        step(tmb_ref, 1, 4 * m + 3)
        return carry

    lax.fori_loop(0, n_pairs // 2, two_pairs, 0)

    for j in range(tt // 2):
        m = jnp.concatenate([ytm_ref[2 * j], ytm_ref[2 * j + 1]], axis=0).T
        for hp in range(H_A // 2):
            even = m[(2 * hp) * SEQ_PER_TILE:(2 * hp + 1) * SEQ_PER_TILE]
            odd = m[(2 * hp + 1) * SEQ_PER_TILE:(2 * hp + 2) * SEQ_PER_TILE]
            cols = slice(hp * LANES, (hp + 1) * LANES)
            y_ref[:, 2 * j, cols] = jnp.where(low, even, pltpu.roll(odd, HD_A, axis=1))
            y_ref[:, 2 * j + 1, cols] = jnp.where(low, pltpu.roll(even, HD_A, axis=1), odd)

    @pl.when(pl.program_id(1) == n_tblocks - 1)
    def _():
        st_ref[...] = s_ref[...]


def _wkv(r, w, k, v, kk, a, s0, lw, lb, rk, tt):
    nseq, t, _ = r.shape
    p = nseq * H_A
    assert tt % 4 == 0 and t % tt == 0 and nseq % SEQ_PER_TILE == 0
    seq = pl.BlockSpec((SEQ_PER_TILE, tt, D_A), lambda g, i: (g, i, 0))
    vec = pl.BlockSpec((HD_A, LANES), lambda g, i: (0, 0))
    st = pl.BlockSpec((HD_A, HD_A, LANES), lambda g, i: (0, 0, g))
    return pl.pallas_call(
        functools.partial(_wkv_kernel, n_tblocks=t // tt),
        grid=(p // LANES, t // tt),
        in_specs=[seq] * 6 + [st, vec, vec, vec],
        out_specs=[seq, st],
        out_shape=[jax.ShapeDtypeStruct((nseq, t, D_A), F32), jax.ShapeDtypeStruct((HD_A, HD_A, p), F32)],
        scratch_shapes=[pltpu.VMEM((HD_A, HD_A, LANES), F32), pltpu.VMEM((2, HD_A, LANES), F32),
                        pltpu.VMEM((6, 2, HD_A, LANES), F32), pltpu.VMEM((6, 2, HD_A, LANES), F32),
                        pltpu.VMEM((tt, HD_A, LANES), F32)],
        compiler_params=_params("arbitrary", "arbitrary"),
        name="wkv_recurrence",
    )(r, w, k, v, kk, a, s0, lw, lb, rk)


def _state_to_tiles(wkv):
    nseq = wkv.shape[0]
    x = wkv.reshape(nseq // SEQ_PER_TILE, SEQ_PER_TILE, H_A, HD_A, HD_A)
    return x.transpose(4, 3, 0, 2, 1).reshape(HD_A, HD_A, nseq * H_A)


def _state_from_tiles(s, nseq):
    x = s.reshape(HD_A, HD_A, nseq // SEQ_PER_TILE, H_A, SEQ_PER_TILE)
    return x.transpose(2, 4, 3, 1, 0).reshape(nseq, H_A, HD_A, HD_A)


def _lane_tile(vec):
    return jnp.repeat(vec.reshape(H_A, HD_A).T, SEQ_PER_TILE, axis=1)


def _dsa_kernel(*refs, q_start, l_valid, topk, keys_on_rows, n_past, lp):
    refs = list(refs)
    q_ref, iq_ref, iw_ref, k_ref, v_ref, ik2_ref, d0_ref, d1_ref = refs[:8]
    del refs[:8]
    kpast_ref = vpast_ref = ktail_ref = vtail_ref = None
    if n_past:
        kpast_ref, vpast_ref = refs[:2]
        del refs[:2]
    o_ref, madd_ref, logit_ref, mx_ref, ls_ref, acc_ref, score_ref, key_ref = refs[:8]
    del refs[:8]
    if not keys_on_rows:
        iwb_ref = refs.pop(0)
    if n_past:
        ktail_ref, vtail_ref = refs
        n_new = k_ref.shape[0]
        for tail_ref, new_ref in ((ktail_ref, k_ref), (vtail_ref, v_ref)):
            tail_ref[...] = jnp.zeros(tail_ref.shape, BF16)
            tail_ref[0:n_new, :] = new_ref[...]
    n_past_tiles = n_past // KEY_TILE

    def kv_tile(new_ref, past_ref, tail_ref, c, cols):
        if not n_past:
            return new_ref[pl.ds(pl.multiple_of(c * KEY_TILE, KEY_TILE), KEY_TILE), cols]
        pc = jnp.minimum(c, n_past_tiles - 1)
        past = past_ref[pl.ds(pl.multiple_of(pc * KEY_TILE, KEY_TILE), KEY_TILE), cols].astype(BF16)
        return jnp.where(c < n_past_tiles, past, tail_ref[:, cols])

    tq = q_ref.shape[0]
    q0 = q_start + pl.program_id(1) * tq
    nt = (((1,), (1,)), ((), ()))
    lane_shift = int(math.log2(LANES))
    chunk_shift = int(math.log2(CHUNK))
    kax = 0 if keys_on_rows else 1
    dc = lax.shift_right_logical(q0, lane_shift)
    n_lane_tiles = jnp.minimum(lax.shift_right_logical(q0 + tq - 1, lane_shift) + 1, lp // LANES)
    n_tiles = lax.shift_right_logical(n_lane_tiles + 1, 1)

    def tile_off(c, width):
        return pl.multiple_of(c * width, width)

    def keys_at(off, width):
        return (pl.ds(off, width), slice(None)) if keys_on_rows else (slice(None), pl.ds(off, width))

    def per_query(x):
        return jnp.sum(x, axis=kax, keepdims=True)

    iw = iw_ref[...] * ((H_I * D_I) ** -0.5)
    if not keys_on_rows:
        for h in range(H_I):
            iwb_ref[h] = jnp.broadcast_to(iw[:, h:h + 1], (tq, LANES))

    def score_lane_tile(c):
        kab = ik2_ref[c]
        acc = [jnp.zeros((LANES, tq) if keys_on_rows else (tq, LANES), F32) for _ in range(2)]
        for hp in range(H_I // 2):
            iq_pair = iq_ref[:, hp * LANES:(hp + 1) * LANES]
            if keys_on_rows:
                d = lax.dot_general(kab, iq_pair, nt, preferred_element_type=F32)
                halves = (d[:LANES], d[LANES:])
                wts = (iw[2 * hp:2 * hp + 1, :], iw[2 * hp + 1:2 * hp + 2, :])
            else:
                d = lax.dot_general(iq_pair, kab, nt, preferred_element_type=F32)
                halves = (d[:, :LANES], d[:, LANES:])
                wts = (iwb_ref[2 * hp], iwb_ref[2 * hp + 1])
            for u in range(2):
                acc[u] = acc[u] + jnp.maximum(halves[u], 0.0) * wts[u]
        score_ref[keys_at(tile_off(c, LANES), LANES)] = acc[0] + acc[1]

    def score_tile(c, carry):
        score_lane_tile(2 * c)
        score_lane_tile(2 * c + 1)
        return carry

    lax.fori_loop(0, n_tiles, score_tile, 0)

    qshape, kshape = ((1, tq), lambda w: (w, 1)) if keys_on_rows else ((tq, 1), lambda w: (1, w))
    qchunk = lax.shift_right_arithmetic(q0 + lax.broadcasted_iota(jnp.int32, qshape, 1 - kax), chunk_shift)

    def visible(lo, width):
        kpos = lo + lax.broadcasted_iota(jnp.int32, kshape(width), kax)
        return (lax.shift_right_arithmetic(kpos, chunk_shift) <= qchunk) & (kpos < l_valid)

    def key_tile(c, carry):
        off = tile_off(c, KEY_TILE)
        bits = pltpu.bitcast(score_ref[keys_at(off, KEY_TILE)] + 0.0, jnp.int32)
        key = bits ^ (lax.shift_right_arithmetic(bits, 31) & jnp.int32(0x7FFFFFFF))
        key_ref[keys_at(off, KEY_TILE)] = jnp.where(visible(off, KEY_TILE), key, jnp.int32(INT_MIN))
        return carry

    lax.fori_loop(0, n_tiles, key_tile, 0)
    kf = jnp.float32(topk)
    sub = 8

    def count(pred):
        def body(c, acc):
            m = jnp.where(pred(key_ref[keys_at(tile_off(c, KEY_TILE), KEY_TILE)]), 1.0, 0.0)
            if keys_on_rows:
                parts = [m[r:r + sub] for r in range(0, KEY_TILE, sub)]
            else:
                parts = [m[:, :LANES], m[:, LANES:]]
            while len(parts) > 1:
                parts = [parts[u] + parts[u + 1] for u in range(0, len(parts), 2)]
            return acc + parts[0]
        acc = lax.fori_loop(0, n_tiles, body, jnp.zeros((sub, tq) if keys_on_rows else (tq, LANES), F32))
        return per_query(acc)

    def search(i, thr):
        cand = thr ^ lax.shift_left(jnp.int32(1), 31 - i)
        return jnp.where(count(lambda kc: kc >= cand) >= kf, cand, thr)

    thr = lax.fori_loop(0, 32, search, jnp.full(qshape, INT_MIN, jnp.int32))

    def store_mask(off, width, sel):
        madd = jnp.where(sel, 0.0, -jnp.inf)
        if keys_on_rows:
            for u in range(width // LANES):
                madd_ref[:, pl.ds(pl.multiple_of(off + u * LANES, LANES), LANES)] = madd[u * LANES:(u + 1) * LANES].T
        else:
            madd_ref[:, pl.ds(off, width)] = madd

    surplus = (count(lambda kc: kc >= thr) > kf) & (thr != jnp.int32(INT_MIN))
    any_surplus = jnp.max(jnp.where(surplus, 1.0, 0.0))

    @pl.when(any_surplus == 0.0)
    def _():
        def select_tile(c, carry):
            off = tile_off(c, KEY_TILE)
            store_mask(off, KEY_TILE, visible(off, KEY_TILE) & (key_ref[keys_at(off, KEY_TILE)] >= thr))
            return carry

        lax.fori_loop(0, n_tiles, select_tile, 0)

    @pl.when(any_surplus > 0.0)
    def _():
        need = kf - count(lambda kc: kc > thr)
        ii = lax.broadcasted_iota(jnp.int32, (LANES, LANES), 0)
        jj = lax.broadcasted_iota(jnp.int32, (LANES, LANES), 1)
        tri = jnp.where((ii >= jj) if keys_on_rows else (ii <= jj), 1.0, 0.0).astype(BF16)

        def select_tile(c, run):
            off = tile_off(c, LANES)
            keyc = key_ref[keys_at(off, LANES)]
            eqf = jnp.where(keyc == thr, 1.0, 0.0)
            if keys_on_rows:
                within = jnp.dot(tri, eqf.astype(BF16), preferred_element_type=F32)
                total = within[LANES - 1:LANES, :]
            else:
                within = jnp.dot(eqf.astype(BF16), tri, preferred_element_type=F32)
                total = within[:, LANES - 1:LANES]
            take = jnp.where(keyc > thr, 1.0, jnp.where(run + within <= need, eqf, 0.0))
            store_mask(off, LANES, visible(off, LANES) & (take > 0.5))
            return run + total

        lax.fori_loop(0, 2 * n_tiles, select_tile, jnp.zeros(qshape, F32))

    off0 = tile_off(dc, LANES)
    off1 = tile_off(jnp.maximum(dc - 1, 0), LANES)
    for g in range(N_KV_B):
        ksl = slice(g * HD_B, (g + 1) * HD_B)
        qs = jnp.concatenate([q_ref[:, (g * G_B + j) * HD_B:(g * G_B + j + 1) * HD_B] for j in range(G_B)], axis=0)

        def logits_tile(c, carry):
            off = tile_off(c, KEY_TILE)
            s = lax.dot_general(qs, kv_tile(k_ref, kpast_ref, ktail_ref, c, ksl), nt, preferred_element_type=F32)
            md = madd_ref[:, pl.ds(off, KEY_TILE)]
            for j in range(G_B):
                logit_ref[j * tq:(j + 1) * tq, pl.ds(off, KEY_TILE)] = s[j * tq:(j + 1) * tq] * (HD_B ** -0.5) + md
            return carry

        lax.fori_loop(0, n_tiles, logits_tile, 0)
        for j in range(G_B):
            logit_ref[j * tq:(j + 1) * tq, pl.ds(off0, LANES)] += d0_ref[g * G_B + j]

        @pl.when(dc > 0)
        def _():
            for j in range(G_B):
                logit_ref[j * tq:(j + 1) * tq, pl.ds(off1, LANES)] += d1_ref[g * G_B + j]

        mx_ref[...] = jnp.full(mx_ref.shape, -jnp.inf, F32)

        def max_tile(c, carry):
            lg = logit_ref[:, pl.ds(tile_off(c, KEY_TILE), KEY_TILE)]
            mx_ref[...] = jnp.maximum(mx_ref[...], jnp.maximum(lg[:, :LANES], lg[:, LANES:]))
            return carry

        lax.fori_loop(0, n_tiles, max_tile, 0)
        m = jnp.max(mx_ref[...], axis=1, keepdims=True)
        ls_ref[...] = jnp.zeros(ls_ref.shape, F32)
        acc_ref[...] = jnp.zeros(acc_ref.shape, F32)

        def pv_tile(c, carry):
            off = tile_off(c, KEY_TILE)
            p = jnp.exp(logit_ref[:, pl.ds(off, KEY_TILE)] - m)
            ls_ref[...] += p[:, :LANES] + p[:, LANES:]
            acc_ref[...] += jnp.dot(p.astype(BF16), kv_tile(v_ref, vpast_ref, vtail_ref, c, ksl),
                                    preferred_element_type=F32)
            return carry

        lax.fori_loop(0, n_tiles, pv_tile, 0)
        res = acc_ref[...] / jnp.sum(ls_ref[...], axis=1, keepdims=True)
        for j in range(G_B):
            h = g * G_B + j
            o_ref[:, h * HD_B:(h + 1) * HD_B] = res[j * tq:(j + 1) * tq].astype(o_ref.dtype)


def _t5_bucket(rel):
    half = REL_BUCKETS // 2
    exact = half // 2
    side = jnp.where(rel > 0, half, 0)
    n = jnp.abs(rel)
    nf = jnp.maximum(n, 1).astype(F32)
    large = exact + (jnp.log(nf / exact) / math.log(REL_MAX_DIST / exact) * (half - exact)).astype(jnp.int32)
    large = jnp.minimum(large, half - 1)
    return side + jnp.where(n < exact, n, large)


def _dsa(zq, kv_new, row0, b, t, iw, ik_all, rel_bias, tq, q_start, l_valid, k_past=None, v_past=None):
    n_past = 0 if k_past is None else k_past.shape[1]
    lp = ik_all.shape[1]
    topk = min(TOPK_MAX, l_valid // 4)
    assert q_start % LANES == 0 and (tq == LANES or t == tq) and lp % KEY_TILE == 0 and row0 % tq == 0
    assert q_start == n_past and n_past % KEY_TILE == 0 and row0 % t == 0
    assert lp == (n_past + KEY_TILE if n_past else t) and l_valid == n_past + t and (not n_past or t <= KEY_TILE)
    nq = t // tq
    blk0 = row0 // tq
    ikt = ik_all.reshape(b, lp // LANES, LANES, D_I)
    zeros = jnp.zeros_like(ikt)
    ik2 = jnp.concatenate([jnp.concatenate([ikt, zeros], axis=-1), jnp.concatenate([zeros, ikt], axis=-1)], axis=2)
    ji = jnp.arange(LANES, dtype=jnp.int32)[None, :] - jnp.arange(tq, dtype=jnp.int32)[:, None]

    def bias_of(rel):
        hit = _t5_bucket(rel)[None, None] == jnp.arange(REL_BUCKETS, dtype=jnp.int32).reshape((1, -1) + (1,) * rel.ndim)
        return jnp.sum(jnp.where(hit, rel_bias.T.reshape((H_B, REL_BUCKETS) + (1,) * rel.ndim), 0.0), axis=1)

    far = bias_of(jnp.full((1, 1), -2 * LANES, jnp.int32))
    d0 = bias_of(ji) - far
    d1 = bias_of(ji - LANES) - far
    keys_on_rows = tq == LANES
    kern = functools.partial(_dsa_kernel, q_start=q_start, l_valid=l_valid, topk=topk, keys_on_rows=keys_on_rows,
                             n_past=n_past, lp=lp)
    qspec = lambda col: pl.BlockSpec((tq, D_B), lambda bi, i: (blk0 + bi * nq + i, col))
    new_spec = lambda col: pl.BlockSpec((t, D_KV), lambda bi, i: (row0 // t + bi, col))
    cspec = pl.BlockSpec((H_B, tq, LANES), lambda bi, i: (0, 0, 0))
    rows = G_B * tq
    if keys_on_rows:
        iw, iw_spec = iw.T, pl.BlockSpec((H_I, tq), lambda bi, i: (0, bi * nq + i))
        mask_scratch = [pltpu.VMEM((lp, tq), F32), pltpu.VMEM((lp, tq), jnp.int32)]
    else:
        iw_spec = pl.BlockSpec((tq, H_I), lambda bi, i: (bi * nq + i, 0))
        mask_scratch = [pltpu.VMEM((tq, lp), F32), pltpu.VMEM((tq, lp), jnp.int32), pltpu.VMEM((H_I, tq, LANES), F32)]
    past_specs, past_args, tail_scratch = [], [], []
    if n_past:
        past_specs = [pl.BlockSpec((None, n_past, D_KV), lambda bi, i: (bi, 0, 0))] * 2
        past_args = [k_past, v_past]
        tail_scratch = [pltpu.VMEM((KEY_TILE, D_KV), BF16)] * 2
    return pl.pallas_call(
        kern,
        grid=(b, nq),
        in_specs=[qspec(0), qspec(1), iw_spec, new_spec(0), new_spec(1),
                  pl.BlockSpec((None, lp // LANES, 2 * LANES, LANES), lambda bi, i: (bi, 0, 0, 0)), cspec, cspec]
        + past_specs,
        out_specs=pl.BlockSpec((tq, D_B), lambda bi, i: (bi * nq + i, 0)),
        out_shape=jax.ShapeDtypeStruct((b * t, D_B), BF16),
        scratch_shapes=[pltpu.VMEM((tq, lp), F32), pltpu.VMEM((rows, lp), F32), pltpu.VMEM((rows, LANES), F32),
                        pltpu.VMEM((rows, LANES), F32), pltpu.VMEM((rows, HD_B), F32)] + mask_scratch + tail_scratch,
        compiler_params=_params("parallel", "arbitrary"),
        name="dsa_attention",
    )(zq, zq, iw, kv_new, kv_new, ik2, d0, d1, *past_args)


def _mem_kernel(q_ref, k_ref, v_ref, o_ref):
    nt = (((1,), (1,)), ((), ()))
    for h in range(H_M):
        sl = slice(h * HD_M, (h + 1) * HD_M)
        s = lax.dot_general(q_ref[:, sl], k_ref[:, sl], nt, preferred_element_type=F32) * (HD_M ** -0.5)
        m = jnp.max(s, axis=1, keepdims=True)
        p = jnp.exp(s - m)
        den = jnp.sum(p, axis=1, keepdims=True)
        o = jnp.dot(p.astype(BF16), v_ref[:, sl], preferred_element_type=F32)
        o_ref[:, sl] = (o / den).astype(o_ref.dtype)


def _mem_attention(zq, row0, b, t, mk, mv, tq):
    nq = t // tq
    blk0 = row0 // tq
    return pl.pallas_call(
        _mem_kernel,
        grid=(b, nq),
        in_specs=[pl.BlockSpec((tq, D_M), lambda bi, i: (blk0 + bi * nq + i, 2)),
                  pl.BlockSpec((None, N_MEM, D_M), lambda bi, i: (bi, 0, 0)),
                  pl.BlockSpec((None, N_MEM, D_M), lambda bi, i: (bi, 0, 0))],
        out_specs=pl.BlockSpec((tq, D_M), lambda bi, i: (bi * nq + i, 0)),
        out_shape=jax.ShapeDtypeStruct((b * t, D_M), BF16),
        compiler_params=_params("parallel", "parallel"),
        name="memory_attention",
    )(zq, mk, mv)


def _mix_kernel(yap_ref, yas_ref, gp_ref, gs_ref, obp_ref, obs_ref, omp_ref, oms_ref, ga_ref, gb_ref, gm_ref,
                pa_ref, pb_ref, pm_ref, o_ref, *, na):
    i = pl.program_id(1)
    oa = (_pick(i, na, yap_ref, yas_ref) * _pick(i, na, gp_ref, gs_ref)).astype(BF16)
    acc = _sigmoid(ga_ref[...]) * jnp.dot(oa, pa_ref[...], preferred_element_type=F32)
    acc = acc + _sigmoid(gb_ref[...]) * jnp.dot(_pick(i, na, obp_ref, obs_ref), pb_ref[...],
                                                preferred_element_type=F32)
    acc = acc + _sigmoid(gm_ref[...]) * jnp.dot(_pick(i, na, omp_ref, oms_ref), pm_ref[...],
                                                preferred_element_type=F32)
    o_ref[...] = acc.astype(o_ref.dtype)


def _mix(ya2, g2, ob2, om2, zg, pa, pb, pm, tm, tn):
    n = zg.shape[0]
    nj = D_MODEL // tn
    na = g2[0].shape[0] // tm
    two = lambda w: (pl.BlockSpec((tm, w), lambda j, i: (jnp.minimum(i, na - 1), 0)),
                     pl.BlockSpec((tm, w), lambda j, i: (jnp.maximum(i - na, 0), 0)))
    gate = lambda br: pl.BlockSpec((tm, tn), lambda j, i, br=br: (i, br * nj + j))
    wt = pl.BlockSpec((D_A, tn), lambda j, i: (0, j))
    return pl.pallas_call(
        functools.partial(_mix_kernel, na=na),
        grid=(nj, n // tm),
        in_specs=[*two(D_A), *two(D_A), *two(D_B), *two(D_M), gate(0), gate(1), gate(2), wt, wt, wt],
        out_specs=pl.BlockSpec((tm, tn), lambda j, i: (i, j)),
        out_shape=jax.ShapeDtypeStruct((n, D_MODEL), BF16),
        compiler_params=_params("parallel", "parallel"),
        name="branch_mix",
    )(*ya2, *g2, *ob2, *om2, zg, zg, zg, pa, pb, pm)


def _wo_kernel(m_ref, w_ref, xa_ref, xb_ref, g_ref, h_ref, u_ref, *, na):
    x = _pick(pl.program_id(0), na, xa_ref, xb_ref)
    h = x + jnp.dot(m_ref[...], w_ref[...], preferred_element_type=F32)
    h_ref[...] = h
    y = h * lax.rsqrt(jnp.mean(h * h, axis=-1, keepdims=True) + NORM_EPS)
    u_ref[...] = (y * g_ref[...]).astype(u_ref.dtype)


def _wo_residual_norm(mixed, w_o, xa, xb, g, tm):
    n = mixed.shape[0]
    na = xa.shape[0] // tm
    tile = pl.BlockSpec((tm, D_MODEL), lambda i: (i, 0))
    sa, sb = _two_part_specs(tm, D_MODEL, na)
    return pl.pallas_call(
        functools.partial(_wo_kernel, na=na),
        grid=(n // tm,),
        in_specs=[tile, pl.BlockSpec((D_MODEL, D_MODEL), lambda i: (0, 0)), sa, sb,
                  pl.BlockSpec((1, D_MODEL), lambda i: (0, 0))],
        out_specs=[tile, tile],
        out_shape=[jax.ShapeDtypeStruct((n, D_MODEL), F32), jax.ShapeDtypeStruct((n, D_MODEL), BF16)],
        compiler_params=_params("parallel"),
        name="wo_residual_norm",
    )(mixed, w_o, xa, xb, g.reshape(1, D_MODEL))


def _moe_kernel(te_ref, nu_ref, x_ref, gate_ref, w1_ref, w3_ref, w2_ref, o_ref):
    i, j = pl.program_id(0), pl.program_id(1)

    @pl.when((i >= nu_ref[0]) & (j == 0))
    def _():
        o_ref[...] = jnp.zeros(o_ref.shape, F32)

    @pl.when(i < nu_ref[0])
    def _():
        x = x_ref[...]
        a = jnp.dot(x, w1_ref[...], preferred_element_type=F32)
        b = jnp.dot(x, w3_ref[...], preferred_element_type=F32)
        lane = lax.broadcasted_iota(jnp.int32, gate_ref.shape, 1)
        ge = jnp.sum(jnp.where(lane == j, gate_ref[...], 0.0), axis=1, keepdims=True)
        hid = (a * _sigmoid(a)) * b * ge
        out = jnp.dot(hid.astype(BF16), w2_ref[...], preferred_element_type=F32)

        @pl.when(j == 0)
        def _():
            o_ref[...] = out

        @pl.when(j > 0)
        def _():
            o_ref[...] += out


def _moe_grouped(x_sorted, gate_sorted, tile_experts, n_used, w1, w3, w2):
    npad = x_sorted.shape[0]
    ew = lambda i, j, te, nu: (te[j, i], 0, 0)
    return pl.pallas_call(
        _moe_kernel,
        grid_spec=pltpu.PrefetchScalarGridSpec(
            num_scalar_prefetch=2,
            grid=(npad // MOE_TM, TOP_K_INNER),
            in_specs=[pl.BlockSpec((MOE_TM, D_MODEL), lambda i, j, te, nu: (i, 0)),
                      pl.BlockSpec((MOE_TM, TOP_K_INNER), lambda i, j, te, nu: (i, 0)),
                      pl.BlockSpec((None, D_MODEL, D_EXPERT), ew), pl.BlockSpec((None, D_MODEL, D_EXPERT), ew),
                      pl.BlockSpec((None, D_EXPERT, D_MODEL), ew)],
            out_specs=pl.BlockSpec((MOE_TM, D_MODEL), lambda i, j, te, nu: (i, 0))),
        out_shape=jax.ShapeDtypeStruct((npad, D_MODEL), F32),
        compiler_params=_params("parallel", "arbitrary"),
        name="moe_grouped",
    )(tile_experts, n_used, x_sorted, gate_sorted, w1, w3, w2)


def _route(logits, b_grp, b_rt):
    n = logits.shape[0]
    g_logits = logits[:, :N_GROUPS] + b_grp
    e_logits = (logits[:, N_GROUPS:N_GROUPS + N_EXPERTS] + b_rt).reshape(n, N_GROUPS, EXPERTS_PER_GROUP)
    g_idx = jnp.argmax(g_logits, axis=-1).astype(jnp.int32)
    g_w = jnp.max(jax.nn.softmax(g_logits, axis=-1), axis=-1, keepdims=True)
    onehot_g = g_idx[:, None] == jnp.arange(N_GROUPS, dtype=jnp.int32)[None, :]
    e_in = jnp.sum(jnp.where(onehot_g[:, :, None], e_logits, 0.0), axis=1)
    lane = jnp.arange(EXPERTS_PER_GROUP, dtype=jnp.int32)[None, :]
    i1 = jnp.argmax(e_in, axis=-1).astype(jnp.int32)
    v1 = jnp.max(e_in, axis=-1, keepdims=True)
    rest = jnp.where(lane == i1[:, None], -jnp.inf, e_in)
    i2 = jnp.argmax(rest, axis=-1).astype(jnp.int32)
    v2 = jnp.max(rest, axis=-1, keepdims=True)
    w12 = g_w * jax.nn.softmax(jnp.concatenate([v1, v2], axis=-1), axis=-1)
    lo, hi = jnp.minimum(i1, i2), jnp.maximum(i1, i2)
    gate2 = jnp.where((i1 < i2)[:, None], w12, w12[:, ::-1])
    n_pairs = EXPERTS_PER_GROUP * (EXPERTS_PER_GROUP - 1) // 2
    n_cls = N_GROUPS * n_pairs
    cls = g_idx * n_pairs + lo * (2 * EXPERTS_PER_GROUP - lo - 1) // 2 + hi - lo - 1
    pair_lo = jnp.array([a for a in range(EXPERTS_PER_GROUP) for _ in range(a + 1, EXPERTS_PER_GROUP)], jnp.int32)
    pair_hi = jnp.array([c for a in range(EXPERTS_PER_GROUP) for c in range(a + 1, EXPERTS_PER_GROUP)], jnp.int32)
    cls_ids = jnp.arange(n_cls, dtype=jnp.int32)
    cls_experts = jnp.stack([(cls_ids // n_pairs) * EXPERTS_PER_GROUP + jnp.tile(pair_lo, N_GROUPS),
                             (cls_ids // n_pairs) * EXPERTS_PER_GROUP + jnp.tile(pair_hi, N_GROUPS)])

    onehot_c = cls[:, None] == cls_ids[None, :]
    counts = jnp.sum(onehot_c.astype(jnp.int32), axis=0)
    tiles = (counts + MOE_TM - 1) // MOE_TM
    tile_end = jnp.cumsum(tiles)
    slot0 = (tile_end - tiles) * MOE_TM
    start = jnp.cumsum(counts) - counts
    order = jnp.argsort(cls, stable=True).astype(jnp.int32)
    rank = jnp.sum(jnp.where(onehot_c, jnp.cumsum(onehot_c.astype(jnp.int32), axis=0) - 1, 0), axis=1)
    inv = jnp.sum(jnp.where(onehot_c, slot0[None, :], 0), axis=1) + rank
    npad = n + n_cls * MOE_TM
    slots = jnp.arange(npad, dtype=jnp.int32)
    slot_cls = jnp.minimum(jnp.sum(jnp.where(slots[:, None] >= tile_end[None, :] * MOE_TM, 1, 0), axis=1), n_cls - 1)
    in_cls = slot_cls[:, None] == cls_ids[None, :]
    pos = slots - jnp.sum(jnp.where(in_cls, slot0[None, :], 0), axis=1)
    used = pos < jnp.sum(jnp.where(in_cls, counts[None, :], 0), axis=1)
    src = jnp.where(used, pos + jnp.sum(jnp.where(in_cls, start[None, :], 0), axis=1), 0)
    perm = order[src]
    gate_sorted = jnp.where(used[:, None], gate2[perm], 0.0)
    tile_in_cls = slot_cls[::MOE_TM, None] == cls_ids[None, :]
    tile_experts = jnp.sum(jnp.where(tile_in_cls[None], cls_experts[:, None, :], 0), axis=-1).astype(jnp.int32)
    return perm, inv, gate_sorted, tile_experts, tile_end[-1:].astype(jnp.int32)


def _final_kernel(h_ref, m_ref, g_ref, yp_ref, ys_ref, *, na):
    h = h_ref[...] + m_ref[...]
    y = h * lax.rsqrt(jnp.mean(h * h, axis=-1, keepdims=True) + NORM_EPS) * g_ref[...]
    i = pl.program_id(0)

    @pl.when(i < na)
    def _():
        yp_ref[...] = y

    @pl.when(i >= na)
    def _():
        ys_ref[...] = y


def _final_norm(h, moe, g, n_p, tm):
    n = h.shape[0]
    na = n_p // tm
    tile = pl.BlockSpec((tm, D_MODEL), lambda i: (i, 0))
    sa, sb = _two_part_specs(tm, D_MODEL, na)
    return pl.pallas_call(
        functools.partial(_final_kernel, na=na),
        grid=(n // tm,),
        in_specs=[tile, tile, pl.BlockSpec((1, D_MODEL), lambda i: (0, 0))],
        out_specs=[sa, sb],
        out_shape=[jax.ShapeDtypeStruct((n_p, D_MODEL), F32), jax.ShapeDtypeStruct((n - n_p, D_MODEL), F32)],
        compiler_params=_params("arbitrary"),
        name="residual_final_norm",
    )(h, moe, g.reshape(1, D_MODEL))


def _pad_cols(w, width):
    return jnp.pad(w, ((0, 0), (0, width - w.shape[1])))


def _pad_rows(w, height):
    return jnp.pad(w, ((0, height - w.shape[0]), (0, 0)))


def _pad_lora(x):
    o = 3 * D_A
    pad = lambda t, w: jnp.pad(t, [(0, 0)] * (t.ndim - 1) + [(0, w - t.shape[-1])])
    return jnp.concatenate([x[..., :o], pad(x[..., o:o + W_LORA], WL_PAD),
                            pad(x[..., o + W_LORA:o + W_LORA + A_LORA], AL_PAD),
                            pad(x[..., o + W_LORA + A_LORA:], GL_PAD)], axis=-1)


def _unpad_lora(x):
    o = 3 * D_A
    return jnp.concatenate([x[..., :o], x[..., o:o + W_LORA], x[..., o + WL_PAD:o + WL_PAD + A_LORA],
                            x[..., o + WL_PAD + AL_PAD:o + WL_PAD + AL_PAD + G_LORA]], axis=-1)


def kernel(x_prompt, x_sample, cache_dsa_k, cache_dsa_v, cache_idx_k, state_rwkv_shift, state_rwkv_wkv, cache_mem_k, cache_mem_v, mem_prompt, rel_bias, g_attn, w_in, rwkv_mu, rwkv_w0, rwkv_w_dec, rwkv_a0, rwkv_w_a, rwkv_w_g, rwkv_k_k, rwkv_k_a, rwkv_r_k, rwkv_lnx_w, rwkv_lnx_b, g_mem, w_mem_kv, p_a, p_b, p_m, w_o, g_ffn, w_grp, b_grp, w_rt, b_rt, w1, w3, w2, g_final):
    assert w_in.shape[0] == 1, "single layer"
    bp, tp, _ = x_prompt.shape
    bs, ts, _ = x_sample.shape
    past = cache_dsa_k.shape[2]
    n_p, n_s = bp * tp, bs * ts
    n = n_p + n_s
    tm = 1024
    xp = x_prompt.reshape(n_p, D_MODEL)
    xs = x_sample.reshape(n_s, D_MODEL)

    offs = [0]
    for s in IN_SIZES:
        offs.append(offs[-1] + s)
    seg = lambda i: w_in[0][:, offs[i]:offs[i + 1]]
    w_a_cols = _pad_lora(seg(0)).astype(BF16)
    w_qim = jnp.concatenate([seg(1), seg(4), seg(7)], axis=1).astype(BF16)
    w_kvi = _pad_cols(jnp.concatenate([seg(2), seg(3), seg(5), seg(6)], axis=1), KVI_PAD).astype(BF16)
    w_gate = seg(8).astype(BF16)
    rw = dict(mu=_pad_lora(rwkv_mu[0])[None], w0=rwkv_w0, a0=rwkv_a0, k_k=rwkv_k_k, k_a=rwkv_k_a,
              w_dec=_pad_rows(rwkv_w_dec[0], WL_PAD).astype(BF16), w_a=_pad_rows(rwkv_w_a[0], AL_PAD).astype(BF16),
              w_g=_pad_rows(rwkv_w_g[0], GL_PAD).astype(BF16))

    u = _rmsnorm_bf16(xp, xs, g_attn[0], 512)
    z_a = _matmul(u, w_a_cols, F32, tm, A_PAD // 4, "proj_rwkv")
    z_qim = _matmul(u, w_qim, BF16, tm, 1024, "proj_queries")
    z_kvi, kv_bf = _matmul_kv(u, w_kvi, tm)
    z_g = _matmul(u, w_gate, F32, tm, 1024, "proj_gates")

    mem = mem_prompt.reshape(bp * N_MEM, D_MODEL)
    um = _rmsnorm_bf16(mem[:bp * N_MEM // 2], mem[bp * N_MEM // 2:], g_mem[0], 512)
    mkv = _matmul(um, w_mem_kv[0].astype(BF16), F32, 1024, 1024, "proj_mem_kv")
    mk_p = mkv[:, :D_M].reshape(bp, N_MEM, D_M)
    mv_p = mkv[:, D_M:].reshape(bp, N_MEM, D_M)

    lw_vec, lb_vec, rk_vec = rwkv_lnx_w[0], rwkv_lnx_b[0], rwkv_r_k[0].reshape(D_A)

    def rwkv_group(row0, nseq, t, shift0, wkv0, tb, tt):
        r, w, k, v, kk, a, g, last = _rwkv_prep(z_a, row0, nseq, t, shift0, rw, tb)
        y, s_fin = _wkv(r, w, k, v, kk, a, _state_to_tiles(wkv0), _lane_tile(lw_vec), _lane_tile(lb_vec),
                        _lane_tile(rk_vec), tt)
        return y.reshape(nseq * t, D_A), g, _state_from_tiles(s_fin, nseq), _unpad_lora(last)

    ya_p, g_p, wkv_p, shift_p = rwkv_group(0, bp, tp, jnp.zeros((bp, 1, A_PAD), F32),
                                           jnp.zeros((bp, H_A, HD_A, HD_A), F32), 256, 32)
    ya_s, g_s, wkv_s, shift_s = rwkv_group(n_p, bs, ts, _pad_lora(state_rwkv_shift[0]), state_rwkv_wkv[0], ts, ts)

    k_new, v_new = z_kvi[:, :D_KV], z_kvi[:, D_KV:2 * D_KV]
    ik_new = z_kvi[:, 2 * D_KV:2 * D_KV + D_I]
    iw = z_kvi[:, 2 * D_KV + D_I:2 * D_KV + D_I + H_I]
    grp = lambda t, sl, b, tlen: t[sl].reshape(b, tlen, t.shape[-1])
    sp, ss = slice(0, n_p), slice(n_p, n)
    ob_p = _dsa(z_qim, kv_bf, 0, bp, tp, iw[sp], grp(ik_new, sp, bp, tp).astype(BF16), rel_bias,
                Q_BLOCK if tp % Q_BLOCK == 0 else tp, 0, tp)
    l_s = past + ts
    ik_s = jnp.pad(jnp.concatenate([cache_idx_k[0].astype(BF16), grp(ik_new, ss, bs, ts).astype(BF16)], axis=1),
                   ((0, 0), (0, past + KEY_TILE - l_s), (0, 0)))
    ob_s = _dsa(z_qim, kv_bf, n_p, bs, ts, iw[ss], ik_s, rel_bias, Q_BLOCK if ts % Q_BLOCK == 0 else ts, past, l_s,
                cache_dsa_k[0].reshape(bs, past, D_KV), cache_dsa_v[0].reshape(bs, past, D_KV))

    om_p = _mem_attention(z_qim, 0, bp, tp, mk_p.astype(BF16), mv_p.astype(BF16), 256)
    om_s = _mem_attention(z_qim, n_p, bs, ts, cache_mem_k[0].reshape(bs, N_MEM, D_M).astype(BF16),
                          cache_mem_v[0].reshape(bs, N_MEM, D_M).astype(BF16), ts)

    mixed = _mix((ya_p, ya_s), (g_p, g_s), (ob_p, ob_s), (om_p, om_s), z_g, p_a[0].astype(BF16), p_b[0].astype(BF16),
                 p_m[0].astype(BF16), 256, 1024)
    h, u2 = _wo_residual_norm(mixed, w_o[0].astype(BF16), xp, xs, g_ffn[0], 256)

    w_route = _pad_cols(jnp.concatenate([w_grp[0], w_rt[0]], axis=1), LANES).astype(BF16)
    logits = _matmul(u2, w_route, F32, tm, LANES, "moe_router")
    perm, inv, gate_sorted, tile_experts, n_used = _route(logits, b_grp[0], b_rt[0])
    moe_sorted = _moe_grouped(u2[perm], gate_sorted, tile_experts, n_used,
                              _cast_bf16(w1[0]), _cast_bf16(w3[0]), _cast_bf16(w2[0]))
    y_p, y_s = _final_norm(h, moe_sorted[inv], g_final, n_p, 512)

    st = lambda t, b, tlen, shape: t.reshape((1, b, tlen) + shape)
    return (y_p.reshape(bp, tp, D_MODEL), y_s.reshape(bs, ts, D_MODEL),
            st(k_new[sp], bp, tp, (N_KV_B, HD_B)), st(v_new[sp], bp, tp, (N_KV_B, HD_B)), st(ik_new[sp], bp, tp, (D_I,)),
            shift_p[None], wkv_p[None],
            mk_p.reshape(1, bp, N_MEM, H_M, HD_M), mv_p.reshape(1, bp, N_MEM, H_M, HD_M),
            st(k_new[ss], bs, ts, (N_KV_B, HD_B)), st(v_new[ss], bs, ts, (N_KV_B, HD_B)), st(ik_new[ss], bs, ts, (D_I,)),
            shift_s[None], wkv_s[None])
```

```python
import functools
import math

import jax
import jax.numpy as jnp
from jax import lax
from jax.experimental import pallas as pl
from jax.experimental.pallas import tpu as pltpu

F32 = jnp.float32
BF16 = jnp.bfloat16

LANES = 128
D_MODEL = 2048
CHUNK = 64
NORM_EPS = 1e-6
H_A, HD_A = 16, 64
D_A = H_A * HD_A
W_LORA, A_LORA, G_LORA = 64, 64, 160
LNX_EPS = 64e-5
A_COLS = 3 * D_A + W_LORA + A_LORA + G_LORA
H_B, N_KV_B, HD_B = 8, 2, 128
G_B = H_B // N_KV_B
D_B = H_B * HD_B
D_KV = N_KV_B * HD_B
H_I, D_I = 16, 64
TOPK_MAX = 256
Q_BLOCK = 128
REL_BUCKETS = 32
REL_MAX_DIST = 128
N_MEM, H_M, HD_M = 256, 4, 256
D_M = H_M * HD_M
N_BRANCH = 3
IN_SIZES = (A_COLS, D_B, D_KV, D_KV, H_I * D_I, D_I, H_I, D_M, N_BRANCH * D_MODEL)
N_GROUPS, EXPERTS_PER_GROUP = 4, 4
N_EXPERTS = N_GROUPS * EXPERTS_PER_GROUP
TOP_K_INNER = 2
D_EXPERT = 512

WL_PAD, AL_PAD, GL_PAD = 128, 128, 256
A_PAD = 3 * D_A + WL_PAD + AL_PAD + GL_PAD
KVI_PAD = 2 * D_KV + LANES
KEY_TILE = 2 * LANES
SEQ_PER_TILE = LANES // H_A
MOE_TM = 256
INT_MIN = -(2 ** 31)
VMEM_LIMIT = 48 * 1024 * 1024


def _params(*sem):
    return pltpu.CompilerParams(dimension_semantics=sem, vmem_limit_bytes=VMEM_LIMIT)


def _two_part_specs(tm, width, na):
    return (pl.BlockSpec((tm, width), lambda i: (jnp.minimum(i, na - 1), 0)),
            pl.BlockSpec((tm, width), lambda i: (jnp.maximum(i - na, 0), 0)))


def _pick(i, na, a_ref, b_ref):
    return jnp.where(i < na, a_ref[...], b_ref[...])


def _norm2_kernel(xa_ref, xb_ref, g_ref, o_ref, *, na):
    x = _pick(pl.program_id(0), na, xa_ref, xb_ref)
    y = x * lax.rsqrt(jnp.mean(x * x, axis=-1, keepdims=True) + NORM_EPS)
    o_ref[...] = (y * g_ref[...]).astype(o_ref.dtype)


def _rmsnorm_bf16(xa, xb, g, tm):
    d = xa.shape[1]
    na, nb = xa.shape[0] // tm, xb.shape[0] // tm
    sa, sb = _two_part_specs(tm, d, na)
    return pl.pallas_call(
        functools.partial(_norm2_kernel, na=na),
        grid=(na + nb,),
        in_specs=[sa, sb, pl.BlockSpec((1, d), lambda i: (0, 0))],
        out_specs=pl.BlockSpec((tm, d), lambda i: (i, 0)),
        out_shape=jax.ShapeDtypeStruct(((na + nb) * tm, d), BF16),
        compiler_params=_params("parallel"),
        name="rmsnorm_bf16",
    )(xa, xb, g.reshape(1, d))


def _mm_kernel(a_ref, w_ref, o_ref):
    o_ref[...] = jnp.dot(a_ref[...], w_ref[...], preferred_element_type=F32).astype(o_ref.dtype)


def _matmul(a, w, out_dtype, tm, tn, name):
    n, k = a.shape
    m = w.shape[1]
    return pl.pallas_call(
        _mm_kernel,
        grid=(n // tm, m // tn),
        in_specs=[pl.BlockSpec((tm, k), lambda i, j: (i, 0)), pl.BlockSpec((k, tn), lambda i, j: (0, j))],
        out_specs=pl.BlockSpec((tm, tn), lambda i, j: (i, j)),
        out_shape=jax.ShapeDtypeStruct((n, m), out_dtype),
        compiler_params=_params("parallel", "parallel"),
        name=name,
    )(a, w)


def _mm_kv_kernel(a_ref, w_ref, o_ref, kv_ref):
    acc = jnp.dot(a_ref[...], w_ref[...], preferred_element_type=F32)
    o_ref[...] = acc
    kv_ref[...] = acc[:, :2 * D_KV].astype(kv_ref.dtype)


def _matmul_kv(a, w, tm):
    n, k = a.shape
    m = w.shape[1]
    return pl.pallas_call(
        _mm_kv_kernel,
        grid=(n // tm,),
        in_specs=[pl.BlockSpec((tm, k), lambda i: (i, 0)), pl.BlockSpec((k, m), lambda i: (0, 0))],
        out_specs=[pl.BlockSpec((tm, m), lambda i: (i, 0)), pl.BlockSpec((tm, 2 * D_KV), lambda i: (i, 0))],
        out_shape=[jax.ShapeDtypeStruct((n, m), F32), jax.ShapeDtypeStruct((n, 2 * D_KV), BF16)],
        compiler_params=_params("parallel"),
        name="proj_kv",
    )(a, w)


def _cast_kernel(x_ref, o_ref):
    o_ref[...] = x_ref[...].astype(o_ref.dtype)


def _cast_bf16(x):
    e, r, c = x.shape
    spec = pl.BlockSpec((None, r, c), lambda i: (i, 0, 0))
    return pl.pallas_call(
        _cast_kernel, grid=(e,), in_specs=[spec], out_specs=spec,
        out_shape=jax.ShapeDtypeStruct(x.shape, BF16), compiler_params=_params("parallel"), name="cast_bf16",
    )(x)


def _softplus(x):
    return jnp.maximum(x, 0.0) + jnp.log1p(jnp.exp(-jnp.abs(x)))


def _sigmoid(x):
    return 0.5 * jnp.tanh(0.5 * x) + 0.5


def _rwkv_prep_kernel(z_ref, sh_ref, mu_ref, w0_ref, wdec_ref, a0_ref, wa_ref, wg_ref, kk_ref, ka_ref,
                      r_out, w_out, k_out, v_out, kk_out, a_out, g_out, last_out, carry_ref):
    tb = z_ref.shape[0]

    @pl.when(pl.program_id(1) == 0)
    def _():
        carry_ref[...] = sh_ref[...]

    z = z_ref[...]
    row = lax.broadcasted_iota(jnp.int32, (tb, 1), 0)
    prev = jnp.where(row == 0, carry_ref[...], pltpu.roll(z, 1, axis=0))
    carry_ref[...] = z[tb - 1:tb, :]
    last_out[...] = z[tb - 1:tb, :]
    zm = z + (prev - z) * mu_ref[...]
    r = zm[:, 0:D_A]
    k = zm[:, D_A:2 * D_A]
    v = zm[:, 2 * D_A:3 * D_A]
    o = 3 * D_A
    wl = zm[:, o:o + WL_PAD]
    al = zm[:, o + WL_PAD:o + WL_PAD + AL_PAD]
    gl = zm[:, o + WL_PAD + AL_PAD:]
    lw = jnp.dot(jnp.tanh(wl).astype(BF16), wdec_ref[...], preferred_element_type=F32)
    wv = -_softplus(-(w0_ref[...] + lw)) - 0.5
    a = _sigmoid(a0_ref[...] + jnp.dot(al.astype(BF16), wa_ref[...], preferred_element_type=F32))
    r_out[...] = r
    w_out[...] = jnp.exp(-jnp.exp(wv))
    k_out[...] = k * (1.0 + (a - 1.0) * ka_ref[...])
    v_out[...] = v
    kk_out[...] = k * kk_ref[...]
    a_out[...] = a
    g_out[...] = jnp.dot(_sigmoid(gl).astype(BF16), wg_ref[...], preferred_element_type=F32)


def _rwkv_prep(z, row0, nseq, t, shift0, wts, tb):
    nt = t // tb
    blk0 = row0 // tb
    row = lambda c: pl.BlockSpec((1, c), lambda s, i: (0, 0))
    full = lambda a, b: pl.BlockSpec((a, b), lambda s, i: (0, 0))
    out_spec = pl.BlockSpec((None, tb, D_A), lambda s, i: (s, i, 0))
    out_sds = jax.ShapeDtypeStruct((nseq, t, D_A), F32)
    g_spec = pl.BlockSpec((tb, D_A), lambda s, i: (s * nt + i, 0))
    g_sds = jax.ShapeDtypeStruct((nseq * t, D_A), F32)
    return pl.pallas_call(
        _rwkv_prep_kernel,
        grid=(nseq, nt),
        in_specs=[pl.BlockSpec((tb, A_PAD), lambda s, i: (blk0 + s * nt + i, 0)),
                  pl.BlockSpec((None, 1, A_PAD), lambda s, i: (s, 0, 0)),
                  row(A_PAD), row(D_A), full(WL_PAD, D_A), row(D_A), full(AL_PAD, D_A), full(GL_PAD, D_A),
                  row(D_A), row(D_A)],
        out_specs=[out_spec] * 6 + [g_spec, pl.BlockSpec((None, 1, A_PAD), lambda s, i: (s, 0, 0))],
        out_shape=[out_sds] * 6 + [g_sds, jax.ShapeDtypeStruct((nseq, 1, A_PAD), F32)],
        scratch_shapes=[pltpu.VMEM((1, A_PAD), F32)],
        compiler_params=_params("arbitrary", "arbitrary"),
        name="rwkv_prep",
    )(z, shift0, wts["mu"], wts["w0"], wts["w_dec"], wts["a0"], wts["w_a"], wts["w_g"], wts["k_k"], wts["k_a"])


def _wkv_kernel(r_ref, w_ref, k_ref, v_ref, kk_ref, a_ref, s0_ref, lw_ref, lb_ref, rk_ref,
                y_ref, st_ref, s_ref, ab_ref, tma_ref, tmb_ref, ytm_ref, *, n_tblocks):
    tt = r_ref.shape[1]
    low = lax.broadcasted_iota(jnp.int32, (SEQ_PER_TILE, LANES), 1) < HD_A

    @pl.when(pl.program_id(1) == 0)
    def _():
        s_ref[...] = s0_ref[...]

    n_pairs = tt // 2

    def to_tiles(buf_ref, j):
        j = jnp.minimum(j, n_pairs - 1)
        for ai, ref in enumerate((r_ref, w_ref, k_ref, v_ref, kk_ref, a_ref)):
            x0 = ref[:, 2 * j, :]
            x1 = ref[:, 2 * j + 1, :]
            rows = []
            for h in range(H_A):
                sl = slice((h // 2) * LANES, (h // 2 + 1) * LANES)
                if h % 2 == 0:
                    rows.append(jnp.where(low, x0[:, sl], pltpu.roll(x1[:, sl], HD_A, axis=1)))
                else:
                    rows.append(jnp.where(low, pltpu.roll(x0[:, sl], HD_A, axis=1), x1[:, sl]))
            m = jnp.concatenate(rows, axis=0).T
            buf_ref[ai, 0] = m[:HD_A]
            buf_ref[ai, 1] = m[HD_A:]

    def step(buf_ref, u, t):
        kk = buf_ref[4, u]
        ss = jnp.sum(kk * kk, axis=0, keepdims=True)
        kkn = kk * lax.rsqrt(jnp.maximum(ss, 1e-24))
        ab_ref[0] = -kkn
        ab_ref[1] = kkn * buf_ref[5, u]
        halves = []
        for rows in (slice(0, HD_A // 2), slice(HD_A // 2, HD_A)):
            vh = buf_ref[3, u, rows, :]
            sa = jnp.zeros((HD_A // 2, LANES), F32)
            for k in range(HD_A):
                sa = sa + s_ref[k, rows, :] * ab_ref[0, k:k + 1, :]
            yh = jnp.zeros((HD_A // 2, LANES), F32)
            for k in range(HD_A):
                s_new = (s_ref[k, rows, :] * buf_ref[1, u, k:k + 1, :] + sa * ab_ref[1, k:k + 1, :]
                         + vh * buf_ref[2, u, k:k + 1, :])
                s_ref[k, rows, :] = s_new
                yh = yh + s_new * buf_ref[0, u, k:k + 1, :]
            halves.append(yh)
        y = jnp.concatenate(halves, axis=0)
        vv = buf_ref[3, u]
        mean = jnp.mean(y, axis=0, keepdims=True)
        d = y - mean
        var = jnp.mean(d * d, axis=0, keepdims=True)
        yn = d * lax.rsqrt(var + LNX_EPS) * lw_ref[...] + lb_ref[...]
        bonus = jnp.sum(buf_ref[0, u] * buf_ref[2, u] * rk_ref[...], axis=0, keepdims=True) * vv
        ytm_ref[t] = yn + bonus

    to_tiles(tma_ref, 0)

    def two_pairs(m, carry):
        to_tiles(tmb_ref, 2 * m + 1)
        step(tma_ref, 0, 4 * m)
        step(tma_ref, 1, 4 * m + 1)
        to_tiles(tma_ref, 2 * m + 2)
        step(tmb_ref, 0, 4 * m + 2)
        step(tmb_ref, 1, 4 * m + 3)
        return carry

    lax.fori_loop(0, n_pairs // 2, two_pairs, 0)

    for j in range(tt // 2):
        m = jnp.concatenate([ytm_ref[2 * j], ytm_ref[2 * j + 1]], axis=0).T
        for hp in range(H_A // 2):
            even = m[(2 * hp) * SEQ_PER_TILE:(2 * hp + 1) * SEQ_PER_TILE]
            odd = m[(2 * hp + 1) * SEQ_PER_TILE:(2 * hp + 2) * SEQ_PER_TILE]
            cols = slice(hp * LANES, (hp + 1) * LANES)
            y_ref[:, 2 * j, cols] = jnp.where(low, even, pltpu.roll(odd, HD_A, axis=1))
            y_ref[:, 2 * j + 1, cols] = jnp.where(low, pltpu.roll(even, HD_A, axis=1), odd)

    @pl.when(pl.program_id(1) == n_tblocks - 1)
    def _():
        st_ref[...] = s_ref[...]


def _wkv(r, w, k, v, kk, a, s0, lw, lb, rk, tt):
    nseq, t, _ = r.shape
    p = nseq * H_A
    assert tt % 4 == 0 and t % tt == 0 and nseq % SEQ_PER_TILE == 0
    seq = pl.BlockSpec((SEQ_PER_TILE, tt, D_A), lambda g, i: (g, i, 0))
    vec = pl.BlockSpec((HD_A, LANES), lambda g, i: (0, 0))
    st = pl.BlockSpec((HD_A, HD_A, LANES), lambda g, i: (0, 0, g))
    return pl.pallas_call(
        functools.partial(_wkv_kernel, n_tblocks=t // tt),
        grid=(p // LANES, t // tt),
        in_specs=[seq] * 6 + [st, vec, vec, vec],
        out_specs=[seq, st],
        out_shape=[jax.ShapeDtypeStruct((nseq, t, D_A), F32), jax.ShapeDtypeStruct((HD_A, HD_A, p), F32)],
        scratch_shapes=[pltpu.VMEM((HD_A, HD_A, LANES), F32), pltpu.VMEM((2, HD_A, LANES), F32),
                        pltpu.VMEM((6, 2, HD_A, LANES), F32), pltpu.VMEM((6, 2, HD_A, LANES), F32),
                        pltpu.VMEM((tt, HD_A, LANES), F32)],
        compiler_params=_params("arbitrary", "arbitrary"),
        name="wkv_recurrence",
    )(r, w, k, v, kk, a, s0, lw, lb, rk)


def _state_to_tiles(wkv):
    nseq = wkv.shape[0]
    x = wkv.reshape(nseq // SEQ_PER_TILE, SEQ_PER_TILE, H_A, HD_A, HD_A)
    return x.transpose(4, 3, 0, 2, 1).reshape(HD_A, HD_A, nseq * H_A)


def _state_from_tiles(s, nseq):
    x = s.reshape(HD_A, HD_A, nseq // SEQ_PER_TILE, H_A, SEQ_PER_TILE)
    return x.transpose(2, 4, 3, 1, 0).reshape(nseq, H_A, HD_A, HD_A)


def _lane_tile(vec):
    return jnp.repeat(vec.reshape(H_A, HD_A).T, SEQ_PER_TILE, axis=1)


def _dsa_kernel(*refs, q_start, l_valid, topk, keys_on_rows, n_past, lp):
    refs = list(refs)
    q_ref, iq_ref, iw_ref, k_ref, v_ref, ik2_ref, d0_ref, d1_ref = refs[:8]
    del refs[:8]
    kpast_ref = vpast_ref = ktail_ref = vtail_ref = None
    if n_past:
        kpast_ref, vpast_ref = refs[:2]
        del refs[:2]
    o_ref, madd_ref, logit_ref, mx_ref, ls_ref, acc_ref, score_ref, key_ref = refs[:8]
    del refs[:8]
    if not keys_on_rows:
        iwb_ref = refs.pop(0)
    if n_past:
        ktail_ref, vtail_ref = refs
        n_new = k_ref.shape[0]
        for tail_ref, new_ref in ((ktail_ref, k_ref), (vtail_ref, v_ref)):
            tail_ref[...] = jnp.zeros(tail_ref.shape, BF16)
            tail_ref[0:n_new, :] = new_ref[...]
    n_past_tiles = n_past // KEY_TILE

    def kv_tile(new_ref, past_ref, tail_ref, c, cols):
        if not n_past:
            return new_ref[pl.ds(pl.multiple_of(c * KEY_TILE, KEY_TILE), KEY_TILE), cols]
        pc = jnp.minimum(c, n_past_tiles - 1)
        past = past_ref[pl.ds(pl.multiple_of(pc * KEY_TILE, KEY_TILE), KEY_TILE), cols].astype(BF16)
        return jnp.where(c < n_past_tiles, past, tail_ref[:, cols])

    tq = q_ref.shape[0]
    q0 = q_start + pl.program_id(1) * tq
    nt = (((1,), (1,)), ((), ()))
    lane_shift = int(math.log2(LANES))
    chunk_shift = int(math.log2(CHUNK))
    kax = 0 if keys_on_rows else 1
    dc = lax.shift_right_logical(q0, lane_shift)
    n_lane_tiles = jnp.minimum(lax.shift_right_logical(q0 + tq - 1, lane_shift) + 1, lp // LANES)
    n_tiles = lax.shift_right_logical(n_lane_tiles + 1, 1)

    def tile_off(c, width):
        return pl.multiple_of(c * width, width)

    def keys_at(off, width):
        return (pl.ds(off, width), slice(None)) if keys_on_rows else (slice(None), pl.ds(off, width))

    def per_query(x):
        return jnp.sum(x, axis=kax, keepdims=True)

    iw = iw_ref[...] * ((H_I * D_I) ** -0.5)
    if not keys_on_rows:
        for h in range(H_I):
            iwb_ref[h] = jnp.broadcast_to(iw[:, h:h + 1], (tq, LANES))

    def score_lane_tile(c):
        kab = ik2_ref[c]
        acc = [jnp.zeros((LANES, tq) if keys_on_rows else (tq, LANES), F32) for _ in range(2)]
        for hp in range(H_I // 2):
            iq_pair = iq_ref[:, hp * LANES:(hp + 1) * LANES]
            if keys_on_rows:
                d = lax.dot_general(kab, iq_pair, nt, preferred_element_type=F32)
                halves = (d[:LANES], d[LANES:])
                wts = (iw[2 * hp:2 * hp + 1, :], iw[2 * hp + 1:2 * hp + 2, :])
            else:
                d = lax.dot_general(iq_pair, kab, nt, preferred_element_type=F32)
                halves = (d[:, :LANES], d[:, LANES:])
                wts = (iwb_ref[2 * hp], iwb_ref[2 * hp + 1])
            for u in range(2):
                acc[u] = acc[u] + jnp.maximum(halves[u], 0.0) * wts[u]
        score_ref[keys_at(tile_off(c, LANES), LANES)] = acc[0] + acc[1]

    def score_tile(c, carry):
        score_lane_tile(2 * c)
        score_lane_tile(2 * c + 1)
        return carry

    lax.fori_loop(0, n_tiles, score_tile, 0)

    qshape, kshape = ((1, tq), lambda w: (w, 1)) if keys_on_rows else ((tq, 1), lambda w: (1, w))
    qchunk = lax.shift_right_arithmetic(q0 + lax.broadcasted_iota(jnp.int32, qshape, 1 - kax), chunk_shift)

    def visible(lo, width):
        kpos = lo + lax.broadcasted_iota(jnp.int32, kshape(width), kax)
        return (lax.shift_right_arithmetic(kpos, chunk_shift) <= qchunk) & (kpos < l_valid)

    def key_tile(c, carry):
        off = tile_off(c, KEY_TILE)
        bits = pltpu.bitcast(score_ref[keys_at(off, KEY_TILE)] + 0.0, jnp.int32)
        key = bits ^ (lax.shift_right_arithmetic(bits, 31) & jnp.int32(0x7FFFFFFF))
        key_ref[keys_at(off, KEY_TILE)] = jnp.where(visible(off, KEY_TILE), key, jnp.int32(INT_MIN))
        return carry

    lax.fori_loop(0, n_tiles, key_tile, 0)
    kf = jnp.float32(topk)
    sub = 8

    def count(pred):
        def body(c, acc):
            m = jnp.where(pred(key_ref[keys_at(tile_off(c, KEY_TILE), KEY_TILE)]), 1.0, 0.0)
            if keys_on_rows:
                parts = [m[r:r + sub] for r in range(0, KEY_TILE, sub)]
            else:
                parts = [m[:, :LANES], m[:, LANES:]]
            while len(parts) > 1:
                parts = [parts[u] + parts[u + 1] for u in range(0, len(parts), 2)]
            return acc + parts[0]
        acc = lax.fori_loop(0, n_tiles, body, jnp.zeros((sub, tq) if keys_on_rows else (tq, LANES), F32))
        return per_query(acc)

    def search(i, thr):
        cand = thr ^ lax.shift_left(jnp.int32(1), 31 - i)
        return jnp.where(count(lambda kc: kc >= cand) >= kf, cand, thr)

    thr = lax.fori_loop(0, 32, search, jnp.full(qshape, INT_MIN, jnp.int32))

    def store_mask(off, width, sel):
        madd = jnp.where(sel, 0.0, -jnp.inf)
        if keys_on_rows:
            for u in range(width // LANES):
                madd_ref[:, pl.ds(pl.multiple_of(off + u * LANES, LANES), LANES)] = madd[u * LANES:(u + 1) * LANES].T
        else:
            madd_ref[:, pl.ds(off, width)] = madd

    surplus = (count(lambda kc: kc >= thr) > kf) & (thr != jnp.int32(INT_MIN))
    any_surplus = jnp.max(jnp.where(surplus, 1.0, 0.0))

    @pl.when(any_surplus == 0.0)
    def _():
        def select_tile(c, carry):
            off = tile_off(c, KEY_TILE)
            store_mask(off, KEY_TILE, visible(off, KEY_TILE) & (key_ref[keys_at(off, KEY_TILE)] >= thr))
            return carry

        lax.fori_loop(0, n_tiles, select_tile, 0)

    @pl.when(any_surplus > 0.0)
    def _():
        need = kf - count(lambda kc: kc > thr)
        ii = lax.broadcasted_iota(jnp.int32, (LANES, LANES), 0)
        jj = lax.broadcasted_iota(jnp.int32, (LANES, LANES), 1)
        tri = jnp.where((ii >= jj) if keys_on_rows else (ii <= jj), 1.0, 0.0).astype(BF16)

        def select_tile(c, run):
            off = tile_off(c, LANES)
            keyc = key_ref[keys_at(off, LANES)]
            eqf = jnp.where(keyc == thr, 1.0, 0.0)
            if keys_on_rows:
                within = jnp.dot(tri, eqf.astype(BF16), preferred_element_type=F32)
                total = within[LANES - 1:LANES, :]
            else:
                within = jnp.dot(eqf.astype(BF16), tri, preferred_element_type=F32)
                total = within[:, LANES - 1:LANES]
            take = jnp.where(keyc > thr, 1.0, jnp.where(run + within <= need, eqf, 0.0))
            store_mask(off, LANES, visible(off, LANES) & (take > 0.5))
            return run + total

        lax.fori_loop(0, 2 * n_tiles, select_tile, jnp.zeros(qshape, F32))

    off0 = tile_off(dc, LANES)
    off1 = tile_off(jnp.maximum(dc - 1, 0), LANES)
    for g in range(N_KV_B):
        ksl = slice(g * HD_B, (g + 1) * HD_B)
        qs = jnp.concatenate([q_ref[:, (g * G_B + j) * HD_B:(g * G_B + j + 1) * HD_B] for j in range(G_B)], axis=0)

        def logits_tile(c, carry):
            off = tile_off(c, KEY_TILE)
            s = lax.dot_general(qs, kv_tile(k_ref, kpast_ref, ktail_ref, c, ksl), nt, preferred_element_type=F32)
            md = madd_ref[:, pl.ds(off, KEY_TILE)]
            for j in range(G_B):
                logit_ref[j * tq:(j + 1) * tq, pl.ds(off, KEY_TILE)] = s[j * tq:(j + 1) * tq] * (HD_B ** -0.5) + md
            return carry

        lax.fori_loop(0, n_tiles, logits_tile, 0)
        for j in range(G_B):
            logit_ref[j * tq:(j + 1) * tq, pl.ds(off0, LANES)] += d0_ref[g * G_B + j]

        @pl.when(dc > 0)
        def _():
            for j in range(G_B):
                logit_ref[j * tq:(j + 1) * tq, pl.ds(off1, LANES)] += d1_ref[g * G_B + j]

        mx_ref[...] = jnp.full(mx_ref.shape, -jnp.inf, F32)

        def max_tile(c, carry):
            lg = logit_ref[:, pl.ds(tile_off(c, KEY_TILE), KEY_TILE)]
            mx_ref[...] = jnp.maximum(mx_ref[...], jnp.maximum(lg[:, :LANES], lg[:, LANES:]))
            return carry

        lax.fori_loop(0, n_tiles, max_tile, 0)
        m = jnp.max(mx_ref[...], axis=1, keepdims=True)
        ls_ref[...] = jnp.zeros(ls_ref.shape, F32)
        acc_ref[...] = jnp.zeros(acc_ref.shape, F32)

        def pv_tile(c, carry):
            off = tile_off(c, KEY_TILE)
            p = jnp.exp(logit_ref[:, pl.ds(off, KEY_TILE)] - m)
            ls_ref[...] += p[:, :LANES] + p[:, LANES:]
            acc_ref[...] += jnp.dot(p.astype(BF16), kv_tile(v_ref, vpast_ref, vtail_ref, c, ksl),
                                    preferred_element_type=F32)
            return carry

        lax.fori_loop(0, n_tiles, pv_tile, 0)
        res = acc_ref[...] / jnp.sum(ls_ref[...], axis=1, keepdims=True)
        for j in range(G_B):
            h = g * G_B + j
            o_ref[:, h * HD_B:(h + 1) * HD_B] = res[j * tq:(j + 1) * tq].astype(o_ref.dtype)


def _t5_bucket(rel):
    half = REL_BUCKETS // 2
    exact = half // 2
    side = jnp.where(rel > 0, half, 0)
    n = jnp.abs(rel)
    nf = jnp.maximum(n, 1).astype(F32)
    large = exact + (jnp.log(nf / exact) / math.log(REL_MAX_DIST / exact) * (half - exact)).astype(jnp.int32)
    large = jnp.minimum(large, half - 1)
    return side + jnp.where(n < exact, n, large)


def _dsa(zq, kv_new, row0, b, t, iw, ik_all, rel_bias, tq, q_start, l_valid, k_past=None, v_past=None):
    n_past = 0 if k_past is None else k_past.shape[1]
    lp = ik_all.shape[1]
    topk = min(TOPK_MAX, l_valid // 4)
    assert q_start % LANES == 0 and (tq == LANES or t == tq) and lp % KEY_TILE == 0 and row0 % tq == 0
    assert q_start == n_past and n_past % KEY_TILE == 0 and row0 % t == 0
    assert lp == (n_past + KEY_TILE if n_past else t) and l_valid == n_past + t and (not n_past or t <= KEY_TILE)
    nq = t // tq
    blk0 = row0 // tq
    ikt = ik_all.reshape(b, lp // LANES, LANES, D_I)
    zeros = jnp.zeros_like(ikt)
    ik2 = jnp.concatenate([jnp.concatenate([ikt, zeros], axis=-1), jnp.concatenate([zeros, ikt], axis=-1)], axis=2)
    ji = jnp.arange(LANES, dtype=jnp.int32)[None, :] - jnp.arange(tq, dtype=jnp.int32)[:, None]

    def bias_of(rel):
        hit = _t5_bucket(rel)[None, None] == jnp.arange(REL_BUCKETS, dtype=jnp.int32).reshape((1, -1) + (1,) * rel.ndim)
        return jnp.sum(jnp.where(hit, rel_bias.T.reshape((H_B, REL_BUCKETS) + (1,) * rel.ndim), 0.0), axis=1)

    far = bias_of(jnp.full((1, 1), -2 * LANES, jnp.int32))
    d0 = bias_of(ji) - far
    d1 = bias_of(ji - LANES) - far
    keys_on_rows = tq == LANES
    kern = functools.partial(_dsa_kernel, q_start=q_start, l_valid=l_valid, topk=topk, keys_on_rows=keys_on_rows,
                             n_past=n_past, lp=lp)
    qspec = lambda col: pl.BlockSpec((tq, D_B), lambda bi, i: (blk0 + bi * nq + i, col))
    new_spec = lambda col: pl.BlockSpec((t, D_KV), lambda bi, i: (row0 // t + bi, col))
    cspec = pl.BlockSpec((H_B, tq, LANES), lambda bi, i: (0, 0, 0))
    rows = G_B * tq
    if keys_on_rows:
        iw, iw_spec = iw.T, pl.BlockSpec((H_I, tq), lambda bi, i: (0, bi * nq + i))
        mask_scratch = [pltpu.VMEM((lp, tq), F32), pltpu.VMEM((lp, tq), jnp.int32)]
    else:
        iw_spec = pl.BlockSpec((tq, H_I), lambda bi, i: (bi * nq + i, 0))
        mask_scratch = [pltpu.VMEM((tq, lp), F32), pltpu.VMEM((tq, lp), jnp.int32), pltpu.VMEM((H_I, tq, LANES), F32)]
    past_specs, past_args, tail_scratch = [], [], []
    if n_past:
        past_specs = [pl.BlockSpec((None, n_past, D_KV), lambda bi, i: (bi, 0, 0))] * 2
        past_args = [k_past, v_past]
        tail_scratch = [pltpu.VMEM((KEY_TILE, D_KV), BF16)] * 2
    return pl.pallas_call(
        kern,
        grid=(b, nq),
        in_specs=[qspec(0), qspec(1), iw_spec, new_spec(0), new_spec(1),
                  pl.BlockSpec((None, lp // LANES, 2 * LANES, LANES), lambda bi, i: (bi, 0, 0, 0)), cspec, cspec]
        + past_specs,
        out_specs=pl.BlockSpec((tq, D_B), lambda bi, i: (bi * nq + i, 0)),
        out_shape=jax.ShapeDtypeStruct((b * t, D_B), BF16),
        scratch_shapes=[pltpu.VMEM((tq, lp), F32), pltpu.VMEM((rows, lp), F32), pltpu.VMEM((rows, LANES), F32),
                        pltpu.VMEM((rows, LANES), F32), pltpu.VMEM((rows, HD_B), F32)] + mask_scratch + tail_scratch,
        compiler_params=_params("parallel", "arbitrary"),
        name="dsa_attention",
    )(zq, zq, iw, kv_new, kv_new, ik2, d0, d1, *past_args)


def _mem_kernel(q_ref, k_ref, v_ref, o_ref):
    nt = (((1,), (1,)), ((), ()))
    for h in range(H_M):
        sl = slice(h * HD_M, (h + 1) * HD_M)
        s = lax.dot_general(q_ref[:, sl], k_ref[:, sl], nt, preferred_element_type=F32) * (HD_M ** -0.5)
        m = jnp.max(s, axis=1, keepdims=True)
        p = jnp.exp(s - m)
        den = jnp.sum(p, axis=1, keepdims=True)
        o = jnp.dot(p.astype(BF16), v_ref[:, sl], preferred_element_type=F32)
        o_ref[:, sl] = (o / den).astype(o_ref.dtype)


def _mem_attention(zq, row0, b, t, mk, mv, tq):
    nq = t // tq
    blk0 = row0 // tq
    return pl.pallas_call(
        _mem_kernel,
        grid=(b, nq),
        in_specs=[pl.BlockSpec((tq, D_M), lambda bi, i: (blk0 + bi * nq + i, 2)),
                  pl.BlockSpec((None, N_MEM, D_M), lambda bi, i: (bi, 0, 0)),
                  pl.BlockSpec((None, N_MEM, D_M), lambda bi, i: (bi, 0, 0))],
        out_specs=pl.BlockSpec((tq, D_M), lambda bi, i: (bi * nq + i, 0)),
        out_shape=jax.ShapeDtypeStruct((b * t, D_M), BF16),
        compiler_params=_params("parallel", "parallel"),
        name="memory_attention",
    )(zq, mk, mv)


def _mix_kernel(yap_ref, yas_ref, gp_ref, gs_ref, obp_ref, obs_ref, omp_ref, oms_ref, ga_ref, gb_ref, gm_ref,
                pa_ref, pb_ref, pm_ref, o_ref, *, na):
    i = pl.program_id(1)
    oa = (_pick(i, na, yap_ref, yas_ref) * _pick(i, na, gp_ref, gs_ref)).astype(BF16)
    acc = _sigmoid(ga_ref[...]) * jnp.dot(oa, pa_ref[...], preferred_element_type=F32)
    acc = acc + _sigmoid(gb_ref[...]) * jnp.dot(_pick(i, na, obp_ref, obs_ref), pb_ref[...],
                                                preferred_element_type=F32)
    acc = acc + _sigmoid(gm_ref[...]) * jnp.dot(_pick(i, na, omp_ref, oms_ref), pm_ref[...],
                                                preferred_element_type=F32)
    o_ref[...] = acc.astype(o_ref.dtype)


def _mix(ya2, g2, ob2, om2, zg, pa, pb, pm, tm, tn):
    n = zg.shape[0]
    nj = D_MODEL // tn
    na = g2[0].shape[0] // tm
    two = lambda w: (pl.BlockSpec((tm, w), lambda j, i: (jnp.minimum(i, na - 1), 0)),
                     pl.BlockSpec((tm, w), lambda j, i: (jnp.maximum(i - na, 0), 0)))
    gate = lambda br: pl.BlockSpec((tm, tn), lambda j, i, br=br: (i, br * nj + j))
    wt = pl.BlockSpec((D_A, tn), lambda j, i: (0, j))
    return pl.pallas_call(
        functools.partial(_mix_kernel, na=na),
        grid=(nj, n // tm),
        in_specs=[*two(D_A), *two(D_A), *two(D_B), *two(D_M), gate(0), gate(1), gate(2), wt, wt, wt],
        out_specs=pl.BlockSpec((tm, tn), lambda j, i: (i, j)),
        out_shape=jax.ShapeDtypeStruct((n, D_MODEL), BF16),
        compiler_params=_params("parallel", "parallel"),
        name="branch_mix",
    )(*ya2, *g2, *ob2, *om2, zg, zg, zg, pa, pb, pm)


def _wo_kernel(m_ref, w_ref, xa_ref, xb_ref, g_ref, h_ref, u_ref, *, na):
    x = _pick(pl.program_id(0), na, xa_ref, xb_ref)
    h = x + jnp.dot(m_ref[...], w_ref[...], preferred_element_type=F32)
    h_ref[...] = h
    y = h * lax.rsqrt(jnp.mean(h * h, axis=-1, keepdims=True) + NORM_EPS)
    u_ref[...] = (y * g_ref[...]).astype(u_ref.dtype)


def _wo_residual_norm(mixed, w_o, xa, xb, g, tm):
    n = mixed.shape[0]
    na = xa.shape[0] // tm
    tile = pl.BlockSpec((tm, D_MODEL), lambda i: (i, 0))
    sa, sb = _two_part_specs(tm, D_MODEL, na)
    return pl.pallas_call(
        functools.partial(_wo_kernel, na=na),
        grid=(n // tm,),
        in_specs=[tile, pl.BlockSpec((D_MODEL, D_MODEL), lambda i: (0, 0)), sa, sb,
                  pl.BlockSpec((1, D_MODEL), lambda i: (0, 0))],
        out_specs=[tile, tile],
        out_shape=[jax.ShapeDtypeStruct((n, D_MODEL), F32), jax.ShapeDtypeStruct((n, D_MODEL), BF16)],
        compiler_params=_params("parallel"),
        name="wo_residual_norm",
    )(mixed, w_o, xa, xb, g.reshape(1, D_MODEL))


def _moe_kernel(te_ref, nu_ref, x_ref, gate_ref, w1_ref, w3_ref, w2_ref, o_ref):
    i, j = pl.program_id(0), pl.program_id(1)

    @pl.when((i >= nu_ref[0]) & (j == 0))
    def _():
        o_ref[...] = jnp.zeros(o_ref.shape, F32)

    @pl.when(i < nu_ref[0])
    def _():
        x = x_ref[...]
        a = jnp.dot(x, w1_ref[...], preferred_element_type=F32)
        b = jnp.dot(x, w3_ref[...], preferred_element_type=F32)
        lane = lax.broadcasted_iota(jnp.int32, gate_ref.shape, 1)
        ge = jnp.sum(jnp.where(lane == j, gate_ref[...], 0.0), axis=1, keepdims=True)
        hid = (a * _sigmoid(a)) * b * ge
        out = jnp.dot(hid.astype(BF16), w2_ref[...], preferred_element_type=F32)

        @pl.when(j == 0)
        def _():
            o_ref[...] = out

        @pl.when(j > 0)
        def _():
            o_ref[...] += out


def _moe_grouped(x_sorted, gate_sorted, tile_experts, n_used, w1, w3, w2):
    npad = x_sorted.shape[0]
    ew = lambda i, j, te, nu: (te[j, i], 0, 0)
    return pl.pallas_call(
        _moe_kernel,
        grid_spec=pltpu.PrefetchScalarGridSpec(
            num_scalar_prefetch=2,
            grid=(npad // MOE_TM, TOP_K_INNER),
            in_specs=[pl.BlockSpec((MOE_TM, D_MODEL), lambda i, j, te, nu: (i, 0)),
                      pl.BlockSpec((MOE_TM, TOP_K_INNER), lambda i, j, te, nu: (i, 0)),
                      pl.BlockSpec((None, D_MODEL, D_EXPERT), ew), pl.BlockSpec((None, D_MODEL, D_EXPERT), ew),
                      pl.BlockSpec((None, D_EXPERT, D_MODEL), ew)],
            out_specs=pl.BlockSpec((MOE_TM, D_MODEL), lambda i, j, te, nu: (i, 0))),
        out_shape=jax.ShapeDtypeStruct((npad, D_MODEL), F32),
        compiler_params=_params("parallel", "arbitrary"),
        name="moe_grouped",
    )(tile_experts, n_used, x_sorted, gate_sorted, w1, w3, w2)


def _route(logits, b_grp, b_rt):
    n = logits.shape[0]
    g_logits = logits[:, :N_GROUPS] + b_grp
    e_logits = (logits[:, N_GROUPS:N_GROUPS + N_EXPERTS] + b_rt).reshape(n, N_GROUPS, EXPERTS_PER_GROUP)
    g_idx = jnp.argmax(g_logits, axis=-1).astype(jnp.int32)
    g_w = jnp.max(jax.nn.softmax(g_logits, axis=-1), axis=-1, keepdims=True)
    onehot_g = g_idx[:, None] == jnp.arange(N_GROUPS, dtype=jnp.int32)[None, :]
    e_in = jnp.sum(jnp.where(onehot_g[:, :, None], e_logits, 0.0), axis=1)
    lane = jnp.arange(EXPERTS_PER_GROUP, dtype=jnp.int32)[None, :]
    i1 = jnp.argmax(e_in, axis=-1).astype(jnp.int32)
    v1 = jnp.max(e_in, axis=-1, keepdims=True)
    rest = jnp.where(lane == i1[:, None], -jnp.inf, e_in)
    i2 = jnp.argmax(rest, axis=-1).astype(jnp.int32)
    v2 = jnp.max(rest, axis=-1, keepdims=True)
    w12 = g_w * jax.nn.softmax(jnp.concatenate([v1, v2], axis=-1), axis=-1)
    lo, hi = jnp.minimum(i1, i2), jnp.maximum(i1, i2)
    gate2 = jnp.where((i1 < i2)[:, None], w12, w12[:, ::-1])
    n_pairs = EXPERTS_PER_GROUP * (EXPERTS_PER_GROUP - 1) // 2
    n_cls = N_GROUPS * n_pairs
    cls = g_idx * n_pairs + lo * (2 * EXPERTS_PER_GROUP - lo - 1) // 2 + hi - lo - 1
    pair_lo = jnp.array([a for a in range(EXPERTS_PER_GROUP) for _ in range(a + 1, EXPERTS_PER_GROUP)], jnp.int32)
    pair_hi = jnp.array([c for a in range(EXPERTS_PER_GROUP) for c in range(a + 1, EXPERTS_PER_GROUP)], jnp.int32)
    cls_ids = jnp.arange(n_cls, dtype=jnp.int32)
    cls_experts = jnp.stack([(cls_ids // n_pairs) * EXPERTS_PER_GROUP + jnp.tile(pair_lo, N_GROUPS),
                             (cls_ids // n_pairs) * EXPERTS_PER_GROUP + jnp.tile(pair_hi, N_GROUPS)])

    onehot_c = cls[:, None] == cls_ids[None, :]
    counts = jnp.sum(onehot_c.astype(jnp.int32), axis=0)
    tiles = (counts + MOE_TM - 1) // MOE_TM
    tile_end = jnp.cumsum(tiles)
    slot0 = (tile_end - tiles) * MOE_TM
    start = jnp.cumsum(counts) - counts
    order = jnp.argsort(cls, stable=True).astype(jnp.int32)
    rank = jnp.sum(jnp.where(onehot_c, jnp.cumsum(onehot_c.astype(jnp.int32), axis=0) - 1, 0), axis=1)
    inv = jnp.sum(jnp.where(onehot_c, slot0[None, :], 0), axis=1) + rank
    npad = n + n_cls * MOE_TM
    slots = jnp.arange(npad, dtype=jnp.int32)
    slot_cls = jnp.minimum(jnp.sum(jnp.where(slots[:, None] >= tile_end[None, :] * MOE_TM, 1, 0), axis=1), n_cls - 1)
    in_cls = slot_cls[:, None] == cls_ids[None, :]
    pos = slots - jnp.sum(jnp.where(in_cls, slot0[None, :], 0), axis=1)
    used = pos < jnp.sum(jnp.where(in_cls, counts[None, :], 0), axis=1)
    src = jnp.where(used, pos + jnp.sum(jnp.where(in_cls, start[None, :], 0), axis=1), slots % n)
    perm = order[src]
    gate_sorted = jnp.where(used[:, None], gate2[perm], 0.0)
    tile_in_cls = slot_cls[::MOE_TM, None] == cls_ids[None, :]
    tile_experts = jnp.sum(jnp.where(tile_in_cls[None], cls_experts[:, None, :], 0), axis=-1).astype(jnp.int32)
    n_used = tile_end[-1]
    last = jnp.sum(jnp.where(jnp.arange(npad // MOE_TM) == n_used - 1, tile_experts[TOP_K_INNER - 1], 0))
    tile_experts = jnp.where(jnp.arange(npad // MOE_TM)[None, :] < n_used, tile_experts, last).astype(jnp.int32)
    return perm, inv, gate_sorted, tile_experts, n_used[None].astype(jnp.int32)


def _final_kernel(h_ref, m_ref, g_ref, yp_ref, ys_ref, *, na):
    h = h_ref[...] + m_ref[...]
    y = h * lax.rsqrt(jnp.mean(h * h, axis=-1, keepdims=True) + NORM_EPS) * g_ref[...]
    i = pl.program_id(0)

    @pl.when(i < na)
    def _():
        yp_ref[...] = y

    @pl.when(i >= na)
    def _():
        ys_ref[...] = y


def _final_norm(h, moe, g, n_p, tm):
    n = h.shape[0]
    na = n_p // tm
    tile = pl.BlockSpec((tm, D_MODEL), lambda i: (i, 0))
    sa, sb = _two_part_specs(tm, D_MODEL, na)
    return pl.pallas_call(
        functools.partial(_final_kernel, na=na),
        grid=(n // tm,),
        in_specs=[tile, tile, pl.BlockSpec((1, D_MODEL), lambda i: (0, 0))],
        out_specs=[sa, sb],
        out_shape=[jax.ShapeDtypeStruct((n_p, D_MODEL), F32), jax.ShapeDtypeStruct((n - n_p, D_MODEL), F32)],
        compiler_params=_params("arbitrary"),
        name="residual_final_norm",
    )(h, moe, g.reshape(1, D_MODEL))


def _pad_cols(w, width):
    return jnp.pad(w, ((0, 0), (0, width - w.shape[1])))


def _pad_rows(w, height):
    return jnp.pad(w, ((0, height - w.shape[0]), (0, 0)))


def _pad_lora(x):
    o = 3 * D_A
    pad = lambda t, w: jnp.pad(t, [(0, 0)] * (t.ndim - 1) + [(0, w - t.shape[-1])])
    return jnp.concatenate([x[..., :o], pad(x[..., o:o + W_LORA], WL_PAD),
                            pad(x[..., o + W_LORA:o + W_LORA + A_LORA], AL_PAD),
                            pad(x[..., o + W_LORA + A_LORA:], GL_PAD)], axis=-1)


def _unpad_lora(x):
    o = 3 * D_A
    return jnp.concatenate([x[..., :o], x[..., o:o + W_LORA], x[..., o + WL_PAD:o + WL_PAD + A_LORA],
                            x[..., o + WL_PAD + AL_PAD:o + WL_PAD + AL_PAD + G_LORA]], axis=-1)


def kernel(x_prompt, x_sample, cache_dsa_k, cache_dsa_v, cache_idx_k, state_rwkv_shift, state_rwkv_wkv, cache_mem_k, cache_mem_v, mem_prompt, rel_bias, g_attn, w_in, rwkv_mu, rwkv_w0, rwkv_w_dec, rwkv_a0, rwkv_w_a, rwkv_w_g, rwkv_k_k, rwkv_k_a, rwkv_r_k, rwkv_lnx_w, rwkv_lnx_b, g_mem, w_mem_kv, p_a, p_b, p_m, w_o, g_ffn, w_grp, b_grp, w_rt, b_rt, w1, w3, w2, g_final):
    assert w_in.shape[0] == 1, "single layer"
    bp, tp, _ = x_prompt.shape
    bs, ts, _ = x_sample.shape
    past = cache_dsa_k.shape[2]
    n_p, n_s = bp * tp, bs * ts
    n = n_p + n_s
    tm = 1024
    xp = x_prompt.reshape(n_p, D_MODEL)
    xs = x_sample.reshape(n_s, D_MODEL)

    offs = [0]
    for s in IN_SIZES:
        offs.append(offs[-1] + s)
    seg = lambda i: w_in[0][:, offs[i]:offs[i + 1]]
    w_a_cols = _pad_lora(seg(0)).astype(BF16)
    w_qim = jnp.concatenate([seg(1), seg(4), seg(7)], axis=1).astype(BF16)
    w_kvi = _pad_cols(jnp.concatenate([seg(2), seg(3), seg(5), seg(6)], axis=1), KVI_PAD).astype(BF16)
    w_gate = seg(8).astype(BF16)
    rw = dict(mu=_pad_lora(rwkv_mu[0])[None], w0=rwkv_w0, a0=rwkv_a0, k_k=rwkv_k_k, k_a=rwkv_k_a,
              w_dec=_pad_rows(rwkv_w_dec[0], WL_PAD).astype(BF16), w_a=_pad_rows(rwkv_w_a[0], AL_PAD).astype(BF16),
              w_g=_pad_rows(rwkv_w_g[0], GL_PAD).astype(BF16))

    u = _rmsnorm_bf16(xp, xs, g_attn[0], 512)
    z_a = _matmul(u, w_a_cols, F32, tm, A_PAD // 4, "proj_rwkv")
    z_qim = _matmul(u, w_qim, BF16, tm, 1024, "proj_queries")
    z_kvi, kv_bf = _matmul_kv(u, w_kvi, tm)
    z_g = _matmul(u, w_gate, F32, tm, 1024, "proj_gates")

    mem = mem_prompt.reshape(bp * N_MEM, D_MODEL)
    um = _rmsnorm_bf16(mem[:bp * N_MEM // 2], mem[bp * N_MEM // 2:], g_mem[0], 512)
    mkv = _matmul(um, w_mem_kv[0].astype(BF16), F32, 1024, 1024, "proj_mem_kv")
    mk_p = mkv[:, :D_M].reshape(bp, N_MEM, D_M)
    mv_p = mkv[:, D_M:].reshape(bp, N_MEM, D_M)

    lw_vec, lb_vec, rk_vec = rwkv_lnx_w[0], rwkv_lnx_b[0], rwkv_r_k[0].reshape(D_A)

    def rwkv_group(row0, nseq, t, shift0, wkv0, tb, tt):
        r, w, k, v, kk, a, g, last = _rwkv_prep(z_a, row0, nseq, t, shift0, rw, tb)
        y, s_fin = _wkv(r, w, k, v, kk, a, _state_to_tiles(wkv0), _lane_tile(lw_vec), _lane_tile(lb_vec),
                        _lane_tile(rk_vec), tt)
        return y.reshape(nseq * t, D_A), g, _state_from_tiles(s_fin, nseq), _unpad_lora(last)

    ya_p, g_p, wkv_p, shift_p = rwkv_group(0, bp, tp, jnp.zeros((bp, 1, A_PAD), F32),
                                           jnp.zeros((bp, H_A, HD_A, HD_A), F32), 256, 32)
    ya_s, g_s, wkv_s, shift_s = rwkv_group(n_p, bs, ts, _pad_lora(state_rwkv_shift[0]), state_rwkv_wkv[0], ts, ts)

    k_new, v_new = z_kvi[:, :D_KV], z_kvi[:, D_KV:2 * D_KV]
    ik_new = z_kvi[:, 2 * D_KV:2 * D_KV + D_I]
    iw = z_kvi[:, 2 * D_KV + D_I:2 * D_KV + D_I + H_I]
    grp = lambda t, sl, b, tlen: t[sl].reshape(b, tlen, t.shape[-1])
    sp, ss = slice(0, n_p), slice(n_p, n)
    ob_p = _dsa(z_qim, kv_bf, 0, bp, tp, iw[sp], grp(ik_new, sp, bp, tp).astype(BF16), rel_bias,
                Q_BLOCK if tp % Q_BLOCK == 0 else tp, 0, tp)
    l_s = past + ts
    ik_s = jnp.pad(jnp.concatenate([cache_idx_k[0].astype(BF16), grp(ik_new, ss, bs, ts).astype(BF16)], axis=1),
                   ((0, 0), (0, past + KEY_TILE - l_s), (0, 0)))
    ob_s = _dsa(z_qim, kv_bf, n_p, bs, ts, iw[ss], ik_s, rel_bias, Q_BLOCK if ts % Q_BLOCK == 0 else ts, past, l_s,
                cache_dsa_k[0].reshape(bs, past, D_KV), cache_dsa_v[0].reshape(bs, past, D_KV))

    om_p = _mem_attention(z_qim, 0, bp, tp, mk_p.astype(BF16), mv_p.astype(BF16), 256)
    om_s = _mem_attention(z_qim, n_p, bs, ts, cache_mem_k[0].reshape(bs, N_MEM, D_M).astype(BF16),
                          cache_mem_v[0].reshape(bs, N_MEM, D_M).astype(BF16), ts)

    mixed = _mix((ya_p, ya_s), (g_p, g_s), (ob_p, ob_s), (om_p, om_s), z_g, p_a[0].astype(BF16), p_b[0].astype(BF16),
                 p_m[0].astype(BF16), 256, 1024)
    h, u2 = _wo_residual_norm(mixed, w_o[0].astype(BF16), xp, xs, g_ffn[0], 256)

    w_route = _pad_cols(jnp.concatenate([w_grp[0], w_rt[0]], axis=1), LANES).astype(BF16)
    logits = _matmul(u2, w_route, F32, tm, LANES, "moe_router")
    perm, inv, gate_sorted, tile_experts, n_used = _route(logits, b_grp[0], b_rt[0])
    moe_sorted = _moe_grouped(u2[perm], gate_sorted, tile_experts, n_used,
                              _cast_bf16(w1[0]), _cast_bf16(w3[0]), _cast_bf16(w2[0]))
    y_p, y_s = _final_norm(h, moe_sorted[inv], g_final, n_p, 512)

    st = lambda t, b, tlen, shape: t.reshape((1, b, tlen) + shape)
    return (y_p.reshape(bp, tp, D_MODEL), y_s.reshape(bs, ts, D_MODEL),
            st(k_new[sp], bp, tp, (N_KV_B, HD_B)), st(v_new[sp], bp, tp, (N_KV_B, HD_B)), st(ik_new[sp], bp, tp, (D_I,)),
            shift_p[None], wkv_p[None],
            mk_p.reshape(1, bp, N_MEM, H_M, HD_M), mv_p.reshape(1, bp, N_MEM, H_M, HD_M),
            st(k_new[ss], bs, ts, (N_KV_B, HD_B)), st(v_new[ss], bs, ts, (N_KV_B, HD_B)), st(ik_new[ss], bs, ts, (D_I,)),
            shift_s[None], wkv_s[None])
```

```python
import functools
import math

import jax
import jax.numpy as jnp
from jax import lax
from jax.experimental import pallas as pl
from jax.experimental.pallas import tpu as pltpu

F32 = jnp.float32
BF16 = jnp.bfloat16

LANES = 128
D_MODEL = 2048
CHUNK = 64
NORM_EPS = 1e-6
H_A, HD_A = 16, 64
D_A = H_A * HD_A
W_LORA, A_LORA, G_LORA = 64, 64, 160
LNX_EPS = 64e-5
A_COLS = 3 * D_A + W_LORA + A_LORA + G_LORA
H_B, N_KV_B, HD_B = 8, 2, 128
G_B = H_B // N_KV_B
D_B = H_B * HD_B
D_KV = N_KV_B * HD_B
H_I, D_I = 16, 64
TOPK_MAX = 256
Q_BLOCK = 128
REL_BUCKETS = 32
REL_MAX_DIST = 128
N_MEM, H_M, HD_M = 256, 4, 256
D_M = H_M * HD_M
N_BRANCH = 3
IN_SIZES = (A_COLS, D_B, D_KV, D_KV, H_I * D_I, D_I, H_I, D_M, N_BRANCH * D_MODEL)
N_GROUPS, EXPERTS_PER_GROUP = 4, 4
N_EXPERTS = N_GROUPS * EXPERTS_PER_GROUP
TOP_K_INNER = 2
D_EXPERT = 512

WL_PAD, AL_PAD, GL_PAD = 128, 128, 256
A_PAD = 3 * D_A + WL_PAD + AL_PAD + GL_PAD
KVI_PAD = 2 * D_KV + LANES
KEY_TILE = 2 * LANES
SEQ_PER_TILE = LANES // H_A
MOE_TM = 256
INT_MIN = -(2 ** 31)
VMEM_LIMIT = 48 * 1024 * 1024


def _params(*sem):
    return pltpu.CompilerParams(dimension_semantics=sem, vmem_limit_bytes=VMEM_LIMIT)


def _two_part_specs(tm, width, na):
    return (pl.BlockSpec((tm, width), lambda i: (jnp.minimum(i, na - 1), 0)),
            pl.BlockSpec((tm, width), lambda i: (jnp.maximum(i - na, 0), 0)))


def _pick(i, na, a_ref, b_ref):
    return jnp.where(i < na, a_ref[...], b_ref[...])


def _norm2_kernel(xa_ref, xb_ref, g_ref, o_ref, *, na):
    x = _pick(pl.program_id(0), na, xa_ref, xb_ref)
    y = x * lax.rsqrt(jnp.mean(x * x, axis=-1, keepdims=True) + NORM_EPS)
    o_ref[...] = (y * g_ref[...]).astype(o_ref.dtype)


def _rmsnorm_bf16(xa, xb, g, tm):
    d = xa.shape[1]
    na, nb = xa.shape[0] // tm, xb.shape[0] // tm
    sa, sb = _two_part_specs(tm, d, na)
    return pl.pallas_call(
        functools.partial(_norm2_kernel, na=na),
        grid=(na + nb,),
        in_specs=[sa, sb, pl.BlockSpec((1, d), lambda i: (0, 0))],
        out_specs=pl.BlockSpec((tm, d), lambda i: (i, 0)),
        out_shape=jax.ShapeDtypeStruct(((na + nb) * tm, d), BF16),
        compiler_params=_params("parallel"),
        name="rmsnorm_bf16",
    )(xa, xb, g.reshape(1, d))


def _mm_kernel(a_ref, w_ref, o_ref):
    o_ref[...] = jnp.dot(a_ref[...], w_ref[...], preferred_element_type=F32).astype(o_ref.dtype)


def _matmul(a, w, out_dtype, tm, tn, name):
    n, k = a.shape
    m = w.shape[1]
    return pl.pallas_call(
        _mm_kernel,
        grid=(n // tm, m // tn),
        in_specs=[pl.BlockSpec((tm, k), lambda i, j: (i, 0)), pl.BlockSpec((k, tn), lambda i, j: (0, j))],
        out_specs=pl.BlockSpec((tm, tn), lambda i, j: (i, j)),
        out_shape=jax.ShapeDtypeStruct((n, m), out_dtype),
        compiler_params=_params("parallel", "parallel"),
        name=name,
    )(a, w)


def _mm_kv_kernel(a_ref, w_ref, o_ref, kv_ref):
    acc = jnp.dot(a_ref[...], w_ref[...], preferred_element_type=F32)
    o_ref[...] = acc
    kv_ref[...] = acc[:, :2 * D_KV].astype(kv_ref.dtype)


def _matmul_kv(a, w, tm):
    n, k = a.shape
    m = w.shape[1]
    return pl.pallas_call(
        _mm_kv_kernel,
        grid=(n // tm,),
        in_specs=[pl.BlockSpec((tm, k), lambda i: (i, 0)), pl.BlockSpec((k, m), lambda i: (0, 0))],
        out_specs=[pl.BlockSpec((tm, m), lambda i: (i, 0)), pl.BlockSpec((tm, 2 * D_KV), lambda i: (i, 0))],
        out_shape=[jax.ShapeDtypeStruct((n, m), F32), jax.ShapeDtypeStruct((n, 2 * D_KV), BF16)],
        compiler_params=_params("parallel"),
        name="proj_kv",
    )(a, w)


def _cast_kernel(x_ref, o_ref):
    o_ref[...] = x_ref[...].astype(o_ref.dtype)


def _cast_bf16(x):
    e, r, c = x.shape
    spec = pl.BlockSpec((None, r, c), lambda i: (i, 0, 0))
    return pl.pallas_call(
        _cast_kernel, grid=(e,), in_specs=[spec], out_specs=spec,
        out_shape=jax.ShapeDtypeStruct(x.shape, BF16), compiler_params=_params("parallel"), name="cast_bf16",
    )(x)


def _softplus(x):
    return jnp.maximum(x, 0.0) + jnp.log1p(jnp.exp(-jnp.abs(x)))


def _sigmoid(x):
    return 0.5 * jnp.tanh(0.5 * x) + 0.5


def _rwkv_prep_kernel(z_ref, sh_ref, mu_ref, w0_ref, wdec_ref, a0_ref, wa_ref, wg_ref, kk_ref, ka_ref,
                      r_out, w_out, k_out, v_out, kk_out, a_out, g_out, last_out, carry_ref):
    tb = z_ref.shape[0]

    @pl.when(pl.program_id(1) == 0)
    def _():
        carry_ref[...] = sh_ref[...]

    z = z_ref[...]
    row = lax.broadcasted_iota(jnp.int32, (tb, 1), 0)
    prev = jnp.where(row == 0, carry_ref[...], pltpu.roll(z, 1, axis=0))
    carry_ref[...] = z[tb - 1:tb, :]
    last_out[...] = z[tb - 1:tb, :]
    zm = z + (prev - z) * mu_ref[...]
    r = zm[:, 0:D_A]
    k = zm[:, D_A:2 * D_A]
    v = zm[:, 2 * D_A:3 * D_A]
    o = 3 * D_A
    wl = zm[:, o:o + WL_PAD]
    al = zm[:, o + WL_PAD:o + WL_PAD + AL_PAD]
    gl = zm[:, o + WL_PAD + AL_PAD:]
    lw = jnp.dot(jnp.tanh(wl).astype(BF16), wdec_ref[...], preferred_element_type=F32)
    wv = -_softplus(-(w0_ref[...] + lw)) - 0.5
    a = _sigmoid(a0_ref[...] + jnp.dot(al.astype(BF16), wa_ref[...], preferred_element_type=F32))
    r_out[...] = r
    w_out[...] = jnp.exp(-jnp.exp(wv))
    k_out[...] = k * (1.0 + (a - 1.0) * ka_ref[...])
    v_out[...] = v
    kk_out[...] = k * kk_ref[...]
    a_out[...] = a
    g_out[...] = jnp.dot(_sigmoid(gl).astype(BF16), wg_ref[...], preferred_element_type=F32)


def _rwkv_prep(z, row0, nseq, t, shift0, wts, tb):
    nt = t // tb
    blk0 = row0 // tb
    row = lambda c: pl.BlockSpec((1, c), lambda s, i: (0, 0))
    full = lambda a, b: pl.BlockSpec((a, b), lambda s, i: (0, 0))
    out_spec = pl.BlockSpec((None, tb, D_A), lambda s, i: (s, i, 0))
    out_sds = jax.ShapeDtypeStruct((nseq, t, D_A), F32)
    g_spec = pl.BlockSpec((tb, D_A), lambda s, i: (s * nt + i, 0))
    g_sds = jax.ShapeDtypeStruct((nseq * t, D_A), F32)
    return pl.pallas_call(
        _rwkv_prep_kernel,
        grid=(nseq, nt),
        in_specs=[pl.BlockSpec((tb, A_PAD), lambda s, i: (blk0 + s * nt + i, 0)),
                  pl.BlockSpec((None, 1, A_PAD), lambda s, i: (s, 0, 0)),
                  row(A_PAD), row(D_A), full(WL_PAD, D_A), row(D_A), full(AL_PAD, D_A), full(GL_PAD, D_A),
                  row(D_A), row(D_A)],
        out_specs=[out_spec] * 6 + [g_spec, pl.BlockSpec((None, 1, A_PAD), lambda s, i: (s, 0, 0))],
        out_shape=[out_sds] * 6 + [g_sds, jax.ShapeDtypeStruct((nseq, 1, A_PAD), F32)],
        scratch_shapes=[pltpu.VMEM((1, A_PAD), F32)],
        compiler_params=_params("arbitrary", "arbitrary"),
        name="rwkv_prep",
    )(z, shift0, wts["mu"], wts["w0"], wts["w_dec"], wts["a0"], wts["w_a"], wts["w_g"], wts["k_k"], wts["k_a"])


def _wkv_kernel(r_ref, w_ref, k_ref, v_ref, kk_ref, a_ref, s0_ref, lw_ref, lb_ref, rk_ref,
                y_ref, st_ref, s_ref, ab_ref, tma_ref, tmb_ref, ytm_ref, *, n_tblocks):
    tt = r_ref.shape[1]
    low = lax.broadcasted_iota(jnp.int32, (SEQ_PER_TILE, LANES), 1) < HD_A

    @pl.when(pl.program_id(1) == 0)
    def _():
        s_ref[...] = s0_ref[...]

    n_pairs = tt // 2

    def to_tiles(buf_ref, j):
        j = jnp.minimum(j, n_pairs - 1)
        for ai, ref in enumerate((r_ref, w_ref, k_ref, v_ref, kk_ref, a_ref)):
            x0 = ref[:, 2 * j, :]
            x1 = ref[:, 2 * j + 1, :]
            rows = []
            for h in range(H_A):
                sl = slice((h // 2) * LANES, (h // 2 + 1) * LANES)
                if h % 2 == 0:
                    rows.append(jnp.where(low, x0[:, sl], pltpu.roll(x1[:, sl], HD_A, axis=1)))
                else:
                    rows.append(jnp.where(low, pltpu.roll(x0[:, sl], HD_A, axis=1), x1[:, sl]))
            m = jnp.concatenate(rows, axis=0).T
            buf_ref[ai, 0] = m[:HD_A]
            buf_ref[ai, 1] = m[HD_A:]

    def step(buf_ref, u, t):
        kk = buf_ref[4, u]
        ss = jnp.sum(kk * kk, axis=0, keepdims=True)
        kkn = kk * lax.rsqrt(jnp.maximum(ss, 1e-24))
        ab_ref[0] = -kkn
        ab_ref[1] = kkn * buf_ref[5, u]
        halves = []
        for rows in (slice(0, HD_A // 2), slice(HD_A // 2, HD_A)):
            vh = buf_ref[3, u, rows, :]
            sa = jnp.zeros((HD_A // 2, LANES), F32)
            for k in range(HD_A):
                sa = sa + s_ref[k, rows, :] * ab_ref[0, k:k + 1, :]
            yh = jnp.zeros((HD_A // 2, LANES), F32)
            for k in range(HD_A):
                s_new = (s_ref[k, rows, :] * buf_ref[1, u, k:k + 1, :] + sa * ab_ref[1, k:k + 1, :]
                         + vh * buf_ref[2, u, k:k + 1, :])
                s_ref[k, rows, :] = s_new
                yh = yh + s_new * buf_ref[0, u, k:k + 1, :]
            halves.append(yh)
        y = jnp.concatenate(halves, axis=0)
        vv = buf_ref[3, u]
        mean = jnp.mean(y, axis=0, keepdims=True)
        d = y - mean
        var = jnp.mean(d * d, axis=0, keepdims=True)
        yn = d * lax.rsqrt(var + LNX_EPS) * lw_ref[...] + lb_ref[...]
        bonus = jnp.sum(buf_ref[0, u] * buf_ref[2, u] * rk_ref[...], axis=0, keepdims=True) * vv
        ytm_ref[t] = yn + bonus

    to_tiles(tma_ref, 0)

    def two_pairs(m, carry):
        to_tiles(tmb_ref, 2 * m + 1)
        step(tma_ref, 0, 4 * m)
        step(tma_ref, 1, 4 * m + 1)
        to_tiles(tma_ref, 2 * m + 2)
        step(tmb_ref, 0, 4 * m + 2)
        step(tmb_ref, 1, 4 * m + 3)
        return carry

    lax.fori_loop(0, n_pairs // 2, two_pairs, 0)

    for j in range(tt // 2):
        m = jnp.concatenate([ytm_ref[2 * j], ytm_ref[2 * j + 1]], axis=0).T
        for hp in range(H_A // 2):
            even = m[(2 * hp) * SEQ_PER_TILE:(2 * hp + 1) * SEQ_PER_TILE]
            odd = m[(2 * hp + 1) * SEQ_PER_TILE:(2 * hp + 2) * SEQ_PER_TILE]
            cols = slice(hp * LANES, (hp + 1) * LANES)
            y_ref[:, 2 * j, cols] = jnp.where(low, even, pltpu.roll(odd, HD_A, axis=1))
            y_ref[:, 2 * j + 1, cols] = jnp.where(low, pltpu.roll(even, HD_A, axis=1), odd)

    @pl.when(pl.program_id(1) == n_tblocks - 1)
    def _():
        st_ref[...] = s_ref[...]


def _wkv(r, w, k, v, kk, a, s0, lw, lb, rk, tt):
    nseq, t, _ = r.shape
    p = nseq * H_A
    assert tt % 4 == 0 and t % tt == 0 and nseq % SEQ_PER_TILE == 0
    seq = pl.BlockSpec((SEQ_PER_TILE, tt, D_A), lambda g, i: (g, i, 0))
    vec = pl.BlockSpec((HD_A, LANES), lambda g, i: (0, 0))
    st = pl.BlockSpec((HD_A, HD_A, LANES), lambda g, i: (0, 0, g))
    return pl.pallas_call(
        functools.partial(_wkv_kernel, n_tblocks=t // tt),
        grid=(p // LANES, t // tt),
        in_specs=[seq] * 6 + [st, vec, vec, vec],
        out_specs=[seq, st],
        out_shape=[jax.ShapeDtypeStruct((nseq, t, D_A), F32), jax.ShapeDtypeStruct((HD_A, HD_A, p), F32)],
        scratch_shapes=[pltpu.VMEM((HD_A, HD_A, LANES), F32), pltpu.VMEM((2, HD_A, LANES), F32),
                        pltpu.VMEM((6, 2, HD_A, LANES), F32), pltpu.VMEM((6, 2, HD_A, LANES), F32),
                        pltpu.VMEM((tt, HD_A, LANES), F32)],
        compiler_params=_params("arbitrary", "arbitrary"),
        name="wkv_recurrence",
    )(r, w, k, v, kk, a, s0, lw, lb, rk)


def _state_to_tiles(wkv):
    nseq = wkv.shape[0]
    x = wkv.reshape(nseq // SEQ_PER_TILE, SEQ_PER_TILE, H_A, HD_A, HD_A)
    return x.transpose(4, 3, 0, 2, 1).reshape(HD_A, HD_A, nseq * H_A)


def _state_from_tiles(s, nseq):
    x = s.reshape(HD_A, HD_A, nseq // SEQ_PER_TILE, H_A, SEQ_PER_TILE)
    return x.transpose(2, 4, 3, 1, 0).reshape(nseq, H_A, HD_A, HD_A)


def _lane_tile(vec):
    return jnp.repeat(vec.reshape(H_A, HD_A).T, SEQ_PER_TILE, axis=1)


def _dsa_kernel(*refs, q_start, l_valid, topk, keys_on_rows, n_past, lp):
    refs = list(refs)
    q_ref, iq_ref, iw_ref, k_ref, v_ref, ik2_ref, d0_ref, d1_ref = refs[:8]
    del refs[:8]
    kpast_ref = vpast_ref = ktail_ref = vtail_ref = None
    if n_past:
        kpast_ref, vpast_ref = refs[:2]
        del refs[:2]
    o_ref, madd_ref, logit_ref, mx_ref, ls_ref, acc_ref, score_ref, key_ref = refs[:8]
    del refs[:8]
    if not keys_on_rows:
        iwb_ref = refs.pop(0)
    if n_past:
        ktail_ref, vtail_ref = refs
        n_new = k_ref.shape[0]
        for tail_ref, new_ref in ((ktail_ref, k_ref), (vtail_ref, v_ref)):
            tail_ref[...] = jnp.zeros(tail_ref.shape, BF16)
            tail_ref[0:n_new, :] = new_ref[...]
    n_past_tiles = n_past // KEY_TILE

    def kv_tile(new_ref, past_ref, tail_ref, c, g):
        cols = slice(g * HD_B, (g + 1) * HD_B)
        if not n_past:
            return new_ref[pl.ds(pl.multiple_of(c * KEY_TILE, KEY_TILE), KEY_TILE), cols]
        pc = jnp.minimum(c, n_past_tiles - 1)
        row0 = pl.multiple_of(pc * (KEY_TILE * N_KV_B), KEY_TILE * N_KV_B)
        past = past_ref[pl.ds(row0 + g, KEY_TILE, stride=N_KV_B), :].astype(BF16)
        return jnp.where(c < n_past_tiles, past, tail_ref[:, cols])

    tq = q_ref.shape[0]
    q0 = q_start + pl.program_id(1) * tq
    nt = (((1,), (1,)), ((), ()))
    lane_shift = int(math.log2(LANES))
    chunk_shift = int(math.log2(CHUNK))
    kax = 0 if keys_on_rows else 1
    dc = lax.shift_right_logical(q0, lane_shift)
    n_lane_tiles = jnp.minimum(lax.shift_right_logical(q0 + tq - 1, lane_shift) + 1, lp // LANES)
    n_tiles = lax.shift_right_logical(n_lane_tiles + 1, 1)

    def tile_off(c, width):
        return pl.multiple_of(c * width, width)

    def keys_at(off, width):
        return (pl.ds(off, width), slice(None)) if keys_on_rows else (slice(None), pl.ds(off, width))

    def per_query(x):
        return jnp.sum(x, axis=kax, keepdims=True)

    iw = iw_ref[...] * ((H_I * D_I) ** -0.5)
    if not keys_on_rows:
        for h in range(H_I):
            iwb_ref[h] = jnp.broadcast_to(iw[:, h:h + 1], (tq, LANES))

    def score_lane_tile(c):
        kab = ik2_ref[c]
        acc = [jnp.zeros((LANES, tq) if keys_on_rows else (tq, LANES), F32) for _ in range(2)]
        for hp in range(H_I // 2):
            iq_pair = iq_ref[:, hp * LANES:(hp + 1) * LANES]
            if keys_on_rows:
                d = lax.dot_general(kab, iq_pair, nt, preferred_element_type=F32)
                halves = (d[:LANES], d[LANES:])
                wts = (iw[2 * hp:2 * hp + 1, :], iw[2 * hp + 1:2 * hp + 2, :])
            else:
                d = lax.dot_general(iq_pair, kab, nt, preferred_element_type=F32)
                halves = (d[:, :LANES], d[:, LANES:])
                wts = (iwb_ref[2 * hp], iwb_ref[2 * hp + 1])
            for u in range(2):
                acc[u] = acc[u] + jnp.maximum(halves[u], 0.0) * wts[u]
        score_ref[keys_at(tile_off(c, LANES), LANES)] = acc[0] + acc[1]

    def score_tile(c, carry):
        score_lane_tile(2 * c)
        score_lane_tile(2 * c + 1)
        return carry

    lax.fori_loop(0, n_tiles, score_tile, 0)

    qshape, kshape = ((1, tq), lambda w: (w, 1)) if keys_on_rows else ((tq, 1), lambda w: (1, w))
    qchunk = lax.shift_right_arithmetic(q0 + lax.broadcasted_iota(jnp.int32, qshape, 1 - kax), chunk_shift)

    def visible(lo, width):
        kpos = lo + lax.broadcasted_iota(jnp.int32, kshape(width), kax)
        return (lax.shift_right_arithmetic(kpos, chunk_shift) <= qchunk) & (kpos < l_valid)

    def key_tile(c, carry):
        off = tile_off(c, KEY_TILE)
        bits = pltpu.bitcast(score_ref[keys_at(off, KEY_TILE)] + 0.0, jnp.int32)
        key = bits ^ (lax.shift_right_arithmetic(bits, 31) & jnp.int32(0x7FFFFFFF))
        key_ref[keys_at(off, KEY_TILE)] = jnp.where(visible(off, KEY_TILE), key, jnp.int32(INT_MIN))
        return carry

    lax.fori_loop(0, n_tiles, key_tile, 0)
    kf = jnp.float32(topk)
    sub = 8

    def count(pred):
        def body(c, acc):
            m = jnp.where(pred(key_ref[keys_at(tile_off(c, KEY_TILE), KEY_TILE)]), 1.0, 0.0)
            if keys_on_rows:
                parts = [m[r:r + sub] for r in range(0, KEY_TILE, sub)]
            else:
                parts = [m[:, :LANES], m[:, LANES:]]
            while len(parts) > 1:
                parts = [parts[u] + parts[u + 1] for u in range(0, len(parts), 2)]
            return acc + parts[0]
        acc = lax.fori_loop(0, n_tiles, body, jnp.zeros((sub, tq) if keys_on_rows else (tq, LANES), F32))
        return per_query(acc)

    def search(i, thr):
        cand = thr ^ lax.shift_left(jnp.int32(1), 31 - i)
        return jnp.where(count(lambda kc: kc >= cand) >= kf, cand, thr)

    thr = lax.fori_loop(0, 32, search, jnp.full(qshape, INT_MIN, jnp.int32))

    def store_mask(off, width, sel):
        madd = jnp.where(sel, 0.0, -jnp.inf)
        if keys_on_rows:
            for u in range(width // LANES):
                madd_ref[:, pl.ds(pl.multiple_of(off + u * LANES, LANES), LANES)] = madd[u * LANES:(u + 1) * LANES].T
        else:
            madd_ref[:, pl.ds(off, width)] = madd

    surplus = (count(lambda kc: kc >= thr) > kf) & (thr != jnp.int32(INT_MIN))
    any_surplus = jnp.max(jnp.where(surplus, 1.0, 0.0))

    @pl.when(any_surplus == 0.0)
    def _():
        def select_tile(c, carry):
            off = tile_off(c, KEY_TILE)
            store_mask(off, KEY_TILE, visible(off, KEY_TILE) & (key_ref[keys_at(off, KEY_TILE)] >= thr))
            return carry

        lax.fori_loop(0, n_tiles, select_tile, 0)

    @pl.when(any_surplus > 0.0)
    def _():
        need = kf - count(lambda kc: kc > thr)
        ii = lax.broadcasted_iota(jnp.int32, (LANES, LANES), 0)
        jj = lax.broadcasted_iota(jnp.int32, (LANES, LANES), 1)
        tri = jnp.where((ii >= jj) if keys_on_rows else (ii <= jj), 1.0, 0.0).astype(BF16)

        def select_tile(c, run):
            off = tile_off(c, LANES)
            keyc = key_ref[keys_at(off, LANES)]
            eqf = jnp.where(keyc == thr, 1.0, 0.0)
            if keys_on_rows:
                within = jnp.dot(tri, eqf.astype(BF16), preferred_element_type=F32)
                total = within[LANES - 1:LANES, :]
            else:
                within = jnp.dot(eqf.astype(BF16), tri, preferred_element_type=F32)
                total = within[:, LANES - 1:LANES]
            take = jnp.where(keyc > thr, 1.0, jnp.where(run + within <= need, eqf, 0.0))
            store_mask(off, LANES, visible(off, LANES) & (take > 0.5))
            return run + total

        lax.fori_loop(0, 2 * n_tiles, select_tile, jnp.zeros(qshape, F32))

    off0 = tile_off(dc, LANES)
    off1 = tile_off(jnp.maximum(dc - 1, 0), LANES)
    for g in range(N_KV_B):
        qs = jnp.concatenate([q_ref[:, (g * G_B + j) * HD_B:(g * G_B + j + 1) * HD_B] for j in range(G_B)], axis=0)

        def logits_tile(c, carry):
            off = tile_off(c, KEY_TILE)
            s = lax.dot_general(qs, kv_tile(k_ref, kpast_ref, ktail_ref, c, g), nt, preferred_element_type=F32)
            md = madd_ref[:, pl.ds(off, KEY_TILE)]
            for j in range(G_B):
                logit_ref[j * tq:(j + 1) * tq, pl.ds(off, KEY_TILE)] = s[j * tq:(j + 1) * tq] * (HD_B ** -0.5) + md
            return carry

        lax.fori_loop(0, n_tiles, logits_tile, 0)
        for j in range(G_B):
            logit_ref[j * tq:(j + 1) * tq, pl.ds(off0, LANES)] += d0_ref[g * G_B + j]

        @pl.when(dc > 0)
        def _():
            for j in range(G_B):
                logit_ref[j * tq:(j + 1) * tq, pl.ds(off1, LANES)] += d1_ref[g * G_B + j]

        mx_ref[...] = jnp.full(mx_ref.shape, -jnp.inf, F32)

        def max_tile(c, carry):
            lg = logit_ref[:, pl.ds(tile_off(c, KEY_TILE), KEY_TILE)]
            mx_ref[...] = jnp.maximum(mx_ref[...], jnp.maximum(lg[:, :LANES], lg[:, LANES:]))
            return carry

        lax.fori_loop(0, n_tiles, max_tile, 0)
        m = jnp.max(mx_ref[...], axis=1, keepdims=True)
        ls_ref[...] = jnp.zeros(ls_ref.shape, F32)
        acc_ref[...] = jnp.zeros(acc_ref.shape, F32)

        def pv_tile(c, carry):
            off = tile_off(c, KEY_TILE)
            p = jnp.exp(logit_ref[:, pl.ds(off, KEY_TILE)] - m)
            ls_ref[...] += p[:, :LANES] + p[:, LANES:]
            acc_ref[...] += jnp.dot(p.astype(BF16), kv_tile(v_ref, vpast_ref, vtail_ref, c, g),
                                    preferred_element_type=F32)
            return carry

        lax.fori_loop(0, n_tiles, pv_tile, 0)
        res = acc_ref[...] / jnp.sum(ls_ref[...], axis=1, keepdims=True)
        for j in range(G_B):
            h = g * G_B + j
            o_ref[:, h * HD_B:(h + 1) * HD_B] = res[j * tq:(j + 1) * tq].astype(o_ref.dtype)


def _t5_bucket(rel):
    half = REL_BUCKETS // 2
    exact = half // 2
    side = jnp.where(rel > 0, half, 0)
    n = jnp.abs(rel)
    nf = jnp.maximum(n, 1).astype(F32)
    large = exact + (jnp.log(nf / exact) / math.log(REL_MAX_DIST / exact) * (half - exact)).astype(jnp.int32)
    large = jnp.minimum(large, half - 1)
    return side + jnp.where(n < exact, n, large)


def _dsa(zq, kv_new, row0, b, t, iw, ik_all, rel_bias, tq, q_start, l_valid, k_past=None, v_past=None):
    n_past = 0 if k_past is None else k_past.shape[1] // N_KV_B
    lp = ik_all.shape[1]
    topk = min(TOPK_MAX, l_valid // 4)
    assert q_start % LANES == 0 and (tq == LANES or t == tq) and lp % KEY_TILE == 0 and row0 % tq == 0
    assert q_start == n_past and n_past % KEY_TILE == 0 and row0 % t == 0
    assert lp == (n_past + KEY_TILE if n_past else t) and l_valid == n_past + t and (not n_past or t <= KEY_TILE)
    nq = t // tq
    blk0 = row0 // tq
    ikt = ik_all.reshape(b, lp // LANES, LANES, D_I)
    zeros = jnp.zeros_like(ikt)
    ik2 = jnp.concatenate([jnp.concatenate([ikt, zeros], axis=-1), jnp.concatenate([zeros, ikt], axis=-1)], axis=2)
    ji = jnp.arange(LANES, dtype=jnp.int32)[None, :] - jnp.arange(tq, dtype=jnp.int32)[:, None]

    def bias_of(rel):
        hit = _t5_bucket(rel)[None, None] == jnp.arange(REL_BUCKETS, dtype=jnp.int32).reshape((1, -1) + (1,) * rel.ndim)
        return jnp.sum(jnp.where(hit, rel_bias.T.reshape((H_B, REL_BUCKETS) + (1,) * rel.ndim), 0.0), axis=1)

    far = bias_of(jnp.full((1, 1), -2 * LANES, jnp.int32))
    d0 = bias_of(ji) - far
    d1 = bias_of(ji - LANES) - far
    keys_on_rows = tq == LANES
    kern = functools.partial(_dsa_kernel, q_start=q_start, l_valid=l_valid, topk=topk, keys_on_rows=keys_on_rows,
                             n_past=n_past, lp=lp)
    qspec = lambda col: pl.BlockSpec((tq, D_B), lambda bi, i: (blk0 + bi * nq + i, col))
    new_spec = lambda col: pl.BlockSpec((t, D_KV), lambda bi, i: (row0 // t + bi, col))
    cspec = pl.BlockSpec((H_B, tq, LANES), lambda bi, i: (0, 0, 0))
    rows = G_B * tq
    if keys_on_rows:
        iw, iw_spec = iw.T, pl.BlockSpec((H_I, tq), lambda bi, i: (0, bi * nq + i))
        mask_scratch = [pltpu.VMEM((lp, tq), F32), pltpu.VMEM((lp, tq), jnp.int32)]
    else:
        iw_spec = pl.BlockSpec((tq, H_I), lambda bi, i: (bi * nq + i, 0))
        mask_scratch = [pltpu.VMEM((tq, lp), F32), pltpu.VMEM((tq, lp), jnp.int32), pltpu.VMEM((H_I, tq, LANES), F32)]
    past_specs, past_args, tail_scratch = [], [], []
    if n_past:
        past_specs = [pl.BlockSpec((None, n_past * N_KV_B, HD_B), lambda bi, i: (bi, 0, 0))] * 2
        past_args = [k_past, v_past]
        tail_scratch = [pltpu.VMEM((KEY_TILE, D_KV), BF16)] * 2
    return pl.pallas_call(
        kern,
        grid=(b, nq),
        in_specs=[qspec(0), qspec(1), iw_spec, new_spec(0), new_spec(1),
                  pl.BlockSpec((None, lp // LANES, 2 * LANES, LANES), lambda bi, i: (bi, 0, 0, 0)), cspec, cspec]
        + past_specs,
        out_specs=pl.BlockSpec((tq, D_B), lambda bi, i: (bi * nq + i, 0)),
        out_shape=jax.ShapeDtypeStruct((b * t, D_B), BF16),
        scratch_shapes=[pltpu.VMEM((tq, lp), F32), pltpu.VMEM((rows, lp), F32), pltpu.VMEM((rows, LANES), F32),
                        pltpu.VMEM((rows, LANES), F32), pltpu.VMEM((rows, HD_B), F32)] + mask_scratch + tail_scratch,
        compiler_params=_params("parallel", "arbitrary"),
        name="dsa_attention",
    )(zq, zq, iw, kv_new, kv_new, ik2, d0, d1, *past_args)


def _mem_kernel(q_ref, k_ref, v_ref, o_ref):
    nt = (((1,), (1,)), ((), ()))
    for h in range(H_M):
        sl = slice(h * HD_M, (h + 1) * HD_M)
        s = lax.dot_general(q_ref[:, sl], k_ref[:, sl], nt, preferred_element_type=F32) * (HD_M ** -0.5)
        m = jnp.max(s, axis=1, keepdims=True)
        p = jnp.exp(s - m)
        den = jnp.sum(p, axis=1, keepdims=True)
        o = jnp.dot(p.astype(BF16), v_ref[:, sl], preferred_element_type=F32)
        o_ref[:, sl] = (o / den).astype(o_ref.dtype)


def _mem_attention(zq, row0, b, t, mk, mv, tq):
    nq = t // tq
    blk0 = row0 // tq
    return pl.pallas_call(
        _mem_kernel,
        grid=(b, nq),
        in_specs=[pl.BlockSpec((tq, D_M), lambda bi, i: (blk0 + bi * nq + i, 2)),
                  pl.BlockSpec((None, N_MEM, D_M), lambda bi, i: (bi, 0, 0)),
                  pl.BlockSpec((None, N_MEM, D_M), lambda bi, i: (bi, 0, 0))],
        out_specs=pl.BlockSpec((tq, D_M), lambda bi, i: (bi * nq + i, 0)),
        out_shape=jax.ShapeDtypeStruct((b * t, D_M), BF16),
        compiler_params=_params("parallel", "parallel"),
        name="memory_attention",
    )(zq, mk, mv)


def _mix_kernel(yap_ref, yas_ref, gp_ref, gs_ref, obp_ref, obs_ref, omp_ref, oms_ref, ga_ref, gb_ref, gm_ref,
                pa_ref, pb_ref, pm_ref, o_ref, *, na):
    i = pl.program_id(1)
    oa = (_pick(i, na, yap_ref, yas_ref) * _pick(i, na, gp_ref, gs_ref)).astype(BF16)
    acc = _sigmoid(ga_ref[...]) * jnp.dot(oa, pa_ref[...], preferred_element_type=F32)
    acc = acc + _sigmoid(gb_ref[...]) * jnp.dot(_pick(i, na, obp_ref, obs_ref), pb_ref[...],
                                                preferred_element_type=F32)
    acc = acc + _sigmoid(gm_ref[...]) * jnp.dot(_pick(i, na, omp_ref, oms_ref), pm_ref[...],
                                                preferred_element_type=F32)
    o_ref[...] = acc.astype(o_ref.dtype)


def _mix(ya2, g2, ob2, om2, zg, pa, pb, pm, tm, tn):
    n = zg.shape[0]
    nj = D_MODEL // tn
    na = g2[0].shape[0] // tm
    two = lambda w: (pl.BlockSpec((tm, w), lambda j, i: (jnp.minimum(i, na - 1), 0)),
                     pl.BlockSpec((tm, w), lambda j, i: (jnp.maximum(i - na, 0), 0)))
    gate = lambda br: pl.BlockSpec((tm, tn), lambda j, i, br=br: (i, br * nj + j))
    wt = pl.BlockSpec((D_A, tn), lambda j, i: (0, j))
    return pl.pallas_call(
        functools.partial(_mix_kernel, na=na),
        grid=(nj, n // tm),
        in_specs=[*two(D_A), *two(D_A), *two(D_B), *two(D_M), gate(0), gate(1), gate(2), wt, wt, wt],
        out_specs=pl.BlockSpec((tm, tn), lambda j, i: (i, j)),
        out_shape=jax.ShapeDtypeStruct((n, D_MODEL), BF16),
        compiler_params=_params("parallel", "parallel"),
        name="branch_mix",
    )(*ya2, *g2, *ob2, *om2, zg, zg, zg, pa, pb, pm)


def _wo_kernel(m_ref, w_ref, xa_ref, xb_ref, g_ref, h_ref, u_ref, *, na):
    x = _pick(pl.program_id(0), na, xa_ref, xb_ref)
    h = x + jnp.dot(m_ref[...], w_ref[...], preferred_element_type=F32)
    h_ref[...] = h
    y = h * lax.rsqrt(jnp.mean(h * h, axis=-1, keepdims=True) + NORM_EPS)
    u_ref[...] = (y * g_ref[...]).astype(u_ref.dtype)


def _wo_residual_norm(mixed, w_o, xa, xb, g, tm):
    n = mixed.shape[0]
    na = xa.shape[0] // tm
    tile = pl.BlockSpec((tm, D_MODEL), lambda i: (i, 0))
    sa, sb = _two_part_specs(tm, D_MODEL, na)
    return pl.pallas_call(
        functools.partial(_wo_kernel, na=na),
        grid=(n // tm,),
        in_specs=[tile, pl.BlockSpec((D_MODEL, D_MODEL), lambda i: (0, 0)), sa, sb,
                  pl.BlockSpec((1, D_MODEL), lambda i: (0, 0))],
        out_specs=[tile, tile],
        out_shape=[jax.ShapeDtypeStruct((n, D_MODEL), F32), jax.ShapeDtypeStruct((n, D_MODEL), BF16)],
        compiler_params=_params("parallel"),
        name="wo_residual_norm",
    )(mixed, w_o, xa, xb, g.reshape(1, D_MODEL))


def _moe_kernel(te_ref, nu_ref, x_ref, gate_ref, *refs):
    o_ref = refs[-1]
    i = pl.program_id(0)

    @pl.when(i >= nu_ref[0])
    def _():
        o_ref[...] = jnp.zeros(o_ref.shape, F32)

    @pl.when(i < nu_ref[0])
    def _():
        x = x_ref[...]
        gate = gate_ref[...]
        out = None
        for j in range(TOP_K_INNER):
            w1_ref, w3_ref, w2_ref = refs[3 * j:3 * j + 3]
            a = jnp.dot(x, w1_ref[...], preferred_element_type=F32)
            b = jnp.dot(x, w3_ref[...], preferred_element_type=F32)
            hid = (a * _sigmoid(a)) * b * gate[:, j:j + 1]
            part = jnp.dot(hid.astype(BF16), w2_ref[...], preferred_element_type=F32)
            out = part if out is None else out + part
        o_ref[...] = out


def _moe_grouped(x_sorted, gate_sorted, tile_experts, n_used, w1, w3, w2):
    npad = x_sorted.shape[0]
    w_specs, w_args = [], []
    for j in range(TOP_K_INNER):
        ew = lambda i, te, nu, j=j: (te[j, i], 0, 0)
        w_specs += [pl.BlockSpec((None, D_MODEL, D_EXPERT), ew), pl.BlockSpec((None, D_MODEL, D_EXPERT), ew),
                    pl.BlockSpec((None, D_EXPERT, D_MODEL), ew)]
        w_args += [w1, w3, w2]
    return pl.pallas_call(
        _moe_kernel,
        grid_spec=pltpu.PrefetchScalarGridSpec(
            num_scalar_prefetch=2,
            grid=(npad // MOE_TM,),
            in_specs=[pl.BlockSpec((MOE_TM, D_MODEL), lambda i, te, nu: (i, 0)),
                      pl.BlockSpec((MOE_TM, TOP_K_INNER), lambda i, te, nu: (i, 0))] + w_specs,
            out_specs=pl.BlockSpec((MOE_TM, D_MODEL), lambda i, te, nu: (i, 0))),
        out_shape=jax.ShapeDtypeStruct((npad, D_MODEL), F32),
        compiler_params=_params("arbitrary"),
        name="moe_grouped",
    )(tile_experts, n_used, x_sorted, gate_sorted, *w_args)


def _route(logits, b_grp, b_rt):
    n = logits.shape[0]
    g_logits = logits[:, :N_GROUPS] + b_grp
    e_logits = (logits[:, N_GROUPS:N_GROUPS + N_EXPERTS] + b_rt).reshape(n, N_GROUPS, EXPERTS_PER_GROUP)
    g_idx = jnp.argmax(g_logits, axis=-1).astype(jnp.int32)
    g_w = jnp.max(jax.nn.softmax(g_logits, axis=-1), axis=-1, keepdims=True)
    onehot_g = g_idx[:, None] == jnp.arange(N_GROUPS, dtype=jnp.int32)[None, :]
    e_in = jnp.sum(jnp.where(onehot_g[:, :, None], e_logits, 0.0), axis=1)
    lane = jnp.arange(EXPERTS_PER_GROUP, dtype=jnp.int32)[None, :]
    i1 = jnp.argmax(e_in, axis=-1).astype(jnp.int32)
    v1 = jnp.max(e_in, axis=-1, keepdims=True)
    rest = jnp.where(lane == i1[:, None], -jnp.inf, e_in)
    i2 = jnp.argmax(rest, axis=-1).astype(jnp.int32)
    v2 = jnp.max(rest, axis=-1, keepdims=True)
    w12 = g_w * jax.nn.softmax(jnp.concatenate([v1, v2], axis=-1), axis=-1)
    lo, hi = jnp.minimum(i1, i2), jnp.maximum(i1, i2)
    gate2 = jnp.where((i1 < i2)[:, None], w12, w12[:, ::-1])
    n_pairs = EXPERTS_PER_GROUP * (EXPERTS_PER_GROUP - 1) // 2
    n_cls = N_GROUPS * n_pairs
    cls = g_idx * n_pairs + lo * (2 * EXPERTS_PER_GROUP - lo - 1) // 2 + hi - lo - 1
    pair_lo = jnp.array([a for a in range(EXPERTS_PER_GROUP) for _ in range(a + 1, EXPERTS_PER_GROUP)], jnp.int32)
    pair_hi = jnp.array([c for a in range(EXPERTS_PER_GROUP) for c in range(a + 1, EXPERTS_PER_GROUP)], jnp.int32)
    cls_ids = jnp.arange(n_cls, dtype=jnp.int32)
    cls_experts = jnp.stack([(cls_ids // n_pairs) * EXPERTS_PER_GROUP + jnp.tile(pair_lo, N_GROUPS),
                             (cls_ids // n_pairs) * EXPERTS_PER_GROUP + jnp.tile(pair_hi, N_GROUPS)])

    onehot_c = cls[:, None] == cls_ids[None, :]
    counts = jnp.sum(onehot_c.astype(jnp.int32), axis=0)
    tiles = (counts + MOE_TM - 1) // MOE_TM
    tile_end = jnp.cumsum(tiles)
    slot0 = (tile_end - tiles) * MOE_TM
    start = jnp.cumsum(counts) - counts
    order = jnp.argsort(cls, stable=True).astype(jnp.int32)
    rank = jnp.sum(jnp.where(onehot_c, jnp.cumsum(onehot_c.astype(jnp.int32), axis=0) - 1, 0), axis=1)
    inv = jnp.sum(jnp.where(onehot_c, slot0[None, :], 0), axis=1) + rank
    npad = n + n_cls * MOE_TM
    slots = jnp.arange(npad, dtype=jnp.int32)
    slot_cls = jnp.minimum(jnp.sum(jnp.where(slots[:, None] >= tile_end[None, :] * MOE_TM, 1, 0), axis=1), n_cls - 1)
    in_cls = slot_cls[:, None] == cls_ids[None, :]
    pos = slots - jnp.sum(jnp.where(in_cls, slot0[None, :], 0), axis=1)
    used = pos < jnp.sum(jnp.where(in_cls, counts[None, :], 0), axis=1)
    src = jnp.where(used, pos + jnp.sum(jnp.where(in_cls, start[None, :], 0), axis=1), slots % n)
    perm = order[src]
    gate_sorted = jnp.where(used[:, None], gate2[perm], 0.0)
    tile_in_cls = slot_cls[::MOE_TM, None] == cls_ids[None, :]
    tile_experts = jnp.sum(jnp.where(tile_in_cls[None], cls_experts[:, None, :], 0), axis=-1).astype(jnp.int32)
    n_used = tile_end[-1]
    tile_ids = jnp.arange(npad // MOE_TM)[None, :]
    last = jnp.sum(jnp.where(tile_ids == n_used - 1, tile_experts, 0), axis=1, keepdims=True)
    tile_experts = jnp.where(tile_ids < n_used, tile_experts, last).astype(jnp.int32)
    return perm, inv, gate_sorted, tile_experts, n_used[None].astype(jnp.int32)


def _final_kernel(h_ref, m_ref, g_ref, yp_ref, ys_ref, *, na):
    h = h_ref[...] + m_ref[...]
    y = h * lax.rsqrt(jnp.mean(h * h, axis=-1, keepdims=True) + NORM_EPS) * g_ref[...]
    i = pl.program_id(0)

    @pl.when(i < na)
    def _():
        yp_ref[...] = y

    @pl.when(i >= na)
    def _():
        ys_ref[...] = y


def _final_norm(h, moe, g, n_p, tm):
    n = h.shape[0]
    na = n_p // tm
    tile = pl.BlockSpec((tm, D_MODEL), lambda i: (i, 0))
    sa, sb = _two_part_specs(tm, D_MODEL, na)
    return pl.pallas_call(
        functools.partial(_final_kernel, na=na),
        grid=(n // tm,),
        in_specs=[tile, tile, pl.BlockSpec((1, D_MODEL), lambda i: (0, 0))],
        out_specs=[sa, sb],
        out_shape=[jax.ShapeDtypeStruct((n_p, D_MODEL), F32), jax.ShapeDtypeStruct((n - n_p, D_MODEL), F32)],
        compiler_params=_params("arbitrary"),
        name="residual_final_norm",
    )(h, moe, g.reshape(1, D_MODEL))


def _pad_cols(w, width):
    return jnp.pad(w, ((0, 0), (0, width - w.shape[1])))


def _pad_rows(w, height):
    return jnp.pad(w, ((0, height - w.shape[0]), (0, 0)))


def _pad_lora(x):
    o = 3 * D_A
    pad = lambda t, w: jnp.pad(t, [(0, 0)] * (t.ndim - 1) + [(0, w - t.shape[-1])])
    return jnp.concatenate([x[..., :o], pad(x[..., o:o + W_LORA], WL_PAD),
                            pad(x[..., o + W_LORA:o + W_LORA + A_LORA], AL_PAD),
                            pad(x[..., o + W_LORA + A_LORA:], GL_PAD)], axis=-1)


def _unpad_lora(x):
    o = 3 * D_A
    return jnp.concatenate([x[..., :o], x[..., o:o + W_LORA], x[..., o + WL_PAD:o + WL_PAD + A_LORA],
                            x[..., o + WL_PAD + AL_PAD:o + WL_PAD + AL_PAD + G_LORA]], axis=-1)


def kernel(x_prompt, x_sample, cache_dsa_k, cache_dsa_v, cache_idx_k, state_rwkv_shift, state_rwkv_wkv, cache_mem_k, cache_mem_v, mem_prompt, rel_bias, g_attn, w_in, rwkv_mu, rwkv_w0, rwkv_w_dec, rwkv_a0, rwkv_w_a, rwkv_w_g, rwkv_k_k, rwkv_k_a, rwkv_r_k, rwkv_lnx_w, rwkv_lnx_b, g_mem, w_mem_kv, p_a, p_b, p_m, w_o, g_ffn, w_grp, b_grp, w_rt, b_rt, w1, w3, w2, g_final):
    assert w_in.shape[0] == 1, "single layer"
    bp, tp, _ = x_prompt.shape
    bs, ts, _ = x_sample.shape
    past = cache_dsa_k.shape[2]
    n_p, n_s = bp * tp, bs * ts
    n = n_p + n_s
    tm = 1024
    xp = x_prompt.reshape(n_p, D_MODEL)
    xs = x_sample.reshape(n_s, D_MODEL)

    offs = [0]
    for s in IN_SIZES:
        offs.append(offs[-1] + s)
    seg = lambda i: w_in[0][:, offs[i]:offs[i + 1]]
    w_a_cols = _pad_lora(seg(0)).astype(BF16)
    w_qim = jnp.concatenate([seg(1), seg(4), seg(7)], axis=1).astype(BF16)
    w_kvi = _pad_cols(jnp.concatenate([seg(2), seg(3), seg(5), seg(6)], axis=1), KVI_PAD).astype(BF16)
    w_gate = seg(8).astype(BF16)
    rw = dict(mu=_pad_lora(rwkv_mu[0])[None], w0=rwkv_w0, a0=rwkv_a0, k_k=rwkv_k_k, k_a=rwkv_k_a,
              w_dec=_pad_rows(rwkv_w_dec[0], WL_PAD).astype(BF16), w_a=_pad_rows(rwkv_w_a[0], AL_PAD).astype(BF16),
              w_g=_pad_rows(rwkv_w_g[0], GL_PAD).astype(BF16))

    u = _rmsnorm_bf16(xp, xs, g_attn[0], 512)
    z_a = _matmul(u, w_a_cols, F32, tm, A_PAD // 4, "proj_rwkv")
    z_qim = _matmul(u, w_qim, BF16, tm, 1024, "proj_queries")
    z_kvi, kv_bf = _matmul_kv(u, w_kvi, tm)
    z_g = _matmul(u, w_gate, F32, tm, 1024, "proj_gates")

    mem = mem_prompt.reshape(bp * N_MEM, D_MODEL)
    um = _rmsnorm_bf16(mem[:bp * N_MEM // 2], mem[bp * N_MEM // 2:], g_mem[0], 512)
    mkv = _matmul(um, w_mem_kv[0].astype(BF16), F32, 1024, 1024, "proj_mem_kv")
    mk_p = mkv[:, :D_M].reshape(bp, N_MEM, D_M)
    mv_p = mkv[:, D_M:].reshape(bp, N_MEM, D_M)

    lw_vec, lb_vec, rk_vec = rwkv_lnx_w[0], rwkv_lnx_b[0], rwkv_r_k[0].reshape(D_A)

    def rwkv_group(row0, nseq, t, shift0, wkv0, tb, tt):
        r, w, k, v, kk, a, g, last = _rwkv_prep(z_a, row0, nseq, t, shift0, rw, tb)
        y, s_fin = _wkv(r, w, k, v, kk, a, _state_to_tiles(wkv0), _lane_tile(lw_vec), _lane_tile(lb_vec),
                        _lane_tile(rk_vec), tt)
        return y.reshape(nseq * t, D_A), g, _state_from_tiles(s_fin, nseq), _unpad_lora(last)

    ya_p, g_p, wkv_p, shift_p = rwkv_group(0, bp, tp, jnp.zeros((bp, 1, A_PAD), F32),
                                           jnp.zeros((bp, H_A, HD_A, HD_A), F32), 256, 32)
    ya_s, g_s, wkv_s, shift_s = rwkv_group(n_p, bs, ts, _pad_lora(state_rwkv_shift[0]), state_rwkv_wkv[0], ts, ts)

    k_new, v_new = z_kvi[:, :D_KV], z_kvi[:, D_KV:2 * D_KV]
    ik_new = z_kvi[:, 2 * D_KV:2 * D_KV + D_I]
    iw = z_kvi[:, 2 * D_KV + D_I:2 * D_KV + D_I + H_I]
    grp = lambda t, sl, b, tlen: t[sl].reshape(b, tlen, t.shape[-1])
    sp, ss = slice(0, n_p), slice(n_p, n)
    ob_p = _dsa(z_qim, kv_bf, 0, bp, tp, iw[sp], grp(ik_new, sp, bp, tp).astype(BF16), rel_bias,
                Q_BLOCK if tp % Q_BLOCK == 0 else tp, 0, tp)
    l_s = past + ts
    ik_s = jnp.pad(jnp.concatenate([cache_idx_k[0].astype(BF16), grp(ik_new, ss, bs, ts).astype(BF16)], axis=1),
                   ((0, 0), (0, past + KEY_TILE - l_s), (0, 0)))
    ob_s = _dsa(z_qim, kv_bf, n_p, bs, ts, iw[ss], ik_s, rel_bias, Q_BLOCK if ts % Q_BLOCK == 0 else ts, past, l_s,
                cache_dsa_k[0].reshape(bs, past * N_KV_B, HD_B), cache_dsa_v[0].reshape(bs, past * N_KV_B, HD_B))

    om_p = _mem_attention(z_qim, 0, bp, tp, mk_p.astype(BF16), mv_p.astype(BF16), 256)
    om_s = _mem_attention(z_qim, n_p, bs, ts, cache_mem_k[0].reshape(bs, N_MEM, D_M).astype(BF16),
                          cache_mem_v[0].reshape(bs, N_MEM, D_M).astype(BF16), ts)

    mixed = _mix((ya_p, ya_s), (g_p, g_s), (ob_p, ob_s), (om_p, om_s), z_g, p_a[0].astype(BF16), p_b[0].astype(BF16),
                 p_m[0].astype(BF16), 256, 1024)
    h, u2 = _wo_residual_norm(mixed, w_o[0].astype(BF16), xp, xs, g_ffn[0], 256)

    w_route = _pad_cols(jnp.concatenate([w_grp[0], w_rt[0]], axis=1), LANES).astype(BF16)
    logits = _matmul(u2, w_route, F32, tm, LANES, "moe_router")
    perm, inv, gate_sorted, tile_experts, n_used = _route(logits, b_grp[0], b_rt[0])
    moe_sorted = _moe_grouped(u2[perm], gate_sorted, tile_experts, n_used,
                              _cast_bf16(w1[0]), _cast_bf16(w3[0]), _cast_bf16(w2[0]))
    y_p, y_s = _final_norm(h, moe_sorted[inv], g_final, n_p, 512)

    st = lambda t, b, tlen, shape: t.reshape((1, b, tlen) + shape)
    return (y_p.reshape(bp, tp, D_MODEL), y_s.reshape(bs, ts, D_MODEL),
            st(k_new[sp], bp, tp, (N_KV_B, HD_B)), st(v_new[sp], bp, tp, (N_KV_B, HD_B)), st(ik_new[sp], bp, tp, (D_I,)),
            shift_p[None], wkv_p[None],
            mk_p.reshape(1, bp, N_MEM, H_M, HD_M), mv_p.reshape(1, bp, N_MEM, H_M, HD_M),
            st(k_new[ss], bs, ts, (N_KV_B, HD_B)), st(v_new[ss], bs, ts, (N_KV_B, HD_B)), st(ik_new[ss], bs, ts, (D_I,)),
            shift_s[None], wkv_s[None])
```

```python
import functools
import math

import jax
import jax.numpy as jnp
from jax import lax
from jax.experimental import pallas as pl
from jax.experimental.pallas import tpu as pltpu

F32 = jnp.float32
BF16 = jnp.bfloat16

LANES = 128
D_MODEL = 2048
CHUNK = 64
NORM_EPS = 1e-6
H_A, HD_A = 16, 64
D_A = H_A * HD_A
W_LORA, A_LORA, G_LORA = 64, 64, 160
LNX_EPS = 64e-5
A_COLS = 3 * D_A + W_LORA + A_LORA + G_LORA
H_B, N_KV_B, HD_B = 8, 2, 128
G_B = H_B // N_KV_B
D_B = H_B * HD_B
D_KV = N_KV_B * HD_B
H_I, D_I = 16, 64
TOPK_MAX = 256
Q_BLOCK = 128
REL_BUCKETS = 32
REL_MAX_DIST = 128
N_MEM, H_M, HD_M = 256, 4, 256
D_M = H_M * HD_M
N_BRANCH = 3
IN_SIZES = (A_COLS, D_B, D_KV, D_KV, H_I * D_I, D_I, H_I, D_M, N_BRANCH * D_MODEL)
N_GROUPS, EXPERTS_PER_GROUP = 4, 4
N_EXPERTS = N_GROUPS * EXPERTS_PER_GROUP
TOP_K_INNER = 2
D_EXPERT = 512

WL_PAD, AL_PAD, GL_PAD = 128, 128, 256
A_PAD = 3 * D_A + WL_PAD + AL_PAD + GL_PAD
KVI_PAD = 2 * D_KV + LANES
KEY_TILE = 2 * LANES
SEQ_PER_TILE = LANES // H_A
MOE_TM = 256
INT_MIN = -(2 ** 31)
VMEM_LIMIT = 48 * 1024 * 1024


def _params(*sem):
    return pltpu.CompilerParams(dimension_semantics=sem, vmem_limit_bytes=VMEM_LIMIT)


def _two_part_specs(tm, width, na):
    return (pl.BlockSpec((tm, width), lambda i: (jnp.minimum(i, na - 1), 0)),
            pl.BlockSpec((tm, width), lambda i: (jnp.maximum(i - na, 0), 0)))


def _pick(i, na, a_ref, b_ref):
    return jnp.where(i < na, a_ref[...], b_ref[...])


def _norm2_kernel(xa_ref, xb_ref, g_ref, o_ref, *, na):
    x = _pick(pl.program_id(0), na, xa_ref, xb_ref)
    y = x * lax.rsqrt(jnp.mean(x * x, axis=-1, keepdims=True) + NORM_EPS)
    o_ref[...] = (y * g_ref[...]).astype(o_ref.dtype)


def _rmsnorm_bf16(xa, xb, g, tm):
    d = xa.shape[1]
    na, nb = xa.shape[0] // tm, xb.shape[0] // tm
    sa, sb = _two_part_specs(tm, d, na)
    return pl.pallas_call(
        functools.partial(_norm2_kernel, na=na),
        grid=(na + nb,),
        in_specs=[sa, sb, pl.BlockSpec((1, d), lambda i: (0, 0))],
        out_specs=pl.BlockSpec((tm, d), lambda i: (i, 0)),
        out_shape=jax.ShapeDtypeStruct(((na + nb) * tm, d), BF16),
        compiler_params=_params("parallel"),
        name="rmsnorm_bf16",
    )(xa, xb, g.reshape(1, d))


def _mm_kernel(a_ref, w_ref, o_ref, *, logistic):
    acc = jnp.dot(a_ref[...], w_ref[...], preferred_element_type=F32)
    o_ref[...] = (_sigmoid(acc) if logistic else acc).astype(o_ref.dtype)


def _matmul(a, w, out_dtype, tm, tn, name, logistic=False):
    n, k = a.shape
    m = w.shape[1]
    return pl.pallas_call(
        functools.partial(_mm_kernel, logistic=logistic),
        grid=(n // tm, m // tn),
        in_specs=[pl.BlockSpec((tm, k), lambda i, j: (i, 0)), pl.BlockSpec((k, tn), lambda i, j: (0, j))],
        out_specs=pl.BlockSpec((tm, tn), lambda i, j: (i, j)),
        out_shape=jax.ShapeDtypeStruct((n, m), out_dtype),
        compiler_params=_params("parallel", "parallel"),
        name=name,
    )(a, w)


def _mm_kv_kernel(a_ref, w_ref, o_ref, kv_ref):
    acc = jnp.dot(a_ref[...], w_ref[...], preferred_element_type=F32)
    o_ref[...] = acc
    kv_ref[...] = acc[:, :2 * D_KV].astype(kv_ref.dtype)


def _matmul_kv(a, w, tm):
    n, k = a.shape
    m = w.shape[1]
    return pl.pallas_call(
        _mm_kv_kernel,
        grid=(n // tm,),
        in_specs=[pl.BlockSpec((tm, k), lambda i: (i, 0)), pl.BlockSpec((k, m), lambda i: (0, 0))],
        out_specs=[pl.BlockSpec((tm, m), lambda i: (i, 0)), pl.BlockSpec((tm, 2 * D_KV), lambda i: (i, 0))],
        out_shape=[jax.ShapeDtypeStruct((n, m), F32), jax.ShapeDtypeStruct((n, 2 * D_KV), BF16)],
        compiler_params=_params("parallel"),
        name="proj_kv",
    )(a, w)


def _cast_kernel(x_ref, o_ref):
    o_ref[...] = x_ref[...].astype(o_ref.dtype)


def _cast_bf16(x):
    e, r, c = x.shape
    spec = pl.BlockSpec((None, r, c), lambda i: (i, 0, 0))
    return pl.pallas_call(
        _cast_kernel, grid=(e,), in_specs=[spec], out_specs=spec,
        out_shape=jax.ShapeDtypeStruct(x.shape, BF16), compiler_params=_params("parallel"), name="cast_bf16",
    )(x)


def _softplus(x):
    return jnp.maximum(x, 0.0) + jnp.log1p(jnp.exp(-jnp.abs(x)))


def _sigmoid(x):
    return 0.5 * jnp.tanh(0.5 * x) + 0.5


def _rwkv_prep_kernel(z_ref, sh_ref, mu_ref, w0_ref, wdec_ref, a0_ref, wa_ref, wg_ref, kk_ref, ka_ref,
                      r_out, w_out, k_out, v_out, kk_out, a_out, g_out, last_out, carry_ref):
    tb = z_ref.shape[0]

    @pl.when(pl.program_id(1) == 0)
    def _():
        carry_ref[...] = sh_ref[...]

    z = z_ref[...]
    row = lax.broadcasted_iota(jnp.int32, (tb, 1), 0)
    prev = jnp.where(row == 0, carry_ref[...], pltpu.roll(z, 1, axis=0))
    carry_ref[...] = z[tb - 1:tb, :]
    last_out[...] = z[tb - 1:tb, :]
    zm = z + (prev - z) * mu_ref[...]
    r = zm[:, 0:D_A]
    k = zm[:, D_A:2 * D_A]
    v = zm[:, 2 * D_A:3 * D_A]
    o = 3 * D_A
    wl = zm[:, o:o + WL_PAD]
    al = zm[:, o + WL_PAD:o + WL_PAD + AL_PAD]
    gl = zm[:, o + WL_PAD + AL_PAD:]
    lw = jnp.dot(jnp.tanh(wl).astype(BF16), wdec_ref[...], preferred_element_type=F32)
    wv = -_softplus(-(w0_ref[...] + lw)) - 0.5
    a = _sigmoid(a0_ref[...] + jnp.dot(al.astype(BF16), wa_ref[...], preferred_element_type=F32))
    r_out[...] = r
    w_out[...] = jnp.exp(-jnp.exp(wv))
    k_out[...] = k * (1.0 + (a - 1.0) * ka_ref[...])
    v_out[...] = v
    kk_out[...] = k * kk_ref[...]
    a_out[...] = a
    g_out[...] = jnp.dot(_sigmoid(gl).astype(BF16), wg_ref[...], preferred_element_type=F32)


def _rwkv_prep(z, row0, nseq, t, shift0, wts, tb):
    nt = t // tb
    blk0 = row0 // tb
    row = lambda c: pl.BlockSpec((1, c), lambda s, i: (0, 0))
    full = lambda a, b: pl.BlockSpec((a, b), lambda s, i: (0, 0))
    out_spec = pl.BlockSpec((None, tb, D_A), lambda s, i: (s, i, 0))
    out_sds = jax.ShapeDtypeStruct((nseq, t, D_A), F32)
    g_spec = pl.BlockSpec((tb, D_A), lambda s, i: (s * nt + i, 0))
    g_sds = jax.ShapeDtypeStruct((nseq * t, D_A), F32)
    return pl.pallas_call(
        _rwkv_prep_kernel,
        grid=(nseq, nt),
        in_specs=[pl.BlockSpec((tb, A_PAD), lambda s, i: (blk0 + s * nt + i, 0)),
                  pl.BlockSpec((None, 1, A_PAD), lambda s, i: (s, 0, 0)),
                  row(A_PAD), row(D_A), full(WL_PAD, D_A), row(D_A), full(AL_PAD, D_A), full(GL_PAD, D_A),
                  row(D_A), row(D_A)],
        out_specs=[out_spec] * 6 + [g_spec, pl.BlockSpec((None, 1, A_PAD), lambda s, i: (s, 0, 0))],
        out_shape=[out_sds] * 6 + [g_sds, jax.ShapeDtypeStruct((nseq, 1, A_PAD), F32)],
        scratch_shapes=[pltpu.VMEM((1, A_PAD), F32)],
        compiler_params=_params("arbitrary", "arbitrary"),
        name="rwkv_prep",
    )(z, shift0, wts["mu"], wts["w0"], wts["w_dec"], wts["a0"], wts["w_a"], wts["w_g"], wts["k_k"], wts["k_a"])


def _wkv_kernel(r_ref, w_ref, k_ref, v_ref, kk_ref, a_ref, s0_ref, lw_ref, lb_ref, rk_ref,
                y_ref, st_ref, s_ref, ab_ref, tma_ref, tmb_ref, ytm_ref, *, n_tblocks):
    tt = r_ref.shape[1]
    low = lax.broadcasted_iota(jnp.int32, (SEQ_PER_TILE, LANES), 1) < HD_A

    @pl.when(pl.program_id(1) == 0)
    def _():
        s_ref[...] = s0_ref[...]

    n_pairs = tt // 2

    def to_tiles(buf_ref, j):
        j = jnp.minimum(j, n_pairs - 1)
        for ai, ref in enumerate((r_ref, w_ref, k_ref, v_ref, kk_ref, a_ref)):
            x0 = ref[:, 2 * j, :]
            x1 = ref[:, 2 * j + 1, :]
            rows = []
            for h in range(H_A):
                sl = slice((h // 2) * LANES, (h // 2 + 1) * LANES)
                if h % 2 == 0:
                    rows.append(jnp.where(low, x0[:, sl], pltpu.roll(x1[:, sl], HD_A, axis=1)))
                else:
                    rows.append(jnp.where(low, pltpu.roll(x0[:, sl], HD_A, axis=1), x1[:, sl]))
            m = jnp.concatenate(rows, axis=0).T
            buf_ref[ai, 0] = m[:HD_A]
            buf_ref[ai, 1] = m[HD_A:]

    def step(buf_ref, u, t):
        kk = buf_ref[4, u]
        ss = jnp.sum(kk * kk, axis=0, keepdims=True)
        kkn = kk * lax.rsqrt(jnp.maximum(ss, 1e-24))
        ab_ref[0] = -kkn
        ab_ref[1] = kkn * buf_ref[5, u]
        halves = []
        for rows in (slice(0, HD_A // 2), slice(HD_A // 2, HD_A)):
            vh = buf_ref[3, u, rows, :]
            sa = jnp.zeros((HD_A // 2, LANES), F32)
            for k in range(HD_A):
                sa = sa + s_ref[k, rows, :] * ab_ref[0, k:k + 1, :]
            yh = jnp.zeros((HD_A // 2, LANES), F32)
            for k in range(HD_A):
                s_new = (s_ref[k, rows, :] * buf_ref[1, u, k:k + 1, :] + sa * ab_ref[1, k:k + 1, :]
                         + vh * buf_ref[2, u, k:k + 1, :])
                s_ref[k, rows, :] = s_new
                yh = yh + s_new * buf_ref[0, u, k:k + 1, :]
            halves.append(yh)
        y = jnp.concatenate(halves, axis=0)
        vv = buf_ref[3, u]
        mean = jnp.mean(y, axis=0, keepdims=True)
        d = y - mean
        var = jnp.mean(d * d, axis=0, keepdims=True)
        yn = d * lax.rsqrt(var + LNX_EPS) * lw_ref[...] + lb_ref[...]
        bonus = jnp.sum(buf_ref[0, u] * buf_ref[2, u] * rk_ref[...], axis=0, keepdims=True) * vv
        ytm_ref[t] = yn + bonus

    to_tiles(tma_ref, 0)

    def two_pairs(m, carry):
        to_tiles(tmb_ref, 2 * m + 1)
        step(tma_ref, 0, 4 * m)
        step(tma_ref, 1, 4 * m + 1)
        to_tiles(tma_ref, 2 * m + 2)
        step(tmb_ref, 0, 4 * m + 2)
        step(tmb_ref, 1, 4 * m + 3)
        return carry

    lax.fori_loop(0, n_pairs // 2, two_pairs, 0)

    for j in range(tt // 2):
        m = jnp.concatenate([ytm_ref[2 * j], ytm_ref[2 * j + 1]], axis=0).T
        for hp in range(H_A // 2):
            even = m[(2 * hp) * SEQ_PER_TILE:(2 * hp + 1) * SEQ_PER_TILE]
            odd = m[(2 * hp + 1) * SEQ_PER_TILE:(2 * hp + 2) * SEQ_PER_TILE]
            cols = slice(hp * LANES, (hp + 1) * LANES)
            y_ref[:, 2 * j, cols] = jnp.where(low, even, pltpu.roll(odd, HD_A, axis=1))
            y_ref[:, 2 * j + 1, cols] = jnp.where(low, pltpu.roll(even, HD_A, axis=1), odd)

    @pl.when(pl.program_id(1) == n_tblocks - 1)
    def _():
        st_ref[...] = s_ref[...]


def _wkv(r, w, k, v, kk, a, s0, lw, lb, rk, tt):
    nseq, t, _ = r.shape
    p = nseq * H_A
    assert tt % 4 == 0 and t % tt == 0 and nseq % SEQ_PER_TILE == 0
    seq = pl.BlockSpec((SEQ_PER_TILE, tt, D_A), lambda g, i: (g, i, 0))
    vec = pl.BlockSpec((HD_A, LANES), lambda g, i: (0, 0))
    st = pl.BlockSpec((HD_A, HD_A, LANES), lambda g, i: (0, 0, g))
    return pl.pallas_call(
        functools.partial(_wkv_kernel, n_tblocks=t // tt),
        grid=(p // LANES, t // tt),
        in_specs=[seq] * 6 + [st, vec, vec, vec],
        out_specs=[seq, st],
        out_shape=[jax.ShapeDtypeStruct((nseq, t, D_A), F32), jax.ShapeDtypeStruct((HD_A, HD_A, p), F32)],
        scratch_shapes=[pltpu.VMEM((HD_A, HD_A, LANES), F32), pltpu.VMEM((2, HD_A, LANES), F32),
                        pltpu.VMEM((6, 2, HD_A, LANES), F32), pltpu.VMEM((6, 2, HD_A, LANES), F32),
                        pltpu.VMEM((tt, HD_A, LANES), F32)],
        compiler_params=_params("arbitrary", "arbitrary"),
        name="wkv_recurrence",
    )(r, w, k, v, kk, a, s0, lw, lb, rk)


def _state_to_tiles(wkv):
    nseq = wkv.shape[0]
    x = wkv.reshape(nseq // SEQ_PER_TILE, SEQ_PER_TILE, H_A, HD_A, HD_A)
    return x.transpose(4, 3, 0, 2, 1).reshape(HD_A, HD_A, nseq * H_A)


def _state_from_tiles(s, nseq):
    x = s.reshape(HD_A, HD_A, nseq // SEQ_PER_TILE, H_A, SEQ_PER_TILE)
    return x.transpose(2, 4, 3, 1, 0).reshape(nseq, H_A, HD_A, HD_A)


def _lane_tile(vec):
    return jnp.repeat(vec.reshape(H_A, HD_A).T, SEQ_PER_TILE, axis=1)


def _dsa_kernel(*refs, q_start, l_valid, topk, keys_on_rows, n_past, lp):
    refs = list(refs)
    q_ref, iq_ref, iw_ref, k_ref, v_ref, ik2_ref, d0_ref, d1_ref = refs[:8]
    del refs[:8]
    kpast_ref = vpast_ref = ktail_ref = vtail_ref = None
    if n_past:
        kpast_ref, vpast_ref = refs[:2]
        del refs[:2]
    o_ref, madd_ref, logit_ref, mx_ref, ls_ref, acc_ref, score_ref, key_ref = refs[:8]
    del refs[:8]
    if not keys_on_rows:
        iwb_ref = refs.pop(0)
    if n_past:
        ktail_ref, vtail_ref = refs
        n_new = k_ref.shape[0]
        for tail_ref, new_ref in ((ktail_ref, k_ref), (vtail_ref, v_ref)):
            tail_ref[...] = jnp.zeros(tail_ref.shape, BF16)
            tail_ref[0:n_new, :] = new_ref[...]
    n_past_tiles = n_past // KEY_TILE

    def kv_tile(new_ref, past_ref, tail_ref, c, g):
        cols = slice(g * HD_B, (g + 1) * HD_B)
        if not n_past:
            return new_ref[pl.ds(pl.multiple_of(c * KEY_TILE, KEY_TILE), KEY_TILE), cols]
        pc = jnp.minimum(c, n_past_tiles - 1)
        row0 = pl.multiple_of(pc * (KEY_TILE * N_KV_B), KEY_TILE * N_KV_B)
        past = past_ref[pl.ds(row0 + g, KEY_TILE, stride=N_KV_B), :].astype(BF16)
        return jnp.where(c < n_past_tiles, past, tail_ref[:, cols])

    tq = q_ref.shape[0]
    q0 = q_start + pl.program_id(1) * tq
    nt = (((1,), (1,)), ((), ()))
    lane_shift = int(math.log2(LANES))
    chunk_shift = int(math.log2(CHUNK))
    kax = 0 if keys_on_rows else 1
    dc = lax.shift_right_logical(q0, lane_shift)
    n_lane_tiles = jnp.minimum(lax.shift_right_logical(q0 + tq - 1, lane_shift) + 1, lp // LANES)
    n_tiles = lax.shift_right_logical(n_lane_tiles + 1, 1)

    def tile_off(c, width):
        return pl.multiple_of(c * width, width)

    def keys_at(off, width):
        return (pl.ds(off, width), slice(None)) if keys_on_rows else (slice(None), pl.ds(off, width))

    def per_query(x):
        return jnp.sum(x, axis=kax, keepdims=True)

    iw = iw_ref[...] * ((H_I * D_I) ** -0.5)
    if not keys_on_rows:
        for h in range(H_I):
            iwb_ref[h] = jnp.broadcast_to(iw[:, h:h + 1], (tq, LANES))

    def score_lane_tile(c):
        kab = ik2_ref[c]
        acc = [jnp.zeros((LANES, tq) if keys_on_rows else (tq, LANES), F32) for _ in range(2)]
        for hp in range(H_I // 2):
            iq_pair = iq_ref[:, hp * LANES:(hp + 1) * LANES]
            if keys_on_rows:
                d = lax.dot_general(kab, iq_pair, nt, preferred_element_type=F32)
                halves = (d[:LANES], d[LANES:])
                wts = (iw[2 * hp:2 * hp + 1, :], iw[2 * hp + 1:2 * hp + 2, :])
            else:
                d = lax.dot_general(iq_pair, kab, nt, preferred_element_type=F32)
                halves = (d[:, :LANES], d[:, LANES:])
                wts = (iwb_ref[2 * hp], iwb_ref[2 * hp + 1])
            for u in range(2):
                acc[u] = acc[u] + jnp.maximum(halves[u], 0.0) * wts[u]
        score_ref[keys_at(tile_off(c, LANES), LANES)] = acc[0] + acc[1]

    def score_tile(c, carry):
        score_lane_tile(2 * c)
        score_lane_tile(2 * c + 1)
        return carry

    lax.fori_loop(0, n_tiles, score_tile, 0)

    qshape, kshape = ((1, tq), lambda w: (w, 1)) if keys_on_rows else ((tq, 1), lambda w: (1, w))
    qchunk = lax.shift_right_arithmetic(q0 + lax.broadcasted_iota(jnp.int32, qshape, 1 - kax), chunk_shift)

    def visible(lo, width):
        kpos = lo + lax.broadcasted_iota(jnp.int32, kshape(width), kax)
        return (lax.shift_right_arithmetic(kpos, chunk_shift) <= qchunk) & (kpos < l_valid)

    def key_tile(c, carry):
        off = tile_off(c, KEY_TILE)
        bits = pltpu.bitcast(score_ref[keys_at(off, KEY_TILE)] + 0.0, jnp.int32)
        key = bits ^ (lax.shift_right_arithmetic(bits, 31) & jnp.int32(0x7FFFFFFF))
        key_ref[keys_at(off, KEY_TILE)] = jnp.where(visible(off, KEY_TILE), key, jnp.int32(INT_MIN))
        return carry

    lax.fori_loop(0, n_tiles, key_tile, 0)
    kf = jnp.float32(topk)
    sub = 8

    def count(pred):
        def body(c, acc):
            m = jnp.where(pred(key_ref[keys_at(tile_off(c, KEY_TILE), KEY_TILE)]), 1.0, 0.0)
            if keys_on_rows:
                parts = [m[r:r + sub] for r in range(0, KEY_TILE, sub)]
            else:
                parts = [m[:, :LANES], m[:, LANES:]]
            while len(parts) > 1:
                parts = [parts[u] + parts[u + 1] for u in range(0, len(parts), 2)]
            return acc + parts[0]
        acc = lax.fori_loop(0, n_tiles, body, jnp.zeros((sub, tq) if keys_on_rows else (tq, LANES), F32))
        return per_query(acc)

    def search(i, thr):
        cand = thr ^ lax.shift_left(jnp.int32(1), 31 - i)
        return jnp.where(count(lambda kc: kc >= cand) >= kf, cand, thr)

    thr = lax.fori_loop(0, 32, search, jnp.full(qshape, INT_MIN, jnp.int32))

    def store_mask(off, width, sel):
        madd = jnp.where(sel, 0.0, -jnp.inf)
        if keys_on_rows:
            for u in range(width // LANES):
                madd_ref[:, pl.ds(pl.multiple_of(off + u * LANES, LANES), LANES)] = madd[u * LANES:(u + 1) * LANES].T
        else:
            madd_ref[:, pl.ds(off, width)] = madd

    surplus = (count(lambda kc: kc >= thr) > kf) & (thr != jnp.int32(INT_MIN))
    any_surplus = jnp.max(jnp.where(surplus, 1.0, 0.0))

    @pl.when(any_surplus == 0.0)
    def _():
        def select_tile(c, carry):
            off = tile_off(c, KEY_TILE)
            store_mask(off, KEY_TILE, visible(off, KEY_TILE) & (key_ref[keys_at(off, KEY_TILE)] >= thr))
            return carry

        lax.fori_loop(0, n_tiles, select_tile, 0)

    @pl.when(any_surplus > 0.0)
    def _():
        need = kf - count(lambda kc: kc > thr)
        ii = lax.broadcasted_iota(jnp.int32, (LANES, LANES), 0)
        jj = lax.broadcasted_iota(jnp.int32, (LANES, LANES), 1)
        tri = jnp.where((ii >= jj) if keys_on_rows else (ii <= jj), 1.0, 0.0).astype(BF16)

        def select_tile(c, run):
            off = tile_off(c, LANES)
            keyc = key_ref[keys_at(off, LANES)]
            eqf = jnp.where(keyc == thr, 1.0, 0.0)
            if keys_on_rows:
                within = jnp.dot(tri, eqf.astype(BF16), preferred_element_type=F32)
                total = within[LANES - 1:LANES, :]
            else:
                within = jnp.dot(eqf.astype(BF16), tri, preferred_element_type=F32)
                total = within[:, LANES - 1:LANES]
            take = jnp.where(keyc > thr, 1.0, jnp.where(run + within <= need, eqf, 0.0))
            store_mask(off, LANES, visible(off, LANES) & (take > 0.5))
            return run + total

        lax.fori_loop(0, 2 * n_tiles, select_tile, jnp.zeros(qshape, F32))

    off0 = tile_off(dc, LANES)
    off1 = tile_off(jnp.maximum(dc - 1, 0), LANES)
    for g in range(N_KV_B):
        qs = jnp.concatenate([q_ref[:, (g * G_B + j) * HD_B:(g * G_B + j + 1) * HD_B] for j in range(G_B)], axis=0)

        def logits_tile(c, carry):
            off = tile_off(c, KEY_TILE)
            s = lax.dot_general(qs, kv_tile(k_ref, kpast_ref, ktail_ref, c, g), nt, preferred_element_type=F32)
            md = madd_ref[:, pl.ds(off, KEY_TILE)]
            for j in range(G_B):
                logit_ref[j * tq:(j + 1) * tq, pl.ds(off, KEY_TILE)] = s[j * tq:(j + 1) * tq] * (HD_B ** -0.5) + md
            return carry

        lax.fori_loop(0, n_tiles, logits_tile, 0)
        for j in range(G_B):
            logit_ref[j * tq:(j + 1) * tq, pl.ds(off0, LANES)] += d0_ref[g * G_B + j]

        @pl.when(dc > 0)
        def _():
            for j in range(G_B):
                logit_ref[j * tq:(j + 1) * tq, pl.ds(off1, LANES)] += d1_ref[g * G_B + j]

        mx_ref[...] = jnp.full(mx_ref.shape, -jnp.inf, F32)

        def max_tile(c, carry):
            lg = logit_ref[:, pl.ds(tile_off(c, KEY_TILE), KEY_TILE)]
            mx_ref[...] = jnp.maximum(mx_ref[...], jnp.maximum(lg[:, :LANES], lg[:, LANES:]))
            return carry

        lax.fori_loop(0, n_tiles, max_tile, 0)
        m = jnp.max(mx_ref[...], axis=1, keepdims=True)
        ls_ref[...] = jnp.zeros(ls_ref.shape, F32)
        acc_ref[...] = jnp.zeros(acc_ref.shape, F32)

        def pv_tile(c, carry):
            off = tile_off(c, KEY_TILE)
            p = jnp.exp(logit_ref[:, pl.ds(off, KEY_TILE)] - m)
            ls_ref[...] += p[:, :LANES] + p[:, LANES:]
            acc_ref[...] += jnp.dot(p.astype(BF16), kv_tile(v_ref, vpast_ref, vtail_ref, c, g),
                                    preferred_element_type=F32)
            return carry

        lax.fori_loop(0, n_tiles, pv_tile, 0)
        res = acc_ref[...] / jnp.sum(ls_ref[...], axis=1, keepdims=True)
        for j in range(G_B):
            h = g * G_B + j
            o_ref[:, h * HD_B:(h + 1) * HD_B] = res[j * tq:(j + 1) * tq].astype(o_ref.dtype)


def _t5_bucket(rel):
    half = REL_BUCKETS // 2
    exact = half // 2
    side = jnp.where(rel > 0, half, 0)
    n = jnp.abs(rel)
    nf = jnp.maximum(n, 1).astype(F32)
    large = exact + (jnp.log(nf / exact) / math.log(REL_MAX_DIST / exact) * (half - exact)).astype(jnp.int32)
    large = jnp.minimum(large, half - 1)
    return side + jnp.where(n < exact, n, large)


def _dsa(zq, kv_new, row0, b, t, iw, ik_all, rel_bias, tq, q_start, l_valid, k_past=None, v_past=None):
    n_past = 0 if k_past is None else k_past.shape[1] // N_KV_B
    lp = ik_all.shape[1]
    topk = min(TOPK_MAX, l_valid // 4)
    assert q_start % LANES == 0 and (tq == LANES or t == tq) and lp % KEY_TILE == 0 and row0 % tq == 0
    assert q_start == n_past and n_past % KEY_TILE == 0 and row0 % t == 0
    assert lp == (n_past + KEY_TILE if n_past else t) and l_valid == n_past + t and (not n_past or t <= KEY_TILE)
    nq = t // tq
    blk0 = row0 // tq
    ikt = ik_all.reshape(b, lp // LANES, LANES, D_I)
    zeros = jnp.zeros_like(ikt)
    ik2 = jnp.concatenate([jnp.concatenate([ikt, zeros], axis=-1), jnp.concatenate([zeros, ikt], axis=-1)], axis=2)
    ji = jnp.arange(LANES, dtype=jnp.int32)[None, :] - jnp.arange(tq, dtype=jnp.int32)[:, None]

    def bias_of(rel):
        hit = _t5_bucket(rel)[None, None] == jnp.arange(REL_BUCKETS, dtype=jnp.int32).reshape((1, -1) + (1,) * rel.ndim)
        return jnp.sum(jnp.where(hit, rel_bias.T.reshape((H_B, REL_BUCKETS) + (1,) * rel.ndim), 0.0), axis=1)

    far = bias_of(jnp.full((1, 1), -2 * LANES, jnp.int32))
    d0 = bias_of(ji) - far
    d1 = bias_of(ji - LANES) - far
    keys_on_rows = tq == LANES
    kern = functools.partial(_dsa_kernel, q_start=q_start, l_valid=l_valid, topk=topk, keys_on_rows=keys_on_rows,
                             n_past=n_past, lp=lp)
    qspec = lambda col: pl.BlockSpec((tq, D_B), lambda bi, i: (blk0 + bi * nq + i, col))
    new_spec = lambda col: pl.BlockSpec((t, D_KV), lambda bi, i: (row0 // t + bi, col))
    cspec = pl.BlockSpec((H_B, tq, LANES), lambda bi, i: (0, 0, 0))
    rows = G_B * tq
    if keys_on_rows:
        iw, iw_spec = iw.T, pl.BlockSpec((H_I, tq), lambda bi, i: (0, bi * nq + i))
        mask_scratch = [pltpu.VMEM((lp, tq), F32), pltpu.VMEM((lp, tq), jnp.int32)]
    else:
        iw_spec = pl.BlockSpec((tq, H_I), lambda bi, i: (bi * nq + i, 0))
        mask_scratch = [pltpu.VMEM((tq, lp), F32), pltpu.VMEM((tq, lp), jnp.int32), pltpu.VMEM((H_I, tq, LANES), F32)]
    past_specs, past_args, tail_scratch = [], [], []
    if n_past:
        past_specs = [pl.BlockSpec((None, n_past * N_KV_B, HD_B), lambda bi, i: (bi, 0, 0))] * 2
        past_args = [k_past, v_past]
        tail_scratch = [pltpu.VMEM((KEY_TILE, D_KV), BF16)] * 2
    return pl.pallas_call(
        kern,
        grid=(b, nq),
        in_specs=[qspec(0), qspec(1), iw_spec, new_spec(0), new_spec(1),
                  pl.BlockSpec((None, lp // LANES, 2 * LANES, LANES), lambda bi, i: (bi, 0, 0, 0)), cspec, cspec]
        + past_specs,
        out_specs=pl.BlockSpec((tq, D_B), lambda bi, i: (bi * nq + i, 0)),
        out_shape=jax.ShapeDtypeStruct((b * t, D_B), BF16),
        scratch_shapes=[pltpu.VMEM((tq, lp), F32), pltpu.VMEM((rows, lp), F32), pltpu.VMEM((rows, LANES), F32),
                        pltpu.VMEM((rows, LANES), F32), pltpu.VMEM((rows, HD_B), F32)] + mask_scratch + tail_scratch,
        compiler_params=_params("parallel", "arbitrary"),
        name="dsa_attention",
    )(zq, zq, iw, kv_new, kv_new, ik2, d0, d1, *past_args)


def _mem_kernel(q_ref, k_ref, v_ref, o_ref):
    nt = (((1,), (1,)), ((), ()))
    for h in range(H_M):
        sl = slice(h * HD_M, (h + 1) * HD_M)
        s = lax.dot_general(q_ref[:, sl], k_ref[:, sl], nt, preferred_element_type=F32) * (HD_M ** -0.5)
        m = jnp.max(s, axis=1, keepdims=True)
        p = jnp.exp(s - m)
        den = jnp.sum(p, axis=1, keepdims=True)
        o = jnp.dot(p.astype(BF16), v_ref[:, sl], preferred_element_type=F32)
        o_ref[:, sl] = (o / den).astype(o_ref.dtype)


def _mem_attention(zq, row0, b, t, mk, mv, tq):
    nq = t // tq
    blk0 = row0 // tq
    return pl.pallas_call(
        _mem_kernel,
        grid=(b, nq),
        in_specs=[pl.BlockSpec((tq, D_M), lambda bi, i: (blk0 + bi * nq + i, 2)),
                  pl.BlockSpec((None, N_MEM, D_M), lambda bi, i: (bi, 0, 0)),
                  pl.BlockSpec((None, N_MEM, D_M), lambda bi, i: (bi, 0, 0))],
        out_specs=pl.BlockSpec((tq, D_M), lambda bi, i: (bi * nq + i, 0)),
        out_shape=jax.ShapeDtypeStruct((b * t, D_M), BF16),
        compiler_params=_params("parallel", "parallel"),
        name="memory_attention",
    )(zq, mk, mv)


def _mix_kernel(yap_ref, yas_ref, gp_ref, gs_ref, obp_ref, obs_ref, omp_ref, oms_ref, ga_ref, gb_ref, gm_ref,
                pa_ref, pb_ref, pm_ref, o_ref, *, na):
    i = pl.program_id(1)
    oa = (_pick(i, na, yap_ref, yas_ref) * _pick(i, na, gp_ref, gs_ref)).astype(BF16)
    acc = ga_ref[...].astype(F32) * jnp.dot(oa, pa_ref[...], preferred_element_type=F32)
    acc = acc + gb_ref[...].astype(F32) * jnp.dot(_pick(i, na, obp_ref, obs_ref), pb_ref[...],
                                                  preferred_element_type=F32)
    acc = acc + gm_ref[...].astype(F32) * jnp.dot(_pick(i, na, omp_ref, oms_ref), pm_ref[...],
                                                  preferred_element_type=F32)
    o_ref[...] = acc.astype(o_ref.dtype)


def _mix(ya2, g2, ob2, om2, zg, pa, pb, pm, tm, tn):
    n = zg.shape[0]
    nj = D_MODEL // tn
    na = g2[0].shape[0] // tm
    two = lambda w: (pl.BlockSpec((tm, w), lambda j, i: (jnp.minimum(i, na - 1), 0)),
                     pl.BlockSpec((tm, w), lambda j, i: (jnp.maximum(i - na, 0), 0)))
    gate = lambda br: pl.BlockSpec((tm, tn), lambda j, i, br=br: (i, br * nj + j))
    wt = pl.BlockSpec((D_A, tn), lambda j, i: (0, j))
    return pl.pallas_call(
        functools.partial(_mix_kernel, na=na),
        grid=(nj, n // tm),
        in_specs=[*two(D_A), *two(D_A), *two(D_B), *two(D_M), gate(0), gate(1), gate(2), wt, wt, wt],
        out_specs=pl.BlockSpec((tm, tn), lambda j, i: (i, j)),
        out_shape=jax.ShapeDtypeStruct((n, D_MODEL), BF16),
        compiler_params=_params("parallel", "parallel"),
        name="branch_mix",
    )(*ya2, *g2, *ob2, *om2, zg, zg, zg, pa, pb, pm)


def _wo_kernel(m_ref, w_ref, xa_ref, xb_ref, g_ref, h_ref, u_ref, *, na):
    x = _pick(pl.program_id(0), na, xa_ref, xb_ref)
    h = x + jnp.dot(m_ref[...], w_ref[...], preferred_element_type=F32)
    h_ref[...] = h
    y = h * lax.rsqrt(jnp.mean(h * h, axis=-1, keepdims=True) + NORM_EPS)
    u_ref[...] = (y * g_ref[...]).astype(u_ref.dtype)


def _wo_residual_norm(mixed, w_o, xa, xb, g, tm):
    n = mixed.shape[0]
    na = xa.shape[0] // tm
    tile = pl.BlockSpec((tm, D_MODEL), lambda i: (i, 0))
    sa, sb = _two_part_specs(tm, D_MODEL, na)
    return pl.pallas_call(
        functools.partial(_wo_kernel, na=na),
        grid=(n // tm,),
        in_specs=[tile, pl.BlockSpec((D_MODEL, D_MODEL), lambda i: (0, 0), pipeline_mode=pl.Buffered(1)), sa, sb,
                  pl.BlockSpec((1, D_MODEL), lambda i: (0, 0))],
        out_specs=[tile, tile],
        out_shape=[jax.ShapeDtypeStruct((n, D_MODEL), F32), jax.ShapeDtypeStruct((n, D_MODEL), BF16)],
        compiler_params=_params("parallel"),
        name="wo_residual_norm",
    )(mixed, w_o, xa, xb, g.reshape(1, D_MODEL))


def _moe_kernel(te_ref, nu_ref, x_ref, gate_ref, *refs):
    o_ref = refs[-1]
    i = pl.program_id(0)

    @pl.when(i >= nu_ref[0])
    def _():
        o_ref[...] = jnp.zeros(o_ref.shape, F32)

    @pl.when(i < nu_ref[0])
    def _():
        x = x_ref[...]
        gate = gate_ref[...]
        out = None
        for j in range(TOP_K_INNER):
            w1_ref, w3_ref, w2_ref = refs[3 * j:3 * j + 3]
            a = jnp.dot(x, w1_ref[...], preferred_element_type=F32)
            b = jnp.dot(x, w3_ref[...], preferred_element_type=F32)
            hid = (a * _sigmoid(a)) * b * gate[:, j:j + 1]
            part = jnp.dot(hid.astype(BF16), w2_ref[...], preferred_element_type=F32)
            out = part if out is None else out + part
        o_ref[...] = out


def _moe_grouped(x_sorted, gate_sorted, tile_experts, n_used, w1, w3, w2):
    npad = x_sorted.shape[0]
    w_specs, w_args = [], []
    for j in range(TOP_K_INNER):
        ew = lambda i, te, nu, j=j: (te[j, i], 0, 0)
        w_specs += [pl.BlockSpec((None, D_MODEL, D_EXPERT), ew), pl.BlockSpec((None, D_MODEL, D_EXPERT), ew),
                    pl.BlockSpec((None, D_EXPERT, D_MODEL), ew)]
        w_args += [w1, w3, w2]
    return pl.pallas_call(
        _moe_kernel,
        grid_spec=pltpu.PrefetchScalarGridSpec(
            num_scalar_prefetch=2,
            grid=(npad // MOE_TM,),
            in_specs=[pl.BlockSpec((MOE_TM, D_MODEL), lambda i, te, nu: (i, 0)),
                      pl.BlockSpec((MOE_TM, TOP_K_INNER), lambda i, te, nu: (i, 0))] + w_specs,
            out_specs=pl.BlockSpec((MOE_TM, D_MODEL), lambda i, te, nu: (i, 0))),
        out_shape=jax.ShapeDtypeStruct((npad, D_MODEL), F32),
        compiler_params=_params("arbitrary"),
        name="moe_grouped",
    )(tile_experts, n_used, x_sorted, gate_sorted, *w_args)


def _route(logits, b_grp, b_rt):
    n = logits.shape[0]
    g_logits = logits[:, :N_GROUPS] + b_grp
    e_logits = (logits[:, N_GROUPS:N_GROUPS + N_EXPERTS] + b_rt).reshape(n, N_GROUPS, EXPERTS_PER_GROUP)
    g_idx = jnp.argmax(g_logits, axis=-1).astype(jnp.int32)
    g_w = jnp.max(jax.nn.softmax(g_logits, axis=-1), axis=-1, keepdims=True)
    onehot_g = g_idx[:, None] == jnp.arange(N_GROUPS, dtype=jnp.int32)[None, :]
    e_in = jnp.sum(jnp.where(onehot_g[:, :, None], e_logits, 0.0), axis=1)
    lane = jnp.arange(EXPERTS_PER_GROUP, dtype=jnp.int32)[None, :]
    i1 = jnp.argmax(e_in, axis=-1).astype(jnp.int32)
    v1 = jnp.max(e_in, axis=-1, keepdims=True)
    rest = jnp.where(lane == i1[:, None], -jnp.inf, e_in)
    i2 = jnp.argmax(rest, axis=-1).astype(jnp.int32)
    v2 = jnp.max(rest, axis=-1, keepdims=True)
    w12 = g_w * jax.nn.softmax(jnp.concatenate([v1, v2], axis=-1), axis=-1)
    lo, hi = jnp.minimum(i1, i2), jnp.maximum(i1, i2)
    gate2 = jnp.where((i1 < i2)[:, None], w12, w12[:, ::-1])
    n_pairs = EXPERTS_PER_GROUP * (EXPERTS_PER_GROUP - 1) // 2
    n_cls = N_GROUPS * n_pairs
    cls = g_idx * n_pairs + lo * (2 * EXPERTS_PER_GROUP - lo - 1) // 2 + hi - lo - 1
    pair_lo = jnp.array([a for a in range(EXPERTS_PER_GROUP) for _ in range(a + 1, EXPERTS_PER_GROUP)], jnp.int32)
    pair_hi = jnp.array([c for a in range(EXPERTS_PER_GROUP) for c in range(a + 1, EXPERTS_PER_GROUP)], jnp.int32)
    cls_ids = jnp.arange(n_cls, dtype=jnp.int32)
    cls_experts = jnp.stack([(cls_ids // n_pairs) * EXPERTS_PER_GROUP + jnp.tile(pair_lo, N_GROUPS),
                             (cls_ids // n_pairs) * EXPERTS_PER_GROUP + jnp.tile(pair_hi, N_GROUPS)])

    onehot_c = cls[:, None] == cls_ids[None, :]
    counts = jnp.sum(onehot_c.astype(jnp.int32), axis=0)
    tiles = (counts + MOE_TM - 1) // MOE_TM
    tile_end = jnp.cumsum(tiles)
    slot0 = (tile_end - tiles) * MOE_TM
    start = jnp.cumsum(counts) - counts
    order = jnp.argsort(cls, stable=True).astype(jnp.int32)
    rank = jnp.sum(jnp.where(onehot_c, jnp.cumsum(onehot_c.astype(jnp.int32), axis=0) - 1, 0), axis=1)
    inv = jnp.sum(jnp.where(onehot_c, slot0[None, :], 0), axis=1) + rank
    npad = n + n_cls * MOE_TM
    slots = jnp.arange(npad, dtype=jnp.int32)
    slot_cls = jnp.minimum(jnp.sum(jnp.where(slots[:, None] >= tile_end[None, :] * MOE_TM, 1, 0), axis=1), n_cls - 1)
    in_cls = slot_cls[:, None] == cls_ids[None, :]
    pos = slots - jnp.sum(jnp.where(in_cls, slot0[None, :], 0), axis=1)
    used = pos < jnp.sum(jnp.where(in_cls, counts[None, :], 0), axis=1)
    src = jnp.where(used, pos + jnp.sum(jnp.where(in_cls, start[None, :], 0), axis=1), slots % n)
    perm = order[src]
    gate_sorted = jnp.where(used[:, None], gate2[perm], 0.0)
    tile_in_cls = slot_cls[::MOE_TM, None] == cls_ids[None, :]
    tile_experts = jnp.sum(jnp.where(tile_in_cls[None], cls_experts[:, None, :], 0), axis=-1).astype(jnp.int32)
    n_used = tile_end[-1]
    tile_ids = jnp.arange(npad // MOE_TM)[None, :]
    last = jnp.sum(jnp.where(tile_ids == n_used - 1, tile_experts, 0), axis=1, keepdims=True)
    tile_experts = jnp.where(tile_ids < n_used, tile_experts, last).astype(jnp.int32)
    return perm, inv, gate_sorted, tile_experts, n_used[None].astype(jnp.int32)


def _final_kernel(h_ref, m_ref, g_ref, yp_ref, ys_ref, *, na):
    h = h_ref[...] + m_ref[...]
    y = h * lax.rsqrt(jnp.mean(h * h, axis=-1, keepdims=True) + NORM_EPS) * g_ref[...]
    i = pl.program_id(0)

    @pl.when(i < na)
    def _():
        yp_ref[...] = y

    @pl.when(i >= na)
    def _():
        ys_ref[...] = y


def _final_norm(h, moe, g, n_p, tm):
    n = h.shape[0]
    na = n_p // tm
    tile = pl.BlockSpec((tm, D_MODEL), lambda i: (i, 0))
    sa, sb = _two_part_specs(tm, D_MODEL, na)
    return pl.pallas_call(
        functools.partial(_final_kernel, na=na),
        grid=(n // tm,),
        in_specs=[tile, tile, pl.BlockSpec((1, D_MODEL), lambda i: (0, 0))],
        out_specs=[sa, sb],
        out_shape=[jax.ShapeDtypeStruct((n_p, D_MODEL), F32), jax.ShapeDtypeStruct((n - n_p, D_MODEL), F32)],
        compiler_params=_params("arbitrary"),
        name="residual_final_norm",
    )(h, moe, g.reshape(1, D_MODEL))


def _pad_cols(w, width):
    return jnp.pad(w, ((0, 0), (0, width - w.shape[1])))


def _pad_rows(w, height):
    return jnp.pad(w, ((0, height - w.shape[0]), (0, 0)))


def _pad_lora(x):
    o = 3 * D_A
    pad = lambda t, w: jnp.pad(t, [(0, 0)] * (t.ndim - 1) + [(0, w - t.shape[-1])])
    return jnp.concatenate([x[..., :o], pad(x[..., o:o + W_LORA], WL_PAD),
                            pad(x[..., o + W_LORA:o + W_LORA + A_LORA], AL_PAD),
                            pad(x[..., o + W_LORA + A_LORA:], GL_PAD)], axis=-1)


def _unpad_lora(x):
    o = 3 * D_A
    return jnp.concatenate([x[..., :o], x[..., o:o + W_LORA], x[..., o + WL_PAD:o + WL_PAD + A_LORA],
                            x[..., o + WL_PAD + AL_PAD:o + WL_PAD + AL_PAD + G_LORA]], axis=-1)


def kernel(x_prompt, x_sample, cache_dsa_k, cache_dsa_v, cache_idx_k, state_rwkv_shift, state_rwkv_wkv, cache_mem_k, cache_mem_v, mem_prompt, rel_bias, g_attn, w_in, rwkv_mu, rwkv_w0, rwkv_w_dec, rwkv_a0, rwkv_w_a, rwkv_w_g, rwkv_k_k, rwkv_k_a, rwkv_r_k, rwkv_lnx_w, rwkv_lnx_b, g_mem, w_mem_kv, p_a, p_b, p_m, w_o, g_ffn, w_grp, b_grp, w_rt, b_rt, w1, w3, w2, g_final):
    assert w_in.shape[0] == 1, "single layer"
    bp, tp, _ = x_prompt.shape
    bs, ts, _ = x_sample.shape
    past = cache_dsa_k.shape[2]
    n_p, n_s = bp * tp, bs * ts
    n = n_p + n_s
    tm = 1024
    xp = x_prompt.reshape(n_p, D_MODEL)
    xs = x_sample.reshape(n_s, D_MODEL)

    offs = [0]
    for s in IN_SIZES:
        offs.append(offs[-1] + s)
    seg = lambda i: w_in[0][:, offs[i]:offs[i + 1]]
    w_a_cols = _pad_lora(seg(0)).astype(BF16)
    w_qim = jnp.concatenate([seg(1), seg(4), seg(7)], axis=1).astype(BF16)
    w_kvi = _pad_cols(jnp.concatenate([seg(2), seg(3), seg(5), seg(6)], axis=1), KVI_PAD).astype(BF16)
    w_gate = seg(8).astype(BF16)
    rw = dict(mu=_pad_lora(rwkv_mu[0])[None], w0=rwkv_w0, a0=rwkv_a0, k_k=rwkv_k_k, k_a=rwkv_k_a,
              w_dec=_pad_rows(rwkv_w_dec[0], WL_PAD).astype(BF16), w_a=_pad_rows(rwkv_w_a[0], AL_PAD).astype(BF16),
              w_g=_pad_rows(rwkv_w_g[0], GL_PAD).astype(BF16))

    u = _rmsnorm_bf16(xp, xs, g_attn[0], 512)
    z_a = _matmul(u, w_a_cols, F32, tm, A_PAD // 4, "proj_rwkv")
    z_qim = _matmul(u, w_qim, BF16, tm, 1024, "proj_queries")
    z_kvi, kv_bf = _matmul_kv(u, w_kvi, tm)
    z_g = _matmul(u, w_gate, BF16, tm, 1024, "proj_gates", logistic=True)

    mem = mem_prompt.reshape(bp * N_MEM, D_MODEL)
    um = _rmsnorm_bf16(mem[:bp * N_MEM // 2], mem[bp * N_MEM // 2:], g_mem[0], 512)
    mkv = _matmul(um, w_mem_kv[0].astype(BF16), F32, 1024, 1024, "proj_mem_kv")
    mk_p = mkv[:, :D_M].reshape(bp, N_MEM, D_M)
    mv_p = mkv[:, D_M:].reshape(bp, N_MEM, D_M)

    lw_vec, lb_vec, rk_vec = rwkv_lnx_w[0], rwkv_lnx_b[0], rwkv_r_k[0].reshape(D_A)

    def rwkv_group(row0, nseq, t, shift0, wkv0, tb, tt):
        r, w, k, v, kk, a, g, last = _rwkv_prep(z_a, row0, nseq, t, shift0, rw, tb)
        y, s_fin = _wkv(r, w, k, v, kk, a, _state_to_tiles(wkv0), _lane_tile(lw_vec), _lane_tile(lb_vec),
                        _lane_tile(rk_vec), tt)
        return y.reshape(nseq * t, D_A), g, _state_from_tiles(s_fin, nseq), _unpad_lora(last)

    ya_p, g_p, wkv_p, shift_p = rwkv_group(0, bp, tp, jnp.zeros((bp, 1, A_PAD), F32),
                                           jnp.zeros((bp, H_A, HD_A, HD_A), F32), 256, 32)
    ya_s, g_s, wkv_s, shift_s = rwkv_group(n_p, bs, ts, _pad_lora(state_rwkv_shift[0]), state_rwkv_wkv[0], ts, ts)

    k_new, v_new = z_kvi[:, :D_KV], z_kvi[:, D_KV:2 * D_KV]
    ik_new = z_kvi[:, 2 * D_KV:2 * D_KV + D_I]
    iw = z_kvi[:, 2 * D_KV + D_I:2 * D_KV + D_I + H_I]
    grp = lambda t, sl, b, tlen: t[sl].reshape(b, tlen, t.shape[-1])
    sp, ss = slice(0, n_p), slice(n_p, n)
    ob_p = _dsa(z_qim, kv_bf, 0, bp, tp, iw[sp], grp(ik_new, sp, bp, tp).astype(BF16), rel_bias,
                Q_BLOCK if tp % Q_BLOCK == 0 else tp, 0, tp)
    l_s = past + ts
    ik_s = jnp.pad(jnp.concatenate([cache_idx_k[0].astype(BF16), grp(ik_new, ss, bs, ts).astype(BF16)], axis=1),
                   ((0, 0), (0, past + KEY_TILE - l_s), (0, 0)))
    ob_s = _dsa(z_qim, kv_bf, n_p, bs, ts, iw[ss], ik_s, rel_bias, Q_BLOCK if ts % Q_BLOCK == 0 else ts, past, l_s,
                cache_dsa_k[0].reshape(bs, past * N_KV_B, HD_B), cache_dsa_v[0].reshape(bs, past * N_KV_B, HD_B))

    om_p = _mem_attention(z_qim, 0, bp, tp, mk_p.astype(BF16), mv_p.astype(BF16), 256)
    om_s = _mem_attention(z_qim, n_p, bs, ts, cache_mem_k[0].reshape(bs, N_MEM, D_M).astype(BF16),
                          cache_mem_v[0].reshape(bs, N_MEM, D_M).astype(BF16), ts)

    mixed = _mix((ya_p, ya_s), (g_p, g_s), (ob_p, ob_s), (om_p, om_s), z_g, p_a[0].astype(BF16), p_b[0].astype(BF16),
                 p_m[0].astype(BF16), 512, 512)
    h, u2 = _wo_residual_norm(mixed, w_o[0].astype(BF16), xp, xs, g_ffn[0], 512)

    w_route = _pad_cols(jnp.concatenate([w_grp[0], w_rt[0]], axis=1), LANES).astype(BF16)
    logits = _matmul(u2, w_route, F32, tm, LANES, "moe_router")
    perm, inv, gate_sorted, tile_experts, n_used = _route(logits, b_grp[0], b_rt[0])
    moe_sorted = _moe_grouped(u2[perm], gate_sorted, tile_experts, n_used,
                              _cast_bf16(w1[0]), _cast_bf16(w3[0]), _cast_bf16(w2[0]))
    y_p, y_s = _final_norm(h, moe_sorted[inv], g_final, n_p, 512)

    st = lambda t, b, tlen, shape: t.reshape((1, b, tlen) + shape)
    return (y_p.reshape(bp, tp, D_MODEL), y_s.reshape(bs, ts, D_MODEL),
            st(k_new[sp], bp, tp, (N_KV_B, HD_B)), st(v_new[sp], bp, tp, (N_KV_B, HD_B)), st(ik_new[sp], bp, tp, (D_I,)),
            shift_p[None], wkv_p[None],
            mk_p.reshape(1, bp, N_MEM, H_M, HD_M), mv_p.reshape(1, bp, N_MEM, H_M, HD_M),
            st(k_new[ss], bs, ts, (N_KV_B, HD_B)), st(v_new[ss], bs, ts, (N_KV_B, HD_B)), st(ik_new[ss], bs, ts, (D_I,)),
            shift_s[None], wkv_s[None])
```

```python
import functools
import math

import jax
import jax.numpy as jnp
from jax import lax
from jax.experimental import pallas as pl
from jax.experimental.pallas import tpu as pltpu

F32 = jnp.float32
BF16 = jnp.bfloat16

LANES = 128
D_MODEL = 2048
CHUNK = 64
NORM_EPS = 1e-6
H_A, HD_A = 16, 64
D_A = H_A * HD_A
W_LORA, A_LORA, G_LORA = 64, 64, 160
LNX_EPS = 64e-5
A_COLS = 3 * D_A + W_LORA + A_LORA + G_LORA
H_B, N_KV_B, HD_B = 8, 2, 128
G_B = H_B // N_KV_B
D_B = H_B * HD_B
D_KV = N_KV_B * HD_B
H_I, D_I = 16, 64
TOPK_MAX = 256
Q_BLOCK = 128
REL_BUCKETS = 32
REL_MAX_DIST = 128
N_MEM, H_M, HD_M = 256, 4, 256
D_M = H_M * HD_M
N_BRANCH = 3
IN_SIZES = (A_COLS, D_B, D_KV, D_KV, H_I * D_I, D_I, H_I, D_M, N_BRANCH * D_MODEL)
N_GROUPS, EXPERTS_PER_GROUP = 4, 4
N_EXPERTS = N_GROUPS * EXPERTS_PER_GROUP
TOP_K_INNER = 2
D_EXPERT = 512

WL_PAD, AL_PAD, GL_PAD = 128, 128, 256
A_PAD = 3 * D_A + WL_PAD + AL_PAD + GL_PAD
KVI_PAD = 2 * D_KV + LANES
KEY_TILE = 2 * LANES
SEQ_PER_TILE = LANES // H_A
MOE_TM = 256
INT_MIN = -(2 ** 31)
VMEM_LIMIT = 48 * 1024 * 1024


def _params(*sem):
    return pltpu.CompilerParams(dimension_semantics=sem, vmem_limit_bytes=VMEM_LIMIT)


def _two_part_specs(tm, width, na):
    return (pl.BlockSpec((tm, width), lambda i: (jnp.minimum(i, na - 1), 0)),
            pl.BlockSpec((tm, width), lambda i: (jnp.maximum(i - na, 0), 0)))


def _pick(i, na, a_ref, b_ref):
    return jnp.where(i < na, a_ref[...], b_ref[...])


def _norm2_kernel(xa_ref, xb_ref, g_ref, o_ref, *, na):
    x = _pick(pl.program_id(0), na, xa_ref, xb_ref)
    y = x * lax.rsqrt(jnp.mean(x * x, axis=-1, keepdims=True) + NORM_EPS)
    o_ref[...] = (y * g_ref[...]).astype(o_ref.dtype)


def _rmsnorm_bf16(xa, xb, g, tm):
    d = xa.shape[1]
    na, nb = xa.shape[0] // tm, xb.shape[0] // tm
    sa, sb = _two_part_specs(tm, d, na)
    return pl.pallas_call(
        functools.partial(_norm2_kernel, na=na),
        grid=(na + nb,),
        in_specs=[sa, sb, pl.BlockSpec((1, d), lambda i: (0, 0))],
        out_specs=pl.BlockSpec((tm, d), lambda i: (i, 0)),
        out_shape=jax.ShapeDtypeStruct(((na + nb) * tm, d), BF16),
        compiler_params=_params("parallel"),
        name="rmsnorm_bf16",
    )(xa, xb, g.reshape(1, d))


def _mm_kernel(a_ref, w_ref, o_ref):
    o_ref[...] = jnp.dot(a_ref[...], w_ref[...], preferred_element_type=F32).astype(o_ref.dtype)


def _matmul(a, w, out_dtype, tm, tn, name):
    n, k = a.shape
    m = w.shape[1]
    return pl.pallas_call(
        _mm_kernel,
        grid=(n // tm, m // tn),
        in_specs=[pl.BlockSpec((tm, k), lambda i, j: (i, 0)), pl.BlockSpec((k, tn), lambda i, j: (0, j))],
        out_specs=pl.BlockSpec((tm, tn), lambda i, j: (i, j)),
        out_shape=jax.ShapeDtypeStruct((n, m), out_dtype),
        compiler_params=_params("parallel", "parallel"),
        name=name,
    )(a, w)


def _mm_kv_kernel(a_ref, w_ref, o_ref, kv_ref):
    acc = jnp.dot(a_ref[...], w_ref[...], preferred_element_type=F32)
    o_ref[...] = acc
    kv_ref[...] = acc[:, :2 * D_KV].astype(kv_ref.dtype)


def _matmul_kv(a, w, tm):
    n, k = a.shape
    m = w.shape[1]
    return pl.pallas_call(
        _mm_kv_kernel,
        grid=(n // tm,),
        in_specs=[pl.BlockSpec((tm, k), lambda i: (i, 0)), pl.BlockSpec((k, m), lambda i: (0, 0))],
        out_specs=[pl.BlockSpec((tm, m), lambda i: (i, 0)), pl.BlockSpec((tm, 2 * D_KV), lambda i: (i, 0))],
        out_shape=[jax.ShapeDtypeStruct((n, m), F32), jax.ShapeDtypeStruct((n, 2 * D_KV), BF16)],
        compiler_params=_params("parallel"),
        name="proj_kv",
    )(a, w)


def _cast_kernel(x_ref, o_ref):
    o_ref[...] = x_ref[...].astype(o_ref.dtype)


def _cast_bf16(x):
    e, r, c = x.shape
    spec = pl.BlockSpec((None, r, c), lambda i: (i, 0, 0))
    return pl.pallas_call(
        _cast_kernel, grid=(e,), in_specs=[spec], out_specs=spec,
        out_shape=jax.ShapeDtypeStruct(x.shape, BF16), compiler_params=_params("parallel"), name="cast_bf16",
    )(x)


def _softplus(x):
    return jnp.maximum(x, 0.0) + jnp.log1p(jnp.exp(-jnp.abs(x)))


def _sigmoid(x):
    return 0.5 * jnp.tanh(0.5 * x) + 0.5


def _rwkv_prep_kernel(z_ref, sh_ref, mu_ref, w0_ref, wdec_ref, a0_ref, wa_ref, wg_ref, kk_ref, ka_ref,
                      r_out, w_out, k_out, v_out, kk_out, a_out, g_out, last_out, carry_ref):
    tb = z_ref.shape[0]

    @pl.when(pl.program_id(1) == 0)
    def _():
        carry_ref[...] = sh_ref[...]

    z = z_ref[...]
    row = lax.broadcasted_iota(jnp.int32, (tb, 1), 0)
    prev = jnp.where(row == 0, carry_ref[...], pltpu.roll(z, 1, axis=0))
    carry_ref[...] = z[tb - 1:tb, :]
    last_out[...] = z[tb - 1:tb, :]
    zm = z + (prev - z) * mu_ref[...]
    r = zm[:, 0:D_A]
    k = zm[:, D_A:2 * D_A]
    v = zm[:, 2 * D_A:3 * D_A]
    o = 3 * D_A
    wl = zm[:, o:o + WL_PAD]
    al = zm[:, o + WL_PAD:o + WL_PAD + AL_PAD]
    gl = zm[:, o + WL_PAD + AL_PAD:]
    lw = jnp.dot(jnp.tanh(wl).astype(BF16), wdec_ref[...], preferred_element_type=F32)
    wv = -_softplus(-(w0_ref[...] + lw)) - 0.5
    a = _sigmoid(a0_ref[...] + jnp.dot(al.astype(BF16), wa_ref[...], preferred_element_type=F32))
    r_out[...] = r
    w_out[...] = jnp.exp(-jnp.exp(wv))
    k_out[...] = k * (1.0 + (a - 1.0) * ka_ref[...])
    v_out[...] = v
    kk_out[...] = k * kk_ref[...]
    a_out[...] = a
    g_out[...] = jnp.dot(_sigmoid(gl).astype(BF16), wg_ref[...], preferred_element_type=F32)


def _rwkv_prep(z, row0, nseq, t, shift0, wts, tb):
    nt = t // tb
    blk0 = row0 // tb
    row = lambda c: pl.BlockSpec((1, c), lambda s, i: (0, 0))
    full = lambda a, b: pl.BlockSpec((a, b), lambda s, i: (0, 0))
    out_spec = pl.BlockSpec((None, tb, D_A), lambda s, i: (s, i, 0))
    out_sds = jax.ShapeDtypeStruct((nseq, t, D_A), F32)
    g_spec = pl.BlockSpec((tb, D_A), lambda s, i: (s * nt + i, 0))
    g_sds = jax.ShapeDtypeStruct((nseq * t, D_A), F32)
    return pl.pallas_call(
        _rwkv_prep_kernel,
        grid=(nseq, nt),
        in_specs=[pl.BlockSpec((tb, A_PAD), lambda s, i: (blk0 + s * nt + i, 0)),
                  pl.BlockSpec((None, 1, A_PAD), lambda s, i: (s, 0, 0)),
                  row(A_PAD), row(D_A), full(WL_PAD, D_A), row(D_A), full(AL_PAD, D_A), full(GL_PAD, D_A),
                  row(D_A), row(D_A)],
        out_specs=[out_spec] * 6 + [g_spec, pl.BlockSpec((None, 1, A_PAD), lambda s, i: (s, 0, 0))],
        out_shape=[out_sds] * 6 + [g_sds, jax.ShapeDtypeStruct((nseq, 1, A_PAD), F32)],
        scratch_shapes=[pltpu.VMEM((1, A_PAD), F32)],
        compiler_params=_params("arbitrary", "arbitrary"),
        name="rwkv_prep",
    )(z, shift0, wts["mu"], wts["w0"], wts["w_dec"], wts["a0"], wts["w_a"], wts["w_g"], wts["k_k"], wts["k_a"])


def _wkv_kernel(r_ref, w_ref, k_ref, v_ref, kk_ref, a_ref, s0_ref, lw_ref, lb_ref, rk_ref,
                y_ref, st_ref, s_ref, ab_ref, tma_ref, tmb_ref, ytm_ref, *, n_tblocks):
    tt = r_ref.shape[1]
    low = lax.broadcasted_iota(jnp.int32, (SEQ_PER_TILE, LANES), 1) < HD_A

    @pl.when(pl.program_id(1) == 0)
    def _():
        s_ref[...] = s0_ref[...]

    n_pairs = tt // 2

    def to_tiles(buf_ref, j):
        j = jnp.minimum(j, n_pairs - 1)
        for ai, ref in enumerate((r_ref, w_ref, k_ref, v_ref, kk_ref, a_ref)):
            x0 = ref[:, 2 * j, :]
            x1 = ref[:, 2 * j + 1, :]
            rows = []
            for h in range(H_A):
                sl = slice((h // 2) * LANES, (h // 2 + 1) * LANES)
                if h % 2 == 0:
                    rows.append(jnp.where(low, x0[:, sl], pltpu.roll(x1[:, sl], HD_A, axis=1)))
                else:
                    rows.append(jnp.where(low, pltpu.roll(x0[:, sl], HD_A, axis=1), x1[:, sl]))
            m = jnp.concatenate(rows, axis=0).T
            buf_ref[ai, 0] = m[:HD_A]
            buf_ref[ai, 1] = m[HD_A:]

    def step(buf_ref, u, t):
        kk = buf_ref[4, u]
        ss = jnp.sum(kk * kk, axis=0, keepdims=True)
        kkn = kk * lax.rsqrt(jnp.maximum(ss, 1e-24))
        ab_ref[0] = -kkn
        ab_ref[1] = kkn * buf_ref[5, u]
        halves = []
        for rows in (slice(0, HD_A // 2), slice(HD_A // 2, HD_A)):
            vh = buf_ref[3, u, rows, :]
            sa = jnp.zeros((HD_A // 2, LANES), F32)
            for k in range(HD_A):
                sa = sa + s_ref[k, rows, :] * ab_ref[0, k:k + 1, :]
            yh = jnp.zeros((HD_A // 2, LANES), F32)
            for k in range(HD_A):
                s_new = (s_ref[k, rows, :] * buf_ref[1, u, k:k + 1, :] + sa * ab_ref[1, k:k + 1, :]
                         + vh * buf_ref[2, u, k:k + 1, :])
                s_ref[k, rows, :] = s_new
                yh = yh + s_new * buf_ref[0, u, k:k + 1, :]
            halves.append(yh)
        y = jnp.concatenate(halves, axis=0)
        vv = buf_ref[3, u]
        mean = jnp.mean(y, axis=0, keepdims=True)
        d = y - mean
        var = jnp.mean(d * d, axis=0, keepdims=True)
        yn = d * lax.rsqrt(var + LNX_EPS) * lw_ref[...] + lb_ref[...]
        bonus = jnp.sum(buf_ref[0, u] * buf_ref[2, u] * rk_ref[...], axis=0, keepdims=True) * vv
        ytm_ref[t] = yn + bonus

    to_tiles(tma_ref, 0)

    def two_pairs(m, carry):
        to_tiles(tmb_ref, 2 * m + 1)
        step(tma_ref, 0, 4 * m)
        step(tma_ref, 1, 4 * m + 1)
        to_tiles(tma_ref, 2 * m + 2)
        step(tmb_ref, 0, 4 * m + 2)
        step(tmb_ref, 1, 4 * m + 3)
        return carry

    lax.fori_loop(0, n_pairs // 2, two_pairs, 0)

    for j in range(tt // 2):
        m = jnp.concatenate([ytm_ref[2 * j], ytm_ref[2 * j + 1]], axis=0).T
        for hp in range(H_A // 2):
            even = m[(2 * hp) * SEQ_PER_TILE:(2 * hp + 1) * SEQ_PER_TILE]
            odd = m[(2 * hp + 1) * SEQ_PER_TILE:(2 * hp + 2) * SEQ_PER_TILE]
            cols = slice(hp * LANES, (hp + 1) * LANES)
            y_ref[:, 2 * j, cols] = jnp.where(low, even, pltpu.roll(odd, HD_A, axis=1))
            y_ref[:, 2 * j + 1, cols] = jnp.where(low, pltpu.roll(even, HD_A, axis=1), odd)

    @pl.when(pl.program_id(1) == n_tblocks - 1)
    def _():
        st_ref[...] = s_ref[...]


def _wkv(r, w, k, v, kk, a, s0, lw, lb, rk, tt):
    nseq, t, _ = r.shape
    p = nseq * H_A
    assert tt % 4 == 0 and t % tt == 0 and nseq % SEQ_PER_TILE == 0
    seq = pl.BlockSpec((SEQ_PER_TILE, tt, D_A), lambda g, i: (g, i, 0))
    vec = pl.BlockSpec((HD_A, LANES), lambda g, i: (0, 0))
    st = pl.BlockSpec((HD_A, HD_A, LANES), lambda g, i: (0, 0, g))
    return pl.pallas_call(
        functools.partial(_wkv_kernel, n_tblocks=t // tt),
        grid=(p // LANES, t // tt),
        in_specs=[seq] * 6 + [st, vec, vec, vec],
        out_specs=[seq, st],
        out_shape=[jax.ShapeDtypeStruct((nseq, t, D_A), F32), jax.ShapeDtypeStruct((HD_A, HD_A, p), F32)],
        scratch_shapes=[pltpu.VMEM((HD_A, HD_A, LANES), F32), pltpu.VMEM((2, HD_A, LANES), F32),
                        pltpu.VMEM((6, 2, HD_A, LANES), F32), pltpu.VMEM((6, 2, HD_A, LANES), F32),
                        pltpu.VMEM((tt, HD_A, LANES), F32)],
        compiler_params=_params("arbitrary", "arbitrary"),
        name="wkv_recurrence",
    )(r, w, k, v, kk, a, s0, lw, lb, rk)


def _state_to_tiles(wkv):
    nseq = wkv.shape[0]
    x = wkv.reshape(nseq // SEQ_PER_TILE, SEQ_PER_TILE, H_A, HD_A, HD_A)
    return x.transpose(4, 3, 0, 2, 1).reshape(HD_A, HD_A, nseq * H_A)


def _state_from_tiles(s, nseq):
    x = s.reshape(HD_A, HD_A, nseq // SEQ_PER_TILE, H_A, SEQ_PER_TILE)
    return x.transpose(2, 4, 3, 1, 0).reshape(nseq, H_A, HD_A, HD_A)


def _lane_tile(vec):
    return jnp.repeat(vec.reshape(H_A, HD_A).T, SEQ_PER_TILE, axis=1)


def _dsa_kernel(*refs, q_start, l_valid, topk, keys_on_rows, n_past, lp):
    refs = list(refs)
    q_ref, iq_ref, iw_ref, k_ref, v_ref, ik2_ref, d0_ref, d1_ref = refs[:8]
    del refs[:8]
    kpast_ref = vpast_ref = ktail_ref = vtail_ref = None
    if n_past:
        kpast_ref, vpast_ref = refs[:2]
        del refs[:2]
    o_ref, madd_ref, logit_ref, mx_ref, ls_ref, acc_ref, score_ref, key_ref = refs[:8]
    del refs[:8]
    if not keys_on_rows:
        iwb_ref = refs.pop(0)
    if n_past:
        ktail_ref, vtail_ref = refs
        n_new = k_ref.shape[0]
        for tail_ref, new_ref in ((ktail_ref, k_ref), (vtail_ref, v_ref)):
            tail_ref[...] = jnp.zeros(tail_ref.shape, BF16)
            tail_ref[0:n_new, :] = new_ref[...]
    n_past_tiles = n_past // KEY_TILE

    def kv_tile(new_ref, past_ref, tail_ref, c, g):
        cols = slice(g * HD_B, (g + 1) * HD_B)
        if not n_past:
            return new_ref[pl.ds(pl.multiple_of(c * KEY_TILE, KEY_TILE), KEY_TILE), cols]
        pc = jnp.minimum(c, n_past_tiles - 1)
        row0 = pl.multiple_of(pc * (KEY_TILE * N_KV_B), KEY_TILE * N_KV_B)
        past = past_ref[pl.ds(row0 + g, KEY_TILE, stride=N_KV_B), :].astype(BF16)
        return jnp.where(c < n_past_tiles, past, tail_ref[:, cols])

    tq = q_ref.shape[0]
    q0 = q_start + pl.program_id(1) * tq
    nt = (((1,), (1,)), ((), ()))
    lane_shift = int(math.log2(LANES))
    chunk_shift = int(math.log2(CHUNK))
    kax = 0 if keys_on_rows else 1
    dc = lax.shift_right_logical(q0, lane_shift)
    n_lane_tiles = jnp.minimum(lax.shift_right_logical(q0 + tq - 1, lane_shift) + 1, lp // LANES)
    n_tiles = lax.shift_right_logical(n_lane_tiles + 1, 1)

    def tile_off(c, width):
        return pl.multiple_of(c * width, width)

    def keys_at(off, width):
        return (pl.ds(off, width), slice(None)) if keys_on_rows else (slice(None), pl.ds(off, width))

    def per_query(x):
        return jnp.sum(x, axis=kax, keepdims=True)

    iw = iw_ref[...] * ((H_I * D_I) ** -0.5)
    if not keys_on_rows:
        for h in range(H_I):
            iwb_ref[h] = jnp.broadcast_to(iw[:, h:h + 1], (tq, LANES))

    def score_lane_tile(c):
        kab = ik2_ref[c]
        acc = [jnp.zeros((LANES, tq) if keys_on_rows else (tq, LANES), F32) for _ in range(2)]
        for hp in range(H_I // 2):
            iq_pair = iq_ref[:, hp * LANES:(hp + 1) * LANES]
            if keys_on_rows:
                d = lax.dot_general(kab, iq_pair, nt, preferred_element_type=F32)
                halves = (d[:LANES], d[LANES:])
                wts = (iw[2 * hp:2 * hp + 1, :], iw[2 * hp + 1:2 * hp + 2, :])
            else:
                d = lax.dot_general(iq_pair, kab, nt, preferred_element_type=F32)
                halves = (d[:, :LANES], d[:, LANES:])
                wts = (iwb_ref[2 * hp], iwb_ref[2 * hp + 1])
            for u in range(2):
                acc[u] = acc[u] + jnp.maximum(halves[u], 0.0) * wts[u]
        score_ref[keys_at(tile_off(c, LANES), LANES)] = acc[0] + acc[1]

    def score_tile(c, carry):
        score_lane_tile(2 * c)
        score_lane_tile(2 * c + 1)
        return carry

    lax.fori_loop(0, n_tiles, score_tile, 0)

    qshape, kshape = ((1, tq), lambda w: (w, 1)) if keys_on_rows else ((tq, 1), lambda w: (1, w))
    qchunk = lax.shift_right_arithmetic(q0 + lax.broadcasted_iota(jnp.int32, qshape, 1 - kax), chunk_shift)

    def visible(lo, width):
        kpos = lo + lax.broadcasted_iota(jnp.int32, kshape(width), kax)
        return (lax.shift_right_arithmetic(kpos, chunk_shift) <= qchunk) & (kpos < l_valid)

    def key_tile(c, carry):
        off = tile_off(c, KEY_TILE)
        bits = pltpu.bitcast(score_ref[keys_at(off, KEY_TILE)] + 0.0, jnp.int32)
        key = bits ^ (lax.shift_right_arithmetic(bits, 31) & jnp.int32(0x7FFFFFFF))
        key_ref[keys_at(off, KEY_TILE)] = jnp.where(visible(off, KEY_TILE), key, jnp.int32(INT_MIN))
        return carry

    lax.fori_loop(0, n_tiles, key_tile, 0)
    kf = jnp.float32(topk)
    sub = 8

    def count(pred):
        def body(c, acc):
            m = jnp.where(pred(key_ref[keys_at(tile_off(c, KEY_TILE), KEY_TILE)]), 1.0, 0.0)
            if keys_on_rows:
                parts = [m[r:r + sub] for r in range(0, KEY_TILE, sub)]
            else:
                parts = [m[:, :LANES], m[:, LANES:]]
            while len(parts) > 1:
                parts = [parts[u] + parts[u + 1] for u in range(0, len(parts), 2)]
            return acc + parts[0]
        acc = lax.fori_loop(0, n_tiles, body, jnp.zeros((sub, tq) if keys_on_rows else (tq, LANES), F32))
        return per_query(acc)

    def search(i, thr):
        cand = thr ^ lax.shift_left(jnp.int32(1), 31 - i)
        return jnp.where(count(lambda kc: kc >= cand) >= kf, cand, thr)

    thr = lax.fori_loop(0, 32, search, jnp.full(qshape, INT_MIN, jnp.int32))

    def store_mask(off, width, sel):
        madd = jnp.where(sel, 0.0, -jnp.inf)
        if keys_on_rows:
            for u in range(width // LANES):
                madd_ref[:, pl.ds(pl.multiple_of(off + u * LANES, LANES), LANES)] = madd[u * LANES:(u + 1) * LANES].T
        else:
            madd_ref[:, pl.ds(off, width)] = madd

    surplus = (count(lambda kc: kc >= thr) > kf) & (thr != jnp.int32(INT_MIN))
    any_surplus = jnp.max(jnp.where(surplus, 1.0, 0.0))

    @pl.when(any_surplus == 0.0)
    def _():
        def select_tile(c, carry):
            off = tile_off(c, KEY_TILE)
            store_mask(off, KEY_TILE, visible(off, KEY_TILE) & (key_ref[keys_at(off, KEY_TILE)] >= thr))
            return carry

        lax.fori_loop(0, n_tiles, select_tile, 0)

    @pl.when(any_surplus > 0.0)
    def _():
        need = kf - count(lambda kc: kc > thr)
        ii = lax.broadcasted_iota(jnp.int32, (LANES, LANES), 0)
        jj = lax.broadcasted_iota(jnp.int32, (LANES, LANES), 1)
        tri = jnp.where((ii >= jj) if keys_on_rows else (ii <= jj), 1.0, 0.0).astype(BF16)

        def select_tile(c, run):
            off = tile_off(c, LANES)
            keyc = key_ref[keys_at(off, LANES)]
            eqf = jnp.where(keyc == thr, 1.0, 0.0)
            if keys_on_rows:
                within = jnp.dot(tri, eqf.astype(BF16), preferred_element_type=F32)
                total = within[LANES - 1:LANES, :]
            else:
                within = jnp.dot(eqf.astype(BF16), tri, preferred_element_type=F32)
                total = within[:, LANES - 1:LANES]
            take = jnp.where(keyc > thr, 1.0, jnp.where(run + within <= need, eqf, 0.0))
            store_mask(off, LANES, visible(off, LANES) & (take > 0.5))
            return run + total

        lax.fori_loop(0, 2 * n_tiles, select_tile, jnp.zeros(qshape, F32))

    off0 = tile_off(dc, LANES)
    off1 = tile_off(jnp.maximum(dc - 1, 0), LANES)
    for g in range(N_KV_B):
        qs = jnp.concatenate([q_ref[:, (g * G_B + j) * HD_B:(g * G_B + j + 1) * HD_B] for j in range(G_B)], axis=0)

        def logits_tile(c, carry):
            off = tile_off(c, KEY_TILE)
            s = lax.dot_general(qs, kv_tile(k_ref, kpast_ref, ktail_ref, c, g), nt, preferred_element_type=F32)
            md = madd_ref[:, pl.ds(off, KEY_TILE)]
            for j in range(G_B):
                logit_ref[j * tq:(j + 1) * tq, pl.ds(off, KEY_TILE)] = s[j * tq:(j + 1) * tq] * (HD_B ** -0.5) + md
            return carry

        lax.fori_loop(0, n_tiles, logits_tile, 0)
        for j in range(G_B):
            logit_ref[j * tq:(j + 1) * tq, pl.ds(off0, LANES)] += d0_ref[g * G_B + j]

        @pl.when(dc > 0)
        def _():
            for j in range(G_B):
                logit_ref[j * tq:(j + 1) * tq, pl.ds(off1, LANES)] += d1_ref[g * G_B + j]

        mx_ref[...] = jnp.full(mx_ref.shape, -jnp.inf, F32)

        def max_tile(c, carry):
            lg = logit_ref[:, pl.ds(tile_off(c, KEY_TILE), KEY_TILE)]
            mx_ref[...] = jnp.maximum(mx_ref[...], jnp.maximum(lg[:, :LANES], lg[:, LANES:]))
            return carry

        lax.fori_loop(0, n_tiles, max_tile, 0)
        m = jnp.max(mx_ref[...], axis=1, keepdims=True)
        ls_ref[...] = jnp.zeros(ls_ref.shape, F32)
        acc_ref[...] = jnp.zeros(acc_ref.shape, F32)

        def pv_tile(c, carry):
            off = tile_off(c, KEY_TILE)
            p = jnp.exp(logit_ref[:, pl.ds(off, KEY_TILE)] - m)
            ls_ref[...] += p[:, :LANES] + p[:, LANES:]
            acc_ref[...] += jnp.dot(p.astype(BF16), kv_tile(v_ref, vpast_ref, vtail_ref, c, g),
                                    preferred_element_type=F32)
            return carry

        lax.fori_loop(0, n_tiles, pv_tile, 0)
        res = acc_ref[...] / jnp.sum(ls_ref[...], axis=1, keepdims=True)
        for j in range(G_B):
            h = g * G_B + j
            o_ref[:, h * HD_B:(h + 1) * HD_B] = res[j * tq:(j + 1) * tq].astype(o_ref.dtype)


def _t5_bucket(rel):
    half = REL_BUCKETS // 2
    exact = half // 2
    side = jnp.where(rel > 0, half, 0)
    n = jnp.abs(rel)
    nf = jnp.maximum(n, 1).astype(F32)
    large = exact + (jnp.log(nf / exact) / math.log(REL_MAX_DIST / exact) * (half - exact)).astype(jnp.int32)
    large = jnp.minimum(large, half - 1)
    return side + jnp.where(n < exact, n, large)


def _dsa(zq, kv_new, row0, b, t, iw, ik_all, rel_bias, tq, q_start, l_valid, k_past=None, v_past=None):
    n_past = 0 if k_past is None else k_past.shape[1] // N_KV_B
    lp = ik_all.shape[1]
    topk = min(TOPK_MAX, l_valid // 4)
    assert q_start % LANES == 0 and (tq == LANES or t == tq) and lp % KEY_TILE == 0 and row0 % tq == 0
    assert q_start == n_past and n_past % KEY_TILE == 0 and row0 % t == 0
    assert lp == (n_past + KEY_TILE if n_past else t) and l_valid == n_past + t and (not n_past or t <= KEY_TILE)
    nq = t // tq
    blk0 = row0 // tq
    ikt = ik_all.reshape(b, lp // LANES, LANES, D_I)
    zeros = jnp.zeros_like(ikt)
    ik2 = jnp.concatenate([jnp.concatenate([ikt, zeros], axis=-1), jnp.concatenate([zeros, ikt], axis=-1)], axis=2)
    ji = jnp.arange(LANES, dtype=jnp.int32)[None, :] - jnp.arange(tq, dtype=jnp.int32)[:, None]

    def bias_of(rel):
        hit = _t5_bucket(rel)[None, None] == jnp.arange(REL_BUCKETS, dtype=jnp.int32).reshape((1, -1) + (1,) * rel.ndim)
        return jnp.sum(jnp.where(hit, rel_bias.T.reshape((H_B, REL_BUCKETS) + (1,) * rel.ndim), 0.0), axis=1)

    far = bias_of(jnp.full((1, 1), -2 * LANES, jnp.int32))
    d0 = bias_of(ji) - far
    d1 = bias_of(ji - LANES) - far
    keys_on_rows = tq == LANES
    kern = functools.partial(_dsa_kernel, q_start=q_start, l_valid=l_valid, topk=topk, keys_on_rows=keys_on_rows,
                             n_past=n_past, lp=lp)
    qspec = lambda col: pl.BlockSpec((tq, D_B), lambda bi, i: (blk0 + bi * nq + i, col))
    new_spec = lambda col: pl.BlockSpec((t, D_KV), lambda bi, i: (row0 // t + bi, col))
    cspec = pl.BlockSpec((H_B, tq, LANES), lambda bi, i: (0, 0, 0))
    rows = G_B * tq
    if keys_on_rows:
        iw, iw_spec = iw.T, pl.BlockSpec((H_I, tq), lambda bi, i: (0, bi * nq + i))
        mask_scratch = [pltpu.VMEM((lp, tq), F32), pltpu.VMEM((lp, tq), jnp.int32)]
    else:
        iw_spec = pl.BlockSpec((tq, H_I), lambda bi, i: (bi * nq + i, 0))
        mask_scratch = [pltpu.VMEM((tq, lp), F32), pltpu.VMEM((tq, lp), jnp.int32), pltpu.VMEM((H_I, tq, LANES), F32)]
    past_specs, past_args, tail_scratch = [], [], []
    if n_past:
        past_specs = [pl.BlockSpec((None, n_past * N_KV_B, HD_B), lambda bi, i: (bi, 0, 0))] * 2
        past_args = [k_past, v_past]
        tail_scratch = [pltpu.VMEM((KEY_TILE, D_KV), BF16)] * 2
    return pl.pallas_call(
        kern,
        grid=(b, nq),
        in_specs=[qspec(0), qspec(1), iw_spec, new_spec(0), new_spec(1),
                  pl.BlockSpec((None, lp // LANES, 2 * LANES, LANES), lambda bi, i: (bi, 0, 0, 0)), cspec, cspec]
        + past_specs,
        out_specs=pl.BlockSpec((tq, D_B), lambda bi, i: (bi * nq + i, 0)),
        out_shape=jax.ShapeDtypeStruct((b * t, D_B), BF16),
        scratch_shapes=[pltpu.VMEM((tq, lp), F32), pltpu.VMEM((rows, lp), F32), pltpu.VMEM((rows, LANES), F32),
                        pltpu.VMEM((rows, LANES), F32), pltpu.VMEM((rows, HD_B), F32)] + mask_scratch + tail_scratch,
        compiler_params=_params("parallel", "arbitrary"),
        name="dsa_attention",
    )(zq, zq, iw, kv_new, kv_new, ik2, d0, d1, *past_args)


def _mem_kernel(q_ref, k_ref, v_ref, o_ref):
    nt = (((1,), (1,)), ((), ()))
    for h in range(H_M):
        sl = slice(h * HD_M, (h + 1) * HD_M)
        s = lax.dot_general(q_ref[:, sl], k_ref[:, sl].astype(BF16), nt, preferred_element_type=F32) * (HD_M ** -0.5)
        m = jnp.max(s, axis=1, keepdims=True)
        p = jnp.exp(s - m)
        den = jnp.sum(p, axis=1, keepdims=True)
        o = jnp.dot(p.astype(BF16), v_ref[:, sl].astype(BF16), preferred_element_type=F32)
        o_ref[:, sl] = (o / den).astype(o_ref.dtype)


def _mem_attention(zq, row0, b, t, mk, mv, tq):
    nq = t // tq
    blk0 = row0 // tq
    return pl.pallas_call(
        _mem_kernel,
        grid=(b, nq),
        in_specs=[pl.BlockSpec((tq, D_M), lambda bi, i: (blk0 + bi * nq + i, 2)),
                  pl.BlockSpec((None, N_MEM, D_M), lambda bi, i: (bi, 0, 0)),
                  pl.BlockSpec((None, N_MEM, D_M), lambda bi, i: (bi, 0, 0))],
        out_specs=pl.BlockSpec((tq, D_M), lambda bi, i: (bi * nq + i, 0)),
        out_shape=jax.ShapeDtypeStruct((b * t, D_M), BF16),
        compiler_params=_params("parallel", "parallel"),
        name="memory_attention",
    )(zq, mk, mv)


def _mix_kernel(yap_ref, yas_ref, gp_ref, gs_ref, obp_ref, obs_ref, omp_ref, oms_ref, ga_ref, gb_ref, gm_ref,
                pa_ref, pb_ref, pm_ref, o_ref, *, na):
    i = pl.program_id(1)
    oa = (_pick(i, na, yap_ref, yas_ref) * _pick(i, na, gp_ref, gs_ref)).astype(BF16)
    acc = _sigmoid(ga_ref[...]) * jnp.dot(oa, pa_ref[...], preferred_element_type=F32)
    acc = acc + _sigmoid(gb_ref[...]) * jnp.dot(_pick(i, na, obp_ref, obs_ref), pb_ref[...],
                                                preferred_element_type=F32)
    acc = acc + _sigmoid(gm_ref[...]) * jnp.dot(_pick(i, na, omp_ref, oms_ref), pm_ref[...],
                                                preferred_element_type=F32)
    o_ref[...] = acc.astype(o_ref.dtype)


def _mix(ya2, g2, ob2, om2, zg, pa, pb, pm, tm, tn):
    n = zg.shape[0]
    nj = D_MODEL // tn
    na = g2[0].shape[0] // tm
    two = lambda w: (pl.BlockSpec((tm, w), lambda j, i: (jnp.minimum(i, na - 1), 0)),
                     pl.BlockSpec((tm, w), lambda j, i: (jnp.maximum(i - na, 0), 0)))
    gate = lambda br: pl.BlockSpec((tm, tn), lambda j, i, br=br: (i, br * nj + j))
    wt = pl.BlockSpec((D_A, tn), lambda j, i: (0, j))
    return pl.pallas_call(
        functools.partial(_mix_kernel, na=na),
        grid=(nj, n // tm),
        in_specs=[*two(D_A), *two(D_A), *two(D_B), *two(D_M), gate(0), gate(1), gate(2), wt, wt, wt],
        out_specs=pl.BlockSpec((tm, tn), lambda j, i: (i, j)),
        out_shape=jax.ShapeDtypeStruct((n, D_MODEL), BF16),
        compiler_params=_params("parallel", "parallel"),
        name="branch_mix",
    )(*ya2, *g2, *ob2, *om2, zg, zg, zg, pa, pb, pm)


def _wo_kernel(m_ref, w_ref, xa_ref, xb_ref, g_ref, h_ref, u_ref, *, na):
    x = _pick(pl.program_id(0), na, xa_ref, xb_ref)
    h = x + jnp.dot(m_ref[...], w_ref[...], preferred_element_type=F32)
    h_ref[...] = h
    y = h * lax.rsqrt(jnp.mean(h * h, axis=-1, keepdims=True) + NORM_EPS)
    u_ref[...] = (y * g_ref[...]).astype(u_ref.dtype)


def _wo_residual_norm(mixed, w_o, xa, xb, g, tm):
    n = mixed.shape[0]
    na = xa.shape[0] // tm
    tile = pl.BlockSpec((tm, D_MODEL), lambda i: (i, 0))
    sa, sb = _two_part_specs(tm, D_MODEL, na)
    return pl.pallas_call(
        functools.partial(_wo_kernel, na=na),
        grid=(n // tm,),
        in_specs=[tile, pl.BlockSpec((D_MODEL, D_MODEL), lambda i: (0, 0), pipeline_mode=pl.Buffered(1)), sa, sb,
                  pl.BlockSpec((1, D_MODEL), lambda i: (0, 0))],
        out_specs=[tile, tile],
        out_shape=[jax.ShapeDtypeStruct((n, D_MODEL), F32), jax.ShapeDtypeStruct((n, D_MODEL), BF16)],
        compiler_params=_params("parallel"),
        name="wo_residual_norm",
    )(mixed, w_o, xa, xb, g.reshape(1, D_MODEL))


def _moe_kernel(te_ref, nu_ref, x_ref, gate_ref, *refs):
    o_ref = refs[-1]
    i = pl.program_id(0)

    @pl.when(i >= nu_ref[0])
    def _():
        o_ref[...] = jnp.zeros(o_ref.shape, F32)

    @pl.when(i < nu_ref[0])
    def _():
        x = x_ref[...]
        gate = gate_ref[...]
        out = None
        for j in range(TOP_K_INNER):
            w1_ref, w3_ref, w2_ref = refs[3 * j:3 * j + 3]
            a = jnp.dot(x, w1_ref[...], preferred_element_type=F32)
            b = jnp.dot(x, w3_ref[...], preferred_element_type=F32)
            hid = (a * _sigmoid(a)) * b * gate[:, j:j + 1]
            part = jnp.dot(hid.astype(BF16), w2_ref[...], preferred_element_type=F32)
            out = part if out is None else out + part
        o_ref[...] = out


def _moe_grouped(x_sorted, gate_sorted, tile_experts, n_used, w1, w3, w2):
    npad = x_sorted.shape[0]
    w_specs, w_args = [], []
    for j in range(TOP_K_INNER):
        ew = lambda i, te, nu, j=j: (te[j, i], 0, 0)
        w_specs += [pl.BlockSpec((None, D_MODEL, D_EXPERT), ew), pl.BlockSpec((None, D_MODEL, D_EXPERT), ew),
                    pl.BlockSpec((None, D_EXPERT, D_MODEL), ew)]
        w_args += [w1, w3, w2]
    return pl.pallas_call(
        _moe_kernel,
        grid_spec=pltpu.PrefetchScalarGridSpec(
            num_scalar_prefetch=2,
            grid=(npad // MOE_TM,),
            in_specs=[pl.BlockSpec((MOE_TM, D_MODEL), lambda i, te, nu: (i, 0)),
                      pl.BlockSpec((MOE_TM, TOP_K_INNER), lambda i, te, nu: (i, 0))] + w_specs,
            out_specs=pl.BlockSpec((MOE_TM, D_MODEL), lambda i, te, nu: (i, 0))),
        out_shape=jax.ShapeDtypeStruct((npad, D_MODEL), F32),
        compiler_params=_params("arbitrary"),
        name="moe_grouped",
    )(tile_experts, n_used, x_sorted, gate_sorted, *w_args)


def _route(logits, b_grp, b_rt):
    n = logits.shape[0]
    g_logits = logits[:, :N_GROUPS] + b_grp
    e_logits = (logits[:, N_GROUPS:N_GROUPS + N_EXPERTS] + b_rt).reshape(n, N_GROUPS, EXPERTS_PER_GROUP)
    g_idx = jnp.argmax(g_logits, axis=-1).astype(jnp.int32)
    g_w = jnp.max(jax.nn.softmax(g_logits, axis=-1), axis=-1, keepdims=True)
    onehot_g = g_idx[:, None] == jnp.arange(N_GROUPS, dtype=jnp.int32)[None, :]
    e_in = jnp.sum(jnp.where(onehot_g[:, :, None], e_logits, 0.0), axis=1)
    lane = jnp.arange(EXPERTS_PER_GROUP, dtype=jnp.int32)[None, :]
    i1 = jnp.argmax(e_in, axis=-1).astype(jnp.int32)
    v1 = jnp.max(e_in, axis=-1, keepdims=True)
    rest = jnp.where(lane == i1[:, None], -jnp.inf, e_in)
    i2 = jnp.argmax(rest, axis=-1).astype(jnp.int32)
    v2 = jnp.max(rest, axis=-1, keepdims=True)
    w12 = g_w * jax.nn.softmax(jnp.concatenate([v1, v2], axis=-1), axis=-1)
    lo, hi = jnp.minimum(i1, i2), jnp.maximum(i1, i2)
    gate2 = jnp.where((i1 < i2)[:, None], w12, w12[:, ::-1])
    n_pairs = EXPERTS_PER_GROUP * (EXPERTS_PER_GROUP - 1) // 2
    n_cls = N_GROUPS * n_pairs
    cls = g_idx * n_pairs + lo * (2 * EXPERTS_PER_GROUP - lo - 1) // 2 + hi - lo - 1
    pair_lo = jnp.array([a for a in range(EXPERTS_PER_GROUP) for _ in range(a + 1, EXPERTS_PER_GROUP)], jnp.int32)
    pair_hi = jnp.array([c for a in range(EXPERTS_PER_GROUP) for c in range(a + 1, EXPERTS_PER_GROUP)], jnp.int32)
    cls_ids = jnp.arange(n_cls, dtype=jnp.int32)
    cls_experts = jnp.stack([(cls_ids // n_pairs) * EXPERTS_PER_GROUP + jnp.tile(pair_lo, N_GROUPS),
                             (cls_ids // n_pairs) * EXPERTS_PER_GROUP + jnp.tile(pair_hi, N_GROUPS)])

    onehot_c = cls[:, None] == cls_ids[None, :]
    counts = jnp.sum(onehot_c.astype(jnp.int32), axis=0)
    tiles = (counts + MOE_TM - 1) // MOE_TM
    tile_end = jnp.cumsum(tiles)
    slot0 = (tile_end - tiles) * MOE_TM
    start = jnp.cumsum(counts) - counts
    order = jnp.argsort(cls, stable=True).astype(jnp.int32)
    rank = jnp.sum(jnp.where(onehot_c, jnp.cumsum(onehot_c.astype(jnp.int32), axis=0) - 1, 0), axis=1)
    inv = jnp.sum(jnp.where(onehot_c, slot0[None, :], 0), axis=1) + rank
    npad = n + n_cls * MOE_TM
    slots = jnp.arange(npad, dtype=jnp.int32)
    slot_cls = jnp.minimum(jnp.sum(jnp.where(slots[:, None] >= tile_end[None, :] * MOE_TM, 1, 0), axis=1), n_cls - 1)
    in_cls = slot_cls[:, None] == cls_ids[None, :]
    pos = slots - jnp.sum(jnp.where(in_cls, slot0[None, :], 0), axis=1)
    used = pos < jnp.sum(jnp.where(in_cls, counts[None, :], 0), axis=1)
    src = jnp.where(used, pos + jnp.sum(jnp.where(in_cls, start[None, :], 0), axis=1), slots % n)
    perm = order[src]
    gate_sorted = jnp.where(used[:, None], gate2[perm], 0.0)
    tile_in_cls = slot_cls[::MOE_TM, None] == cls_ids[None, :]
    tile_experts = jnp.sum(jnp.where(tile_in_cls[None], cls_experts[:, None, :], 0), axis=-1).astype(jnp.int32)
    n_used = tile_end[-1]
    tile_ids = jnp.arange(npad // MOE_TM)[None, :]
    last = jnp.sum(jnp.where(tile_ids == n_used - 1, tile_experts, 0), axis=1, keepdims=True)
    tile_experts = jnp.where(tile_ids < n_used, tile_experts, last).astype(jnp.int32)
    return perm, inv, gate_sorted, tile_experts, n_used[None].astype(jnp.int32)


def _final_kernel(h_ref, m_ref, g_ref, yp_ref, ys_ref, *, na):
    h = h_ref[...] + m_ref[...]
    y = h * lax.rsqrt(jnp.mean(h * h, axis=-1, keepdims=True) + NORM_EPS) * g_ref[...]
    i = pl.program_id(0)

    @pl.when(i < na)
    def _():
        yp_ref[...] = y

    @pl.when(i >= na)
    def _():
        ys_ref[...] = y


def _final_norm(h, moe, g, n_p, tm):
    n = h.shape[0]
    na = n_p // tm
    tile = pl.BlockSpec((tm, D_MODEL), lambda i: (i, 0))
    sa, sb = _two_part_specs(tm, D_MODEL, na)
    return pl.pallas_call(
        functools.partial(_final_kernel, na=na),
        grid=(n // tm,),
        in_specs=[tile, tile, pl.BlockSpec((1, D_MODEL), lambda i: (0, 0))],
        out_specs=[sa, sb],
        out_shape=[jax.ShapeDtypeStruct((n_p, D_MODEL), F32), jax.ShapeDtypeStruct((n - n_p, D_MODEL), F32)],
        compiler_params=_params("arbitrary"),
        name="residual_final_norm",
    )(h, moe, g.reshape(1, D_MODEL))


def _pad_cols(w, width):
    return jnp.pad(w, ((0, 0), (0, width - w.shape[1])))


def _pad_rows(w, height):
    return jnp.pad(w, ((0, height - w.shape[0]), (0, 0)))


def _pad_lora(x):
    o = 3 * D_A
    pad = lambda t, w: jnp.pad(t, [(0, 0)] * (t.ndim - 1) + [(0, w - t.shape[-1])])
    return jnp.concatenate([x[..., :o], pad(x[..., o:o + W_LORA], WL_PAD),
                            pad(x[..., o + W_LORA:o + W_LORA + A_LORA], AL_PAD),
                            pad(x[..., o + W_LORA + A_LORA:], GL_PAD)], axis=-1)


def _unpad_lora(x):
    o = 3 * D_A
    return jnp.concatenate([x[..., :o], x[..., o:o + W_LORA], x[..., o + WL_PAD:o + WL_PAD + A_LORA],
                            x[..., o + WL_PAD + AL_PAD:o + WL_PAD + AL_PAD + G_LORA]], axis=-1)


def kernel(x_prompt, x_sample, cache_dsa_k, cache_dsa_v, cache_idx_k, state_rwkv_shift, state_rwkv_wkv, cache_mem_k, cache_mem_v, mem_prompt, rel_bias, g_attn, w_in, rwkv_mu, rwkv_w0, rwkv_w_dec, rwkv_a0, rwkv_w_a, rwkv_w_g, rwkv_k_k, rwkv_k_a, rwkv_r_k, rwkv_lnx_w, rwkv_lnx_b, g_mem, w_mem_kv, p_a, p_b, p_m, w_o, g_ffn, w_grp, b_grp, w_rt, b_rt, w1, w3, w2, g_final):
    assert w_in.shape[0] == 1, "single layer"
    bp, tp, _ = x_prompt.shape
    bs, ts, _ = x_sample.shape
    past = cache_dsa_k.shape[2]
    n_p, n_s = bp * tp, bs * ts
    n = n_p + n_s
    tm = 1024
    xp = x_prompt.reshape(n_p, D_MODEL)
    xs = x_sample.reshape(n_s, D_MODEL)

    offs = [0]
    for s in IN_SIZES:
        offs.append(offs[-1] + s)
    seg = lambda i: w_in[0][:, offs[i]:offs[i + 1]]
    w_a_cols = _pad_lora(seg(0)).astype(BF16)
    w_qim = jnp.concatenate([seg(1), seg(4), seg(7)], axis=1).astype(BF16)
    w_kvi = _pad_cols(jnp.concatenate([seg(2), seg(3), seg(5), seg(6)], axis=1), KVI_PAD).astype(BF16)
    w_gate = seg(8).astype(BF16)
    rw = dict(mu=_pad_lora(rwkv_mu[0])[None], w0=rwkv_w0, a0=rwkv_a0, k_k=rwkv_k_k, k_a=rwkv_k_a,
              w_dec=_pad_rows(rwkv_w_dec[0], WL_PAD).astype(BF16), w_a=_pad_rows(rwkv_w_a[0], AL_PAD).astype(BF16),
              w_g=_pad_rows(rwkv_w_g[0], GL_PAD).astype(BF16))

    u = _rmsnorm_bf16(xp, xs, g_attn[0], 512)
    z_a = _matmul(u, w_a_cols, F32, tm, A_PAD // 4, "proj_rwkv")
    z_qim = _matmul(u, w_qim, BF16, tm, 1024, "proj_queries")
    z_kvi, kv_bf = _matmul_kv(u, w_kvi, tm)
    z_g = _matmul(u, w_gate, F32, tm, 1024, "proj_gates")

    mem = mem_prompt.reshape(bp * N_MEM, D_MODEL)
    um = _rmsnorm_bf16(mem[:bp * N_MEM // 2], mem[bp * N_MEM // 2:], g_mem[0], 512)
    mkv = _matmul(um, w_mem_kv[0].astype(BF16), F32, 1024, 1024, "proj_mem_kv")
    mk_p = mkv[:, :D_M].reshape(bp, N_MEM, D_M)
    mv_p = mkv[:, D_M:].reshape(bp, N_MEM, D_M)

    lw_vec, lb_vec, rk_vec = rwkv_lnx_w[0], rwkv_lnx_b[0], rwkv_r_k[0].reshape(D_A)

    def rwkv_group(row0, nseq, t, shift0, wkv0, tb, tt):
        r, w, k, v, kk, a, g, last = _rwkv_prep(z_a, row0, nseq, t, shift0, rw, tb)
        y, s_fin = _wkv(r, w, k, v, kk, a, _state_to_tiles(wkv0), _lane_tile(lw_vec), _lane_tile(lb_vec),
                        _lane_tile(rk_vec), tt)
        return y.reshape(nseq * t, D_A), g, _state_from_tiles(s_fin, nseq), _unpad_lora(last)

    ya_p, g_p, wkv_p, shift_p = rwkv_group(0, bp, tp, jnp.zeros((bp, 1, A_PAD), F32),
                                           jnp.zeros((bp, H_A, HD_A, HD_A), F32), 256, 32)
    ya_s, g_s, wkv_s, shift_s = rwkv_group(n_p, bs, ts, _pad_lora(state_rwkv_shift[0]), state_rwkv_wkv[0], ts, ts)

    k_new, v_new = z_kvi[:, :D_KV], z_kvi[:, D_KV:2 * D_KV]
    ik_new = z_kvi[:, 2 * D_KV:2 * D_KV + D_I]
    iw = z_kvi[:, 2 * D_KV + D_I:2 * D_KV + D_I + H_I]
    grp = lambda t, sl, b, tlen: t[sl].reshape(b, tlen, t.shape[-1])
    sp, ss = slice(0, n_p), slice(n_p, n)
    ob_p = _dsa(z_qim, kv_bf, 0, bp, tp, iw[sp], grp(ik_new, sp, bp, tp).astype(BF16), rel_bias,
                Q_BLOCK if tp % Q_BLOCK == 0 else tp, 0, tp)
    l_s = past + ts
    ik_s = jnp.pad(jnp.concatenate([cache_idx_k[0].astype(BF16), grp(ik_new, ss, bs, ts).astype(BF16)], axis=1),
                   ((0, 0), (0, past + KEY_TILE - l_s), (0, 0)))
    ob_s = _dsa(z_qim, kv_bf, n_p, bs, ts, iw[ss], ik_s, rel_bias, Q_BLOCK if ts % Q_BLOCK == 0 else ts, past, l_s,
                cache_dsa_k[0].reshape(bs, past * N_KV_B, HD_B), cache_dsa_v[0].reshape(bs, past * N_KV_B, HD_B))

    om_p = _mem_attention(z_qim, 0, bp, tp, mk_p, mv_p, 256)
    om_s = _mem_attention(z_qim, n_p, bs, ts, cache_mem_k[0].reshape(bs, N_MEM, D_M),
                          cache_mem_v[0].reshape(bs, N_MEM, D_M), ts)

    mixed = _mix((ya_p, ya_s), (g_p, g_s), (ob_p, ob_s), (om_p, om_s), z_g, p_a[0].astype(BF16), p_b[0].astype(BF16),
                 p_m[0].astype(BF16), 256, 1024)
    h, u2 = _wo_residual_norm(mixed, w_o[0].astype(BF16), xp, xs, g_ffn[0], 512)

    w_route = _pad_cols(jnp.concatenate([w_grp[0], w_rt[0]], axis=1), LANES).astype(BF16)
    logits = _matmul(u2, w_route, F32, tm, LANES, "moe_router")
    perm, inv, gate_sorted, tile_experts, n_used = _route(logits, b_grp[0], b_rt[0])
    moe_sorted = _moe_grouped(u2[perm], gate_sorted, tile_experts, n_used,
                              _cast_bf16(w1[0]), _cast_bf16(w3[0]), _cast_bf16(w2[0]))
    y_p, y_s = _final_norm(h, moe_sorted[inv], g_final, n_p, 512)

    st = lambda t, b, tlen, shape: t.reshape((1, b, tlen) + shape)
    return (y_p.reshape(bp, tp, D_MODEL), y_s.reshape(bs, ts, D_MODEL),
            st(k_new[sp], bp, tp, (N_KV_B, HD_B)), st(v_new[sp], bp, tp, (N_KV_B, HD_B)), st(ik_new[sp], bp, tp, (D_I,)),
            shift_p[None], wkv_p[None],
            mk_p.reshape(1, bp, N_MEM, H_M, HD_M), mv_p.reshape(1, bp, N_MEM, H_M, HD_M),
            st(k_new[ss], bs, ts, (N_KV_B, HD_B)), st(v_new[ss], bs, ts, (N_KV_B, HD_B)), st(ik_new[ss], bs, ts, (D_I,)),
            shift_s[None], wkv_s[None])
```

```python
import functools
import math

import jax
import jax.numpy as jnp
from jax import lax
from jax.experimental import pallas as pl
from jax.experimental.pallas import tpu as pltpu

F32 = jnp.float32
BF16 = jnp.bfloat16

LANES = 128
D_MODEL = 2048
CHUNK = 64
NORM_EPS = 1e-6
H_A, HD_A = 16, 64
D_A = H_A * HD_A
W_LORA, A_LORA, G_LORA = 64, 64, 160
LNX_EPS = 64e-5
A_COLS = 3 * D_A + W_LORA + A_LORA + G_LORA
H_B, N_KV_B, HD_B = 8, 2, 128
G_B = H_B // N_KV_B
D_B = H_B * HD_B
D_KV = N_KV_B * HD_B
H_I, D_I = 16, 64
TOPK_MAX = 256
Q_BLOCK = 128
REL_BUCKETS = 32
REL_MAX_DIST = 128
N_MEM, H_M, HD_M = 256, 4, 256
D_M = H_M * HD_M
N_BRANCH = 3
IN_SIZES = (A_COLS, D_B, D_KV, D_KV, H_I * D_I, D_I, H_I, D_M, N_BRANCH * D_MODEL)
N_GROUPS, EXPERTS_PER_GROUP = 4, 4
N_EXPERTS = N_GROUPS * EXPERTS_PER_GROUP
TOP_K_INNER = 2
D_EXPERT = 512

WL_PAD, AL_PAD, GL_PAD = 128, 128, 256
A_PAD = 3 * D_A + WL_PAD + AL_PAD + GL_PAD
KVI_PAD = 2 * D_KV + LANES
KEY_TILE = 2 * LANES
SEQ_PER_TILE = LANES // H_A
MOE_TM = 256
INT_MIN = -(2 ** 31)
VMEM_LIMIT = 48 * 1024 * 1024


def _params(*sem):
    return pltpu.CompilerParams(dimension_semantics=sem, vmem_limit_bytes=VMEM_LIMIT)


def _two_part_specs(tm, width, na):
    return (pl.BlockSpec((tm, width), lambda i: (jnp.minimum(i, na - 1), 0)),
            pl.BlockSpec((tm, width), lambda i: (jnp.maximum(i - na, 0), 0)))


def _pick(i, na, a_ref, b_ref):
    return jnp.where(i < na, a_ref[...], b_ref[...])


def _norm2_kernel(xa_ref, xb_ref, g_ref, o_ref, *, na):
    x = _pick(pl.program_id(0), na, xa_ref, xb_ref)
    y = x * lax.rsqrt(jnp.mean(x * x, axis=-1, keepdims=True) + NORM_EPS)
    o_ref[...] = (y * g_ref[...]).astype(o_ref.dtype)


def _rmsnorm_bf16(xa, xb, g, tm):
    d = xa.shape[1]
    na, nb = xa.shape[0] // tm, xb.shape[0] // tm
    sa, sb = _two_part_specs(tm, d, na)
    return pl.pallas_call(
        functools.partial(_norm2_kernel, na=na),
        grid=(na + nb,),
        in_specs=[sa, sb, pl.BlockSpec((1, d), lambda i: (0, 0))],
        out_specs=pl.BlockSpec((tm, d), lambda i: (i, 0)),
        out_shape=jax.ShapeDtypeStruct(((na + nb) * tm, d), BF16),
        compiler_params=_params("parallel"),
        name="rmsnorm_bf16",
    )(xa, xb, g.reshape(1, d))


def _mm_kernel(a_ref, w_ref, o_ref):
    o_ref[...] = jnp.dot(a_ref[...], w_ref[...], preferred_element_type=F32).astype(o_ref.dtype)


def _matmul(a, w, out_dtype, tm, tn, name):
    n, k = a.shape
    m = w.shape[1]
    return pl.pallas_call(
        _mm_kernel,
        grid=(n // tm, m // tn),
        in_specs=[pl.BlockSpec((tm, k), lambda i, j: (i, 0)), pl.BlockSpec((k, tn), lambda i, j: (0, j))],
        out_specs=pl.BlockSpec((tm, tn), lambda i, j: (i, j)),
        out_shape=jax.ShapeDtypeStruct((n, m), out_dtype),
        compiler_params=_params("parallel", "parallel"),
        name=name,
    )(a, w)


def _mm_kv_kernel(a_ref, w_ref, o_ref, kv_ref):
    acc = jnp.dot(a_ref[...], w_ref[...], preferred_element_type=F32)
    o_ref[...] = acc
    kv_ref[...] = acc[:, :2 * D_KV].astype(kv_ref.dtype)


def _matmul_kv(a, w, tm):
    n, k = a.shape
    m = w.shape[1]
    return pl.pallas_call(
        _mm_kv_kernel,
        grid=(n // tm,),
        in_specs=[pl.BlockSpec((tm, k), lambda i: (i, 0)), pl.BlockSpec((k, m), lambda i: (0, 0))],
        out_specs=[pl.BlockSpec((tm, m), lambda i: (i, 0)), pl.BlockSpec((tm, 2 * D_KV), lambda i: (i, 0))],
        out_shape=[jax.ShapeDtypeStruct((n, m), F32), jax.ShapeDtypeStruct((n, 2 * D_KV), BF16)],
        compiler_params=_params("parallel"),
        name="proj_kv",
    )(a, w)


def _cast_kernel(x_ref, o_ref):
    o_ref[...] = x_ref[...].astype(o_ref.dtype)


def _cast_bf16(x):
    e, r, c = x.shape
    spec = pl.BlockSpec((None, r, c), lambda i: (i, 0, 0))
    return pl.pallas_call(
        _cast_kernel, grid=(e,), in_specs=[spec], out_specs=spec,
        out_shape=jax.ShapeDtypeStruct(x.shape, BF16), compiler_params=_params("parallel"), name="cast_bf16",
    )(x)


def _softplus(x):
    return jnp.maximum(x, 0.0) + jnp.log1p(jnp.exp(-jnp.abs(x)))


def _sigmoid(x):
    return 0.5 * jnp.tanh(0.5 * x) + 0.5


def _rwkv_prep_kernel(z_ref, sh_ref, mu_ref, w0_ref, wdec_ref, a0_ref, wa_ref, wg_ref, kk_ref, ka_ref,
                      r_out, w_out, k_out, v_out, kk_out, a_out, g_out, last_out, carry_ref):
    tb = z_ref.shape[0]

    @pl.when(pl.program_id(1) == 0)
    def _():
        carry_ref[...] = sh_ref[...]

    z = z_ref[...]
    row = lax.broadcasted_iota(jnp.int32, (tb, 1), 0)
    prev = jnp.where(row == 0, carry_ref[...], pltpu.roll(z, 1, axis=0))
    carry_ref[...] = z[tb - 1:tb, :]
    last_out[...] = z[tb - 1:tb, :]
    zm = z + (prev - z) * mu_ref[...]
    r = zm[:, 0:D_A]
    k = zm[:, D_A:2 * D_A]
    v = zm[:, 2 * D_A:3 * D_A]
    o = 3 * D_A
    wl = zm[:, o:o + WL_PAD]
    al = zm[:, o + WL_PAD:o + WL_PAD + AL_PAD]
    gl = zm[:, o + WL_PAD + AL_PAD:]
    lw = jnp.dot(jnp.tanh(wl).astype(BF16), wdec_ref[...], preferred_element_type=F32)
    wv = -_softplus(-(w0_ref[...] + lw)) - 0.5
    a = _sigmoid(a0_ref[...] + jnp.dot(al.astype(BF16), wa_ref[...], preferred_element_type=F32))
    r_out[...] = r
    w_out[...] = jnp.exp(-jnp.exp(wv))
    k_out[...] = k * (1.0 + (a - 1.0) * ka_ref[...])
    v_out[...] = v
    kk_out[...] = k * kk_ref[...]
    a_out[...] = a
    g_out[...] = jnp.dot(_sigmoid(gl).astype(BF16), wg_ref[...], preferred_element_type=F32)


def _rwkv_prep(z, row0, nseq, t, shift0, wts, tb):
    nt = t // tb
    blk0 = row0 // tb
    row = lambda c: pl.BlockSpec((1, c), lambda s, i: (0, 0))
    full = lambda a, b: pl.BlockSpec((a, b), lambda s, i: (0, 0))
    out_spec = pl.BlockSpec((None, tb, D_A), lambda s, i: (s, i, 0))
    out_sds = jax.ShapeDtypeStruct((nseq, t, D_A), F32)
    g_spec = pl.BlockSpec((tb, D_A), lambda s, i: (s * nt + i, 0))
    g_sds = jax.ShapeDtypeStruct((nseq * t, D_A), F32)
    return pl.pallas_call(
        _rwkv_prep_kernel,
        grid=(nseq, nt),
        in_specs=[pl.BlockSpec((tb, A_PAD), lambda s, i: (blk0 + s * nt + i, 0)),
                  pl.BlockSpec((None, 1, A_PAD), lambda s, i: (s, 0, 0)),
                  row(A_PAD), row(D_A), full(WL_PAD, D_A), row(D_A), full(AL_PAD, D_A), full(GL_PAD, D_A),
                  row(D_A), row(D_A)],
        out_specs=[out_spec] * 6 + [g_spec, pl.BlockSpec((None, 1, A_PAD), lambda s, i: (s, 0, 0))],
        out_shape=[out_sds] * 6 + [g_sds, jax.ShapeDtypeStruct((nseq, 1, A_PAD), F32)],
        scratch_shapes=[pltpu.VMEM((1, A_PAD), F32)],
        compiler_params=_params("arbitrary", "arbitrary"),
        name="rwkv_prep",
    )(z, shift0, wts["mu"], wts["w0"], wts["w_dec"], wts["a0"], wts["w_a"], wts["w_g"], wts["k_k"], wts["k_a"])


def _wkv_kernel(r_ref, w_ref, k_ref, v_ref, kk_ref, a_ref, s0_ref, lw_ref, lb_ref, rk_ref,
                y_ref, st_ref, s_ref, ab_ref, tma_ref, tmb_ref, ytm_ref, *, n_tblocks):
    tt = r_ref.shape[1]
    low = lax.broadcasted_iota(jnp.int32, (SEQ_PER_TILE, LANES), 1) < HD_A

    @pl.when(pl.program_id(1) == 0)
    def _():
        s_ref[...] = s0_ref[...]

    n_pairs = tt // 2

    def to_tiles(buf_ref, j):
        j = jnp.minimum(j, n_pairs - 1)
        for ai, ref in enumerate((r_ref, w_ref, k_ref, v_ref, kk_ref, a_ref)):
            x0 = ref[:, 2 * j, :]
            x1 = ref[:, 2 * j + 1, :]
            rows = []
            for h in range(H_A):
                sl = slice((h // 2) * LANES, (h // 2 + 1) * LANES)
                if h % 2 == 0:
                    rows.append(jnp.where(low, x0[:, sl], pltpu.roll(x1[:, sl], HD_A, axis=1)))
                else:
                    rows.append(jnp.where(low, pltpu.roll(x0[:, sl], HD_A, axis=1), x1[:, sl]))
            m = jnp.concatenate(rows, axis=0).T
            buf_ref[ai, 0] = m[:HD_A]
            buf_ref[ai, 1] = m[HD_A:]

    def step(buf_ref, u, t):
        kk = buf_ref[4, u]
        ss = jnp.sum(kk * kk, axis=0, keepdims=True)
        kkn = kk * lax.rsqrt(jnp.maximum(ss, 1e-24))
        ab_ref[0] = -kkn
        ab_ref[1] = kkn * buf_ref[5, u]
        halves = []
        for rows in (slice(0, HD_A // 2), slice(HD_A // 2, HD_A)):
            vh = buf_ref[3, u, rows, :]
            sa = jnp.zeros((HD_A // 2, LANES), F32)
            for k in range(HD_A):
                sa = sa + s_ref[k, rows, :] * ab_ref[0, k:k + 1, :]
            yh = jnp.zeros((HD_A // 2, LANES), F32)
            for k in range(HD_A):
                s_new = (s_ref[k, rows, :] * buf_ref[1, u, k:k + 1, :] + sa * ab_ref[1, k:k + 1, :]
                         + vh * buf_ref[2, u, k:k + 1, :])
                s_ref[k, rows, :] = s_new
                yh = yh + s_new * buf_ref[0, u, k:k + 1, :]
            halves.append(yh)
        y = jnp.concatenate(halves, axis=0)
        vv = buf_ref[3, u]
        mean = jnp.mean(y, axis=0, keepdims=True)
        d = y - mean
        var = jnp.mean(d * d, axis=0, keepdims=True)
        yn = d * lax.rsqrt(var + LNX_EPS) * lw_ref[...] + lb_ref[...]
        bonus = jnp.sum(buf_ref[0, u] * buf_ref[2, u] * rk_ref[...], axis=0, keepdims=True) * vv
        ytm_ref[t] = yn + bonus

    to_tiles(tma_ref, 0)

    def two_pairs(m, carry):
        to_tiles(tmb_ref, 2 * m + 1)
        step(tma_ref, 0, 4 * m)
        step(tma_ref, 1, 4 * m + 1)
        to_tiles(tma_ref, 2 * m + 2)
        step(tmb_ref, 0, 4 * m + 2)
        step(tmb_ref, 1, 4 * m + 3)
        return carry

    lax.fori_loop(0, n_pairs // 2, two_pairs, 0)

    for j in range(tt // 2):
        m = jnp.concatenate([ytm_ref[2 * j], ytm_ref[2 * j + 1]], axis=0).T
        for hp in range(H_A // 2):
            even = m[(2 * hp) * SEQ_PER_TILE:(2 * hp + 1) * SEQ_PER_TILE]
            odd = m[(2 * hp + 1) * SEQ_PER_TILE:(2 * hp + 2) * SEQ_PER_TILE]
            cols = slice(hp * LANES, (hp + 1) * LANES)
            y_ref[:, 2 * j, cols] = jnp.where(low, even, pltpu.roll(odd, HD_A, axis=1))
            y_ref[:, 2 * j + 1, cols] = jnp.where(low, pltpu.roll(even, HD_A, axis=1), odd)

    @pl.when(pl.program_id(1) == n_tblocks - 1)
    def _():
        st_ref[...] = s_ref[...]


def _wkv(r, w, k, v, kk, a, s0, lw, lb, rk, tt):
    nseq, t, _ = r.shape
    p = nseq * H_A
    assert tt % 4 == 0 and t % tt == 0 and nseq % SEQ_PER_TILE == 0
    seq = pl.BlockSpec((SEQ_PER_TILE, tt, D_A), lambda g, i: (g, i, 0))
    vec = pl.BlockSpec((HD_A, LANES), lambda g, i: (0, 0))
    st = pl.BlockSpec((HD_A, HD_A, LANES), lambda g, i: (0, 0, g))
    return pl.pallas_call(
        functools.partial(_wkv_kernel, n_tblocks=t // tt),
        grid=(p // LANES, t // tt),
        in_specs=[seq] * 6 + [st, vec, vec, vec],
        out_specs=[seq, st],
        out_shape=[jax.ShapeDtypeStruct((nseq, t, D_A), F32), jax.ShapeDtypeStruct((HD_A, HD_A, p), F32)],
        scratch_shapes=[pltpu.VMEM((HD_A, HD_A, LANES), F32), pltpu.VMEM((2, HD_A, LANES), F32),
                        pltpu.VMEM((6, 2, HD_A, LANES), F32), pltpu.VMEM((6, 2, HD_A, LANES), F32),
                        pltpu.VMEM((tt, HD_A, LANES), F32)],
        compiler_params=_params("arbitrary", "arbitrary"),
        name="wkv_recurrence",
    )(r, w, k, v, kk, a, s0, lw, lb, rk)


def _state_to_tiles(wkv):
    nseq = wkv.shape[0]
    x = wkv.reshape(nseq // SEQ_PER_TILE, SEQ_PER_TILE, H_A, HD_A, HD_A)
    return x.transpose(4, 3, 0, 2, 1).reshape(HD_A, HD_A, nseq * H_A)


def _state_from_tiles(s, nseq):
    x = s.reshape(HD_A, HD_A, nseq // SEQ_PER_TILE, H_A, SEQ_PER_TILE)
    return x.transpose(2, 4, 3, 1, 0).reshape(nseq, H_A, HD_A, HD_A)


def _lane_tile(vec):
    return jnp.repeat(vec.reshape(H_A, HD_A).T, SEQ_PER_TILE, axis=1)


def _dsa_kernel(*refs, q_start, l_valid, topk, keys_on_rows, n_past, lp):
    refs = list(refs)
    q_ref, iq_ref, iw_ref, k_ref, v_ref, ik2_ref, d0_ref, d1_ref = refs[:8]
    del refs[:8]
    kpast_ref = vpast_ref = ktail_ref = vtail_ref = None
    if n_past:
        kpast_ref, vpast_ref = refs[:2]
        del refs[:2]
    o_ref, madd_ref, logit_ref, mx_ref, ls_ref, acc_ref, score_ref, key_ref = refs[:8]
    del refs[:8]
    if not keys_on_rows:
        iwb_ref = refs.pop(0)
    if n_past:
        ktail_ref, vtail_ref = refs
        n_new = k_ref.shape[0]
        for tail_ref, new_ref in ((ktail_ref, k_ref), (vtail_ref, v_ref)):
            tail_ref[...] = jnp.zeros(tail_ref.shape, BF16)
            tail_ref[0:n_new, :] = new_ref[...]
    n_past_tiles = n_past // KEY_TILE

    def kv_tile(new_ref, past_ref, tail_ref, c, g):
        cols = slice(g * HD_B, (g + 1) * HD_B)
        if not n_past:
            return new_ref[pl.ds(pl.multiple_of(c * KEY_TILE, KEY_TILE), KEY_TILE), cols]
        pc = jnp.minimum(c, n_past_tiles - 1)
        row0 = pl.multiple_of(pc * (KEY_TILE * N_KV_B), KEY_TILE * N_KV_B)
        past = past_ref[pl.ds(row0 + g, KEY_TILE, stride=N_KV_B), :].astype(BF16)
        return jnp.where(c < n_past_tiles, past, tail_ref[:, cols])

    tq = q_ref.shape[0]
    q0 = q_start + pl.program_id(1) * tq
    nt = (((1,), (1,)), ((), ()))
    lane_shift = int(math.log2(LANES))
    chunk_shift = int(math.log2(CHUNK))
    kax = 0 if keys_on_rows else 1
    dc = lax.shift_right_logical(q0, lane_shift)
    n_lane_tiles = jnp.minimum(lax.shift_right_logical(q0 + tq - 1, lane_shift) + 1, lp // LANES)
    n_tiles = lax.shift_right_logical(n_lane_tiles + 1, 1)

    def tile_off(c, width):
        return pl.multiple_of(c * width, width)

    def keys_at(off, width):
        return (pl.ds(off, width), slice(None)) if keys_on_rows else (slice(None), pl.ds(off, width))

    def per_query(x):
        return jnp.sum(x, axis=kax, keepdims=True)

    iw = iw_ref[...] * ((H_I * D_I) ** -0.5)
    if not keys_on_rows:
        for h in range(H_I):
            iwb_ref[h] = jnp.broadcast_to(iw[:, h:h + 1], (tq, LANES))

    def score_lane_tile(c):
        kab = ik2_ref[c]
        acc = [jnp.zeros((LANES, tq) if keys_on_rows else (tq, LANES), F32) for _ in range(2)]
        for hp in range(H_I // 2):
            iq_pair = iq_ref[:, hp * LANES:(hp + 1) * LANES]
            if keys_on_rows:
                d = lax.dot_general(kab, iq_pair, nt, preferred_element_type=F32)
                halves = (d[:LANES], d[LANES:])
                wts = (iw[2 * hp:2 * hp + 1, :], iw[2 * hp + 1:2 * hp + 2, :])
            else:
                d = lax.dot_general(iq_pair, kab, nt, preferred_element_type=F32)
                halves = (d[:, :LANES], d[:, LANES:])
                wts = (iwb_ref[2 * hp], iwb_ref[2 * hp + 1])
            for u in range(2):
                acc[u] = acc[u] + jnp.maximum(halves[u], 0.0) * wts[u]
        score_ref[keys_at(tile_off(c, LANES), LANES)] = acc[0] + acc[1]

    def score_tile(c, carry):
        score_lane_tile(2 * c)
        score_lane_tile(2 * c + 1)
        return carry

    lax.fori_loop(0, n_tiles, score_tile, 0)

    qshape, kshape = ((1, tq), lambda w: (w, 1)) if keys_on_rows else ((tq, 1), lambda w: (1, w))
    qchunk = lax.shift_right_arithmetic(q0 + lax.broadcasted_iota(jnp.int32, qshape, 1 - kax), chunk_shift)

    def visible(lo, width):
        kpos = lo + lax.broadcasted_iota(jnp.int32, kshape(width), kax)
        return (lax.shift_right_arithmetic(kpos, chunk_shift) <= qchunk) & (kpos < l_valid)

    def key_tile(c, carry):
        off = tile_off(c, KEY_TILE)
        bits = pltpu.bitcast(score_ref[keys_at(off, KEY_TILE)] + 0.0, jnp.int32)
        key = bits ^ (lax.shift_right_arithmetic(bits, 31) & jnp.int32(0x7FFFFFFF))
        key_ref[keys_at(off, KEY_TILE)] = jnp.where(visible(off, KEY_TILE), key, jnp.int32(INT_MIN))
        return carry

    lax.fori_loop(0, n_tiles, key_tile, 0)
    kf = jnp.float32(topk)
    sub = 8

    def count(pred):
        def body(c, acc):
            m = jnp.where(pred(key_ref[keys_at(tile_off(c, KEY_TILE), KEY_TILE)]), 1.0, 0.0)
            if keys_on_rows:
                parts = [m[r:r + sub] for r in range(0, KEY_TILE, sub)]
            else:
                parts = [m[:, :LANES], m[:, LANES:]]
            while len(parts) > 1:
                parts = [parts[u] + parts[u + 1] for u in range(0, len(parts), 2)]
            return acc + parts[0]
        acc = lax.fori_loop(0, n_tiles, body, jnp.zeros((sub, tq) if keys_on_rows else (tq, LANES), F32))
        return per_query(acc)

    def search(i, thr):
        cand = thr ^ lax.shift_left(jnp.int32(1), 31 - i)
        return jnp.where(count(lambda kc: kc >= cand) >= kf, cand, thr)

    thr = lax.fori_loop(0, 32, search, jnp.full(qshape, INT_MIN, jnp.int32))

    def store_mask(off, width, sel):
        madd = jnp.where(sel, 0.0, -jnp.inf)
        if keys_on_rows:
            for u in range(width // LANES):
                madd_ref[:, pl.ds(pl.multiple_of(off + u * LANES, LANES), LANES)] = madd[u * LANES:(u + 1) * LANES].T
        else:
            madd_ref[:, pl.ds(off, width)] = madd

    surplus = (count(lambda kc: kc >= thr) > kf) & (thr != jnp.int32(INT_MIN))
    any_surplus = jnp.max(jnp.where(surplus, 1.0, 0.0))

    @pl.when(any_surplus == 0.0)
    def _():
        def select_tile(c, carry):
            off = tile_off(c, KEY_TILE)
            store_mask(off, KEY_TILE, visible(off, KEY_TILE) & (key_ref[keys_at(off, KEY_TILE)] >= thr))
            return carry

        lax.fori_loop(0, n_tiles, select_tile, 0)

    @pl.when(any_surplus > 0.0)
    def _():
        need = kf - count(lambda kc: kc > thr)
        ii = lax.broadcasted_iota(jnp.int32, (LANES, LANES), 0)
        jj = lax.broadcasted_iota(jnp.int32, (LANES, LANES), 1)
        tri = jnp.where((ii >= jj) if keys_on_rows else (ii <= jj), 1.0, 0.0).astype(BF16)

        def select_tile(c, run):
            off = tile_off(c, LANES)
            keyc = key_ref[keys_at(off, LANES)]
            eqf = jnp.where(keyc == thr, 1.0, 0.0)
            if keys_on_rows:
                within = jnp.dot(tri, eqf.astype(BF16), preferred_element_type=F32)
                total = within[LANES - 1:LANES, :]
            else:
                within = jnp.dot(eqf.astype(BF16), tri, preferred_element_type=F32)
                total = within[:, LANES - 1:LANES]
            take = jnp.where(keyc > thr, 1.0, jnp.where(run + within <= need, eqf, 0.0))
            store_mask(off, LANES, visible(off, LANES) & (take > 0.5))
            return run + total

        lax.fori_loop(0, 2 * n_tiles, select_tile, jnp.zeros(qshape, F32))

    off0 = tile_off(dc, LANES)
    off1 = tile_off(jnp.maximum(dc - 1, 0), LANES)
    for g in range(N_KV_B):
        qs = jnp.concatenate([q_ref[:, (g * G_B + j) * HD_B:(g * G_B + j + 1) * HD_B] for j in range(G_B)], axis=0)

        def logits_tile(c, carry):
            off = tile_off(c, KEY_TILE)
            s = lax.dot_general(qs, kv_tile(k_ref, kpast_ref, ktail_ref, c, g), nt, preferred_element_type=F32)
            md = madd_ref[:, pl.ds(off, KEY_TILE)]
            for j in range(G_B):
                logit_ref[j * tq:(j + 1) * tq, pl.ds(off, KEY_TILE)] = s[j * tq:(j + 1) * tq] * (HD_B ** -0.5) + md
            return carry

        lax.fori_loop(0, n_tiles, logits_tile, 0)
        for j in range(G_B):
            logit_ref[j * tq:(j + 1) * tq, pl.ds(off0, LANES)] += d0_ref[g * G_B + j]

        @pl.when(dc > 0)
        def _():
            for j in range(G_B):
                logit_ref[j * tq:(j + 1) * tq, pl.ds(off1, LANES)] += d1_ref[g * G_B + j]

        mx_ref[...] = jnp.full(mx_ref.shape, -jnp.inf, F32)

        def max_tile(c, carry):
            lg = logit_ref[:, pl.ds(tile_off(c, KEY_TILE), KEY_TILE)]
            mx_ref[...] = jnp.maximum(mx_ref[...], jnp.maximum(lg[:, :LANES], lg[:, LANES:]))
            return carry

        lax.fori_loop(0, n_tiles, max_tile, 0)
        m = jnp.max(mx_ref[...], axis=1, keepdims=True)
        ls_ref[...] = jnp.zeros(ls_ref.shape, F32)
        acc_ref[...] = jnp.zeros(acc_ref.shape, F32)

        def pv_tile(c, carry):
            off = tile_off(c, KEY_TILE)
            p = jnp.exp(logit_ref[:, pl.ds(off, KEY_TILE)] - m)
            ls_ref[...] += p[:, :LANES] + p[:, LANES:]
            acc_ref[...] += jnp.dot(p.astype(BF16), kv_tile(v_ref, vpast_ref, vtail_ref, c, g),
                                    preferred_element_type=F32)
            return carry

        lax.fori_loop(0, n_tiles, pv_tile, 0)
        res = acc_ref[...] / jnp.sum(ls_ref[...], axis=1, keepdims=True)
        for j in range(G_B):
            h = g * G_B + j
            o_ref[:, h * HD_B:(h + 1) * HD_B] = res[j * tq:(j + 1) * tq].astype(o_ref.dtype)


def _t5_bucket(rel):
    half = REL_BUCKETS // 2
    exact = half // 2
    side = jnp.where(rel > 0, half, 0)
    n = jnp.abs(rel)
    nf = jnp.maximum(n, 1).astype(F32)
    large = exact + (jnp.log(nf / exact) / math.log(REL_MAX_DIST / exact) * (half - exact)).astype(jnp.int32)
    large = jnp.minimum(large, half - 1)
    return side + jnp.where(n < exact, n, large)


def _dsa(zq, kv_new, row0, b, t, iw, ik_all, rel_bias, tq, q_start, l_valid, k_past=None, v_past=None):
    n_past = 0 if k_past is None else k_past.shape[1] // N_KV_B
    lp = ik_all.shape[1]
    topk = min(TOPK_MAX, l_valid // 4)
    assert q_start % LANES == 0 and (tq == LANES or t == tq) and lp % KEY_TILE == 0 and row0 % tq == 0
    assert q_start == n_past and n_past % KEY_TILE == 0 and row0 % t == 0
    assert lp == (n_past + KEY_TILE if n_past else t) and l_valid == n_past + t and (not n_past or t <= KEY_TILE)
    nq = t // tq
    blk0 = row0 // tq
    ikt = ik_all.reshape(b, lp // LANES, LANES, D_I)
    zeros = jnp.zeros_like(ikt)
    ik2 = jnp.concatenate([jnp.concatenate([ikt, zeros], axis=-1), jnp.concatenate([zeros, ikt], axis=-1)], axis=2)
    ji = jnp.arange(LANES, dtype=jnp.int32)[None, :] - jnp.arange(tq, dtype=jnp.int32)[:, None]

    def bias_of(rel):
        hit = _t5_bucket(rel)[None, None] == jnp.arange(REL_BUCKETS, dtype=jnp.int32).reshape((1, -1) + (1,) * rel.ndim)
        return jnp.sum(jnp.where(hit, rel_bias.T.reshape((H_B, REL_BUCKETS) + (1,) * rel.ndim), 0.0), axis=1)

    far = bias_of(jnp.full((1, 1), -2 * LANES, jnp.int32))
    d0 = bias_of(ji) - far
    d1 = bias_of(ji - LANES) - far
    keys_on_rows = tq == LANES
    kern = functools.partial(_dsa_kernel, q_start=q_start, l_valid=l_valid, topk=topk, keys_on_rows=keys_on_rows,
                             n_past=n_past, lp=lp)
    qspec = lambda col: pl.BlockSpec((tq, D_B), lambda bi, i: (blk0 + bi * nq + i, col))
    new_spec = lambda col: pl.BlockSpec((t, D_KV), lambda bi, i: (row0 // t + bi, col))
    cspec = pl.BlockSpec((H_B, tq, LANES), lambda bi, i: (0, 0, 0))
    rows = G_B * tq
    if keys_on_rows:
        iw, iw_spec = iw.T, pl.BlockSpec((H_I, tq), lambda bi, i: (0, bi * nq + i))
        mask_scratch = [pltpu.VMEM((lp, tq), F32), pltpu.VMEM((lp, tq), jnp.int32)]
    else:
        iw_spec = pl.BlockSpec((tq, H_I), lambda bi, i: (bi * nq + i, 0))
        mask_scratch = [pltpu.VMEM((tq, lp), F32), pltpu.VMEM((tq, lp), jnp.int32), pltpu.VMEM((H_I, tq, LANES), F32)]
    past_specs, past_args, tail_scratch = [], [], []
    if n_past:
        past_specs = [pl.BlockSpec((None, n_past * N_KV_B, HD_B), lambda bi, i: (bi, 0, 0))] * 2
        past_args = [k_past, v_past]
        tail_scratch = [pltpu.VMEM((KEY_TILE, D_KV), BF16)] * 2
    return pl.pallas_call(
        kern,
        grid=(b, nq),
        in_specs=[qspec(0), qspec(1), iw_spec, new_spec(0), new_spec(1),
                  pl.BlockSpec((None, lp // LANES, 2 * LANES, LANES), lambda bi, i: (bi, 0, 0, 0)), cspec, cspec]
        + past_specs,
        out_specs=pl.BlockSpec((tq, D_B), lambda bi, i: (bi * nq + i, 0)),
        out_shape=jax.ShapeDtypeStruct((b * t, D_B), BF16),
        scratch_shapes=[pltpu.VMEM((tq, lp), F32), pltpu.VMEM((rows, lp), F32), pltpu.VMEM((rows, LANES), F32),
                        pltpu.VMEM((rows, LANES), F32), pltpu.VMEM((rows, HD_B), F32)] + mask_scratch + tail_scratch,
        compiler_params=_params("parallel", "arbitrary"),
        name="dsa_attention",
    )(zq, zq, iw, kv_new, kv_new, ik2, d0, d1, *past_args)


def _mem_kernel(q_ref, k_ref, v_ref, o_ref):
    nt = (((1,), (1,)), ((), ()))
    for h in range(H_M):
        sl = slice(h * HD_M, (h + 1) * HD_M)
        s = lax.dot_general(q_ref[:, sl], k_ref[:, sl].astype(BF16), nt, preferred_element_type=F32) * (HD_M ** -0.5)
        m = jnp.max(s, axis=1, keepdims=True)
        p = jnp.exp(s - m)
        den = jnp.sum(p, axis=1, keepdims=True)
        o = jnp.dot(p.astype(BF16), v_ref[:, sl].astype(BF16), preferred_element_type=F32)
        o_ref[:, sl] = (o / den).astype(o_ref.dtype)


def _mem_attention(zq, row0, b, t, mk, mv, tq):
    nq = t // tq
    blk0 = row0 // tq
    return pl.pallas_call(
        _mem_kernel,
        grid=(b, nq),
        in_specs=[pl.BlockSpec((tq, D_M), lambda bi, i: (blk0 + bi * nq + i, 2)),
                  pl.BlockSpec((None, N_MEM, D_M), lambda bi, i: (bi, 0, 0)),
                  pl.BlockSpec((None, N_MEM, D_M), lambda bi, i: (bi, 0, 0))],
        out_specs=pl.BlockSpec((tq, D_M), lambda bi, i: (bi * nq + i, 0)),
        out_shape=jax.ShapeDtypeStruct((b * t, D_M), BF16),
        compiler_params=_params("parallel", "parallel"),
        name="memory_attention",
    )(zq, mk, mv)


def _mix_kernel(yap_ref, yas_ref, gp_ref, gs_ref, obp_ref, obs_ref, omp_ref, oms_ref, ga_ref, gb_ref, gm_ref,
                pa_ref, pb_ref, pm_ref, o_ref, *, na):
    i = pl.program_id(1)
    oa = (_pick(i, na, yap_ref, yas_ref) * _pick(i, na, gp_ref, gs_ref)).astype(BF16)
    acc = _sigmoid(ga_ref[...]) * jnp.dot(oa, pa_ref[...], preferred_element_type=F32)
    acc = acc + _sigmoid(gb_ref[...]) * jnp.dot(_pick(i, na, obp_ref, obs_ref), pb_ref[...],
                                                preferred_element_type=F32)
    acc = acc + _sigmoid(gm_ref[...]) * jnp.dot(_pick(i, na, omp_ref, oms_ref), pm_ref[...],
                                                preferred_element_type=F32)
    o_ref[...] = acc.astype(o_ref.dtype)


def _mix(ya2, g2, ob2, om2, zg, pa, pb, pm, tm, tn):
    n = zg.shape[0]
    nj = D_MODEL // tn
    na = g2[0].shape[0] // tm
    two = lambda w: (pl.BlockSpec((tm, w), lambda j, i: (jnp.minimum(i, na - 1), 0)),
                     pl.BlockSpec((tm, w), lambda j, i: (jnp.maximum(i - na, 0), 0)))
    gate = lambda br: pl.BlockSpec((tm, tn), lambda j, i, br=br: (i, br * nj + j))
    wt = pl.BlockSpec((D_A, tn), lambda j, i: (0, j))
    return pl.pallas_call(
        functools.partial(_mix_kernel, na=na),
        grid=(nj, n // tm),
        in_specs=[*two(D_A), *two(D_A), *two(D_B), *two(D_M), gate(0), gate(1), gate(2), wt, wt, wt],
        out_specs=pl.BlockSpec((tm, tn), lambda j, i: (i, j)),
        out_shape=jax.ShapeDtypeStruct((n, D_MODEL), BF16),
        compiler_params=_params("parallel", "parallel"),
        name="branch_mix",
    )(*ya2, *g2, *ob2, *om2, zg, zg, zg, pa, pb, pm)


def _wo_kernel(m_ref, w_ref, xa_ref, xb_ref, g_ref, h_ref, u_ref, *, na):
    x = _pick(pl.program_id(0), na, xa_ref, xb_ref)
    h = x + jnp.dot(m_ref[...], w_ref[...], preferred_element_type=F32)
    h_ref[...] = h
    y = h * lax.rsqrt(jnp.mean(h * h, axis=-1, keepdims=True) + NORM_EPS)
    u_ref[...] = (y * g_ref[...]).astype(u_ref.dtype)


def _wo_residual_norm(mixed, w_o, xa, xb, g, tm):
    n = mixed.shape[0]
    na = xa.shape[0] // tm
    tile = pl.BlockSpec((tm, D_MODEL), lambda i: (i, 0))
    sa, sb = _two_part_specs(tm, D_MODEL, na)
    return pl.pallas_call(
        functools.partial(_wo_kernel, na=na),
        grid=(n // tm,),
        in_specs=[tile, pl.BlockSpec((D_MODEL, D_MODEL), lambda i: (0, 0), pipeline_mode=pl.Buffered(1)), sa, sb,
                  pl.BlockSpec((1, D_MODEL), lambda i: (0, 0))],
        out_specs=[tile, tile],
        out_shape=[jax.ShapeDtypeStruct((n, D_MODEL), F32), jax.ShapeDtypeStruct((n, D_MODEL), BF16)],
        compiler_params=_params("parallel"),
        name="wo_residual_norm",
    )(mixed, w_o, xa, xb, g.reshape(1, D_MODEL))


def _moe_kernel(te_ref, nu_ref, x_ref, gate_ref, *refs):
    o_ref = refs[-1]
    i = pl.program_id(0)

    @pl.when(i >= nu_ref[0])
    def _():
        o_ref[...] = jnp.zeros(o_ref.shape, F32)

    @pl.when(i < nu_ref[0])
    def _():
        x = x_ref[...]
        gate = gate_ref[...]
        out = None
        for j in range(TOP_K_INNER):
            w1_ref, w3_ref, w2_ref = refs[3 * j:3 * j + 3]
            a = jnp.dot(x, w1_ref[...], preferred_element_type=F32)
            b = jnp.dot(x, w3_ref[...], preferred_element_type=F32)
            hid = (a * _sigmoid(a)) * b * gate[:, j:j + 1]
            part = jnp.dot(hid.astype(BF16), w2_ref[...], preferred_element_type=F32)
            out = part if out is None else out + part
        o_ref[...] = out


def _moe_grouped(x_sorted, gate_sorted, tile_experts, n_used, w1, w3, w2):
    npad = x_sorted.shape[0]
    w_specs, w_args = [], []
    for j in range(TOP_K_INNER):
        ew = lambda i, te, nu, j=j: (te[j, i], 0, 0)
        w_specs += [pl.BlockSpec((None, D_MODEL, D_EXPERT), ew), pl.BlockSpec((None, D_MODEL, D_EXPERT), ew),
                    pl.BlockSpec((None, D_EXPERT, D_MODEL), ew)]
        w_args += [w1, w3, w2]
    return pl.pallas_call(
        _moe_kernel,
        grid_spec=pltpu.PrefetchScalarGridSpec(
            num_scalar_prefetch=2,
            grid=(npad // MOE_TM,),
            in_specs=[pl.BlockSpec((MOE_TM, D_MODEL), lambda i, te, nu: (i, 0)),
                      pl.BlockSpec((MOE_TM, TOP_K_INNER), lambda i, te, nu: (i, 0))] + w_specs,
            out_specs=pl.BlockSpec((MOE_TM, D_MODEL), lambda i, te, nu: (i, 0))),
        out_shape=jax.ShapeDtypeStruct((npad, D_MODEL), F32),
        compiler_params=_params("arbitrary"),
        name="moe_grouped",
    )(tile_experts, n_used, x_sorted, gate_sorted, *w_args)


def _route(logits, b_grp, b_rt):
    n = logits.shape[0]
    g_logits = logits[:, :N_GROUPS] + b_grp
    e_logits = (logits[:, N_GROUPS:N_GROUPS + N_EXPERTS] + b_rt).reshape(n, N_GROUPS, EXPERTS_PER_GROUP)
    g_idx = jnp.argmax(g_logits, axis=-1).astype(jnp.int32)
    g_w = jnp.max(jax.nn.softmax(g_logits, axis=-1), axis=-1, keepdims=True)
    onehot_g = g_idx[:, None] == jnp.arange(N_GROUPS, dtype=jnp.int32)[None, :]
    e_in = jnp.sum(jnp.where(onehot_g[:, :, None], e_logits, 0.0), axis=1)
    lane = jnp.arange(EXPERTS_PER_GROUP, dtype=jnp.int32)[None, :]
    i1 = jnp.argmax(e_in, axis=-1).astype(jnp.int32)
    v1 = jnp.max(e_in, axis=-1, keepdims=True)
    rest = jnp.where(lane == i1[:, None], -jnp.inf, e_in)
    i2 = jnp.argmax(rest, axis=-1).astype(jnp.int32)
    v2 = jnp.max(rest, axis=-1, keepdims=True)
    w12 = g_w * jax.nn.softmax(jnp.concatenate([v1, v2], axis=-1), axis=-1)
    lo, hi = jnp.minimum(i1, i2), jnp.maximum(i1, i2)
    gate2 = jnp.where((i1 < i2)[:, None], w12, w12[:, ::-1])
    n_pairs = EXPERTS_PER_GROUP * (EXPERTS_PER_GROUP - 1) // 2
    n_cls = N_GROUPS * n_pairs
    cls = g_idx * n_pairs + lo * (2 * EXPERTS_PER_GROUP - lo - 1) // 2 + hi - lo - 1
    pair_lo = jnp.array([a for a in range(EXPERTS_PER_GROUP) for _ in range(a + 1, EXPERTS_PER_GROUP)], jnp.int32)
    pair_hi = jnp.array([c for a in range(EXPERTS_PER_GROUP) for c in range(a + 1, EXPERTS_PER_GROUP)], jnp.int32)
    cls_ids = jnp.arange(n_cls, dtype=jnp.int32)
    cls_experts = jnp.stack([(cls_ids // n_pairs) * EXPERTS_PER_GROUP + jnp.tile(pair_lo, N_GROUPS),
                             (cls_ids // n_pairs) * EXPERTS_PER_GROUP + jnp.tile(pair_hi, N_GROUPS)])

    onehot_c = cls[:, None] == cls_ids[None, :]
    counts = jnp.sum(onehot_c.astype(jnp.int32), axis=0)
    tiles = (counts + MOE_TM - 1) // MOE_TM
    tile_end = jnp.cumsum(tiles)
    slot0 = (tile_end - tiles) * MOE_TM
    start = jnp.cumsum(counts) - counts
    order = jnp.argsort(cls, stable=True).astype(jnp.int32)
    rank = jnp.sum(jnp.where(onehot_c, jnp.cumsum(onehot_c.astype(jnp.int32), axis=0) - 1, 0), axis=1)
    inv = jnp.sum(jnp.where(onehot_c, slot0[None, :], 0), axis=1) + rank
    npad = n + n_cls * MOE_TM
    slots = jnp.arange(npad, dtype=jnp.int32)
    slot_cls = jnp.minimum(jnp.sum(jnp.where(slots[:, None] >= tile_end[None, :] * MOE_TM, 1, 0), axis=1), n_cls - 1)
    in_cls = slot_cls[:, None] == cls_ids[None, :]
    pos = slots - jnp.sum(jnp.where(in_cls, slot0[None, :], 0), axis=1)
    used = pos < jnp.sum(jnp.where(in_cls, counts[None, :], 0), axis=1)
    src = jnp.where(used, pos + jnp.sum(jnp.where(in_cls, start[None, :], 0), axis=1), slots % n)
    perm = order[src]
    gate_sorted = jnp.where(used[:, None], gate2[perm], 0.0)
    tile_in_cls = slot_cls[::MOE_TM, None] == cls_ids[None, :]
    tile_experts = jnp.sum(jnp.where(tile_in_cls[None], cls_experts[:, None, :], 0), axis=-1).astype(jnp.int32)
    n_used = tile_end[-1]
    tile_ids = jnp.arange(npad // MOE_TM)[None, :]
    last = jnp.sum(jnp.where(tile_ids == n_used - 1, tile_experts, 0), axis=1, keepdims=True)
    tile_experts = jnp.where(tile_ids < n_used, tile_experts, last).astype(jnp.int32)
    return perm, inv, gate_sorted, tile_experts, n_used[None].astype(jnp.int32)


def _final_kernel(h_ref, m_ref, g_ref, yp_ref, ys_ref, *, na):
    h = h_ref[...] + m_ref[...]
    y = h * lax.rsqrt(jnp.mean(h * h, axis=-1, keepdims=True) + NORM_EPS) * g_ref[...]
    i = pl.program_id(0)

    @pl.when(i < na)
    def _():
        yp_ref[...] = y

    @pl.when(i >= na)
    def _():
        ys_ref[...] = y


def _final_norm(h, moe, g, n_p, tm):
    n = h.shape[0]
    na = n_p // tm
    tile = pl.BlockSpec((tm, D_MODEL), lambda i: (i, 0))
    sa, sb = _two_part_specs(tm, D_MODEL, na)
    return pl.pallas_call(
        functools.partial(_final_kernel, na=na),
        grid=(n // tm,),
        in_specs=[tile, tile, pl.BlockSpec((1, D_MODEL), lambda i: (0, 0))],
        out_specs=[sa, sb],
        out_shape=[jax.ShapeDtypeStruct((n_p, D_MODEL), F32), jax.ShapeDtypeStruct((n - n_p, D_MODEL), F32)],
        compiler_params=_params("arbitrary"),
        name="residual_final_norm",
    )(h, moe, g.reshape(1, D_MODEL))


def _pad_cols(w, width):
    return jnp.pad(w, ((0, 0), (0, width - w.shape[1])))


def _pad_rows(w, height):
    return jnp.pad(w, ((0, height - w.shape[0]), (0, 0)))


def _pad_lora(x):
    o = 3 * D_A
    pad = lambda t, w: jnp.pad(t, [(0, 0)] * (t.ndim - 1) + [(0, w - t.shape[-1])])
    return jnp.concatenate([x[..., :o], pad(x[..., o:o + W_LORA], WL_PAD),
                            pad(x[..., o + W_LORA:o + W_LORA + A_LORA], AL_PAD),
                            pad(x[..., o + W_LORA + A_LORA:], GL_PAD)], axis=-1)


def _unpad_lora(x):
    o = 3 * D_A
    return jnp.concatenate([x[..., :o], x[..., o:o + W_LORA], x[..., o + WL_PAD:o + WL_PAD + A_LORA],
                            x[..., o + WL_PAD + AL_PAD:o + WL_PAD + AL_PAD + G_LORA]], axis=-1)


def kernel(x_prompt, x_sample, cache_dsa_k, cache_dsa_v, cache_idx_k, state_rwkv_shift, state_rwkv_wkv, cache_mem_k, cache_mem_v, mem_prompt, rel_bias, g_attn, w_in, rwkv_mu, rwkv_w0, rwkv_w_dec, rwkv_a0, rwkv_w_a, rwkv_w_g, rwkv_k_k, rwkv_k_a, rwkv_r_k, rwkv_lnx_w, rwkv_lnx_b, g_mem, w_mem_kv, p_a, p_b, p_m, w_o, g_ffn, w_grp, b_grp, w_rt, b_rt, w1, w3, w2, g_final):
    assert w_in.shape[0] == 1, "single layer"
    bp, tp, _ = x_prompt.shape
    bs, ts, _ = x_sample.shape
    past = cache_dsa_k.shape[2]
    n_p, n_s = bp * tp, bs * ts
    n = n_p + n_s
    tm = 1024
    xp = x_prompt.reshape(n_p, D_MODEL)
    xs = x_sample.reshape(n_s, D_MODEL)

    offs = [0]
    for s in IN_SIZES:
        offs.append(offs[-1] + s)
    seg = lambda i: w_in[0][:, offs[i]:offs[i + 1]]
    w_a_cols = _pad_lora(seg(0)).astype(BF16)
    w_qim = jnp.concatenate([seg(1), seg(4), seg(7)], axis=1).astype(BF16)
    w_kvi = _pad_cols(jnp.concatenate([seg(2), seg(3), seg(5), seg(6)], axis=1), KVI_PAD).astype(BF16)
    w_gate = seg(8).astype(BF16)
    rw = dict(mu=_pad_lora(rwkv_mu[0])[None], w0=rwkv_w0, a0=rwkv_a0, k_k=rwkv_k_k, k_a=rwkv_k_a,
              w_dec=_pad_rows(rwkv_w_dec[0], WL_PAD).astype(BF16), w_a=_pad_rows(rwkv_w_a[0], AL_PAD).astype(BF16),
              w_g=_pad_rows(rwkv_w_g[0], GL_PAD).astype(BF16))

    u = _rmsnorm_bf16(xp, xs, g_attn[0], 512)
    z_a = _matmul(u, w_a_cols, F32, tm, A_PAD // 2, "proj_rwkv")
    z_qim = _matmul(u, w_qim, BF16, tm, 1024, "proj_queries")
    z_kvi, kv_bf = _matmul_kv(u, w_kvi, tm)
    z_g = _matmul(u, w_gate, F32, tm, 1024, "proj_gates")

    mem = mem_prompt.reshape(bp * N_MEM, D_MODEL)
    um = _rmsnorm_bf16(mem[:bp * N_MEM // 2], mem[bp * N_MEM // 2:], g_mem[0], 512)
    mkv = _matmul(um, w_mem_kv[0].astype(BF16), F32, 1024, 1024, "proj_mem_kv")
    mk_p = mkv[:, :D_M].reshape(bp, N_MEM, D_M)
    mv_p = mkv[:, D_M:].reshape(bp, N_MEM, D_M)

    lw_vec, lb_vec, rk_vec = rwkv_lnx_w[0], rwkv_lnx_b[0], rwkv_r_k[0].reshape(D_A)

    def rwkv_group(row0, nseq, t, shift0, wkv0, tb, tt):
        r, w, k, v, kk, a, g, last = _rwkv_prep(z_a, row0, nseq, t, shift0, rw, tb)
        y, s_fin = _wkv(r, w, k, v, kk, a, _state_to_tiles(wkv0), _lane_tile(lw_vec), _lane_tile(lb_vec),
                        _lane_tile(rk_vec), tt)
        return y.reshape(nseq * t, D_A), g, _state_from_tiles(s_fin, nseq), _unpad_lora(last)

    ya_p, g_p, wkv_p, shift_p = rwkv_group(0, bp, tp, jnp.zeros((bp, 1, A_PAD), F32),
                                           jnp.zeros((bp, H_A, HD_A, HD_A), F32), 256, 32)
    ya_s, g_s, wkv_s, shift_s = rwkv_group(n_p, bs, ts, _pad_lora(state_rwkv_shift[0]), state_rwkv_wkv[0], ts, ts)

    k_new, v_new = z_kvi[:, :D_KV], z_kvi[:, D_KV:2 * D_KV]
    ik_new = z_kvi[:, 2 * D_KV:2 * D_KV + D_I]
    iw = z_kvi[:, 2 * D_KV + D_I:2 * D_KV + D_I + H_I]
    grp = lambda t, sl, b, tlen: t[sl].reshape(b, tlen, t.shape[-1])
    sp, ss = slice(0, n_p), slice(n_p, n)
    ob_p = _dsa(z_qim, kv_bf, 0, bp, tp, iw[sp], grp(ik_new, sp, bp, tp).astype(BF16), rel_bias,
                Q_BLOCK if tp % Q_BLOCK == 0 else tp, 0, tp)
    l_s = past + ts
    ik_s = jnp.pad(jnp.concatenate([cache_idx_k[0].astype(BF16), grp(ik_new, ss, bs, ts).astype(BF16)], axis=1),
                   ((0, 0), (0, past + KEY_TILE - l_s), (0, 0)))
    ob_s = _dsa(z_qim, kv_bf, n_p, bs, ts, iw[ss], ik_s, rel_bias, Q_BLOCK if ts % Q_BLOCK == 0 else ts, past, l_s,
                cache_dsa_k[0].reshape(bs, past * N_KV_B, HD_B), cache_dsa_v[0].reshape(bs, past * N_KV_B, HD_B))

    om_p = _mem_attention(z_qim, 0, bp, tp, mk_p, mv_p, 256)
    om_s = _mem_attention(z_qim, n_p, bs, ts, cache_mem_k[0].reshape(bs, N_MEM, D_M),
                          cache_mem_v[0].reshape(bs, N_MEM, D_M), ts)

    mixed = _mix((ya_p, ya_s), (g_p, g_s), (ob_p, ob_s), (om_p, om_s), z_g, p_a[0].astype(BF16), p_b[0].astype(BF16),
                 p_m[0].astype(BF16), 256, 1024)
    h, u2 = _wo_residual_norm(mixed, w_o[0].astype(BF16), xp, xs, g_ffn[0], 512)

    w_route = _pad_cols(jnp.concatenate([w_grp[0], w_rt[0]], axis=1), LANES).astype(BF16)
    logits = _matmul(u2, w_route, F32, tm, LANES, "moe_router")
    perm, inv, gate_sorted, tile_experts, n_used = _route(logits, b_grp[0], b_rt[0])
    moe_sorted = _moe_grouped(u2[perm], gate_sorted, tile_experts, n_used,
                              _cast_bf16(w1[0]), _cast_bf16(w3[0]), _cast_bf16(w2[0]))
    y_p, y_s = _final_norm(h, moe_sorted[inv], g_final, n_p, 512)

    st = lambda t, b, tlen, shape: t.reshape((1, b, tlen) + shape)
    return (y_p.reshape(bp, tp, D_MODEL), y_s.reshape(bs, ts, D_MODEL),
            st(k_new[sp], bp, tp, (N_KV_B, HD_B)), st(v_new[sp], bp, tp, (N_KV_B, HD_B)), st(ik_new[sp], bp, tp, (D_I,)),
            shift_p[None], wkv_p[None],
            mk_p.reshape(1, bp, N_MEM, H_M, HD_M), mv_p.reshape(1, bp, N_MEM, H_M, HD_M),
            st(k_new[ss], bs, ts, (N_KV_B, HD_B)), st(v_new[ss], bs, ts, (N_KV_B, HD_B)), st(ik_new[ss], bs, ts, (D_I,)),
            shift_s[None], wkv_s[None])
```
